```python
import math
import jax, jax.numpy as jnp
from jax import lax
import numpy as np

D_MODEL = 1024
BATCH = 4
SEQ = 8192
DEPTH = 4

CHUNK = 64
Q_BLOCK = 128
N_MEM = 256
EPS = 1e-6
NEG_INF = -1e30

H_A = 8
DH_A = 64
D_LAT = 128
H_IDX = 8
DH_IDX = 64
TOPK_MAX = 256

H_B = 4
DH_B = 64

H_C = 16
H_C_KV = 2
G_C = H_C // H_C_KV
DH_C = 64
WINDOW = 128
WIN_CHUNKS = WINDOW // CHUNK
BAND = (WIN_CHUNKS + 1) * CHUNK

H_X = 4
DH_X = 64

D_FF = 4 * D_MODEL

N_BUCKETS = 32
MAX_DIST = 1024
H_BIAS = H_A + H_B + H_C

N_EVEN = (DEPTH + 1) // 2
N_ODD = DEPTH // 2

EVEN_SIZES = (H_A * DH_A, D_LAT, H_IDX * DH_IDX, DH_IDX, H_IDX, 2 * H_B * DH_B, 2 * H_B * DH_B, 2 * H_B * DH_B)
EVEN_COLS = sum(EVEN_SIZES)
ODD_SIZES = (H_C * DH_C, H_C_KV * DH_C, H_C_KV * DH_C)
ODD_COLS = sum(ODD_SIZES)

kernel_name = "hybrid_dsa_diff_swa_streaming_block"


def _split_points(sizes):
    pts, acc = [], 0
    for s in sizes[:-1]:
        acc += s
        pts.append(acc)
    return pts


def rms_norm(x, g):
    xf = x.astype(jnp.float32)
    y = xf * lax.rsqrt(jnp.mean(xf * xf, axis=-1, keepdims=True) + EPS)
    return (y * g.astype(jnp.float32)).astype(x.dtype)


def t5_bucket(rel):
    nb = N_BUCKETS // 2
    max_exact = nb // 2
    n = jnp.abs(rel)
    nf = jnp.maximum(n, 1).astype(jnp.float32)
    large = max_exact + (jnp.log(nf / max_exact) / math.log(MAX_DIST / max_exact) * (nb - max_exact)).astype(jnp.int32)
    large = jnp.minimum(large, nb - 1)
    return jnp.where(rel > 0, nb, 0) + jnp.where(n < max_exact, n, large)


def chunk_id(pos):
    return pos // CHUNK


def dsa_attention(q_a, c_kv, q_idx, k_idx, w_idx, w_uk, w_uv, table_a):
    B, S = q_a.shape[0], q_a.shape[1]
    topk = min(TOPK_MAX, S // 4)
    n_blk = S // Q_BLOCK
    spos = jnp.arange(S)
    scale = DH_A ** -0.5
    q_lat = jnp.einsum('bshd,hdl->bshl', q_a, w_uk)

    def block(i):
        t0 = i * Q_BLOCK
        tpos = t0 + jnp.arange(Q_BLOCK)
        ql = lax.dynamic_slice_in_dim(q_lat, t0, Q_BLOCK, axis=1)
        qi = lax.dynamic_slice_in_dim(q_idx, t0, Q_BLOCK, axis=1)
        wi = lax.dynamic_slice_in_dim(w_idx, t0, Q_BLOCK, axis=1)
        act = jax.nn.relu(jnp.einsum('bqhd,bsd->bqhs', qi, k_idx).astype(jnp.float32))
        score = jnp.einsum('bqhs,bqh->bqs', act, wi.astype(jnp.float32))
        adm = chunk_id(spos)[None, :] <= chunk_id(tpos)[:, None]
        score = jnp.where(adm[None], score, -jnp.inf)
        _, sel = lax.top_k(score, topk)
        valid = chunk_id(sel) <= chunk_id(tpos)[None, :, None]
        c_sel = jax.vmap(lambda c, ix: c[ix])(c_kv, sel)
        bias = table_a[t5_bucket(sel - tpos[None, :, None])]
        s = jnp.einsum('bqhl,bqkl->bqkh', ql, c_sel).astype(jnp.float32) * scale + bias
        s = jnp.where(valid[..., None], s, NEG_INF)
        p = jax.nn.softmax(s, axis=2)
        return jnp.einsum('bqkh,bqkl->bqhl', p.astype(c_sel.dtype), c_sel)

    o = lax.map(block, jnp.arange(n_blk))
    o = jnp.moveaxis(o, 0, 1).reshape(B, S, H_A, D_LAT)
    o = jnp.einsum('bshl,hld->bshd', o, w_uv)
    return o.reshape(B, S, H_A * DH_A)


def diff_attention(q_b, k_b, v_b, b_lambda, b_subln, table_b, lam_init):
    B, S = q_b.shape[0], q_b.shape[1]
    n_blk = S // Q_BLOCK
    spos = jnp.arange(S)
    scale = DH_B ** -0.5
    lf = b_lambda.astype(jnp.float32)
    lam = jnp.exp(jnp.sum(lf[0] * lf[1])) - jnp.exp(jnp.sum(lf[2] * lf[3])) + lam_init

    def block(i):
        t0 = i * Q_BLOCK
        tpos = t0 + jnp.arange(Q_BLOCK)
        qb = lax.dynamic_slice_in_dim(q_b, t0, Q_BLOCK, axis=1)
        bias = jnp.moveaxis(table_b[t5_bucket(spos[None, :] - tpos[:, None])], -1, 0)
        adm = chunk_id(spos)[None, :] <= chunk_id(tpos)[:, None]
        s = jnp.einsum('bqhmd,bshmd->bhmqs', qb, k_b).astype(jnp.float32) * scale + bias[None, :, None]
        s = jnp.where(adm, s, NEG_INF)
        p = jax.nn.softmax(s, axis=-1)
        a = p[:, :, 0] - lam * p[:, :, 1]
        return jnp.einsum('bhqs,bshe->bqhe', a.astype(v_b.dtype), v_b)

    o = lax.map(block, jnp.arange(n_blk))
    o = jnp.moveaxis(o, 0, 1).reshape(B, S, H_B, 2 * DH_B)
    o = rms_norm(o, b_subln) * (1.0 - lam_init)
    return o.reshape(B, S, H_B * 2 * DH_B)


def even_mixer(h, w_in, a_kv_norm, w_uk, w_uv, b_lambda, b_subln, w_out, table, lam_init):
    B, S, _ = h.shape
    z = h @ w_in
    q_a, c_kv, q_i, k_i, w_i, q_b, k_b, v_b = jnp.split(z, _split_points(EVEN_SIZES), axis=-1)
    c_kv = rms_norm(c_kv, a_kv_norm)
    o_a = dsa_attention(q_a.reshape(B, S, H_A, DH_A), c_kv, q_i.reshape(B, S, H_IDX, DH_IDX), k_i, w_i,
                        w_uk, w_uv, table[:, :H_A])
    o_b = diff_attention(q_b.reshape(B, S, H_B, 2, DH_B), k_b.reshape(B, S, H_B, 2, DH_B),
                         v_b.reshape(B, S, H_B, 2 * DH_B), b_lambda, b_subln, table[:, H_A:H_A + H_B], lam_init)
    return jnp.concatenate([o_a, o_b], axis=-1) @ w_out


def odd_mixer(h, w_in, sinks, w_out, table):
    B, S, _ = h.shape
    nc = S // CHUNK
    q, k, v = jnp.split(h @ w_in, _split_points(ODD_SIZES), axis=-1)
    q = q.reshape(B, nc, CHUNK, H_C_KV, G_C, DH_C)
    k = k.reshape(B, nc, CHUNK, H_C_KV, DH_C)
    v = v.reshape(B, nc, CHUNK, H_C_KV, DH_C)

    def band(t):
        tp = jnp.pad(t, ((0, 0), (WIN_CHUNKS, 0), (0, 0), (0, 0), (0, 0)))
        return jnp.concatenate([tp[:, j:j + nc] for j in range(WIN_CHUNKS + 1)], axis=2)

    kb, vb = band(k), band(v)
    rel = (jnp.arange(BAND)[None, :] - WIN_CHUNKS * CHUNK) - jnp.arange(CHUNK)[:, None]
    bias = jnp.moveaxis(table[:, H_A + H_B:][t5_bucket(rel)], -1, 0).reshape(H_C_KV, G_C, CHUNK, BAND)
    s = jnp.einsum('bcqkgd,bcskd->bckgqs', q, kb).astype(jnp.float32) * (DH_C ** -0.5) + bias
    valid = (jnp.arange(nc)[:, None] - WIN_CHUNKS + jnp.arange(BAND)[None, :] // CHUNK) >= 0
    s = jnp.where(valid[None, :, None, None, None, :], s, NEG_INF)
    sink = jnp.broadcast_to(sinks.astype(jnp.float32).reshape(H_C_KV, G_C)[None, None, :, :, None, None],
                            s.shape[:-1] + (1,))
    p = jax.nn.softmax(jnp.concatenate([s, sink], axis=-1), axis=-1)[..., :-1]
    o = jnp.einsum('bckgqs,bcskd->bcqkgd', p.astype(vb.dtype), vb).reshape(B, S, H_C * DH_C)
    return o @ w_out


def memory_cross_attention(h, mem, wq, wkv, wo, mem_norm):
    B, S, _ = h.shape
    M = mem.shape[1]
    q = (h @ wq).reshape(B, S, H_X, DH_X)
    kv = (rms_norm(mem, mem_norm) @ wkv).reshape(B, M, 2, H_X, DH_X)
    s = jnp.einsum('bshd,bmhd->bhsm', q, kv[:, :, 0]).astype(jnp.float32) * (DH_X ** -0.5)
    p = jax.nn.softmax(s, axis=-1)
    o = jnp.einsum('bhsm,bmhd->bshd', p.astype(kv.dtype), kv[:, :, 1]).reshape(B, S, H_X * DH_X)
    return o @ wo


def sq_relu_mlp(h, w1, w2):
    return jnp.square(jax.nn.relu(h @ w1)) @ w2


def setup_inputs(seed: int = 0) -> dict:
    key = jax.random.key(seed)
    ks = jax.random.split(key, 20)

    def nrm(k, shape, scale):
        return jax.random.normal(k, shape, jnp.float32) * scale

    D = D_MODEL
    return {
        "x": nrm(ks[0], (BATCH, SEQ, D), 1.0),
        "mem": nrm(ks[1], (BATCH, N_MEM, D), 1.0),
        "rel_bias_table": nrm(ks[2], (N_BUCKETS, H_BIAS), 0.3),
        "norm_g": 1.0 + nrm(ks[3], (DEPTH, 6, D), 0.05),
        "ev_w_in": nrm(ks[4], (N_EVEN, D, EVEN_COLS), D ** -0.5),
        "ev_a_kv_norm": 1.0 + nrm(ks[5], (N_EVEN, D_LAT), 0.05),
        "ev_a_w_uk": nrm(ks[6], (N_EVEN, H_A, DH_A, D_LAT), DH_A ** -0.5),
        "ev_a_w_uv": nrm(ks[7], (N_EVEN, H_A, D_LAT, DH_A), D_LAT ** -0.5),
        "ev_b_lambda": nrm(ks[8], (N_EVEN, 4, DH_B), 0.1),
        "ev_b_subln": 1.0 + nrm(ks[9], (N_EVEN, 2 * DH_B), 0.05),
        "ev_w_out": nrm(ks[10], (N_EVEN, H_A * DH_A + 2 * H_B * DH_B, D), D ** -0.5),
        "od_w_in": nrm(ks[11], (N_ODD, D, ODD_COLS), D ** -0.5),
        "od_sinks": nrm(ks[12], (N_ODD, H_C), 0.5),
        "od_w_out": nrm(ks[13], (N_ODD, H_C * DH_C, D), (H_C * DH_C) ** -0.5),
        "xa_wq": nrm(ks[14], (DEPTH, D, H_X * DH_X), D ** -0.5),
        "xa_wkv": nrm(ks[15], (DEPTH, D, 2 * H_X * DH_X), D ** -0.5),
        "xa_wo": nrm(ks[16], (DEPTH, H_X * DH_X, D), (H_X * DH_X) ** -0.5),
        "xa_mem_norm": 1.0 + nrm(ks[17], (DEPTH, D), 0.05),
        "mlp_w1": nrm(ks[18], (DEPTH, D, D_FF), D ** -0.5),
        "mlp_w2": nrm(ks[19], (DEPTH, D_FF, D), D_FF ** -0.5),
    }


def reference(x, mem, rel_bias_table, norm_g, ev_w_in, ev_a_kv_norm, ev_a_w_uk, ev_a_w_uv, ev_b_lambda,
              ev_b_subln, ev_w_out, od_w_in, od_sinks, od_w_out, xa_wq, xa_wkv, xa_wo, xa_mem_norm,
              mlp_w1, mlp_w2):
    for l in range(DEPTH):
        g = norm_g[l]
        h = rms_norm(x, g[0])
        if l % 2 == 0:
            e = l // 2
            lam_init = 0.8 - 0.6 * math.exp(-0.3 * l)
            y = even_mixer(h, ev_w_in[e], ev_a_kv_norm[e], ev_a_w_uk[e], ev_a_w_uv[e], ev_b_lambda[e],
                           ev_b_subln[e], ev_w_out[e], rel_bias_table, lam_init)
        else:
            o = l // 2
            y = odd_mixer(h, od_w_in[o], od_sinks[o], od_w_out[o], rel_bias_table)
        x = x + rms_norm(y, g[1])
        h = rms_norm(x, g[2])
        y = memory_cross_attention(h, mem, xa_wq[l], xa_wkv[l], xa_wo[l], xa_mem_norm[l])
        x = x + rms_norm(y, g[3])
        h = rms_norm(x, g[4])
        y = sq_relu_mlp(h, mlp_w1[l], mlp_w2[l])
        x = x + rms_norm(y, g[5])
    return x
```

```python
import functools
import math

import numpy as np
import jax
import jax.numpy as jnp
from jax import lax
from jax.experimental import pallas as pl
from jax.experimental.pallas import tpu as pltpu

F32 = jnp.float32
BF16 = jnp.bfloat16

D_MODEL = 1024
CHUNK = 64
EPS = 1e-6
NEG_INF = -1e30
N_MEM = 256

H_A, DH_A, D_LAT = 8, 64, 128
H_IDX, DH_IDX = 8, 64
TOPK_MAX = 256
H_B, DH_B = 4, 64
H_C, H_C_KV, DH_C = 16, 2, 64
G_C = H_C // H_C_KV
WINDOW = 128
H_X, DH_X = 4, 64
D_FF = 4 * D_MODEL
N_BUCKETS = 32
MAX_DIST = 1024

LANES = 128
TQ = 128
TK = 512
SUB = TK // LANES
TM_PROJ = 512
TM_TAIL = 256
FF_CHUNK = 1024
VMEM_LIMIT = 56 * 1024 * 1024
N_BISECT = 14


def _rms(x, g):
    return x * lax.rsqrt(jnp.mean(x * x, axis=-1, keepdims=True) + EPS) * g


def _dot(a, b):
    return jnp.dot(a, b, preferred_element_type=F32)


def _t5_bucket_np(rel):
    nb = N_BUCKETS // 2
    max_exact = nb // 2
    n = np.abs(rel)
    nf = np.maximum(n, 1).astype(np.float32)
    large = max_exact + (np.log(nf / max_exact) / math.log(MAX_DIST / max_exact) * (nb - max_exact)).astype(np.int32)
    large = np.minimum(large, nb - 1)
    return np.where(rel > 0, nb, 0) + np.where(n < max_exact, n, large)


def _sat_blocks():
    d = 1
    while _t5_bucket_np(np.array([-(d * LANES - (LANES - 1))]))[0] != N_BUCKETS // 2 - 1:
        d += 1
    return d


N_SAT = _sat_blocks()
N_BT = N_SAT + 1


def _causal_bias_tiles(table_cols):
    r = np.arange(LANES)
    rel = np.stack([(r[None, :] - d * LANES) - r[:, None] for d in range(N_BT)])
    bucket = _t5_bucket_np(rel)
    tiles = jnp.take(table_cols.astype(F32), jnp.asarray(bucket), axis=0)
    h = table_cols.shape[1]
    return jnp.moveaxis(tiles, -1, 0).reshape(h * N_BT, LANES, LANES)


def _lane_half_mask(rows, half):
    lane = lax.broadcasted_iota(jnp.int32, (rows, LANES), 1)
    return (lane < 64) if half == 0 else (lane >= 64)


def _proj_even_kernel(x_ref, g_ref, w_ref, kvn_ref, wuk_ref,
                      qlat_ref, ckv_ref, qi_ref, qb_ref, kb_ref, vb_ref, k2_ref, wi_ref):
    h = _rms(x_ref[...], g_ref[0:1, :]).astype(BF16)
    z = _dot(h, w_ref[...])
    qa = z[:, 0:512].astype(BF16)
    for hh in range(H_A):
        ql = _dot(qa[:, hh * DH_A:(hh + 1) * DH_A], wuk_ref[hh]) * (DH_A ** -0.5)
        qlat_ref[hh] = ql.astype(BF16)
    ckv_ref[...] = _rms(z[:, 512:640], kvn_ref[...]).astype(BF16)
    qi_ref[...] = z[:, 640:1152].astype(BF16)
    qb_ref[...] = (z[:, 1152:1664] * (DH_B ** -0.5)).astype(BF16)
    kb_ref[...] = z[:, 1664:2176].astype(BF16)
    vb_ref[...] = z[:, 2176:2688].astype(BF16)
    k2_ref[...] = z[:, 2688:2816].astype(BF16)
    wi_ref[...] = z[:, 2816:2816 + H_IDX]


def _proj_even(x2, g, w_in, kvn, w_uk):
    t = x2.shape[0]
    tm = min(TM_PROJ, t)
    cols = [w_in[:, 0:512], w_in[:, 512:640], w_in[:, 640:1152], w_in[:, 1224:1736], w_in[:, 1736:2248],
            w_in[:, 2248:2760], w_in[:, 1152:1216], w_in[:, 1152:1216], w_in[:, 1216:1224],
            jnp.zeros((D_MODEL, LANES - H_IDX), w_in.dtype)]
    w = jnp.concatenate(cols, axis=1).astype(BF16)
    n = w.shape[1]
    row = lambda i: (i, 0)
    full2 = lambda i: (0, 0)
    outs = [
        jax.ShapeDtypeStruct((H_A, t, D_LAT), BF16),
        jax.ShapeDtypeStruct((t, D_LAT), BF16),
        jax.ShapeDtypeStruct((t, 512), BF16),
        jax.ShapeDtypeStruct((t, 512), BF16),
        jax.ShapeDtypeStruct((t, 512), BF16),
        jax.ShapeDtypeStruct((t, 512), BF16),
        jax.ShapeDtypeStruct((t, LANES), BF16),
        jax.ShapeDtypeStruct((t, H_IDX), F32),
    ]
    out_specs = [
        pl.BlockSpec((H_A, tm, D_LAT), lambda i: (0, i, 0)),
        pl.BlockSpec((tm, D_LAT), row),
        pl.BlockSpec((tm, 512), row),
        pl.BlockSpec((tm, 512), row),
        pl.BlockSpec((tm, 512), row),
        pl.BlockSpec((tm, 512), row),
        pl.BlockSpec((tm, LANES), row),
        pl.BlockSpec((tm, H_IDX), row),
    ]
    return pl.pallas_call(
        _proj_even_kernel,
        out_shape=outs,
        grid=(t // tm,),
        in_specs=[
            pl.BlockSpec((tm, D_MODEL), row),
            pl.BlockSpec((6, D_MODEL), full2),
            pl.BlockSpec((D_MODEL, n), full2),
            pl.BlockSpec((1, D_LAT), full2),
            pl.BlockSpec((H_A, DH_A, D_LAT), lambda i: (0, 0, 0)),
        ],
        out_specs=out_specs,
        compiler_params=pltpu.CompilerParams(vmem_limit_bytes=VMEM_LIMIT),
        name="proj_even",
    )(x2, g, w, kvn.reshape(1, D_LAT), w_uk.astype(BF16))


def _proj_odd_kernel(x_ref, g_ref, w_ref, q_ref, k2_ref, v2_ref):
    h = _rms(x_ref[...], g_ref[0:1, :]).astype(BF16)
    z = _dot(h, w_ref[...])
    q_ref[...] = (z[:, 0:1024] * (DH_C ** -0.5)).astype(BF16)
    k2_ref[...] = z[:, 1024:1280].astype(BF16)
    v2_ref[...] = z[:, 1280:1536].astype(BF16)


def _proj_odd(x2, g, w_in):
    t = x2.shape[0]
    tm = min(TM_PROJ, t)
    k0, k1 = w_in[:, 1024:1088], w_in[:, 1088:1152]
    v0, v1 = w_in[:, 1152:1216], w_in[:, 1216:1280]
    w = jnp.concatenate([w_in[:, 0:1024], k0, k0, k1, k1, v0, v0, v1, v1], axis=1).astype(BF16)
    row = lambda i: (i, 0)
    full2 = lambda i: (0, 0)
    return pl.pallas_call(
        _proj_odd_kernel,
        out_shape=[jax.ShapeDtypeStruct((t, 1024), BF16),
                   jax.ShapeDtypeStruct((t, 256), BF16),
                   jax.ShapeDtypeStruct((t, 256), BF16)],
        grid=(t // tm,),
        in_specs=[pl.BlockSpec((tm, D_MODEL), row),
                  pl.BlockSpec((6, D_MODEL), full2),
                  pl.BlockSpec((D_MODEL, 1536), full2)],
        out_specs=[pl.BlockSpec((tm, 1024), row), pl.BlockSpec((tm, 256), row), pl.BlockSpec((tm, 256), row)],
        compiler_params=pltpu.CompilerParams(vmem_limit_bytes=VMEM_LIMIT),
        name="proj_odd",
    )(x2, g, w)


def _bias_row(bias_ref, h, i, j):
    parts = []
    for c in range(SUB):
        d = jnp.clip(i - (j * SUB + c), 0, N_SAT)
        parts.append(bias_ref[h * N_BT + d])
    return jnp.concatenate(parts, axis=1)


def _fold_lanes(x, op):
    acc = x[:, 0:LANES]
    for c in range(1, x.shape[1] // LANES):
        acc = op(acc, x[:, c * LANES:(c + 1) * LANES])
    return acc


def _dsa_kernel(qi_ref, wi_ref, ql_ref, k2t_ref, ct_ref, c_ref, bias_ref, wuv_ref, o_ref,
                qz_ref, sc_ref, p_ref, m_ref, l_ref, acc_ref, *, topk):
    i = pl.program_id(1)
    t0 = i * TQ
    nt = (t0 + TQ + TK - 1) // TK
    kf = float(topk)

    for h in range(H_IDX):
        pair = qi_ref[:, (h // 2) * LANES:(h // 2 + 1) * LANES]
        qz_ref[h * TQ:(h + 1) * TQ, :] = jnp.where(_lane_half_mask(TQ, h % 2), pair, jnp.zeros_like(pair))

    row = lax.broadcasted_iota(jnp.int32, (TQ, 1), 0) + t0
    limit = (row // CHUNK + 1) * CHUNK
    small = limit <= topk
    w = wi_ref[...]

    def score_body(j, carry):
        rmax, rmin = carry
        z = _dot(qz_ref[...], k2t_ref[j])
        sc = jnp.zeros((TQ, TK), F32)
        for h in range(H_IDX):
            sc = sc + jnp.maximum(z[h * TQ:(h + 1) * TQ, :], 0.0) * w[:, h:h + 1]
        kpos = lax.broadcasted_iota(jnp.int32, (TQ, TK), 1) + j * TK
        adm = kpos < limit
        sc_ref[j] = jnp.where(adm, sc, -jnp.inf)
        rmax = jnp.maximum(rmax, _fold_lanes(jnp.where(adm, sc, -jnp.inf), jnp.maximum))
        rmin = jnp.minimum(rmin, _fold_lanes(jnp.where(adm, sc, jnp.inf), jnp.minimum))
        return rmax, rmin

    rmax, rmin = lax.fori_loop(
        0, nt, score_body,
        (jnp.full((TQ, LANES), -jnp.inf, F32), jnp.full((TQ, LANES), jnp.inf, F32)))
    hi0 = jnp.max(rmax, axis=1, keepdims=True)
    lo0 = jnp.min(rmin, axis=1, keepdims=True)

    def count_ge(thr):
        def body(j, acc):
            return acc + _fold_lanes(jnp.where(sc_ref[j] >= thr, 1.0, 0.0), jnp.add)
        return jnp.sum(lax.fori_loop(0, nt, body, jnp.zeros((TQ, LANES), F32)), axis=1, keepdims=True)

    def count_gt(thr):
        def body(j, acc):
            return acc + _fold_lanes(jnp.where(sc_ref[j] > thr, 1.0, 0.0), jnp.add)
        return jnp.sum(lax.fori_loop(0, nt, body, jnp.zeros((TQ, LANES), F32)), axis=1, keepdims=True)

    def max_le(thr):
        def body(j, acc):
            t = sc_ref[j]
            return jnp.maximum(acc, _fold_lanes(jnp.where(t <= thr, t, -jnp.inf), jnp.maximum))
        return jnp.max(lax.fori_loop(0, nt, body, jnp.full((TQ, LANES), -jnp.inf, F32)), axis=1, keepdims=True)

    def max_lt(thr):
        def body(j, acc):
            t = sc_ref[j]
            return jnp.maximum(acc, _fold_lanes(jnp.where(t < thr, t, -jnp.inf), jnp.maximum))
        return jnp.max(lax.fori_loop(0, nt, body, jnp.full((TQ, LANES), -jnp.inf, F32)), axis=1, keepdims=True)

    def bisect_body(_, carry):
        lo, hi = carry
        mid = lo * 0.5 + hi * 0.5
        ge = count_ge(mid) >= kf
        return jnp.where(ge, mid, lo), jnp.where(ge, hi, mid)

    lo, hi = lax.fori_loop(0, N_BISECT, bisect_body, (lo0, hi0))

    v0 = max_le(hi)
    c0 = count_ge(v0)

    def walk_cond(carry):
        _, cnt, it = carry
        pending = jnp.where((cnt < kf) & jnp.logical_not(small), 1.0, 0.0)
        return (jnp.max(pending) > 0.0) & (it < topk + 2)

    def walk_body(carry):
        v, cnt, it = carry
        move = (cnt < kf) & jnp.logical_not(small)
        v = jnp.where(move, max_lt(v), v)
        return v, count_ge(v), it + 1

    tau, cnt, _ = lax.while_loop(walk_cond, walk_body, (v0, c0, jnp.int32(0)))
    tau = jnp.where(small, -jnp.finfo(F32).max, tau)

    tied = (cnt > kf) & jnp.logical_not(small)

    @pl.when(jnp.max(jnp.where(tied, 1.0, 0.0)) > 0.0)
    def _():
        need = kf - count_gt(tau)

        def count_eq_le(pos):
            def body(j, acc):
                kpos = (lax.broadcasted_iota(jnp.int32, (TQ, TK), 1) + j * TK).astype(F32)
                hit = (sc_ref[j] == tau) & (kpos <= pos)
                return acc + _fold_lanes(jnp.where(hit, 1.0, 0.0), jnp.add)
            return jnp.sum(lax.fori_loop(0, nt, body, jnp.zeros((TQ, LANES), F32)), axis=1, keepdims=True)

        def pos_body(_, carry):
            lo_p, hi_p = carry
            mid = jnp.floor((lo_p + hi_p) * 0.5)
            ok = count_eq_le(mid) >= need
            return jnp.where(ok, lo_p, mid), jnp.where(ok, mid, hi_p)

        n_steps = int(math.ceil(math.log2(sc_ref.shape[0] * TK))) + 1
        _, cut = lax.fori_loop(0, n_steps, pos_body,
                               (jnp.full((TQ, 1), -1.0, F32), jnp.full((TQ, 1), float(sc_ref.shape[0] * TK), F32)))

        def drop_body(j, _):
            kpos = (lax.broadcasted_iota(jnp.int32, (TQ, TK), 1) + j * TK).astype(F32)
            t = sc_ref[j]
            sc_ref[j] = jnp.where(tied & (t == tau) & (kpos > cut), -jnp.inf, t)
            return 0

        lax.fori_loop(0, nt, drop_body, 0)

    m_ref[...] = jnp.full(m_ref.shape, NEG_INF, F32)
    l_ref[...] = jnp.zeros(l_ref.shape, F32)
    acc_ref[...] = jnp.zeros(acc_ref.shape, F32)
    ql = ql_ref[...].reshape(H_A * TQ, D_LAT)

    def attn_body(j, _):
        s_all = _dot(ql, ct_ref[j])
        sel = sc_ref[j] >= tau
        for h in range(H_A):
            rows = slice(h * TQ, (h + 1) * TQ)
            s = s_all[rows, :] + _bias_row(bias_ref, h, i, j)
            s = jnp.where(sel, s, NEG_INF)
            m_old = m_ref[rows, :]
            m_new = jnp.maximum(m_old, jnp.max(s, axis=1, keepdims=True))
            p = jnp.exp(s - m_new)
            alpha = jnp.exp(m_old - m_new)
            l_ref[rows, :] = alpha * l_ref[rows, :] + jnp.sum(p, axis=1, keepdims=True)
            m_ref[rows, :] = m_new
            acc_ref[rows, :] = acc_ref[rows, :] * alpha
            p_ref[rows, :] = p.astype(BF16)
        acc_ref[...] += _dot(p_ref[...], c_ref[j])
        return 0

    lax.fori_loop(0, nt, attn_body, 0)

    for pr in range(H_A // 2):
        out = jnp.zeros((TQ, LANES), F32)
        for h in (2 * pr, 2 * pr + 1):
            rows = slice(h * TQ, (h + 1) * TQ)
            o = (acc_ref[rows, :] / l_ref[rows, :]).astype(BF16)
            out = out + _dot(o, wuv_ref[h])
        o_ref[:, pr * LANES:(pr + 1) * LANES] = out.astype(BF16)


def _dsa(qi, wi, qlat, k2, ckv, bias_a, w_uv, b, s):
    nq, nt = s // TQ, s // TK
    topk = min(TOPK_MAX, s // 4)
    k2t = jnp.swapaxes(k2.reshape(b, nt, TK, LANES), 2, 3)
    ct = jnp.swapaxes(ckv.reshape(b, nt, TK, D_LAT), 2, 3)
    c4 = ckv.reshape(b, nt, TK, D_LAT)
    z = jnp.zeros_like(w_uv)
    wuv_pad = jnp.where((jnp.arange(H_A) % 2 == 0)[:, None, None],
                        jnp.concatenate([w_uv, z], axis=2), jnp.concatenate([z, w_uv], axis=2)).astype(BF16)
    qrow = lambda bb, i: (bb * nq + i, 0)
    kv4 = lambda bb, i: (bb, 0, 0, 0)
    return pl.pallas_call(
        functools.partial(_dsa_kernel, topk=topk),
        out_shape=jax.ShapeDtypeStruct((b * s, H_A * DH_A), BF16),
        grid=(b, nq),
        in_specs=[
            pl.BlockSpec((TQ, 512), qrow),
            pl.BlockSpec((TQ, H_IDX), qrow),
            pl.BlockSpec((H_A, TQ, D_LAT), lambda bb, i: (0, bb * nq + i, 0)),
            pl.BlockSpec((None, nt, LANES, TK), kv4),
            pl.BlockSpec((None, nt, D_LAT, TK), kv4),
            pl.BlockSpec((None, nt, TK, D_LAT), kv4),
            pl.BlockSpec((H_A * N_BT, LANES, LANES), lambda bb, i: (0, 0, 0)),
            pl.BlockSpec((H_A, D_LAT, LANES), lambda bb, i: (0, 0, 0)),
        ],
        out_specs=pl.BlockSpec((TQ, H_A * DH_A), qrow),
        scratch_shapes=[
            pltpu.VMEM((H_IDX * TQ, LANES), BF16),
            pltpu.VMEM((nt, TQ, TK), F32),
            pltpu.VMEM((H_A * TQ, TK), BF16),
            pltpu.VMEM((H_A * TQ, 1), F32),
            pltpu.VMEM((H_A * TQ, 1), F32),
            pltpu.VMEM((H_A * TQ, D_LAT), F32),
        ],
        compiler_params=pltpu.CompilerParams(vmem_limit_bytes=VMEM_LIMIT),
        name="dsa_attention",
    )(qi, wi, qlat, k2t, ct, c4, bias_a, wuv_pad)


def _diff_kernel(q_ref, kt_ref, v_ref, bias_ref, lam_ref, sub_ref, o_ref,
                 p_ref, m_ref, l_ref, acc_ref, *, lam_init):
    h = pl.program_id(1)
    i = pl.program_id(2)
    t0 = i * TQ
    nt = (t0 + TQ + TK - 1) // TK
    q = q_ref[...]
    q2 = jnp.concatenate([jnp.where(_lane_half_mask(TQ, 0), q, jnp.zeros_like(q)),
                          jnp.where(_lane_half_mask(TQ, 1), q, jnp.zeros_like(q))], axis=0)
    row = lax.broadcasted_iota(jnp.int32, (TQ, 1), 0) + t0
    limit = (row // CHUNK + 1) * CHUNK
    m_ref[...] = jnp.full(m_ref.shape, NEG_INF, F32)
    l_ref[...] = jnp.zeros(l_ref.shape, F32)
    acc_ref[...] = jnp.zeros(acc_ref.shape, F32)

    def body(j, _):
        s2 = _dot(q2, kt_ref[j])
        kpos = lax.broadcasted_iota(jnp.int32, (TQ, TK), 1) + j * TK
        adm = kpos < limit
        bias = _bias_row(bias_ref, h, i, j)
        for mp in range(2):
            rows = slice(mp * TQ, (mp + 1) * TQ)
            s = jnp.where(adm, s2[rows, :] + bias, NEG_INF)
            m_old = m_ref[rows, :]
            m_new = jnp.maximum(m_old, jnp.max(s, axis=1, keepdims=True))
            p = jnp.exp(s - m_new)
            alpha = jnp.exp(m_old - m_new)
            l_ref[rows, :] = alpha * l_ref[rows, :] + jnp.sum(p, axis=1, keepdims=True)
            m_ref[rows, :] = m_new
            acc_ref[rows, :] = acc_ref[rows, :] * alpha
            p_ref[rows, :] = p.astype(BF16)
        acc_ref[...] += _dot(p_ref[...], v_ref[j])
        return 0

    lax.fori_loop(0, nt, body, 0)

    lf = lam_ref[...]
    lam = (jnp.exp(jnp.sum(lf[0:1, :] * lf[1:2, :], axis=1, keepdims=True))
           - jnp.exp(jnp.sum(lf[2:3, :] * lf[3:4, :], axis=1, keepdims=True)) + lam_init)
    o = acc_ref[0:TQ, :] / l_ref[0:TQ, :] - lam * (acc_ref[TQ:2 * TQ, :] / l_ref[TQ:2 * TQ, :])
    o_ref[...] = (_rms(o, sub_ref[...]) * (1.0 - lam_init)).astype(BF16)


def _diff(qb, kb, vb, bias_b, b_lambda, b_subln, lam_init, b, s):
    nq, nt = s // TQ, s // TK
    kt = jnp.transpose(kb.reshape(b, nt, TK, H_B, 2 * DH_B), (0, 3, 1, 4, 2))
    v4 = vb.reshape(b, nt, TK, H_B * 2 * DH_B)
    return pl.pallas_call(
        functools.partial(_diff_kernel, lam_init=lam_init),
        out_shape=jax.ShapeDtypeStruct((b * s, H_B * 2 * DH_B), BF16),
        grid=(b, H_B, nq),
        in_specs=[
            pl.BlockSpec((TQ, LANES), lambda bb, h, i: (bb * nq + i, h)),
            pl.BlockSpec((None, None, nt, LANES, TK), lambda bb, h, i: (bb, h, 0, 0, 0)),
            pl.BlockSpec((None, nt, TK, LANES), lambda bb, h, i: (bb, 0, 0, h)),
            pl.BlockSpec((H_B * N_BT, LANES, LANES), lambda bb, h, i: (0, 0, 0)),
            pl.BlockSpec((4, DH_B), lambda bb, h, i: (0, 0)),
            pl.BlockSpec((1, 2 * DH_B), lambda bb, h, i: (0, 0)),
        ],
        out_specs=pl.BlockSpec((TQ, LANES), lambda bb, h, i: (bb * nq + i, h)),
        scratch_shapes=[
            pltpu.VMEM((2 * TQ, TK), BF16),
            pltpu.VMEM((2 * TQ, 1), F32),
            pltpu.VMEM((2 * TQ, 1), F32),
            pltpu.VMEM((2 * TQ, LANES), F32),
        ],
        compiler_params=pltpu.CompilerParams(vmem_limit_bytes=VMEM_LIMIT),
        name="diff_attention",
    )(qb, kt, v4, bias_b, b_lambda.astype(F32), b_subln.reshape(1, 2 * DH_B).astype(F32))


def _swa_kernel(q_ref, ka_ref, kb_ref, va_ref, vb_ref, bias_ref, sink_ref, o_ref):
    i = pl.program_id(1)
    rblk = lax.broadcasted_iota(jnp.int32, (TQ, 2 * TQ), 0) // CHUNK + WINDOW // CHUNK
    col = lax.broadcasted_iota(jnp.int32, (TQ, 2 * TQ), 1)
    cblk = col // CHUNK
    valid = (cblk <= rblk) & (cblk >= rblk - WINDOW // CHUNK) & ((col >= TQ) | (i > 0))
    sinks = sink_ref[...]
    for pr in range(H_C // 2):
        kv = (2 * pr) // G_C
        qp = q_ref[:, pr * LANES:(pr + 1) * LANES]
        out = jnp.zeros((TQ, LANES), F32)
        for half in range(2):
            h = 2 * pr + half
            hm = _lane_half_mask(TQ, half)
            qz = jnp.where(hm, qp, jnp.zeros_like(qp))
            s = jnp.concatenate([_dot(qz, ka_ref[kv]), _dot(qz, kb_ref[kv])], axis=1) + bias_ref[h]
            s = jnp.where(valid, s, NEG_INF)
            sink = sinks[:, h:h + 1]
            m = jnp.maximum(jnp.max(s, axis=1, keepdims=True), sink)
            e = jnp.exp(s - m)
            den = jnp.sum(e, axis=1, keepdims=True) + jnp.exp(sink - m)
            p = (e / den).astype(BF16)
            va = va_ref[:, kv * LANES:(kv + 1) * LANES]
            vb = vb_ref[:, kv * LANES:(kv + 1) * LANES]
            out = out + _dot(p[:, 0:TQ], jnp.where(hm, va, jnp.zeros_like(va)))
            out = out + _dot(p[:, TQ:2 * TQ], jnp.where(hm, vb, jnp.zeros_like(vb)))
        o_ref[:, pr * LANES:(pr + 1) * LANES] = out.astype(BF16)


def _swa(q, k2, v2, table_c, sinks, b, s):
    nq = s // TQ
    r = np.arange(TQ)
    rel = (np.arange(2 * TQ)[None, :] - TQ) - r[:, None]
    bias = jnp.moveaxis(jnp.take(table_c.astype(F32), jnp.asarray(_t5_bucket_np(rel)), axis=0), -1, 0)
    k2p = jnp.pad(k2.reshape(b, s, 2 * LANES), ((0, 0), (TQ, 0), (0, 0)))
    v2p = jnp.pad(v2.reshape(b, s, 2 * LANES), ((0, 0), (TQ, 0), (0, 0)))
    kt = jnp.transpose(k2p.reshape(b, s + TQ, H_C_KV, LANES), (0, 2, 3, 1))
    return pl.pallas_call(
        _swa_kernel,
        out_shape=jax.ShapeDtypeStruct((b * s, H_C * DH_C), BF16),
        grid=(b, nq),
        in_specs=[
            pl.BlockSpec((TQ, H_C * DH_C), lambda bb, i: (bb * nq + i, 0)),
            pl.BlockSpec((None, H_C_KV, LANES, TQ), lambda bb, i: (bb, 0, 0, i)),
            pl.BlockSpec((None, H_C_KV, LANES, TQ), lambda bb, i: (bb, 0, 0, i + 1)),
            pl.BlockSpec((None, TQ, 2 * LANES), lambda bb, i: (bb, i, 0)),
            pl.BlockSpec((None, TQ, 2 * LANES), lambda bb, i: (bb, i + 1, 0)),
            pl.BlockSpec((H_C, TQ, 2 * TQ), lambda bb, i: (0, 0, 0)),
            pl.BlockSpec((1, H_C), lambda bb, i: (0, 0)),
        ],
        out_specs=pl.BlockSpec((TQ, H_C * DH_C), lambda bb, i: (bb * nq + i, 0)),
        compiler_params=pltpu.CompilerParams(vmem_limit_bytes=VMEM_LIMIT),
        name="swa_attention",
    )(q, kt, kt, v2p, v2p, bias, sinks.reshape(1, H_C).astype(F32))


def _memkv_kernel(mem_ref, g_ref, w_ref, k_ref, v_ref):
    h = _rms(mem_ref[...], g_ref[...]).astype(BF16)
    kv = _dot(h, w_ref[...])
    k_ref[...] = kv[:, 0:H_X * DH_X].astype(BF16)
    v_ref[...] = kv[:, H_X * DH_X:].astype(BF16)


def _memkv(mem, g, wkv):
    b, m, _ = mem.shape
    n = H_X * DH_X
    return pl.pallas_call(
        _memkv_kernel,
        out_shape=[jax.ShapeDtypeStruct((b, m, n), BF16), jax.ShapeDtypeStruct((b, m, n), BF16)],
        grid=(b,),
        in_specs=[pl.BlockSpec((None, m, D_MODEL), lambda i: (i, 0, 0)),
                  pl.BlockSpec((1, D_MODEL), lambda i: (0, 0)),
                  pl.BlockSpec((D_MODEL, 2 * n), lambda i: (0, 0))],
        out_specs=[pl.BlockSpec((None, m, n), lambda i: (i, 0, 0)), pl.BlockSpec((None, m, n), lambda i: (i, 0, 0))],
        compiler_params=pltpu.CompilerParams(vmem_limit_bytes=VMEM_LIMIT),
        name="mem_kv",
    )(mem, g.reshape(1, D_MODEL), wkv.astype(BF16))


def _tail_kernel(*refs, n_a):
    x_ref, g_ref = refs[0], refs[1]
    a_refs = refs[2:2 + n_a]
    wout_refs = refs[2 + n_a:2 + 2 * n_a]
    wq_ref, kt_ref, v_ref, wo_ref, w1_ref, w2_ref, o_ref = refs[2 + 2 * n_a:]
    tm = x_ref.shape[0]
    g = g_ref[...]
    x = x_ref[...]

    y = _dot(a_refs[0][...], wout_refs[0][...])
    for a_ref, w_ref in zip(a_refs[1:], wout_refs[1:]):
        y = y + _dot(a_ref[...], w_ref[...])
    x = x + _rms(y, g[1:2, :])

    hq = _rms(x, g[2:3, :]).astype(BF16)
    q = (_dot(hq, wq_ref[...]) * (DH_X ** -0.5)).astype(BF16)
    pairs = []
    for pr in range(H_X // 2):
        qp = q[:, pr * LANES:(pr + 1) * LANES]
        vp = v_ref[:, pr * LANES:(pr + 1) * LANES]
        out = jnp.zeros((tm, LANES), F32)
        for half in range(2):
            qz = jnp.where(_lane_half_mask(tm, half), qp, jnp.zeros_like(qp))
            s = _dot(qz, kt_ref[pr * LANES:(pr + 1) * LANES, :])
            m = jnp.max(s, axis=1, keepdims=True)
            e = jnp.exp(s - m)
            p = (e / jnp.sum(e, axis=1, keepdims=True)).astype(BF16)
            out = out + _dot(p, jnp.where(_lane_half_mask(vp.shape[0], half), vp, jnp.zeros_like(vp)))
        pairs.append(out.astype(BF16))
    o = jnp.concatenate(pairs, axis=1)
    x = x + _rms(_dot(o, wo_ref[...]), g[3:4, :])

    hm = _rms(x, g[4:5, :]).astype(BF16)
    y = jnp.zeros((tm, D_MODEL), F32)
    for c in range(D_FF // FF_CHUNK):
        a = jnp.maximum(_dot(hm, w1_ref[:, c * FF_CHUNK:(c + 1) * FF_CHUNK]), 0.0)
        y = y + _dot((a * a).astype(BF16), w2_ref[c * FF_CHUNK:(c + 1) * FF_CHUNK, :])
    o_ref[...] = x + _rms(y, g[5:6, :])


def _tail(x2, g, a_list, wout_list, wq, kt, v, wo, w1, w2, b, s):
    t = x2.shape[0]
    tm = min(TM_TAIL, s)
    per_b = s // tm
    n_a = len(a_list)
    row = lambda i: (i, 0)
    full2 = lambda i: (0, 0)
    once = dict(pipeline_mode=pl.Buffered(1))
    in_specs = [pl.BlockSpec((tm, D_MODEL), row), pl.BlockSpec((6, D_MODEL), full2)]
    in_specs += [pl.BlockSpec((tm, a.shape[1]), row) for a in a_list]
    in_specs += [pl.BlockSpec(w.shape, full2, **once) for w in wout_list]
    in_specs += [
        pl.BlockSpec(wq.shape, full2, **once),
        pl.BlockSpec((None,) + kt.shape[1:], lambda i: (i // per_b, 0, 0)),
        pl.BlockSpec((None,) + v.shape[1:], lambda i: (i // per_b, 0, 0)),
        pl.BlockSpec(wo.shape, full2, **once),
        pl.BlockSpec(w1.shape, full2, **once),
        pl.BlockSpec(w2.shape, full2, **once),
    ]
    return pl.pallas_call(
        functools.partial(_tail_kernel, n_a=n_a),
        out_shape=jax.ShapeDtypeStruct((t, D_MODEL), F32),
        grid=(t // tm,),
        in_specs=in_specs,
        out_specs=pl.BlockSpec((tm, D_MODEL), row),
        compiler_params=pltpu.CompilerParams(vmem_limit_bytes=VMEM_LIMIT),
        name="tail",
    )(x2, g, *a_list, *wout_list, wq, kt, v, wo, w1, w2)


def kernel(x, mem, rel_bias_table, norm_g, ev_w_in, ev_a_kv_norm, ev_a_w_uk, ev_a_w_uv, ev_b_lambda, ev_b_subln, ev_w_out, od_w_in, od_sinks, od_w_out, xa_wq, xa_wkv, xa_wo, xa_mem_norm, mlp_w1, mlp_w2):
    b, s, d = x.shape
    depth = norm_g.shape[0]
    assert d == D_MODEL and s % TK == 0 and TK % TQ == 0
    x2 = x.reshape(b * s, d)
    bias_a = _causal_bias_tiles(rel_bias_table[:, :H_A])
    bias_b = _causal_bias_tiles(rel_bias_table[:, H_A:H_A + H_B])
    table_c = rel_bias_table[:, H_A + H_B:]
    for l in range(depth):
        g = norm_g[l].astype(F32)
        if l % 2 == 0:
            e = l // 2
            lam_init = 0.8 - 0.6 * math.exp(-0.3 * l)
            qlat, ckv, qi, qb, kb, vb, k2, wi = _proj_even(x2, g, ev_w_in[e], ev_a_kv_norm[e], ev_a_w_uk[e])
            oa = _dsa(qi, wi, qlat, k2, ckv, bias_a, ev_a_w_uv[e], b, s)
            ob = _diff(qb, kb, vb, bias_b, ev_b_lambda[e], ev_b_subln[e], lam_init, b, s)
            n_a = H_A * DH_A
            a_list = [oa, ob]
            wout_list = [ev_w_out[e][:n_a].astype(BF16), ev_w_out[e][n_a:].astype(BF16)]
        else:
            o = l // 2
            q, k2, v2 = _proj_odd(x2, g, od_w_in[o])
            a_list = [_swa(q, k2, v2, table_c, od_sinks[o], b, s)]
            wout_list = [od_w_out[o].astype(BF16)]
        mk, mv = _memkv(mem, xa_mem_norm[l], xa_wkv[l])
        x2 = _tail(x2, g, a_list, wout_list, xa_wq[l].astype(BF16), jnp.swapaxes(mk, 1, 2), mv,
                   xa_wo[l].astype(BF16), mlp_w1[l].astype(BF16), mlp_w2[l].astype(BF16), b, s)
    return x2.reshape(b, s, d)
```

```python
import functools
import math

import numpy as np
import jax
import jax.numpy as jnp
from jax import lax
from jax.experimental import pallas as pl
from jax.experimental.pallas import tpu as pltpu

F32 = jnp.float32
BF16 = jnp.bfloat16

D_MODEL = 1024
CHUNK = 64
EPS = 1e-6
NEG_INF = -1e30
LOG2E = math.log2(math.e)

H_A, DH_A, D_LAT = 8, 64, 128
H_IDX, DH_IDX = 8, 64
TOPK_MAX = 256
H_B, DH_B = 4, 64
H_C, H_C_KV, DH_C = 16, 2, 64
G_C = H_C // H_C_KV
WINDOW = 128
H_X, DH_X = 4, 64
D_FF = 4 * D_MODEL
N_BUCKETS = 32
MAX_DIST = 1024

LANES = 128
FOLD_ROWS = 64
TQ = 128
TQ_DIFF = 256
TK = 512
SUB = TK // LANES
ONES_ROWS = 16
TM_PROJ = 512
TM_TAIL = 256
FF_CHUNK = 1024
VMEM_LIMIT = 56 * 1024 * 1024
N_BISECT = 14


def _rms(x, g):
    return x * lax.rsqrt(jnp.mean(x * x, axis=-1, keepdims=True) + EPS) * g


def _dot(a, b):
    return jnp.dot(a, b, preferred_element_type=F32)


def _t5_bucket_np(rel):
    nb = N_BUCKETS // 2
    max_exact = nb // 2
    n = np.abs(rel)
    nf = np.maximum(n, 1).astype(np.float32)
    large = max_exact + (np.log(nf / max_exact) / math.log(MAX_DIST / max_exact) * (nb - max_exact)).astype(np.int32)
    large = np.minimum(large, nb - 1)
    return np.where(rel > 0, nb, 0) + np.where(n < max_exact, n, large)


def _sat_blocks():
    d = 1
    while _t5_bucket_np(np.array([-(d * LANES - (LANES - 1))]))[0] != N_BUCKETS // 2 - 1:
        d += 1
    return d


N_SAT = _sat_blocks()
N_BT = N_SAT + 1


def _toeplitz(u, rows, cols):
    length = u.shape[-1]
    flat = jnp.tile(u, (1,) * (u.ndim - 1) + (rows,))[..., :rows * (length - 1)]
    return flat.reshape(u.shape[:-1] + (rows, length - 1))[..., :cols]


def _causal_bias_tiles(table_cols):
    length = 2 * LANES + 1
    m = np.arange(length)
    diff = np.where(m < LANES + 1, m, m - length)
    rel = -diff[None, :] - (np.arange(N_BT) * LANES)[:, None]
    tab = table_cols.astype(F32)
    u = jnp.take(tab, jnp.asarray(_t5_bucket_np(rel)), axis=0)
    u = (u - tab[N_BUCKETS // 2 - 1][None, None, :]) * LOG2E
    tiles = _toeplitz(jnp.moveaxis(u, -1, 0), LANES, LANES)
    return tiles.reshape(table_cols.shape[1] * N_BT, LANES, LANES)


def _fold_rows(x, op):
    acc = x[0:FOLD_ROWS, :]
    for r in range(1, x.shape[0] // FOLD_ROWS):
        acc = op(acc, x[r * FOLD_ROWS:(r + 1) * FOLD_ROWS, :])
    return acc


def _bias_tile(bias_ref, h, qb0, n_qb, j):
    rows = []
    for c in range(SUB):
        cols = [bias_ref[h * N_BT + jnp.clip(qb0 + r - (j * SUB + c), 0, N_SAT)] for r in range(n_qb)]
        rows.append(cols[0] if n_qb == 1 else jnp.concatenate(cols, axis=1))
    return jnp.concatenate(rows, axis=0)


def _proj_even_kernel(x_ref, g_ref, w_ref, kvn_ref, wuk_ref,
                      qlat_ref, ckv_ref, qi_ref, qb_ref, kb_ref, vb_ref, ki_ref, wi_ref):
    h = _rms(x_ref[...], g_ref[0:1, :]).astype(BF16)
    z = _dot(h, w_ref[...])
    qa = z[:, 0:512].astype(BF16)
    for hh in range(H_A):
        ql = _dot(qa[:, hh * DH_A:(hh + 1) * DH_A], wuk_ref[hh]) * (DH_A ** -0.5 * LOG2E)
        qlat_ref[:, hh * D_LAT:(hh + 1) * D_LAT] = ql.astype(BF16)
    ckv_ref[...] = _rms(z[:, 512:640], kvn_ref[...]).astype(BF16)
    qi_ref[...] = z[:, 640:1152].astype(BF16)
    qb_ref[...] = (z[:, 1152:1664] * (DH_B ** -0.5 * LOG2E)).astype(BF16)
    kb_ref[...] = z[:, 1664:2176].astype(BF16)
    vb_ref[...] = z[:, 2176:2688].astype(BF16)
    ki_ref[...] = z[:, 2688:2688 + DH_IDX].astype(BF16)
    wi_ref[...] = z[:, 2816:2816 + H_IDX]


def _proj_even(x2, g, w_in, kvn, w_uk):
    t = x2.shape[0]
    tm = min(TM_PROJ, t)
    pad = jnp.zeros((D_MODEL, LANES - DH_IDX), w_in.dtype)
    cols = [w_in[:, 0:512], w_in[:, 512:640], w_in[:, 640:1152], w_in[:, 1224:1736], w_in[:, 1736:2248],
            w_in[:, 2248:2760], w_in[:, 1152:1216], pad, w_in[:, 1216:1224],
            jnp.zeros((D_MODEL, LANES - H_IDX), w_in.dtype)]
    w = jnp.concatenate(cols, axis=1).astype(BF16)
    n = w.shape[1]
    row = lambda i: (i, 0)
    full2 = lambda i: (0, 0)
    widths = [(H_A * D_LAT, BF16), (D_LAT, BF16), (512, BF16), (512, BF16), (512, BF16), (512, BF16),
              (DH_IDX, BF16), (H_IDX, F32)]
    return pl.pallas_call(
        _proj_even_kernel,
        out_shape=[jax.ShapeDtypeStruct((t, c), dt) for c, dt in widths],
        grid=(t // tm,),
        in_specs=[
            pl.BlockSpec((tm, D_MODEL), row),
            pl.BlockSpec((6, D_MODEL), full2),
            pl.BlockSpec((D_MODEL, n), full2),
            pl.BlockSpec((1, D_LAT), full2),
            pl.BlockSpec((H_A, DH_A, D_LAT), lambda i: (0, 0, 0)),
        ],
        out_specs=[pl.BlockSpec((tm, c), row) for c, _ in widths],
        compiler_params=pltpu.CompilerParams(vmem_limit_bytes=VMEM_LIMIT),
        name="proj_even",
    )(x2, g, w, kvn.reshape(1, D_LAT), w_uk.astype(BF16))


def _proj_odd_kernel(x_ref, g_ref, w_ref, q_ref, k_ref, v_ref):
    h = _rms(x_ref[...], g_ref[0:1, :]).astype(BF16)
    z = _dot(h, w_ref[...])
    q_ref[...] = (z[:, 0:1024] * (DH_C ** -0.5 * LOG2E)).astype(BF16)
    k_ref[...] = z[:, 1024:1152].astype(BF16)
    v_ref[...] = z[:, 1152:1280].astype(BF16)


def _proj_odd(x2, g, w_in):
    t = x2.shape[0]
    tm = min(TM_PROJ, t)
    row = lambda i: (i, 0)
    full2 = lambda i: (0, 0)
    return pl.pallas_call(
        _proj_odd_kernel,
        out_shape=[jax.ShapeDtypeStruct((t, 1024), BF16),
                   jax.ShapeDtypeStruct((t, 128), BF16),
                   jax.ShapeDtypeStruct((t, 128), BF16)],
        grid=(t // tm,),
        in_specs=[pl.BlockSpec((tm, D_MODEL), row),
                  pl.BlockSpec((6, D_MODEL), full2),
                  pl.BlockSpec((D_MODEL, 1280), full2)],
        out_specs=[pl.BlockSpec((tm, 1024), row), pl.BlockSpec((tm, 128), row), pl.BlockSpec((tm, 128), row)],
        compiler_params=pltpu.CompilerParams(vmem_limit_bytes=VMEM_LIMIT),
        name="proj_odd",
    )(x2, g, w_in.astype(BF16))


def _dsa_kernel(qit_ref, wit_ref, qlt_ref, k_ref, c_ref, ctx_ref, bias_ref, wuvt_ref, o_ref,
                sc_ref, p_ref, acc_ref, *, topk):
    i = pl.program_id(1)
    t0 = i * TQ
    nt = (t0 + TQ + TK - 1) // TK
    n_far = jnp.maximum(i - N_SAT + 1, 0) // SUB
    kf = float(topk)

    qpos = lax.broadcasted_iota(jnp.int32, (1, TQ), 1) + t0
    limit = (qpos // CHUNK + 1) * CHUNK
    small = limit <= topk
    big = jnp.logical_not(small)
    w = wit_ref[...]
    qit = qit_ref[...]

    def score_body(j, carry):
        rmax, rmin = carry
        z = _dot(k_ref[j], qit)
        sc = jnp.maximum(z[:, 0:TQ], 0.0) * w[0:1, :]
        for h in range(1, H_IDX):
            sc = sc + jnp.maximum(z[:, h * TQ:(h + 1) * TQ], 0.0) * w[h:h + 1, :]
        kpos = lax.broadcasted_iota(jnp.int32, (TK, TQ), 0) + j * TK
        adm = kpos < limit
        sc_ref[j] = jnp.where(adm, sc, -jnp.inf)
        rmax = jnp.maximum(rmax, _fold_rows(jnp.where(adm, sc, -jnp.inf), jnp.maximum))
        rmin = jnp.minimum(rmin, _fold_rows(jnp.where(adm, sc, jnp.inf), jnp.minimum))
        return rmax, rmin

    rmax, rmin = lax.fori_loop(
        0, nt, score_body,
        (jnp.full((FOLD_ROWS, TQ), -jnp.inf, F32), jnp.full((FOLD_ROWS, TQ), jnp.inf, F32)))
    hi0 = jnp.max(rmax, axis=0, keepdims=True)
    lo0 = jnp.min(rmin, axis=0, keepdims=True)

    def col_reduce(tile_fn, op, init, final):
        def body(j, acc):
            return op(acc, _fold_rows(tile_fn(j), op))
        return final(lax.fori_loop(0, nt, body, jnp.full((FOLD_ROWS, TQ), init, F32)), axis=0, keepdims=True)

    def count_ge(thr):
        return col_reduce(lambda j: jnp.where(sc_ref[j] >= thr, 1.0, 0.0), jnp.add, 0.0, jnp.sum)

    def count_gt(thr):
        return col_reduce(lambda j: jnp.where(sc_ref[j] > thr, 1.0, 0.0), jnp.add, 0.0, jnp.sum)

    def max_le(thr):
        def tile(j):
            t = sc_ref[j]
            return jnp.where(t <= thr, t, -jnp.inf)
        return col_reduce(tile, jnp.maximum, -jnp.inf, jnp.max)

    def max_lt(thr):
        def tile(j):
            t = sc_ref[j]
            return jnp.where(t < thr, t, -jnp.inf)
        return col_reduce(tile, jnp.maximum, -jnp.inf, jnp.max)

    def bisect_body(_, carry):
        lo, hi = carry
        mid = lo * 0.5 + hi * 0.5
        ge = count_ge(mid) >= kf
        return jnp.where(ge, mid, lo), jnp.where(ge, hi, mid)

    lo, hi = lax.fori_loop(0, N_BISECT, bisect_body, (lo0, hi0))

    v0 = max_le(hi)
    c0 = count_ge(v0)

    def walk_cond(carry):
        _, cnt, it = carry
        pending = jnp.where((cnt < kf) & big, 1.0, 0.0)
        return (jnp.max(pending) > 0.0) & (it < topk + 2)

    def walk_body(carry):
        v, cnt, it = carry
        v = jnp.where((cnt < kf) & big, max_lt(v), v)
        return v, count_ge(v), it + 1

    tau, cnt, _ = lax.while_loop(walk_cond, walk_body, (v0, c0, jnp.int32(0)))
    tau = jnp.where(small, -jnp.finfo(F32).max, tau)

    tied = (cnt > kf) & big

    @pl.when(jnp.max(jnp.where(tied, 1.0, 0.0)) > 0.0)
    def _():
        need = kf - count_gt(tau)
        n_keys = sc_ref.shape[0] * TK

        def kpos_f(j):
            return (lax.broadcasted_iota(jnp.int32, (TK, TQ), 0) + j * TK).astype(F32)

        def count_eq_le(pos):
            return col_reduce(lambda j: jnp.where((sc_ref[j] == tau) & (kpos_f(j) <= pos), 1.0, 0.0),
                              jnp.add, 0.0, jnp.sum)

        def pos_body(_, carry):
            lo_p, hi_p = carry
            mid = jnp.floor((lo_p + hi_p) * 0.5)
            ok = count_eq_le(mid) >= need
            return jnp.where(ok, lo_p, mid), jnp.where(ok, mid, hi_p)

        n_steps = int(math.ceil(math.log2(n_keys))) + 1
        _, cut = lax.fori_loop(0, n_steps, pos_body,
                               (jnp.full((1, TQ), -1.0, F32), jnp.full((1, TQ), float(n_keys), F32)))

        def drop_body(j, _):
            t = sc_ref[j]
            sc_ref[j] = jnp.where(tied & (t == tau) & (kpos_f(j) > cut), -jnp.inf, t)
            return 0

        lax.fori_loop(0, nt, drop_body, 0)

    acc_ref[...] = jnp.zeros(acc_ref.shape, F32)
    qlt = qlt_ref[...]

    def attn_tile(j, m, with_bias):
        s_all = _dot(c_ref[j], qlt)
        sel = sc_ref[j] >= tau
        m_new, alpha = [], []
        for h in range(H_A):
            cols = slice(h * TQ, (h + 1) * TQ)
            s = s_all[:, cols]
            if with_bias:
                s = s + _bias_tile(bias_ref, h, i, 1, j)
            s = jnp.where(sel, s, NEG_INF)
            m_old = m[:, cols]
            m_h = jnp.maximum(m_old, jnp.max(s, axis=0, keepdims=True))
            p_ref[:, cols] = jnp.exp2(s - m_h).astype(BF16)
            alpha.append(jnp.exp2(m_old - m_h))
            m_new.append(m_h)
        acc_ref[...] = acc_ref[...] * jnp.concatenate(alpha, axis=1) + _dot(ctx_ref[j], p_ref[...])
        return jnp.concatenate(m_new, axis=1)

    m = jnp.full((1, H_A * TQ), NEG_INF, F32)
    m = lax.fori_loop(0, n_far, lambda j, mm: attn_tile(j, mm, False), m)
    lax.fori_loop(n_far, nt, lambda j, mm: attn_tile(j, mm, True), m)

    outs = []
    for h in range(H_A):
        cols = slice(h * TQ, (h + 1) * TQ)
        o_t = (acc_ref[0:D_LAT, cols] / acc_ref[D_LAT:D_LAT + 1, cols]).astype(BF16)
        outs.append(_dot(wuvt_ref[h], o_t))
    for pr in range(H_A // 2):
        pair = jnp.concatenate([outs[2 * pr], outs[2 * pr + 1]], axis=0)
        o_ref[:, pr * LANES:(pr + 1) * LANES] = pair.T.astype(BF16)


def _with_ones_rows(xt):
    return jnp.concatenate([xt, jnp.ones(xt.shape[:-2] + (ONES_ROWS, xt.shape[-1]), xt.dtype)], axis=-2)


def _dsa(qi, wi, qlat, ki, ckv, bias_a, w_uv, b, s):
    nq, nt = s // TQ, s // TK
    topk = min(TOPK_MAX, s // 4)
    qit = jnp.transpose(qi.reshape(b * nq, TQ, H_IDX, DH_IDX), (0, 3, 2, 1)).reshape(b * nq, DH_IDX, H_IDX * TQ)
    qlt = jnp.transpose(qlat.reshape(b * nq, TQ, H_A, D_LAT), (0, 3, 2, 1)).reshape(b * nq, D_LAT, H_A * TQ)
    wit = jnp.swapaxes(wi.reshape(b * nq, TQ, H_IDX), 1, 2)
    k4 = ki.reshape(b, nt, TK, DH_IDX)
    c4 = ckv.reshape(b, nt, TK, D_LAT)
    ctx = _with_ones_rows(jnp.swapaxes(c4, 2, 3))
    wuvt = jnp.swapaxes(w_uv, 1, 2).astype(BF16)
    qblk = lambda bb, i: (bb * nq + i, 0, 0)
    kv4 = lambda bb, i: (bb, 0, 0, 0)
    return pl.pallas_call(
        functools.partial(_dsa_kernel, topk=topk),
        out_shape=jax.ShapeDtypeStruct((b * s, H_A * DH_A), BF16),
        grid=(b, nq),
        in_specs=[
            pl.BlockSpec((None, DH_IDX, H_IDX * TQ), qblk),
            pl.BlockSpec((None, H_IDX, TQ), qblk),
            pl.BlockSpec((None, D_LAT, H_A * TQ), qblk),
            pl.BlockSpec((None, nt, TK, DH_IDX), kv4),
            pl.BlockSpec((None, nt, TK, D_LAT), kv4),
            pl.BlockSpec((None, nt, D_LAT + ONES_ROWS, TK), kv4),
            pl.BlockSpec((H_A * N_BT, LANES, LANES), lambda bb, i: (0, 0, 0)),
            pl.BlockSpec((H_A, DH_A, D_LAT), lambda bb, i: (0, 0, 0)),
        ],
        out_specs=pl.BlockSpec((TQ, H_A * DH_A), lambda bb, i: (bb * nq + i, 0)),
        scratch_shapes=[
            pltpu.VMEM((nt, TK, TQ), F32),
            pltpu.VMEM((TK, H_A * TQ), BF16),
            pltpu.VMEM((D_LAT + ONES_ROWS, H_A * TQ), F32),
        ],
        compiler_params=pltpu.CompilerParams(vmem_limit_bytes=VMEM_LIMIT),
        name="dsa_attention",
    )(qit, wit, qlt, k4, c4, ctx, bias_a, wuvt)


def _diff_kernel(qt_ref, k_ref, vtx_ref, bias_ref, lam_ref, sub_ref, o_ref, p_ref, acc_ref, *, lam_init):
    h = pl.program_id(1)
    i = pl.program_id(2)
    tq = TQ_DIFF
    n_qb = tq // LANES
    t0 = i * tq
    nt = (t0 + tq + TK - 1) // TK
    n_far = jnp.maximum(i * n_qb - N_SAT + 1, 0) // SUB
    qt = qt_ref[...]
    top = lax.broadcasted_iota(jnp.int32, qt.shape, 0) < DH_B
    q2 = jnp.concatenate([jnp.where(top, qt, jnp.zeros_like(qt)), jnp.where(top, jnp.zeros_like(qt), qt)], axis=1)
    qpos = lax.broadcasted_iota(jnp.int32, (1, tq), 1) + t0
    limit = (qpos // CHUNK + 1) * CHUNK
    acc_ref[...] = jnp.zeros(acc_ref.shape, F32)

    def tile(j, m, near):
        s2 = _dot(k_ref[j], q2)
        if near:
            bias = _bias_tile(bias_ref, h, i * n_qb, n_qb, j)
            adm = (lax.broadcasted_iota(jnp.int32, (TK, tq), 0) + j * TK) < limit
        m_new, alpha = [], []
        for mp in range(2):
            cols = slice(mp * tq, (mp + 1) * tq)
            s = s2[:, cols]
            if near:
                s = jnp.where(adm, s + bias, NEG_INF)
            m_old = m[:, cols]
            m_h = jnp.maximum(m_old, jnp.max(s, axis=0, keepdims=True))
            p_ref[:, cols] = jnp.exp2(s - m_h).astype(BF16)
            alpha.append(jnp.exp2(m_old - m_h))
            m_new.append(m_h)
        acc_ref[...] = acc_ref[...] * jnp.concatenate(alpha, axis=1) + _dot(vtx_ref[j], p_ref[...])
        return jnp.concatenate(m_new, axis=1)

    m = jnp.full((1, 2 * tq), NEG_INF, F32)
    m = lax.fori_loop(0, n_far, lambda j, mm: tile(j, mm, False), m)
    lax.fori_loop(n_far, nt, lambda j, mm: tile(j, mm, True), m)

    lf = lam_ref[...]
    lam = (jnp.exp(jnp.sum(lf[0:1, :] * lf[1:2, :], axis=1, keepdims=True))
           - jnp.exp(jnp.sum(lf[2:3, :] * lf[3:4, :], axis=1, keepdims=True)) + lam_init)
    dv = 2 * DH_B
    o = (acc_ref[0:dv, 0:tq] / acc_ref[dv:dv + 1, 0:tq]
         - lam * (acc_ref[0:dv, tq:2 * tq] / acc_ref[dv:dv + 1, tq:2 * tq]))
    o = o * lax.rsqrt(jnp.mean(o * o, axis=0, keepdims=True) + EPS) * sub_ref[...] * (1.0 - lam_init)
    o_ref[...] = o.T.astype(BF16)


def _diff(qb, kb, vb, bias_b, b_lambda, b_subln, lam_init, b, s):
    tq = TQ_DIFF
    nq, nt = s // tq, s // TK
    dv = 2 * DH_B
    qt = jnp.transpose(qb.reshape(b, nq, tq, H_B, dv), (0, 3, 1, 4, 2))
    k4 = kb.reshape(b, nt, TK, H_B * dv)
    vtx = _with_ones_rows(jnp.transpose(vb.reshape(b, nt, TK, H_B, dv), (0, 3, 1, 4, 2)))
    return pl.pallas_call(
        functools.partial(_diff_kernel, lam_init=lam_init),
        out_shape=jax.ShapeDtypeStruct((b * s, H_B * dv), BF16),
        grid=(b, H_B, nq),
        in_specs=[
            pl.BlockSpec((None, None, None, dv, tq), lambda bb, h, i: (bb, h, i, 0, 0)),
            pl.BlockSpec((None, nt, TK, dv), lambda bb, h, i: (bb, 0, 0, h)),
            pl.BlockSpec((None, None, nt, dv + ONES_ROWS, TK), lambda bb, h, i: (bb, h, 0, 0, 0)),
            pl.BlockSpec((H_B * N_BT, LANES, LANES), lambda bb, h, i: (0, 0, 0)),
            pl.BlockSpec((4, DH_B), lambda bb, h, i: (0, 0)),
            pl.BlockSpec((dv, 1), lambda bb, h, i: (0, 0)),
        ],
        out_specs=pl.BlockSpec((tq, dv), lambda bb, h, i: (bb * nq + i, h)),
        scratch_shapes=[
            pltpu.VMEM((TK, 2 * tq), BF16),
            pltpu.VMEM((dv + ONES_ROWS, 2 * tq), F32),
        ],
        compiler_params=pltpu.CompilerParams(vmem_limit_bytes=VMEM_LIMIT),
        name="diff_attention",
    )(qt, k4, vtx, bias_b, b_lambda.astype(F32), b_subln.reshape(dv, 1).astype(F32))


def _swa_kernel(qt_ref, ka_ref, kb_ref, vta_ref, vtb_ref, bias_ref, sink_ref, o_ref):
    i = pl.program_id(1)
    pad_pen = jnp.where(i == 0, NEG_INF, 0.0)
    for kv in range(H_C_KV):
        qt = qt_ref[kv]
        s = jnp.concatenate([_dot(ka_ref[kv], qt) + pad_pen, _dot(kb_ref[kv], qt)], axis=0) + bias_ref[kv]
        sink = sink_ref[kv]
        m = jnp.maximum(jnp.max(s, axis=0, keepdims=True), sink)
        p = jnp.exp2(s - m).astype(BF16)
        acc = _dot(vta_ref[kv], p[0:TQ, :]) + _dot(vtb_ref[kv], p[TQ:2 * TQ, :])
        o = acc[0:DH_C, :] / (acc[DH_C:DH_C + 1, :] + jnp.exp2(sink - m))
        for pr in range(G_C // 2):
            pair = jnp.concatenate([o[:, (2 * pr) * TQ:(2 * pr + 1) * TQ], o[:, (2 * pr + 1) * TQ:(2 * pr + 2) * TQ]],
                                   axis=0)
            col = (kv * G_C + 2 * pr) * DH_C
            o_ref[:, col:col + LANES] = pair.T.astype(BF16)


def _swa(q, k, v, table_c, sinks, b, s):
    assert WINDOW == TQ and TQ % CHUNK == 0
    nq = s // TQ
    length = 3 * TQ + 1
    m = np.arange(length)
    diff = np.where(m < 2 * TQ + 1, m, m - length)
    u = jnp.take(table_c.astype(F32), jnp.asarray(_t5_bucket_np(diff - TQ)), axis=0) * LOG2E
    bias = _toeplitz(jnp.moveaxis(u, -1, 0), TQ, 2 * TQ)
    rblk = np.arange(TQ)[:, None] // CHUNK + WINDOW // CHUNK
    cblk = np.arange(2 * TQ)[None, :] // CHUNK
    valid = (cblk <= rblk) & (cblk >= rblk - WINDOW // CHUNK)
    bias = jnp.where(jnp.asarray(valid)[None], bias, NEG_INF)
    bias_t = jnp.transpose(bias.reshape(H_C_KV, G_C, TQ, 2 * TQ), (0, 3, 1, 2)).reshape(H_C_KV, 2 * TQ, G_C * TQ)
    sink_row = jnp.repeat(sinks.astype(F32).reshape(H_C_KV, 1, G_C) * LOG2E, TQ, axis=2)

    qt = jnp.transpose(q.reshape(b * nq, TQ, H_C_KV, G_C, DH_C), (0, 2, 4, 3, 1)).reshape(
        b * nq, H_C_KV, DH_C, G_C * TQ)
    kp = jnp.pad(k.reshape(b, s, H_C_KV, DH_C), ((0, 0), (TQ, 0), (0, 0), (0, 0)))
    vp = jnp.pad(v.reshape(b, s, H_C_KV, DH_C), ((0, 0), (TQ, 0), (0, 0), (0, 0)))
    k4 = jnp.swapaxes(kp, 1, 2)
    vtx = _with_ones_rows(jnp.transpose(vp, (0, 2, 3, 1)))
    dvx = DH_C + ONES_ROWS
    return pl.pallas_call(
        _swa_kernel,
        out_shape=jax.ShapeDtypeStruct((b * s, H_C * DH_C), BF16),
        grid=(b, nq),
        in_specs=[
            pl.BlockSpec((None, H_C_KV, DH_C, G_C * TQ), lambda bb, i: (bb * nq + i, 0, 0, 0)),
            pl.BlockSpec((None, H_C_KV, TQ, DH_C), lambda bb, i: (bb, 0, i, 0)),
            pl.BlockSpec((None, H_C_KV, TQ, DH_C), lambda bb, i: (bb, 0, i + 1, 0)),
            pl.BlockSpec((None, H_C_KV, dvx, TQ), lambda bb, i: (bb, 0, 0, i)),
            pl.BlockSpec((None, H_C_KV, dvx, TQ), lambda bb, i: (bb, 0, 0, i + 1)),
            pl.BlockSpec((H_C_KV, 2 * TQ, G_C * TQ), lambda bb, i: (0, 0, 0)),
            pl.BlockSpec((H_C_KV, 1, G_C * TQ), lambda bb, i: (0, 0, 0)),
        ],
        out_specs=pl.BlockSpec((TQ, H_C * DH_C), lambda bb, i: (bb * nq + i, 0)),
        compiler_params=pltpu.CompilerParams(vmem_limit_bytes=VMEM_LIMIT),
        name="swa_attention",
    )(qt, k4, k4, vtx, vtx, bias_t, sink_row)


def _memkv_kernel(mem_ref, g_ref, w_ref, k_ref, v_ref):
    h = _rms(mem_ref[...], g_ref[...]).astype(BF16)
    kv = _dot(h, w_ref[...])
    k_ref[...] = kv[:, 0:H_X * DH_X].astype(BF16)
    v_ref[...] = kv[:, H_X * DH_X:].astype(BF16)


def _memkv(mem, g, wkv):
    b, m, _ = mem.shape
    n = H_X * DH_X
    return pl.pallas_call(
        _memkv_kernel,
        out_shape=[jax.ShapeDtypeStruct((b, m, n), BF16), jax.ShapeDtypeStruct((b, m, n), BF16)],
        grid=(b,),
        in_specs=[pl.BlockSpec((None, m, D_MODEL), lambda i: (i, 0, 0)),
                  pl.BlockSpec((1, D_MODEL), lambda i: (0, 0)),
                  pl.BlockSpec((D_MODEL, 2 * n), lambda i: (0, 0))],
        out_specs=[pl.BlockSpec((None, m, n), lambda i: (i, 0, 0)), pl.BlockSpec((None, m, n), lambda i: (i, 0, 0))],
        compiler_params=pltpu.CompilerParams(vmem_limit_bytes=VMEM_LIMIT),
        name="mem_kv",
    )(mem, g.reshape(1, D_MODEL), wkv.astype(BF16))


def _lane_half_mask(rows, half):
    lane = lax.broadcasted_iota(jnp.int32, (rows, LANES), 1)
    return (lane < 64) if half == 0 else (lane >= 64)


def _tail_kernel(*refs, n_a):
    x_ref, g_ref = refs[0], refs[1]
    a_refs = refs[2:2 + n_a]
    wout_refs = refs[2 + n_a:2 + 2 * n_a]
    wq_ref, kt_ref, v_ref, wo_ref, w1_ref, w2_ref, o_ref = refs[2 + 2 * n_a:]
    tm = x_ref.shape[0]
    g = g_ref[...]
    x = x_ref[...]

    y = _dot(a_refs[0][...], wout_refs[0][...])
    for a_ref, w_ref in zip(a_refs[1:], wout_refs[1:]):
        y = y + _dot(a_ref[...], w_ref[...])
    x = x + _rms(y, g[1:2, :])

    hq = _rms(x, g[2:3, :]).astype(BF16)
    q = (_dot(hq, wq_ref[...]) * (DH_X ** -0.5)).astype(BF16)
    pairs = []
    for pr in range(H_X // 2):
        qp = q[:, pr * LANES:(pr + 1) * LANES]
        vp = v_ref[:, pr * LANES:(pr + 1) * LANES]
        out = jnp.zeros((tm, LANES), F32)
        for half in range(2):
            qz = jnp.where(_lane_half_mask(tm, half), qp, jnp.zeros_like(qp))
            s = _dot(qz, kt_ref[pr * LANES:(pr + 1) * LANES, :])
            m = jnp.max(s, axis=1, keepdims=True)
            e = jnp.exp(s - m)
            p = (e / jnp.sum(e, axis=1, keepdims=True)).astype(BF16)
            out = out + _dot(p, jnp.where(_lane_half_mask(vp.shape[0], half), vp, jnp.zeros_like(vp)))
        pairs.append(out.astype(BF16))
    o = jnp.concatenate(pairs, axis=1)
    x = x + _rms(_dot(o, wo_ref[...]), g[3:4, :])

    hm = _rms(x, g[4:5, :]).astype(BF16)
    y = jnp.zeros((tm, D_MODEL), F32)
    for c in range(D_FF // FF_CHUNK):
        a = jnp.maximum(_dot(hm, w1_ref[:, c * FF_CHUNK:(c + 1) * FF_CHUNK]), 0.0)
        y = y + _dot((a * a).astype(BF16), w2_ref[c * FF_CHUNK:(c + 1) * FF_CHUNK, :])
    o_ref[...] = x + _rms(y, g[5:6, :])


def _tail(x2, g, a_list, wout_list, wq, kt, v, wo, w1, w2, b, s):
    t = x2.shape[0]
    tm = min(TM_TAIL, s)
    per_b = s // tm
    n_a = len(a_list)
    row = lambda i: (i, 0)
    full2 = lambda i: (0, 0)
    once = dict(pipeline_mode=pl.Buffered(1))
    in_specs = [pl.BlockSpec((tm, D_MODEL), row), pl.BlockSpec((6, D_MODEL), full2)]
    in_specs += [pl.BlockSpec((tm, a.shape[1]), row) for a in a_list]
    in_specs += [pl.BlockSpec(w.shape, full2, **once) for w in wout_list]
    in_specs += [
        pl.BlockSpec(wq.shape, full2, **once),
        pl.BlockSpec((None,) + kt.shape[1:], lambda i: (i // per_b, 0, 0)),
        pl.BlockSpec((None,) + v.shape[1:], lambda i: (i // per_b, 0, 0)),
        pl.BlockSpec(wo.shape, full2, **once),
        pl.BlockSpec(w1.shape, full2, **once),
        pl.BlockSpec(w2.shape, full2, **once),
    ]
    return pl.pallas_call(
        functools.partial(_tail_kernel, n_a=n_a),
        out_shape=jax.ShapeDtypeStruct((t, D_MODEL), F32),
        grid=(t // tm,),
        in_specs=in_specs,
        out_specs=pl.BlockSpec((tm, D_MODEL), row),
        compiler_params=pltpu.CompilerParams(vmem_limit_bytes=VMEM_LIMIT),
        name="tail",
    )(x2, g, *a_list, *wout_list, wq, kt, v, wo, w1, w2)


def kernel(x, mem, rel_bias_table, norm_g, ev_w_in, ev_a_kv_norm, ev_a_w_uk, ev_a_w_uv, ev_b_lambda, ev_b_subln, ev_w_out, od_w_in, od_sinks, od_w_out, xa_wq, xa_wkv, xa_wo, xa_mem_norm, mlp_w1, mlp_w2):
    b, s, d = x.shape
    depth = norm_g.shape[0]
    assert d == D_MODEL and s % TK == 0 and TK % TQ_DIFF == 0 and TQ_DIFF % TQ == 0
    x2 = x.reshape(b * s, d)
    bias_a = _causal_bias_tiles(rel_bias_table[:, :H_A])
    bias_b = _causal_bias_tiles(rel_bias_table[:, H_A:H_A + H_B])
    table_c = rel_bias_table[:, H_A + H_B:]
    for l in range(depth):
        g = norm_g[l].astype(F32)
        if l % 2 == 0:
            e = l // 2
            lam_init = 0.8 - 0.6 * math.exp(-0.3 * l)
            qlat, ckv, qi, qb, kb, vb, ki, wi = _proj_even(x2, g, ev_w_in[e], ev_a_kv_norm[e], ev_a_w_uk[e])
            oa = _dsa(qi, wi, qlat, ki, ckv, bias_a, ev_a_w_uv[e], b, s)
            ob = _diff(qb, kb, vb, bias_b, ev_b_lambda[e], ev_b_subln[e], lam_init, b, s)
            n_a = H_A * DH_A
            a_list = [oa, ob]
            wout_list = [ev_w_out[e][:n_a].astype(BF16), ev_w_out[e][n_a:].astype(BF16)]
        else:
            o = l // 2
            q, k, v = _proj_odd(x2, g, od_w_in[o])
            a_list = [_swa(q, k, v, table_c, od_sinks[o], b, s)]
            wout_list = [od_w_out[o].astype(BF16)]
        mk, mv = _memkv(mem, xa_mem_norm[l], xa_wkv[l])
        x2 = _tail(x2, g, a_list, wout_list, xa_wq[l].astype(BF16), jnp.swapaxes(mk, 1, 2), mv,
                   xa_wo[l].astype(BF16), mlp_w1[l].astype(BF16), mlp_w2[l].astype(BF16), b, s)
    return x2.reshape(b, s, d)
```

```python
import functools
import math

import numpy as np
import jax
import jax.numpy as jnp
from jax import lax
from jax.experimental import pallas as pl
from jax.experimental.pallas import tpu as pltpu

F32 = jnp.float32
BF16 = jnp.bfloat16

D_MODEL = 1024
CHUNK = 64
EPS = 1e-6
NEG_INF = -1e30
LOG2E = math.log2(math.e)

H_A, DH_A, D_LAT = 8, 64, 128
H_IDX, DH_IDX = 8, 64
TOPK_MAX = 256
H_B, DH_B = 4, 64
H_C, H_C_KV, DH_C = 16, 2, 64
G_C = H_C // H_C_KV
WINDOW = 128
H_X, DH_X = 4, 64
D_FF = 4 * D_MODEL
N_BUCKETS = 32
MAX_DIST = 1024

LANES = 128
FOLD_ROWS = 64
TQ = 128
TQ_DIFF = 256
TK = 512
SUB = TK // LANES
ONES_ROWS = 16
TM_PROJ = 512
TM_TAIL = 256
FF_CHUNK = 1024
VMEM_LIMIT = 56 * 1024 * 1024
N_BISECT = 14
GROUPS = (4, 2, 1)
GROUPS_NEAR = (2, 1)


def _rms(x, g):
    return x * lax.rsqrt(jnp.mean(x * x, axis=-1, keepdims=True) + EPS) * g


def _dot(a, b):
    return jnp.dot(a, b, preferred_element_type=F32)


def _t5_bucket_np(rel):
    nb = N_BUCKETS // 2
    max_exact = nb // 2
    n = np.abs(rel)
    nf = np.maximum(n, 1).astype(np.float32)
    large = max_exact + (np.log(nf / max_exact) / math.log(MAX_DIST / max_exact) * (nb - max_exact)).astype(np.int32)
    large = np.minimum(large, nb - 1)
    return np.where(rel > 0, nb, 0) + np.where(n < max_exact, n, large)


def _sat_blocks():
    d = 1
    while _t5_bucket_np(np.array([-(d * LANES - (LANES - 1))]))[0] != N_BUCKETS // 2 - 1:
        d += 1
    return d


N_SAT = _sat_blocks()
N_BT = N_SAT + 1


def _toeplitz(u, rows, cols):
    length = u.shape[-1]
    flat = jnp.tile(u, (1,) * (u.ndim - 1) + (rows,))[..., :rows * (length - 1)]
    return flat.reshape(u.shape[:-1] + (rows, length - 1))[..., :cols]


def _causal_bias_tiles(table_cols):
    length = 2 * LANES + 1
    m = np.arange(length)
    diff = np.where(m < LANES + 1, m, m - length)
    rel = -diff[None, :] - (np.arange(N_BT) * LANES)[:, None]
    tab = table_cols.astype(F32)
    u = jnp.take(tab, jnp.asarray(_t5_bucket_np(rel)), axis=0)
    u = (u - tab[N_BUCKETS // 2 - 1][None, None, :]) * LOG2E
    tiles = _toeplitz(jnp.moveaxis(u, -1, 0), LANES, LANES)
    return tiles.reshape(table_cols.shape[1] * N_BT, LANES, LANES)


def _fold_rows(x, op):
    acc = x[0:FOLD_ROWS, :]
    for r in range(1, x.shape[0] // FOLD_ROWS):
        acc = op(acc, x[r * FOLD_ROWS:(r + 1) * FOLD_ROWS, :])
    return acc


def _grouped_loop(lo, hi, carry, fn, groups):
    for g in groups:
        n = (hi - lo) // g
        carry = lax.fori_loop(0, n, lambda u, c, lo=lo, g=g: fn([lo + u * g + k for k in range(g)], c), carry)
        lo = lo + n * g
    return carry


def _bias_tile(bias_ref, h, qb0, n_qb, j):
    rows = []
    for c in range(SUB):
        cols = [bias_ref[h * N_BT + jnp.clip(qb0 + r - (j * SUB + c), 0, N_SAT)] for r in range(n_qb)]
        rows.append(cols[0] if n_qb == 1 else jnp.concatenate(cols, axis=1))
    return jnp.concatenate(rows, axis=0)


def _proj_even_kernel(x_ref, g_ref, w_ref, kvn_ref, wuk_ref,
                      qlat_ref, ckv_ref, qi_ref, qb_ref, kb_ref, vb_ref, ki_ref, wi_ref):
    h = _rms(x_ref[...], g_ref[0:1, :]).astype(BF16)
    z = _dot(h, w_ref[...])
    qa = z[:, 0:512].astype(BF16)
    for hh in range(H_A):
        ql = _dot(qa[:, hh * DH_A:(hh + 1) * DH_A], wuk_ref[hh]) * (DH_A ** -0.5 * LOG2E)
        qlat_ref[:, hh * D_LAT:(hh + 1) * D_LAT] = ql.astype(BF16)
    ckv_ref[...] = _rms(z[:, 512:640], kvn_ref[...]).astype(BF16)
    qi_ref[...] = z[:, 640:1152].astype(BF16)
    qb_ref[...] = (z[:, 1152:1664] * (DH_B ** -0.5 * LOG2E)).astype(BF16)
    kb_ref[...] = z[:, 1664:2176].astype(BF16)
    vb_ref[...] = z[:, 2176:2688].astype(BF16)
    ki_ref[...] = z[:, 2688:2688 + DH_IDX].astype(BF16)
    wi_ref[...] = z[:, 2816:2816 + H_IDX]


def _proj_even(x2, g, w_in, kvn, w_uk):
    t = x2.shape[0]
    tm = min(TM_PROJ, t)
    pad = jnp.zeros((D_MODEL, LANES - DH_IDX), w_in.dtype)
    cols = [w_in[:, 0:512], w_in[:, 512:640], w_in[:, 640:1152], w_in[:, 1224:1736], w_in[:, 1736:2248],
            w_in[:, 2248:2760], w_in[:, 1152:1216], pad, w_in[:, 1216:1224],
            jnp.zeros((D_MODEL, LANES - H_IDX), w_in.dtype)]
    w = jnp.concatenate(cols, axis=1).astype(BF16)
    n = w.shape[1]
    row = lambda i: (i, 0)
    full2 = lambda i: (0, 0)
    widths = [(H_A * D_LAT, BF16), (D_LAT, BF16), (512, BF16), (512, BF16), (512, BF16), (512, BF16),
              (DH_IDX, BF16), (H_IDX, F32)]
    return pl.pallas_call(
        _proj_even_kernel,
        out_shape=[jax.ShapeDtypeStruct((t, c), dt) for c, dt in widths],
        grid=(t // tm,),
        in_specs=[
            pl.BlockSpec((tm, D_MODEL), row),
            pl.BlockSpec((6, D_MODEL), full2),
            pl.BlockSpec((D_MODEL, n), full2),
            pl.BlockSpec((1, D_LAT), full2),
            pl.BlockSpec((H_A, DH_A, D_LAT), lambda i: (0, 0, 0)),
        ],
        out_specs=[pl.BlockSpec((tm, c), row) for c, _ in widths],
        compiler_params=pltpu.CompilerParams(vmem_limit_bytes=VMEM_LIMIT),
        name="proj_even",
    )(x2, g, w, kvn.reshape(1, D_LAT), w_uk.astype(BF16))


def _proj_odd_kernel(x_ref, g_ref, w_ref, q_ref, k_ref, v_ref):
    h = _rms(x_ref[...], g_ref[0:1, :]).astype(BF16)
    z = _dot(h, w_ref[...])
    q_ref[...] = (z[:, 0:1024] * (DH_C ** -0.5 * LOG2E)).astype(BF16)
    k_ref[...] = z[:, 1024:1152].astype(BF16)
    v_ref[...] = z[:, 1152:1280].astype(BF16)


def _proj_odd(x2, g, w_in):
    t = x2.shape[0]
    tm = min(TM_PROJ, t)
    row = lambda i: (i, 0)
    full2 = lambda i: (0, 0)
    return pl.pallas_call(
        _proj_odd_kernel,
        out_shape=[jax.ShapeDtypeStruct((t, 1024), BF16),
                   jax.ShapeDtypeStruct((t, 128), BF16),
                   jax.ShapeDtypeStruct((t, 128), BF16)],
        grid=(t // tm,),
        in_specs=[pl.BlockSpec((tm, D_MODEL), row),
                  pl.BlockSpec((6, D_MODEL), full2),
                  pl.BlockSpec((D_MODEL, 1280), full2)],
        out_specs=[pl.BlockSpec((tm, 1024), row), pl.BlockSpec((tm, 128), row), pl.BlockSpec((tm, 128), row)],
        compiler_params=pltpu.CompilerParams(vmem_limit_bytes=VMEM_LIMIT),
        name="proj_odd",
    )(x2, g, w_in.astype(BF16))


def _dsa_kernel(qit_ref, wit_ref, qlt_ref, k_ref, c_ref, ctx_ref, bias_ref, wuvt_ref, o_ref,
                sc_ref, acc_ref, *p_refs, topk):
    i = pl.program_id(1)
    t0 = i * TQ
    nt = (t0 + TQ + TK - 1) // TK
    n_far = jnp.maximum(i - N_SAT + 1, 0) // SUB
    kf = float(topk)

    qpos = lax.broadcasted_iota(jnp.int32, (1, TQ), 1) + t0
    limit = (qpos // CHUNK + 1) * CHUNK
    small = limit <= topk
    big = jnp.logical_not(small)
    w = wit_ref[...]
    qit = qit_ref[...]

    def score_tiles(js, carry):
        rmax, rmin = carry
        zs = [_dot(k_ref[j], qit) for j in js]
        for j, z in zip(js, zs):
            sc = jnp.maximum(z[:, 0:TQ], 0.0) * w[0:1, :]
            for h in range(1, H_IDX):
                sc = sc + jnp.maximum(z[:, h * TQ:(h + 1) * TQ], 0.0) * w[h:h + 1, :]
            kpos = lax.broadcasted_iota(jnp.int32, (TK, TQ), 0) + j * TK
            adm = kpos < limit
            sc_ref[j] = jnp.where(adm, sc, -jnp.inf)
            rmax = jnp.maximum(rmax, _fold_rows(jnp.where(adm, sc, -jnp.inf), jnp.maximum))
            rmin = jnp.minimum(rmin, _fold_rows(jnp.where(adm, sc, jnp.inf), jnp.minimum))
        return rmax, rmin

    rmax, rmin = _grouped_loop(
        0, nt, (jnp.full((FOLD_ROWS, TQ), -jnp.inf, F32), jnp.full((FOLD_ROWS, TQ), jnp.inf, F32)),
        score_tiles, GROUPS)
    hi0 = jnp.max(rmax, axis=0, keepdims=True)
    lo0 = jnp.min(rmin, axis=0, keepdims=True)

    def col_reduce(tile_fn, op, init, final):
        def body(j, acc):
            return op(acc, _fold_rows(tile_fn(j), op))
        return final(lax.fori_loop(0, nt, body, jnp.full((FOLD_ROWS, TQ), init, F32)), axis=0, keepdims=True)

    def count_ge(thr):
        return col_reduce(lambda j: jnp.where(sc_ref[j] >= thr, 1.0, 0.0), jnp.add, 0.0, jnp.sum)

    def count_gt(thr):
        return col_reduce(lambda j: jnp.where(sc_ref[j] > thr, 1.0, 0.0), jnp.add, 0.0, jnp.sum)

    def max_le(thr):
        def tile(j):
            t = sc_ref[j]
            return jnp.where(t <= thr, t, -jnp.inf)
        return col_reduce(tile, jnp.maximum, -jnp.inf, jnp.max)

    def max_lt(thr):
        def tile(j):
            t = sc_ref[j]
            return jnp.where(t < thr, t, -jnp.inf)
        return col_reduce(tile, jnp.maximum, -jnp.inf, jnp.max)

    def bisect_body(_, carry):
        lo, hi = carry
        mid = lo * 0.5 + hi * 0.5
        ge = count_ge(mid) >= kf
        return jnp.where(ge, mid, lo), jnp.where(ge, hi, mid)

    lo, hi = lax.fori_loop(0, N_BISECT, bisect_body, (lo0, hi0))

    v0 = max_le(hi)
    c0 = count_ge(v0)

    def walk_cond(carry):
        _, cnt, it = carry
        pending = jnp.where((cnt < kf) & big, 1.0, 0.0)
        return (jnp.max(pending) > 0.0) & (it < topk + 2)

    def walk_body(carry):
        v, cnt, it = carry
        v = jnp.where((cnt < kf) & big, max_lt(v), v)
        return v, count_ge(v), it + 1

    tau, cnt, _ = lax.while_loop(walk_cond, walk_body, (v0, c0, jnp.int32(0)))
    tau = jnp.where(small, -jnp.finfo(F32).max, tau)

    tied = (cnt > kf) & big

    @pl.when(jnp.max(jnp.where(tied, 1.0, 0.0)) > 0.0)
    def _():
        need = kf - count_gt(tau)
        n_keys = sc_ref.shape[0] * TK

        def kpos_f(j):
            return (lax.broadcasted_iota(jnp.int32, (TK, TQ), 0) + j * TK).astype(F32)

        def count_eq_le(pos):
            return col_reduce(lambda j: jnp.where((sc_ref[j] == tau) & (kpos_f(j) <= pos), 1.0, 0.0),
                              jnp.add, 0.0, jnp.sum)

        def pos_body(_, carry):
            lo_p, hi_p = carry
            mid = jnp.floor((lo_p + hi_p) * 0.5)
            ok = count_eq_le(mid) >= need
            return jnp.where(ok, lo_p, mid), jnp.where(ok, mid, hi_p)

        n_steps = int(math.ceil(math.log2(n_keys))) + 1
        _, cut = lax.fori_loop(0, n_steps, pos_body,
                               (jnp.full((1, TQ), -1.0, F32), jnp.full((1, TQ), float(n_keys), F32)))

        def drop_body(j, _):
            t = sc_ref[j]
            sc_ref[j] = jnp.where(tied & (t == tau) & (kpos_f(j) > cut), -jnp.inf, t)
            return 0

        lax.fori_loop(0, nt, drop_body, 0)

    acc_ref[...] = jnp.zeros(acc_ref.shape, F32)
    qlt = qlt_ref[...]

    def attn_tiles(js, m, with_bias):
        s_alls = [_dot(c_ref[j], qlt) for j in js]
        for g, (j, s_all) in enumerate(zip(js, s_alls)):
            p_ref = p_refs[g]
            sel = sc_ref[j] >= tau
            m_new, alpha = [], []
            for h in range(H_A):
                cols = slice(h * TQ, (h + 1) * TQ)
                s = s_all[:, cols]
                if with_bias:
                    s = s + _bias_tile(bias_ref, h, i, 1, j)
                s = jnp.where(sel, s, NEG_INF)
                m_old = m[:, cols]
                m_h = jnp.maximum(m_old, jnp.max(_fold_rows(s, jnp.maximum), axis=0, keepdims=True))
                p_ref[:, cols] = jnp.exp2(s - m_h).astype(BF16)
                alpha.append(jnp.exp2(m_old - m_h))
                m_new.append(m_h)
            acc_ref[...] = acc_ref[...] * jnp.concatenate(alpha, axis=1) + _dot(ctx_ref[j], p_ref[...])
            m = jnp.concatenate(m_new, axis=1)
        return m

    m = jnp.full((1, H_A * TQ), NEG_INF, F32)
    m = _grouped_loop(0, n_far, m, lambda js, mm: attn_tiles(js, mm, False), GROUPS)
    _grouped_loop(n_far, nt, m, lambda js, mm: attn_tiles(js, mm, True), GROUPS_NEAR)

    outs = []
    for h in range(H_A):
        cols = slice(h * TQ, (h + 1) * TQ)
        o_t = (acc_ref[0:D_LAT, cols] / acc_ref[D_LAT:D_LAT + 1, cols]).astype(BF16)
        outs.append(_dot(wuvt_ref[h], o_t))
    for pr in range(H_A // 2):
        pair = jnp.concatenate([outs[2 * pr], outs[2 * pr + 1]], axis=0)
        o_ref[:, pr * LANES:(pr + 1) * LANES] = pair.T.astype(BF16)


def _with_ones_rows(xt):
    return jnp.concatenate([xt, jnp.ones(xt.shape[:-2] + (ONES_ROWS, xt.shape[-1]), xt.dtype)], axis=-2)


def _dsa(qi, wi, qlat, ki, ckv, bias_a, w_uv, b, s):
    nq, nt = s // TQ, s // TK
    topk = min(TOPK_MAX, s // 4)
    qit = jnp.transpose(qi.reshape(b * nq, TQ, H_IDX, DH_IDX), (0, 3, 2, 1)).reshape(b * nq, DH_IDX, H_IDX * TQ)
    qlt = jnp.transpose(qlat.reshape(b * nq, TQ, H_A, D_LAT), (0, 3, 2, 1)).reshape(b * nq, D_LAT, H_A * TQ)
    wit = jnp.swapaxes(wi.reshape(b * nq, TQ, H_IDX), 1, 2)
    k4 = ki.reshape(b, nt, TK, DH_IDX)
    c4 = ckv.reshape(b, nt, TK, D_LAT)
    ctx = _with_ones_rows(jnp.swapaxes(c4, 2, 3))
    wuvt = jnp.swapaxes(w_uv, 1, 2).astype(BF16)
    qblk = lambda bb, i: (bb * nq + i, 0, 0)
    kv4 = lambda bb, i: (bb, 0, 0, 0)
    return pl.pallas_call(
        functools.partial(_dsa_kernel, topk=topk),
        out_shape=jax.ShapeDtypeStruct((b * s, H_A * DH_A), BF16),
        grid=(b, nq),
        in_specs=[
            pl.BlockSpec((None, DH_IDX, H_IDX * TQ), qblk),
            pl.BlockSpec((None, H_IDX, TQ), qblk),
            pl.BlockSpec((None, D_LAT, H_A * TQ), qblk),
            pl.BlockSpec((None, nt, TK, DH_IDX), kv4),
            pl.BlockSpec((None, nt, TK, D_LAT), kv4),
            pl.BlockSpec((None, nt, D_LAT + ONES_ROWS, TK), kv4),
            pl.BlockSpec((H_A * N_BT, LANES, LANES), lambda bb, i: (0, 0, 0)),
            pl.BlockSpec((H_A, DH_A, D_LAT), lambda bb, i: (0, 0, 0)),
        ],
        out_specs=pl.BlockSpec((TQ, H_A * DH_A), lambda bb, i: (bb * nq + i, 0)),
        scratch_shapes=[
            pltpu.VMEM((nt, TK, TQ), F32),
            pltpu.VMEM((D_LAT + ONES_ROWS, H_A * TQ), F32),
        ] + [pltpu.VMEM((TK, H_A * TQ), BF16)] * max(GROUPS),
        compiler_params=pltpu.CompilerParams(vmem_limit_bytes=VMEM_LIMIT),
        name="dsa_attention",
    )(qit, wit, qlt, k4, c4, ctx, bias_a, wuvt)


def _diff_kernel(qt_ref, k_ref, vtx_ref, bias_ref, lam_ref, sub_ref, o_ref, acc_ref, *p_refs, lam_init):
    h = pl.program_id(1)
    i = pl.program_id(2)
    tq = TQ_DIFF
    n_qb = tq // LANES
    t0 = i * tq
    nt = (t0 + tq + TK - 1) // TK
    n_far = jnp.maximum(i * n_qb - N_SAT + 1, 0) // SUB
    qt = qt_ref[...]
    top = lax.broadcasted_iota(jnp.int32, qt.shape, 0) < DH_B
    q2 = jnp.concatenate([jnp.where(top, qt, jnp.zeros_like(qt)), jnp.where(top, jnp.zeros_like(qt), qt)], axis=1)
    qpos = lax.broadcasted_iota(jnp.int32, (1, tq), 1) + t0
    limit = (qpos // CHUNK + 1) * CHUNK
    acc_ref[...] = jnp.zeros(acc_ref.shape, F32)

    def tiles(js, m, near):
        s2s = [_dot(k_ref[j], q2) for j in js]
        for g, (j, s2) in enumerate(zip(js, s2s)):
            p_ref = p_refs[g]
            if near:
                bias = _bias_tile(bias_ref, h, i * n_qb, n_qb, j)
                adm = (lax.broadcasted_iota(jnp.int32, (TK, tq), 0) + j * TK) < limit
            m_new, alpha = [], []
            for mp in range(2):
                cols = slice(mp * tq, (mp + 1) * tq)
                s = s2[:, cols]
                if near:
                    s = jnp.where(adm, s + bias, NEG_INF)
                m_old = m[:, cols]
                m_h = jnp.maximum(m_old, jnp.max(_fold_rows(s, jnp.maximum), axis=0, keepdims=True))
                p_ref[:, cols] = jnp.exp2(s - m_h).astype(BF16)
                alpha.append(jnp.exp2(m_old - m_h))
                m_new.append(m_h)
            acc_ref[...] = acc_ref[...] * jnp.concatenate(alpha, axis=1) + _dot(vtx_ref[j], p_ref[...])
            m = jnp.concatenate(m_new, axis=1)
        return m

    m = jnp.full((1, 2 * tq), NEG_INF, F32)
    m = _grouped_loop(0, n_far, m, lambda js, mm: tiles(js, mm, False), GROUPS)
    _grouped_loop(n_far, nt, m, lambda js, mm: tiles(js, mm, True), GROUPS_NEAR)

    lf = lam_ref[...]
    lam = (jnp.exp(jnp.sum(lf[0:1, :] * lf[1:2, :], axis=1, keepdims=True))
           - jnp.exp(jnp.sum(lf[2:3, :] * lf[3:4, :], axis=1, keepdims=True)) + lam_init)
    dv = 2 * DH_B
    o = (acc_ref[0:dv, 0:tq] / acc_ref[dv:dv + 1, 0:tq]
         - lam * (acc_ref[0:dv, tq:2 * tq] / acc_ref[dv:dv + 1, tq:2 * tq]))
    o = o * lax.rsqrt(jnp.mean(o * o, axis=0, keepdims=True) + EPS) * sub_ref[...] * (1.0 - lam_init)
    o_ref[...] = o.T.astype(BF16)


def _diff(qb, kb, vb, bias_b, b_lambda, b_subln, lam_init, b, s):
    tq = TQ_DIFF
    nq, nt = s // tq, s // TK
    dv = 2 * DH_B
    qt = jnp.transpose(qb.reshape(b, nq, tq, H_B, dv), (0, 3, 1, 4, 2))
    k4 = kb.reshape(b, nt, TK, H_B * dv)
    vtx = _with_ones_rows(jnp.transpose(vb.reshape(b, nt, TK, H_B, dv), (0, 3, 1, 4, 2)))
    return pl.pallas_call(
        functools.partial(_diff_kernel, lam_init=lam_init),
        out_shape=jax.ShapeDtypeStruct((b * s, H_B * dv), BF16),
        grid=(b, H_B, nq),
        in_specs=[
            pl.BlockSpec((None, None, None, dv, tq), lambda bb, h, i: (bb, h, i, 0, 0)),
            pl.BlockSpec((None, nt, TK, dv), lambda bb, h, i: (bb, 0, 0, h)),
            pl.BlockSpec((None, None, nt, dv + ONES_ROWS, TK), lambda bb, h, i: (bb, h, 0, 0, 0)),
            pl.BlockSpec((H_B * N_BT, LANES, LANES), lambda bb, h, i: (0, 0, 0)),
            pl.BlockSpec((4, DH_B), lambda bb, h, i: (0, 0)),
            pl.BlockSpec((dv, 1), lambda bb, h, i: (0, 0)),
        ],
        out_specs=pl.BlockSpec((tq, dv), lambda bb, h, i: (bb * nq + i, h)),
        scratch_shapes=[
            pltpu.VMEM((dv + ONES_ROWS, 2 * tq), F32),
        ] + [pltpu.VMEM((TK, 2 * tq), BF16)] * max(GROUPS),
        compiler_params=pltpu.CompilerParams(vmem_limit_bytes=VMEM_LIMIT),
        name="diff_attention",
    )(qt, k4, vtx, bias_b, b_lambda.astype(F32), b_subln.reshape(dv, 1).astype(F32))


def _swa_kernel(qt_ref, ka_ref, kb_ref, vta_ref, vtb_ref, bias_ref, sink_ref, o_ref):
    i = pl.program_id(1)
    pad_pen = jnp.where(i == 0, NEG_INF, 0.0)
    for kv in range(H_C_KV):
        qt = qt_ref[kv]
        s = jnp.concatenate([_dot(ka_ref[kv], qt) + pad_pen, _dot(kb_ref[kv], qt)], axis=0) + bias_ref[kv]
        sink = sink_ref[kv]
        m = jnp.maximum(jnp.max(s, axis=0, keepdims=True), sink)
        p = jnp.exp2(s - m).astype(BF16)
        acc = _dot(vta_ref[kv], p[0:TQ, :]) + _dot(vtb_ref[kv], p[TQ:2 * TQ, :])
        o = acc[0:DH_C, :] / (acc[DH_C:DH_C + 1, :] + jnp.exp2(sink - m))
        for pr in range(G_C // 2):
            pair = jnp.concatenate([o[:, (2 * pr) * TQ:(2 * pr + 1) * TQ], o[:, (2 * pr + 1) * TQ:(2 * pr + 2) * TQ]],
                                   axis=0)
            col = (kv * G_C + 2 * pr) * DH_C
            o_ref[:, col:col + LANES] = pair.T.astype(BF16)


def _swa(q, k, v, table_c, sinks, b, s):
    assert WINDOW == TQ and TQ % CHUNK == 0
    nq = s // TQ
    length = 3 * TQ + 1
    m = np.arange(length)
    diff = np.where(m < 2 * TQ + 1, m, m - length)
    u = jnp.take(table_c.astype(F32), jnp.asarray(_t5_bucket_np(diff - TQ)), axis=0) * LOG2E
    bias = _toeplitz(jnp.moveaxis(u, -1, 0), TQ, 2 * TQ)
    rblk = np.arange(TQ)[:, None] // CHUNK + WINDOW // CHUNK
    cblk = np.arange(2 * TQ)[None, :] // CHUNK
    valid = (cblk <= rblk) & (cblk >= rblk - WINDOW // CHUNK)
    bias = jnp.where(jnp.asarray(valid)[None], bias, NEG_INF)
    bias_t = jnp.transpose(bias.reshape(H_C_KV, G_C, TQ, 2 * TQ), (0, 3, 1, 2)).reshape(H_C_KV, 2 * TQ, G_C * TQ)
    sink_row = jnp.repeat(sinks.astype(F32).reshape(H_C_KV, 1, G_C) * LOG2E, TQ, axis=2)

    qt = jnp.transpose(q.reshape(b * nq, TQ, H_C_KV, G_C, DH_C), (0, 2, 4, 3, 1)).reshape(
        b * nq, H_C_KV, DH_C, G_C * TQ)
    kp = jnp.pad(k.reshape(b, s, H_C_KV, DH_C), ((0, 0), (TQ, 0), (0, 0), (0, 0)))
    vp = jnp.pad(v.reshape(b, s, H_C_KV, DH_C), ((0, 0), (TQ, 0), (0, 0), (0, 0)))
    k4 = jnp.swapaxes(kp, 1, 2)
    vtx = _with_ones_rows(jnp.transpose(vp, (0, 2, 3, 1)))
    dvx = DH_C + ONES_ROWS
    return pl.pallas_call(
        _swa_kernel,
        out_shape=jax.ShapeDtypeStruct((b * s, H_C * DH_C), BF16),
        grid=(b, nq),
        in_specs=[
            pl.BlockSpec((None, H_C_KV, DH_C, G_C * TQ), lambda bb, i: (bb * nq + i, 0, 0, 0)),
            pl.BlockSpec((None, H_C_KV, TQ, DH_C), lambda bb, i: (bb, 0, i, 0)),
            pl.BlockSpec((None, H_C_KV, TQ, DH_C), lambda bb, i: (bb, 0, i + 1, 0)),
            pl.BlockSpec((None, H_C_KV, dvx, TQ), lambda bb, i: (bb, 0, 0, i)),
            pl.BlockSpec((None, H_C_KV, dvx, TQ), lambda bb, i: (bb, 0, 0, i + 1)),
            pl.BlockSpec((H_C_KV, 2 * TQ, G_C * TQ), lambda bb, i: (0, 0, 0)),
            pl.BlockSpec((H_C_KV, 1, G_C * TQ), lambda bb, i: (0, 0, 0)),
        ],
        out_specs=pl.BlockSpec((TQ, H_C * DH_C), lambda bb, i: (bb * nq + i, 0)),
        compiler_params=pltpu.CompilerParams(vmem_limit_bytes=VMEM_LIMIT),
        name="swa_attention",
    )(qt, k4, k4, vtx, vtx, bias_t, sink_row)


def _memkv_kernel(mem_ref, g_ref, w_ref, k_ref, v_ref):
    h = _rms(mem_ref[...], g_ref[...]).astype(BF16)
    kv = _dot(h, w_ref[...])
    k_ref[...] = kv[:, 0:H_X * DH_X].astype(BF16)
    v_ref[...] = kv[:, H_X * DH_X:].astype(BF16)


def _memkv(mem, g, wkv):
    b, m, _ = mem.shape
    n = H_X * DH_X
    return pl.pallas_call(
        _memkv_kernel,
        out_shape=[jax.ShapeDtypeStruct((b, m, n), BF16), jax.ShapeDtypeStruct((b, m, n), BF16)],
        grid=(b,),
        in_specs=[pl.BlockSpec((None, m, D_MODEL), lambda i: (i, 0, 0)),
                  pl.BlockSpec((1, D_MODEL), lambda i: (0, 0)),
                  pl.BlockSpec((D_MODEL, 2 * n), lambda i: (0, 0))],
        out_specs=[pl.BlockSpec((None, m, n), lambda i: (i, 0, 0)), pl.BlockSpec((None, m, n), lambda i: (i, 0, 0))],
        compiler_params=pltpu.CompilerParams(vmem_limit_bytes=VMEM_LIMIT),
        name="mem_kv",
    )(mem, g.reshape(1, D_MODEL), wkv.astype(BF16))


def _lane_half_mask(rows, half):
    lane = lax.broadcasted_iota(jnp.int32, (rows, LANES), 1)
    return (lane < 64) if half == 0 else (lane >= 64)


def _tail_kernel(*refs, n_a):
    x_ref, g_ref = refs[0], refs[1]
    a_refs = refs[2:2 + n_a]
    wout_refs = refs[2 + n_a:2 + 2 * n_a]
    wq_ref, kt_ref, v_ref, wo_ref, w1_ref, w2_ref, o_ref = refs[2 + 2 * n_a:]
    tm = x_ref.shape[0]
    g = g_ref[...]
    x = x_ref[...]

    y = _dot(a_refs[0][...], wout_refs[0][...])
    for a_ref, w_ref in zip(a_refs[1:], wout_refs[1:]):
        y = y + _dot(a_ref[...], w_ref[...])
    x = x + _rms(y, g[1:2, :])

    hq = _rms(x, g[2:3, :]).astype(BF16)
    q = (_dot(hq, wq_ref[...]) * (DH_X ** -0.5)).astype(BF16)
    pairs = []
    for pr in range(H_X // 2):
        qp = q[:, pr * LANES:(pr + 1) * LANES]
        vp = v_ref[:, pr * LANES:(pr + 1) * LANES]
        out = jnp.zeros((tm, LANES), F32)
        for half in range(2):
            qz = jnp.where(_lane_half_mask(tm, half), qp, jnp.zeros_like(qp))
            s = _dot(qz, kt_ref[pr * LANES:(pr + 1) * LANES, :])
            m = jnp.max(s, axis=1, keepdims=True)
            e = jnp.exp(s - m)
            p = (e / jnp.sum(e, axis=1, keepdims=True)).astype(BF16)
            out = out + _dot(p, jnp.where(_lane_half_mask(vp.shape[0], half), vp, jnp.zeros_like(vp)))
        pairs.append(out.astype(BF16))
    o = jnp.concatenate(pairs, axis=1)
    x = x + _rms(_dot(o, wo_ref[...]), g[3:4, :])

    hm = _rms(x, g[4:5, :]).astype(BF16)
    y = jnp.zeros((tm, D_MODEL), F32)
    for c in range(D_FF // FF_CHUNK):
        a = jnp.maximum(_dot(hm, w1_ref[:, c * FF_CHUNK:(c + 1) * FF_CHUNK]), 0.0)
        y = y + _dot((a * a).astype(BF16), w2_ref[c * FF_CHUNK:(c + 1) * FF_CHUNK, :])
    o_ref[...] = x + _rms(y, g[5:6, :])


def _tail(x2, g, a_list, wout_list, wq, kt, v, wo, w1, w2, b, s):
    t = x2.shape[0]
    tm = min(TM_TAIL, s)
    per_b = s // tm
    n_a = len(a_list)
    row = lambda i: (i, 0)
    full2 = lambda i: (0, 0)
    once = dict(pipeline_mode=pl.Buffered(1))
    in_specs = [pl.BlockSpec((tm, D_MODEL), row), pl.BlockSpec((6, D_MODEL), full2)]
    in_specs += [pl.BlockSpec((tm, a.shape[1]), row) for a in a_list]
    in_specs += [pl.BlockSpec(w.shape, full2, **once) for w in wout_list]
    in_specs += [
        pl.BlockSpec(wq.shape, full2, **once),
        pl.BlockSpec((None,) + kt.shape[1:], lambda i: (i // per_b, 0, 0)),
        pl.BlockSpec((None,) + v.shape[1:], lambda i: (i // per_b, 0, 0)),
        pl.BlockSpec(wo.shape, full2, **once),
        pl.BlockSpec(w1.shape, full2, **once),
        pl.BlockSpec(w2.shape, full2, **once),
    ]
    return pl.pallas_call(
        functools.partial(_tail_kernel, n_a=n_a),
        out_shape=jax.ShapeDtypeStruct((t, D_MODEL), F32),
        grid=(t // tm,),
        in_specs=in_specs,
        out_specs=pl.BlockSpec((tm, D_MODEL), row),
        compiler_params=pltpu.CompilerParams(vmem_limit_bytes=VMEM_LIMIT),
        name="tail",
    )(x2, g, *a_list, *wout_list, wq, kt, v, wo, w1, w2)


def kernel(x, mem, rel_bias_table, norm_g, ev_w_in, ev_a_kv_norm, ev_a_w_uk, ev_a_w_uv, ev_b_lambda, ev_b_subln, ev_w_out, od_w_in, od_sinks, od_w_out, xa_wq, xa_wkv, xa_wo, xa_mem_norm, mlp_w1, mlp_w2):
    b, s, d = x.shape
    depth = norm_g.shape[0]
    assert d == D_MODEL and s % TK == 0 and TK % TQ_DIFF == 0 and TQ_DIFF % TQ == 0
    x2 = x.reshape(b * s, d)
    bias_a = _causal_bias_tiles(rel_bias_table[:, :H_A])
    bias_b = _causal_bias_tiles(rel_bias_table[:, H_A:H_A + H_B])
    table_c = rel_bias_table[:, H_A + H_B:]
    for l in range(depth):
        g = norm_g[l].astype(F32)
        if l % 2 == 0:
            e = l // 2
            lam_init = 0.8 - 0.6 * math.exp(-0.3 * l)
            qlat, ckv, qi, qb, kb, vb, ki, wi = _proj_even(x2, g, ev_w_in[e], ev_a_kv_norm[e], ev_a_w_uk[e])
            oa = _dsa(qi, wi, qlat, ki, ckv, bias_a, ev_a_w_uv[e], b, s)
            ob = _diff(qb, kb, vb, bias_b, ev_b_lambda[e], ev_b_subln[e], lam_init, b, s)
            n_a = H_A * DH_A
            a_list = [oa, ob]
            wout_list = [ev_w_out[e][:n_a].astype(BF16), ev_w_out[e][n_a:].astype(BF16)]
        else:
            o = l // 2
            q, k, v = _proj_odd(x2, g, od_w_in[o])
            a_list = [_swa(q, k, v, table_c, od_sinks[o], b, s)]
            wout_list = [od_w_out[o].astype(BF16)]
        mk, mv = _memkv(mem, xa_mem_norm[l], xa_wkv[l])
        x2 = _tail(x2, g, a_list, wout_list, xa_wq[l].astype(BF16), jnp.swapaxes(mk, 1, 2), mv,
                   xa_wo[l].astype(BF16), mlp_w1[l].astype(BF16), mlp_w2[l].astype(BF16), b, s)
    return x2.reshape(b, s, d)
```

```python
import functools
import math

import numpy as np
import jax
import jax.numpy as jnp
from jax import lax
from jax.experimental import pallas as pl
from jax.experimental.pallas import tpu as pltpu

F32 = jnp.float32
BF16 = jnp.bfloat16

D_MODEL = 1024
CHUNK = 64
EPS = 1e-6
NEG_INF = -1e30
LOG2E = math.log2(math.e)

H_A, DH_A, D_LAT = 8, 64, 128
H_IDX, DH_IDX = 8, 64
TOPK_MAX = 256
H_B, DH_B = 4, 64
H_C, H_C_KV, DH_C = 16, 2, 64
G_C = H_C // H_C_KV
WINDOW = 128
H_X, DH_X = 4, 64
D_FF = 4 * D_MODEL
N_BUCKETS = 32
MAX_DIST = 1024

LANES = 128
FOLD_ROWS = 64
TQ = 128
TQ_DIFF = 256
TK = 512
SUB = TK // LANES
ONES_ROWS = 16
TM_PROJ = 512
TM_TAIL = 512
TAIL_CHAINS = 2
FF_CHUNK = 1024
VMEM_LIMIT = 56 * 1024 * 1024
N_BISECT = 14
GROUPS = (4, 2, 1)
GROUPS_NEAR = (2, 1)
GROUPS_PASS = (4, 1)


def _rms(x, g):
    return x * lax.rsqrt(jnp.mean(x * x, axis=-1, keepdims=True) + EPS) * g


def _dot(a, b):
    return jnp.dot(a, b, preferred_element_type=F32)


def _t5_bucket_np(rel):
    nb = N_BUCKETS // 2
    max_exact = nb // 2
    n = np.abs(rel)
    nf = np.maximum(n, 1).astype(np.float32)
    large = max_exact + (np.log(nf / max_exact) / math.log(MAX_DIST / max_exact) * (nb - max_exact)).astype(np.int32)
    large = np.minimum(large, nb - 1)
    return np.where(rel > 0, nb, 0) + np.where(n < max_exact, n, large)


def _sat_blocks():
    d = 1
    while _t5_bucket_np(np.array([-(d * LANES - (LANES - 1))]))[0] != N_BUCKETS // 2 - 1:
        d += 1
    return d


N_SAT = _sat_blocks()
N_BT = N_SAT + 1


def _toeplitz(u, rows, cols):
    length = u.shape[-1]
    flat = jnp.tile(u, (1,) * (u.ndim - 1) + (rows,))[..., :rows * (length - 1)]
    return flat.reshape(u.shape[:-1] + (rows, length - 1))[..., :cols]


def _causal_bias_tiles(table_cols):
    length = 2 * LANES + 1
    m = np.arange(length)
    diff = np.where(m < LANES + 1, m, m - length)
    rel = -diff[None, :] - (np.arange(N_BT) * LANES)[:, None]
    tab = table_cols.astype(F32)
    u = jnp.take(tab, jnp.asarray(_t5_bucket_np(rel)), axis=0)
    u = (u - tab[N_BUCKETS // 2 - 1][None, None, :]) * LOG2E
    tiles = _toeplitz(jnp.moveaxis(u, -1, 0), LANES, LANES)
    return tiles.reshape(table_cols.shape[1] * N_BT, LANES, LANES)


def _fold_rows(x, op):
    acc = x[0:FOLD_ROWS, :]
    for r in range(1, x.shape[0] // FOLD_ROWS):
        acc = op(acc, x[r * FOLD_ROWS:(r + 1) * FOLD_ROWS, :])
    return acc


def _grouped_loop(lo, hi, carry, fn, groups):
    for g in groups:
        n = (hi - lo) // g
        carry = lax.fori_loop(0, n, lambda u, c, lo=lo, g=g: fn([lo + u * g + k for k in range(g)], c), carry)
        lo = lo + n * g
    return carry


def _bias_tile(bias_ref, h, qb0, n_qb, j):
    rows = []
    for c in range(SUB):
        cols = [bias_ref[h * N_BT + jnp.clip(qb0 + r - (j * SUB + c), 0, N_SAT)] for r in range(n_qb)]
        rows.append(cols[0] if n_qb == 1 else jnp.concatenate(cols, axis=1))
    return jnp.concatenate(rows, axis=0)


def _proj_even_kernel(x_ref, g_ref, w_ref, kvn_ref, wuk_ref,
                      qlat_ref, ckv_ref, qi_ref, qb_ref, kb_ref, vb_ref, ki_ref, wi_ref):
    h = _rms(x_ref[...], g_ref[0:1, :]).astype(BF16)
    z = _dot(h, w_ref[...])
    qa = z[:, 0:512].astype(BF16)
    for hh in range(H_A):
        ql = _dot(qa[:, hh * DH_A:(hh + 1) * DH_A], wuk_ref[hh]) * (DH_A ** -0.5 * LOG2E)
        qlat_ref[:, hh * D_LAT:(hh + 1) * D_LAT] = ql.astype(BF16)
    ckv_ref[...] = _rms(z[:, 512:640], kvn_ref[...]).astype(BF16)
    qi_ref[...] = z[:, 640:1152].astype(BF16)
    qb_ref[...] = (z[:, 1152:1664] * (DH_B ** -0.5 * LOG2E)).astype(BF16)
    kb_ref[...] = z[:, 1664:2176].astype(BF16)
    vb_ref[...] = z[:, 2176:2688].astype(BF16)
    ki_ref[...] = z[:, 2688:2688 + DH_IDX].astype(BF16)
    wi_ref[...] = z[:, 2816:2816 + H_IDX]


def _proj_even(x2, g, w_in, kvn, w_uk):
    t = x2.shape[0]
    tm = min(TM_PROJ, t)
    pad = jnp.zeros((D_MODEL, LANES - DH_IDX), w_in.dtype)
    cols = [w_in[:, 0:512], w_in[:, 512:640], w_in[:, 640:1152], w_in[:, 1224:1736], w_in[:, 1736:2248],
            w_in[:, 2248:2760], w_in[:, 1152:1216], pad, w_in[:, 1216:1224],
            jnp.zeros((D_MODEL, LANES - H_IDX), w_in.dtype)]
    w = jnp.concatenate(cols, axis=1).astype(BF16)
    n = w.shape[1]
    row = lambda i: (i, 0)
    full2 = lambda i: (0, 0)
    widths = [(H_A * D_LAT, BF16), (D_LAT, BF16), (512, BF16), (512, BF16), (512, BF16), (512, BF16),
              (DH_IDX, BF16), (H_IDX, F32)]
    return pl.pallas_call(
        _proj_even_kernel,
        out_shape=[jax.ShapeDtypeStruct((t, c), dt) for c, dt in widths],
        grid=(t // tm,),
        in_specs=[
            pl.BlockSpec((tm, D_MODEL), row),
            pl.BlockSpec((6, D_MODEL), full2),
            pl.BlockSpec((D_MODEL, n), full2),
            pl.BlockSpec((1, D_LAT), full2),
            pl.BlockSpec((H_A, DH_A, D_LAT), lambda i: (0, 0, 0)),
        ],
        out_specs=[pl.BlockSpec((tm, c), row) for c, _ in widths],
        compiler_params=pltpu.CompilerParams(vmem_limit_bytes=VMEM_LIMIT),
        name="proj_even",
    )(x2, g, w, kvn.reshape(1, D_LAT), w_uk.astype(BF16))


def _proj_odd_kernel(x_ref, g_ref, w_ref, q_ref, k_ref, v_ref):
    h = _rms(x_ref[...], g_ref[0:1, :]).astype(BF16)
    z = _dot(h, w_ref[...])
    q_ref[...] = (z[:, 0:1024] * (DH_C ** -0.5 * LOG2E)).astype(BF16)
    k_ref[...] = z[:, 1024:1152].astype(BF16)
    v_ref[...] = z[:, 1152:1280].astype(BF16)


def _proj_odd(x2, g, w_in):
    t = x2.shape[0]
    tm = min(TM_PROJ, t)
    row = lambda i: (i, 0)
    full2 = lambda i: (0, 0)
    return pl.pallas_call(
        _proj_odd_kernel,
        out_shape=[jax.ShapeDtypeStruct((t, 1024), BF16),
                   jax.ShapeDtypeStruct((t, 128), BF16),
                   jax.ShapeDtypeStruct((t, 128), BF16)],
        grid=(t // tm,),
        in_specs=[pl.BlockSpec((tm, D_MODEL), row),
                  pl.BlockSpec((6, D_MODEL), full2),
                  pl.BlockSpec((D_MODEL, 1280), full2)],
        out_specs=[pl.BlockSpec((tm, 1024), row), pl.BlockSpec((tm, 128), row), pl.BlockSpec((tm, 128), row)],
        compiler_params=pltpu.CompilerParams(vmem_limit_bytes=VMEM_LIMIT),
        name="proj_odd",
    )(x2, g, w_in.astype(BF16))


def _dsa_kernel(qit_ref, wit_ref, qlt_ref, k_ref, c_ref, ctx_ref, bias_ref, wuvt_ref, o_ref,
                sc_ref, acc_ref, *p_refs, topk):
    i = pl.program_id(1)
    t0 = i * TQ
    nt = (t0 + TQ + TK - 1) // TK
    n_far = jnp.maximum(i - N_SAT + 1, 0) // SUB
    kf = float(topk)

    qpos = lax.broadcasted_iota(jnp.int32, (1, TQ), 1) + t0
    limit = (qpos // CHUNK + 1) * CHUNK
    small = limit <= topk
    big = jnp.logical_not(small)
    w = wit_ref[...]
    qit = qit_ref[...]

    def score_tiles(js, carry):
        rmax, rmin = carry
        zs = [_dot(k_ref[j], qit) for j in js]
        for j, z in zip(js, zs):
            sc = jnp.maximum(z[:, 0:TQ], 0.0) * w[0:1, :]
            for h in range(1, H_IDX):
                sc = sc + jnp.maximum(z[:, h * TQ:(h + 1) * TQ], 0.0) * w[h:h + 1, :]
            kpos = lax.broadcasted_iota(jnp.int32, (TK, TQ), 0) + j * TK
            adm = kpos < limit
            sc_ref[j] = jnp.where(adm, sc, -jnp.inf)
            rmax = jnp.maximum(rmax, _fold_rows(jnp.where(adm, sc, -jnp.inf), jnp.maximum))
            rmin = jnp.minimum(rmin, _fold_rows(jnp.where(adm, sc, jnp.inf), jnp.minimum))
        return rmax, rmin

    rmax, rmin = _grouped_loop(
        0, nt, (jnp.full((FOLD_ROWS, TQ), -jnp.inf, F32), jnp.full((FOLD_ROWS, TQ), jnp.inf, F32)),
        score_tiles, GROUPS)
    hi0 = jnp.max(rmax, axis=0, keepdims=True)
    lo0 = jnp.min(rmin, axis=0, keepdims=True)

    def col_reduce(tile_fn, op, init, final):
        def fn(js, acc):
            for j in js:
                acc = op(acc, _fold_rows(tile_fn(j), op))
            return acc
        acc = _grouped_loop(0, nt, jnp.full((FOLD_ROWS, TQ), init, F32), fn, GROUPS_PASS)
        return final(acc, axis=0, keepdims=True)

    def count_ge(thr):
        return col_reduce(lambda j: jnp.where(sc_ref[j] >= thr, 1.0, 0.0), jnp.add, 0.0, jnp.sum)

    def count_gt(thr):
        return col_reduce(lambda j: jnp.where(sc_ref[j] > thr, 1.0, 0.0), jnp.add, 0.0, jnp.sum)

    def max_le(thr):
        def tile(j):
            t = sc_ref[j]
            return jnp.where(t <= thr, t, -jnp.inf)
        return col_reduce(tile, jnp.maximum, -jnp.inf, jnp.max)

    def max_lt(thr):
        def tile(j):
            t = sc_ref[j]
            return jnp.where(t < thr, t, -jnp.inf)
        return col_reduce(tile, jnp.maximum, -jnp.inf, jnp.max)

    def bisect_body(_, carry):
        lo, hi = carry
        mid = lo * 0.5 + hi * 0.5
        ge = count_ge(mid) >= kf
        return jnp.where(ge, mid, lo), jnp.where(ge, hi, mid)

    lo, hi = lax.fori_loop(0, N_BISECT, bisect_body, (lo0, hi0))

    v0 = max_le(hi)
    c0 = count_ge(v0)

    def walk_cond(carry):
        _, cnt, it = carry
        pending = jnp.where((cnt < kf) & big, 1.0, 0.0)
        return (jnp.max(pending) > 0.0) & (it < topk + 2)

    def walk_body(carry):
        v, cnt, it = carry
        v = jnp.where((cnt < kf) & big, max_lt(v), v)
        return v, count_ge(v), it + 1

    tau, cnt, _ = lax.while_loop(walk_cond, walk_body, (v0, c0, jnp.int32(0)))
    tau = jnp.where(small, -jnp.finfo(F32).max, tau)

    tied = (cnt > kf) & big

    @pl.when(jnp.max(jnp.where(tied, 1.0, 0.0)) > 0.0)
    def _():
        need = kf - count_gt(tau)
        n_keys = sc_ref.shape[0] * TK

        def kpos_f(j):
            return (lax.broadcasted_iota(jnp.int32, (TK, TQ), 0) + j * TK).astype(F32)

        def count_eq_le(pos):
            return col_reduce(lambda j: jnp.where((sc_ref[j] == tau) & (kpos_f(j) <= pos), 1.0, 0.0),
                              jnp.add, 0.0, jnp.sum)

        def pos_body(_, carry):
            lo_p, hi_p = carry
            mid = jnp.floor((lo_p + hi_p) * 0.5)
            ok = count_eq_le(mid) >= need
            return jnp.where(ok, lo_p, mid), jnp.where(ok, mid, hi_p)

        n_steps = int(math.ceil(math.log2(n_keys))) + 1
        _, cut = lax.fori_loop(0, n_steps, pos_body,
                               (jnp.full((1, TQ), -1.0, F32), jnp.full((1, TQ), float(n_keys), F32)))

        def drop_body(j, _):
            t = sc_ref[j]
            sc_ref[j] = jnp.where(tied & (t == tau) & (kpos_f(j) > cut), -jnp.inf, t)
            return 0

        lax.fori_loop(0, nt, drop_body, 0)

    acc_ref[...] = jnp.zeros(acc_ref.shape, F32)
    qlt = qlt_ref[...]

    def attn_tiles(js, m, with_bias):
        s_alls = [_dot(c_ref[j], qlt) for j in js]
        for g, (j, s_all) in enumerate(zip(js, s_alls)):
            p_ref = p_refs[g]
            sel = sc_ref[j] >= tau
            m_new, alpha = [], []
            for h in range(H_A):
                cols = slice(h * TQ, (h + 1) * TQ)
                s = s_all[:, cols]
                if with_bias:
                    s = s + _bias_tile(bias_ref, h, i, 1, j)
                s = jnp.where(sel, s, NEG_INF)
                m_old = m[:, cols]
                m_h = jnp.maximum(m_old, jnp.max(_fold_rows(s, jnp.maximum), axis=0, keepdims=True))
                p_ref[:, cols] = jnp.exp2(s - m_h).astype(BF16)
                alpha.append(jnp.exp2(m_old - m_h))
                m_new.append(m_h)
            acc_ref[...] = acc_ref[...] * jnp.concatenate(alpha, axis=1) + _dot(ctx_ref[j], p_ref[...])
            m = jnp.concatenate(m_new, axis=1)
        return m

    m = jnp.full((1, H_A * TQ), NEG_INF, F32)
    m = _grouped_loop(0, n_far, m, lambda js, mm: attn_tiles(js, mm, False), GROUPS)
    _grouped_loop(n_far, nt, m, lambda js, mm: attn_tiles(js, mm, True), GROUPS_NEAR)

    outs = []
    for h in range(H_A):
        cols = slice(h * TQ, (h + 1) * TQ)
        o_t = (acc_ref[0:D_LAT, cols] / acc_ref[D_LAT:D_LAT + 1, cols]).astype(BF16)
        outs.append(_dot(wuvt_ref[h], o_t))
    for pr in range(H_A // 2):
        pair = jnp.concatenate([outs[2 * pr], outs[2 * pr + 1]], axis=0)
        o_ref[:, pr * LANES:(pr + 1) * LANES] = pair.T.astype(BF16)


def _with_ones_rows(xt):
    return jnp.concatenate([xt, jnp.ones(xt.shape[:-2] + (ONES_ROWS, xt.shape[-1]), xt.dtype)], axis=-2)


def _dsa(qi, wi, qlat, ki, ckv, bias_a, w_uv, b, s):
    nq, nt = s // TQ, s // TK
    topk = min(TOPK_MAX, s // 4)
    qit = jnp.transpose(qi.reshape(b * nq, TQ, H_IDX, DH_IDX), (0, 3, 2, 1)).reshape(b * nq, DH_IDX, H_IDX * TQ)
    qlt = jnp.transpose(qlat.reshape(b * nq, TQ, H_A, D_LAT), (0, 3, 2, 1)).reshape(b * nq, D_LAT, H_A * TQ)
    wit = jnp.swapaxes(wi.reshape(b * nq, TQ, H_IDX), 1, 2)
    k4 = ki.reshape(b, nt, TK, DH_IDX)
    c4 = ckv.reshape(b, nt, TK, D_LAT)
    ctx = _with_ones_rows(jnp.swapaxes(c4, 2, 3))
    wuvt = jnp.swapaxes(w_uv, 1, 2).astype(BF16)
    qblk = lambda bb, i: (bb * nq + i, 0, 0)
    kv4 = lambda bb, i: (bb, 0, 0, 0)
    return pl.pallas_call(
        functools.partial(_dsa_kernel, topk=topk),
        out_shape=jax.ShapeDtypeStruct((b * s, H_A * DH_A), BF16),
        grid=(b, nq),
        in_specs=[
            pl.BlockSpec((None, DH_IDX, H_IDX * TQ), qblk),
            pl.BlockSpec((None, H_IDX, TQ), qblk),
            pl.BlockSpec((None, D_LAT, H_A * TQ), qblk),
            pl.BlockSpec((None, nt, TK, DH_IDX), kv4),
            pl.BlockSpec((None, nt, TK, D_LAT), kv4),
            pl.BlockSpec((None, nt, D_LAT + ONES_ROWS, TK), kv4),
            pl.BlockSpec((H_A * N_BT, LANES, LANES), lambda bb, i: (0, 0, 0)),
            pl.BlockSpec((H_A, DH_A, D_LAT), lambda bb, i: (0, 0, 0)),
        ],
        out_specs=pl.BlockSpec((TQ, H_A * DH_A), lambda bb, i: (bb * nq + i, 0)),
        scratch_shapes=[
            pltpu.VMEM((nt, TK, TQ), F32),
            pltpu.VMEM((D_LAT + ONES_ROWS, H_A * TQ), F32),
        ] + [pltpu.VMEM((TK, H_A * TQ), BF16)] * max(GROUPS),
        compiler_params=pltpu.CompilerParams(vmem_limit_bytes=VMEM_LIMIT),
        name="dsa_attention",
    )(qit, wit, qlt, k4, c4, ctx, bias_a, wuvt)


def _diff_kernel(qt_ref, k_ref, vtx_ref, bias_ref, lam_ref, sub_ref, o_ref, acc_ref, *p_refs, lam_init):
    h = pl.program_id(1)
    i = pl.program_id(2)
    tq = TQ_DIFF
    n_qb = tq // LANES
    t0 = i * tq
    nt = (t0 + tq + TK - 1) // TK
    n_far = jnp.maximum(i * n_qb - N_SAT + 1, 0) // SUB
    qt = qt_ref[...]
    top = lax.broadcasted_iota(jnp.int32, qt.shape, 0) < DH_B
    q2 = jnp.concatenate([jnp.where(top, qt, jnp.zeros_like(qt)), jnp.where(top, jnp.zeros_like(qt), qt)], axis=1)
    qpos = lax.broadcasted_iota(jnp.int32, (1, tq), 1) + t0
    limit = (qpos // CHUNK + 1) * CHUNK
    acc_ref[...] = jnp.zeros(acc_ref.shape, F32)

    def tiles(js, m, near):
        s2s = [_dot(k_ref[j], q2) for j in js]
        for g, (j, s2) in enumerate(zip(js, s2s)):
            p_ref = p_refs[g]
            if near:
                bias = _bias_tile(bias_ref, h, i * n_qb, n_qb, j)
                adm = (lax.broadcasted_iota(jnp.int32, (TK, tq), 0) + j * TK) < limit
            m_new, alpha = [], []
            for mp in range(2):
                cols = slice(mp * tq, (mp + 1) * tq)
                s = s2[:, cols]
                if near:
                    s = jnp.where(adm, s + bias, NEG_INF)
                m_old = m[:, cols]
                m_h = jnp.maximum(m_old, jnp.max(_fold_rows(s, jnp.maximum), axis=0, keepdims=True))
                p_ref[:, cols] = jnp.exp2(s - m_h).astype(BF16)
                alpha.append(jnp.exp2(m_old - m_h))
                m_new.append(m_h)
            acc_ref[...] = acc_ref[...] * jnp.concatenate(alpha, axis=1) + _dot(vtx_ref[j], p_ref[...])
            m = jnp.concatenate(m_new, axis=1)
        return m

    m = jnp.full((1, 2 * tq), NEG_INF, F32)
    m = _grouped_loop(0, n_far, m, lambda js, mm: tiles(js, mm, False), GROUPS)
    _grouped_loop(n_far, nt, m, lambda js, mm: tiles(js, mm, True), GROUPS_NEAR)

    lf = lam_ref[...]
    lam = (jnp.exp(jnp.sum(lf[0:1, :] * lf[1:2, :], axis=1, keepdims=True))
           - jnp.exp(jnp.sum(lf[2:3, :] * lf[3:4, :], axis=1, keepdims=True)) + lam_init)
    dv = 2 * DH_B
    o = (acc_ref[0:dv, 0:tq] / acc_ref[dv:dv + 1, 0:tq]
         - lam * (acc_ref[0:dv, tq:2 * tq] / acc_ref[dv:dv + 1, tq:2 * tq]))
    o = o * lax.rsqrt(jnp.mean(o * o, axis=0, keepdims=True) + EPS) * sub_ref[...] * (1.0 - lam_init)
    o_ref[...] = o.T.astype(BF16)


def _diff(qb, kb, vb, bias_b, b_lambda, b_subln, lam_init, b, s):
    tq = TQ_DIFF
    nq, nt = s // tq, s // TK
    dv = 2 * DH_B
    qt = jnp.transpose(qb.reshape(b, nq, tq, H_B, dv), (0, 3, 1, 4, 2))
    k4 = kb.reshape(b, nt, TK, H_B * dv)
    vtx = _with_ones_rows(jnp.transpose(vb.reshape(b, nt, TK, H_B, dv), (0, 3, 1, 4, 2)))
    return pl.pallas_call(
        functools.partial(_diff_kernel, lam_init=lam_init),
        out_shape=jax.ShapeDtypeStruct((b * s, H_B * dv), BF16),
        grid=(b, H_B, nq),
        in_specs=[
            pl.BlockSpec((None, None, None, dv, tq), lambda bb, h, i: (bb, h, i, 0, 0)),
            pl.BlockSpec((None, nt, TK, dv), lambda bb, h, i: (bb, 0, 0, h)),
            pl.BlockSpec((None, None, nt, dv + ONES_ROWS, TK), lambda bb, h, i: (bb, h, 0, 0, 0)),
            pl.BlockSpec((H_B * N_BT, LANES, LANES), lambda bb, h, i: (0, 0, 0)),
            pl.BlockSpec((4, DH_B), lambda bb, h, i: (0, 0)),
            pl.BlockSpec((dv, 1), lambda bb, h, i: (0, 0)),
        ],
        out_specs=pl.BlockSpec((tq, dv), lambda bb, h, i: (bb * nq + i, h)),
        scratch_shapes=[
            pltpu.VMEM((dv + ONES_ROWS, 2 * tq), F32),
        ] + [pltpu.VMEM((TK, 2 * tq), BF16)] * max(GROUPS),
        compiler_params=pltpu.CompilerParams(vmem_limit_bytes=VMEM_LIMIT),
        name="diff_attention",
    )(qt, k4, vtx, bias_b, b_lambda.astype(F32), b_subln.reshape(dv, 1).astype(F32))


def _swa_kernel(qt_ref, ka_ref, kb_ref, vta_ref, vtb_ref, bias_ref, sink_ref, o_ref):
    i = pl.program_id(1)
    pad_pen = jnp.where(i == 0, NEG_INF, 0.0)
    for kv in range(H_C_KV):
        qt = qt_ref[kv]
        s = jnp.concatenate([_dot(ka_ref[kv], qt) + pad_pen, _dot(kb_ref[kv], qt)], axis=0) + bias_ref[kv]
        sink = sink_ref[kv]
        m = jnp.maximum(jnp.max(s, axis=0, keepdims=True), sink)
        p = jnp.exp2(s - m).astype(BF16)
        acc = _dot(vta_ref[kv], p[0:TQ, :]) + _dot(vtb_ref[kv], p[TQ:2 * TQ, :])
        o = acc[0:DH_C, :] / (acc[DH_C:DH_C + 1, :] + jnp.exp2(sink - m))
        for pr in range(G_C // 2):
            pair = jnp.concatenate([o[:, (2 * pr) * TQ:(2 * pr + 1) * TQ], o[:, (2 * pr + 1) * TQ:(2 * pr + 2) * TQ]],
                                   axis=0)
            col = (kv * G_C + 2 * pr) * DH_C
            o_ref[:, col:col + LANES] = pair.T.astype(BF16)


def _swa(q, k, v, table_c, sinks, b, s):
    assert WINDOW == TQ and TQ % CHUNK == 0
    nq = s // TQ
    length = 3 * TQ + 1
    m = np.arange(length)
    diff = np.where(m < 2 * TQ + 1, m, m - length)
    u = jnp.take(table_c.astype(F32), jnp.asarray(_t5_bucket_np(diff - TQ)), axis=0) * LOG2E
    bias = _toeplitz(jnp.moveaxis(u, -1, 0), TQ, 2 * TQ)
    rblk = np.arange(TQ)[:, None] // CHUNK + WINDOW // CHUNK
    cblk = np.arange(2 * TQ)[None, :] // CHUNK
    valid = (cblk <= rblk) & (cblk >= rblk - WINDOW // CHUNK)
    bias = jnp.where(jnp.asarray(valid)[None], bias, NEG_INF)
    bias_t = jnp.transpose(bias.reshape(H_C_KV, G_C, TQ, 2 * TQ), (0, 3, 1, 2)).reshape(H_C_KV, 2 * TQ, G_C * TQ)
    sink_row = jnp.repeat(sinks.astype(F32).reshape(H_C_KV, 1, G_C) * LOG2E, TQ, axis=2)

    qt = jnp.transpose(q.reshape(b * nq, TQ, H_C_KV, G_C, DH_C), (0, 2, 4, 3, 1)).reshape(
        b * nq, H_C_KV, DH_C, G_C * TQ)
    kp = jnp.pad(k.reshape(b, s, H_C_KV, DH_C), ((0, 0), (TQ, 0), (0, 0), (0, 0)))
    vp = jnp.pad(v.reshape(b, s, H_C_KV, DH_C), ((0, 0), (TQ, 0), (0, 0), (0, 0)))
    k4 = jnp.swapaxes(kp, 1, 2)
    vtx = _with_ones_rows(jnp.transpose(vp, (0, 2, 3, 1)))
    dvx = DH_C + ONES_ROWS
    return pl.pallas_call(
        _swa_kernel,
        out_shape=jax.ShapeDtypeStruct((b * s, H_C * DH_C), BF16),
        grid=(b, nq),
        in_specs=[
            pl.BlockSpec((None, H_C_KV, DH_C, G_C * TQ), lambda bb, i: (bb * nq + i, 0, 0, 0)),
            pl.BlockSpec((None, H_C_KV, TQ, DH_C), lambda bb, i: (bb, 0, i, 0)),
            pl.BlockSpec((None, H_C_KV, TQ, DH_C), lambda bb, i: (bb, 0, i + 1, 0)),
            pl.BlockSpec((None, H_C_KV, dvx, TQ), lambda bb, i: (bb, 0, 0, i)),
            pl.BlockSpec((None, H_C_KV, dvx, TQ), lambda bb, i: (bb, 0, 0, i + 1)),
            pl.BlockSpec((H_C_KV, 2 * TQ, G_C * TQ), lambda bb, i: (0, 0, 0)),
            pl.BlockSpec((H_C_KV, 1, G_C * TQ), lambda bb, i: (0, 0, 0)),
        ],
        out_specs=pl.BlockSpec((TQ, H_C * DH_C), lambda bb, i: (bb * nq + i, 0)),
        compiler_params=pltpu.CompilerParams(vmem_limit_bytes=VMEM_LIMIT),
        name="swa_attention",
    )(qt, k4, k4, vtx, vtx, bias_t, sink_row)


def _memkv_kernel(mem_ref, g_ref, w_ref, k_ref, v_ref):
    h = _rms(mem_ref[...], g_ref[...]).astype(BF16)
    kv = _dot(h, w_ref[...])
    k_ref[...] = kv[:, 0:H_X * DH_X].astype(BF16)
    v_ref[...] = kv[:, H_X * DH_X:].astype(BF16)


def _memkv(mem, g, wkv):
    b, m, _ = mem.shape
    n = H_X * DH_X
    return pl.pallas_call(
        _memkv_kernel,
        out_shape=[jax.ShapeDtypeStruct((b, m, n), BF16), jax.ShapeDtypeStruct((b, m, n), BF16)],
        grid=(b,),
        in_specs=[pl.BlockSpec((None, m, D_MODEL), lambda i: (i, 0, 0)),
                  pl.BlockSpec((1, D_MODEL), lambda i: (0, 0)),
                  pl.BlockSpec((D_MODEL, 2 * n), lambda i: (0, 0))],
        out_specs=[pl.BlockSpec((None, m, n), lambda i: (i, 0, 0)), pl.BlockSpec((None, m, n), lambda i: (i, 0, 0))],
        compiler_params=pltpu.CompilerParams(vmem_limit_bytes=VMEM_LIMIT),
        name="mem_kv",
    )(mem, g.reshape(1, D_MODEL), wkv.astype(BF16))


def _lane_half_mask(rows, half):
    lane = lax.broadcasted_iota(jnp.int32, (rows, LANES), 1)
    return (lane < 64) if half == 0 else (lane >= 64)


def _tail_kernel(*refs, n_a):
    x_ref, g_ref = refs[0], refs[1]
    a_refs = refs[2:2 + n_a]
    wout_refs = refs[2 + n_a:2 + 2 * n_a]
    wq_ref, kt_ref, v_ref, wo_ref, w1_ref, w2_ref, o_ref = refs[2 + 2 * n_a:]
    tc = x_ref.shape[0] // TAIL_CHAINS
    chains = [slice(c * tc, (c + 1) * tc) for c in range(TAIL_CHAINS)]
    g = g_ref[...]

    ys = []
    for r in chains:
        y = _dot(a_refs[0][r, :], wout_refs[0][...])
        for a_ref, w_ref in zip(a_refs[1:], wout_refs[1:]):
            y = y + _dot(a_ref[r, :], w_ref[...])
        ys.append(y)
    xs = [x_ref[r, :] + _rms(y, g[1:2, :]) for r, y in zip(chains, ys)]

    hqs = [_rms(x, g[2:3, :]).astype(BF16) for x in xs]
    qs = [_dot(hq, wq_ref[...]) for hq in hqs]
    qs = [(q * (DH_X ** -0.5)).astype(BF16) for q in qs]
    pairs = [[] for _ in chains]
    for pr in range(H_X // 2):
        vp = v_ref[:, pr * LANES:(pr + 1) * LANES]
        outs = [jnp.zeros((tc, LANES), F32) for _ in chains]
        for half in range(2):
            qzs = [jnp.where(_lane_half_mask(tc, half), q[:, pr * LANES:(pr + 1) * LANES], jnp.zeros((tc, LANES), BF16))
                   for q in qs]
            ss = [_dot(qz, kt_ref[pr * LANES:(pr + 1) * LANES, :]) for qz in qzs]
            vz = jnp.where(_lane_half_mask(vp.shape[0], half), vp, jnp.zeros_like(vp))
            ps = []
            for sc in ss:
                e = jnp.exp(sc - jnp.max(sc, axis=1, keepdims=True))
                ps.append((e / jnp.sum(e, axis=1, keepdims=True)).astype(BF16))
            outs = [out + _dot(p, vz) for out, p in zip(outs, ps)]
        for c, out in enumerate(outs):
            pairs[c].append(out.astype(BF16))
    ys = [_dot(jnp.concatenate(pc, axis=1), wo_ref[...]) for pc in pairs]
    xs = [x + _rms(y, g[3:4, :]) for x, y in zip(xs, ys)]

    hms = [_rms(x, g[4:5, :]).astype(BF16) for x in xs]
    n_chunks = D_FF // FF_CHUNK
    ys = [jnp.zeros((tc, D_MODEL), F32) for _ in chains]
    a_cur = [_dot(hm, w1_ref[:, 0:FF_CHUNK]) for hm in hms]
    for k in range(n_chunks):
        if k + 1 < n_chunks:
            a_next = [_dot(hm, w1_ref[:, (k + 1) * FF_CHUNK:(k + 2) * FF_CHUNK]) for hm in hms]
        acts = [jnp.square(jnp.maximum(a, 0.0)).astype(BF16) for a in a_cur]
        ys = [y + _dot(act, w2_ref[k * FF_CHUNK:(k + 1) * FF_CHUNK, :]) for y, act in zip(ys, acts)]
        if k + 1 < n_chunks:
            a_cur = a_next
    for r, x, y in zip(chains, xs, ys):
        o_ref[r, :] = x + _rms(y, g[5:6, :])


def _tail(x2, g, a_list, wout_list, wq, kt, v, wo, w1, w2, b, s):
    t = x2.shape[0]
    tm = min(TM_TAIL, s)
    per_b = s // tm
    n_a = len(a_list)
    row = lambda i: (i, 0)
    full2 = lambda i: (0, 0)
    once = dict(pipeline_mode=pl.Buffered(1))
    in_specs = [pl.BlockSpec((tm, D_MODEL), row), pl.BlockSpec((6, D_MODEL), full2)]
    in_specs += [pl.BlockSpec((tm, a.shape[1]), row) for a in a_list]
    in_specs += [pl.BlockSpec(w.shape, full2, **once) for w in wout_list]
    in_specs += [
        pl.BlockSpec(wq.shape, full2, **once),
        pl.BlockSpec((None,) + kt.shape[1:], lambda i: (i // per_b, 0, 0)),
        pl.BlockSpec((None,) + v.shape[1:], lambda i: (i // per_b, 0, 0)),
        pl.BlockSpec(wo.shape, full2, **once),
        pl.BlockSpec(w1.shape, full2, **once),
        pl.BlockSpec(w2.shape, full2, **once),
    ]
    return pl.pallas_call(
        functools.partial(_tail_kernel, n_a=n_a),
        out_shape=jax.ShapeDtypeStruct((t, D_MODEL), F32),
        grid=(t // tm,),
        in_specs=in_specs,
        out_specs=pl.BlockSpec((tm, D_MODEL), row),
        compiler_params=pltpu.CompilerParams(vmem_limit_bytes=VMEM_LIMIT),
        name="tail",
    )(x2, g, *a_list, *wout_list, wq, kt, v, wo, w1, w2)


def kernel(x, mem, rel_bias_table, norm_g, ev_w_in, ev_a_kv_norm, ev_a_w_uk, ev_a_w_uv, ev_b_lambda, ev_b_subln, ev_w_out, od_w_in, od_sinks, od_w_out, xa_wq, xa_wkv, xa_wo, xa_mem_norm, mlp_w1, mlp_w2):
    b, s, d = x.shape
    depth = norm_g.shape[0]
    assert d == D_MODEL and s % TK == 0 and TK % TQ_DIFF == 0 and TQ_DIFF % TQ == 0
    x2 = x.reshape(b * s, d)
    bias_a = _causal_bias_tiles(rel_bias_table[:, :H_A])
    bias_b = _causal_bias_tiles(rel_bias_table[:, H_A:H_A + H_B])
    table_c = rel_bias_table[:, H_A + H_B:]
    for l in range(depth):
        g = norm_g[l].astype(F32)
        if l % 2 == 0:
            e = l // 2
            lam_init = 0.8 - 0.6 * math.exp(-0.3 * l)
            qlat, ckv, qi, qb, kb, vb, ki, wi = _proj_even(x2, g, ev_w_in[e], ev_a_kv_norm[e], ev_a_w_uk[e])
            oa = _dsa(qi, wi, qlat, ki, ckv, bias_a, ev_a_w_uv[e], b, s)
            ob = _diff(qb, kb, vb, bias_b, ev_b_lambda[e], ev_b_subln[e], lam_init, b, s)
            n_a = H_A * DH_A
            a_list = [oa, ob]
            wout_list = [ev_w_out[e][:n_a].astype(BF16), ev_w_out[e][n_a:].astype(BF16)]
        else:
            o = l // 2
            q, k, v = _proj_odd(x2, g, od_w_in[o])
            a_list = [_swa(q, k, v, table_c, od_sinks[o], b, s)]
            wout_list = [od_w_out[o].astype(BF16)]
        mk, mv = _memkv(mem, xa_mem_norm[l], xa_wkv[l])
        x2 = _tail(x2, g, a_list, wout_list, xa_wq[l].astype(BF16), jnp.swapaxes(mk, 1, 2), mv,
                   xa_wo[l].astype(BF16), mlp_w1[l].astype(BF16), mlp_w2[l].astype(BF16), b, s)
    return x2.reshape(b, s, d)
```

```python
import functools
import math

import numpy as np
import jax
import jax.numpy as jnp
from jax import lax
from jax.experimental import pallas as pl
from jax.experimental.pallas import tpu as pltpu

F32 = jnp.float32
BF16 = jnp.bfloat16

D_MODEL = 1024
CHUNK = 64
EPS = 1e-6
NEG_INF = -1e30
LOG2E = math.log2(math.e)

H_A, DH_A, D_LAT = 8, 64, 128
H_IDX, DH_IDX = 8, 64
TOPK_MAX = 256
H_B, DH_B = 4, 64
H_C, H_C_KV, DH_C = 16, 2, 64
G_C = H_C // H_C_KV
WINDOW = 128
H_X, DH_X = 4, 64
D_FF = 4 * D_MODEL
N_BUCKETS = 32
MAX_DIST = 1024

LANES = 128
FOLD_ROWS = 64
TQ = 128
TQ_DIFF = 256
TK = 512
SUB = TK // LANES
ONES_ROWS = 16
TM_PROJ = 512
TM_TAIL = 512
TAIL_CHAINS = 2
FF_CHUNK = 1024
VMEM_LIMIT = 56 * 1024 * 1024
N_BISECT = 28
GROUPS = (4, 2, 1)
GROUPS_NEAR = (2, 1)
GROUPS_PASS = (4, 1)


def _rms(x, g):
    return x * lax.rsqrt(jnp.mean(x * x, axis=-1, keepdims=True) + EPS) * g


def _dot(a, b):
    return jnp.dot(a, b, preferred_element_type=F32)


def _t5_bucket_np(rel):
    nb = N_BUCKETS // 2
    max_exact = nb // 2
    n = np.abs(rel)
    nf = np.maximum(n, 1).astype(np.float32)
    large = max_exact + (np.log(nf / max_exact) / math.log(MAX_DIST / max_exact) * (nb - max_exact)).astype(np.int32)
    large = np.minimum(large, nb - 1)
    return np.where(rel > 0, nb, 0) + np.where(n < max_exact, n, large)


def _sat_blocks():
    d = 1
    while _t5_bucket_np(np.array([-(d * LANES - (LANES - 1))]))[0] != N_BUCKETS // 2 - 1:
        d += 1
    return d


N_SAT = _sat_blocks()
N_BT = N_SAT + 1


def _toeplitz(u, rows, cols):
    length = u.shape[-1]
    flat = jnp.tile(u, (1,) * (u.ndim - 1) + (rows,))[..., :rows * (length - 1)]
    return flat.reshape(u.shape[:-1] + (rows, length - 1))[..., :cols]


def _causal_bias_tiles(table_cols):
    length = 2 * LANES + 1
    m = np.arange(length)
    diff = np.where(m < LANES + 1, m, m - length)
    rel = -diff[None, :] - (np.arange(N_BT) * LANES)[:, None]
    tab = table_cols.astype(F32)
    u = jnp.take(tab, jnp.asarray(_t5_bucket_np(rel)), axis=0)
    u = (u - tab[N_BUCKETS // 2 - 1][None, None, :]) * LOG2E
    tiles = _toeplitz(jnp.moveaxis(u, -1, 0), LANES, LANES)
    return tiles.reshape(table_cols.shape[1] * N_BT, LANES, LANES)


def _fold_rows(x, op):
    acc = x[0:FOLD_ROWS, :]
    for r in range(1, x.shape[0] // FOLD_ROWS):
        acc = op(acc, x[r * FOLD_ROWS:(r + 1) * FOLD_ROWS, :])
    return acc


def _grouped_loop(lo, hi, carry, fn, groups):
    for g in groups:
        n = (hi - lo) // g
        carry = lax.fori_loop(0, n, lambda u, c, lo=lo, g=g: fn([lo + u * g + k for k in range(g)], c), carry)
        lo = lo + n * g
    return carry


def _bias_tile(bias_ref, h, qb0, n_qb, j):
    rows = []
    for c in range(SUB):
        cols = [bias_ref[h * N_BT + jnp.clip(qb0 + r - (j * SUB + c), 0, N_SAT)] for r in range(n_qb)]
        rows.append(cols[0] if n_qb == 1 else jnp.concatenate(cols, axis=1))
    return jnp.concatenate(rows, axis=0)


def _proj_even_kernel(x_ref, g_ref, w_ref, kvn_ref, wuk_ref,
                      qlat_ref, ckv_ref, qi_ref, qb_ref, kb_ref, vb_ref, ki_ref, wi_ref):
    h = _rms(x_ref[...], g_ref[0:1, :]).astype(BF16)
    z = _dot(h, w_ref[...])
    qa = z[:, 0:512].astype(BF16)
    for hh in range(H_A):
        ql = _dot(qa[:, hh * DH_A:(hh + 1) * DH_A], wuk_ref[hh]) * (DH_A ** -0.5 * LOG2E)
        qlat_ref[:, hh * D_LAT:(hh + 1) * D_LAT] = ql.astype(BF16)
    ckv_ref[...] = _rms(z[:, 512:640], kvn_ref[...]).astype(BF16)
    qi_ref[...] = z[:, 640:1152].astype(BF16)
    qb_ref[...] = (z[:, 1152:1664] * (DH_B ** -0.5 * LOG2E)).astype(BF16)
    kb_ref[...] = z[:, 1664:2176].astype(BF16)
    vb_ref[...] = z[:, 2176:2688].astype(BF16)
    ki_ref[...] = z[:, 2688:2688 + DH_IDX].astype(BF16)
    wi_ref[...] = z[:, 2816:2816 + H_IDX]


def _proj_even(x2, g, w_in, kvn, w_uk):
    t = x2.shape[0]
    tm = min(TM_PROJ, t)
    pad = jnp.zeros((D_MODEL, LANES - DH_IDX), w_in.dtype)
    cols = [w_in[:, 0:512], w_in[:, 512:640], w_in[:, 640:1152], w_in[:, 1224:1736], w_in[:, 1736:2248],
            w_in[:, 2248:2760], w_in[:, 1152:1216], pad, w_in[:, 1216:1224],
            jnp.zeros((D_MODEL, LANES - H_IDX), w_in.dtype)]
    w = jnp.concatenate(cols, axis=1).astype(BF16)
    n = w.shape[1]
    row = lambda i: (i, 0)
    full2 = lambda i: (0, 0)
    widths = [(H_A * D_LAT, BF16), (D_LAT, BF16), (512, BF16), (512, BF16), (512, BF16), (512, BF16),
              (DH_IDX, BF16), (H_IDX, F32)]
    return pl.pallas_call(
        _proj_even_kernel,
        out_shape=[jax.ShapeDtypeStruct((t, c), dt) for c, dt in widths],
        grid=(t // tm,),
        in_specs=[
            pl.BlockSpec((tm, D_MODEL), row),
            pl.BlockSpec((6, D_MODEL), full2),
            pl.BlockSpec((D_MODEL, n), full2),
            pl.BlockSpec((1, D_LAT), full2),
            pl.BlockSpec((H_A, DH_A, D_LAT), lambda i: (0, 0, 0)),
        ],
        out_specs=[pl.BlockSpec((tm, c), row) for c, _ in widths],
        compiler_params=pltpu.CompilerParams(vmem_limit_bytes=VMEM_LIMIT),
        name="proj_even",
    )(x2, g, w, kvn.reshape(1, D_LAT), w_uk.astype(BF16))


def _proj_odd_kernel(x_ref, g_ref, w_ref, q_ref, k_ref, v_ref):
    h = _rms(x_ref[...], g_ref[0:1, :]).astype(BF16)
    z = _dot(h, w_ref[...])
    q_ref[...] = (z[:, 0:1024] * (DH_C ** -0.5 * LOG2E)).astype(BF16)
    k_ref[...] = z[:, 1024:1152].astype(BF16)
    v_ref[...] = z[:, 1152:1280].astype(BF16)


def _proj_odd(x2, g, w_in):
    t = x2.shape[0]
    tm = min(TM_PROJ, t)
    row = lambda i: (i, 0)
    full2 = lambda i: (0, 0)
    return pl.pallas_call(
        _proj_odd_kernel,
        out_shape=[jax.ShapeDtypeStruct((t, 1024), BF16),
                   jax.ShapeDtypeStruct((t, 128), BF16),
                   jax.ShapeDtypeStruct((t, 128), BF16)],
        grid=(t // tm,),
        in_specs=[pl.BlockSpec((tm, D_MODEL), row),
                  pl.BlockSpec((6, D_MODEL), full2),
                  pl.BlockSpec((D_MODEL, 1280), full2)],
        out_specs=[pl.BlockSpec((tm, 1024), row), pl.BlockSpec((tm, 128), row), pl.BlockSpec((tm, 128), row)],
        compiler_params=pltpu.CompilerParams(vmem_limit_bytes=VMEM_LIMIT),
        name="proj_odd",
    )(x2, g, w_in.astype(BF16))


def _dsa_kernel(qit_ref, wit_ref, qlt_ref, k_ref, c_ref, ctx_ref, bias_ref, wuvt_ref, o_ref,
                sc_ref, acc_ref, *p_refs, topk):
    i = pl.program_id(1)
    t0 = i * TQ
    nt = (t0 + TQ + TK - 1) // TK
    n_far = jnp.maximum(i - N_SAT + 1, 0) // SUB
    kf = float(topk)

    qpos = lax.broadcasted_iota(jnp.int32, (1, TQ), 1) + t0
    limit = (qpos // CHUNK + 1) * CHUNK
    small = limit <= topk
    big = jnp.logical_not(small)
    w = wit_ref[...]
    qit = qit_ref[...]

    def score_tiles(js, carry):
        rmax, rmin = carry
        zs = [_dot(k_ref[j], qit) for j in js]
        for j, z in zip(js, zs):
            sc = jnp.maximum(z[:, 0:TQ], 0.0) * w[0:1, :]
            for h in range(1, H_IDX):
                sc = sc + jnp.maximum(z[:, h * TQ:(h + 1) * TQ], 0.0) * w[h:h + 1, :]
            kpos = lax.broadcasted_iota(jnp.int32, (TK, TQ), 0) + j * TK
            adm = kpos < limit
            sc_ref[j] = jnp.where(adm, sc, -jnp.inf)
            rmax = jnp.maximum(rmax, _fold_rows(jnp.where(adm, sc, -jnp.inf), jnp.maximum))
            rmin = jnp.minimum(rmin, _fold_rows(jnp.where(adm, sc, jnp.inf), jnp.minimum))
        return rmax, rmin

    rmax, rmin = _grouped_loop(
        0, nt, (jnp.full((FOLD_ROWS, TQ), -jnp.inf, F32), jnp.full((FOLD_ROWS, TQ), jnp.inf, F32)),
        score_tiles, GROUPS)
    hi0 = jnp.max(rmax, axis=0, keepdims=True)
    lo0 = jnp.min(rmin, axis=0, keepdims=True)

    def col_reduce(tile_fn, op, init, final):
        def fn(js, acc):
            for j in js:
                acc = op(acc, _fold_rows(tile_fn(j), op))
            return acc
        acc = _grouped_loop(0, nt, jnp.full((FOLD_ROWS, TQ), init, F32), fn, GROUPS_PASS)
        return final(acc, axis=0, keepdims=True)

    def count_ge(thr):
        return col_reduce(lambda j: jnp.where(sc_ref[j] >= thr, 1.0, 0.0), jnp.add, 0.0, jnp.sum)

    def count_gt(thr):
        return col_reduce(lambda j: jnp.where(sc_ref[j] > thr, 1.0, 0.0), jnp.add, 0.0, jnp.sum)

    def max_le(thr):
        def tile(j):
            t = sc_ref[j]
            return jnp.where(t <= thr, t, -jnp.inf)
        return col_reduce(tile, jnp.maximum, -jnp.inf, jnp.max)

    def max_lt(thr):
        def tile(j):
            t = sc_ref[j]
            return jnp.where(t < thr, t, -jnp.inf)
        return col_reduce(tile, jnp.maximum, -jnp.inf, jnp.max)

    def bisect_body(_, carry):
        lo, hi = carry
        mid = lo * 0.5 + hi * 0.5
        ge = count_ge(mid) >= kf
        return jnp.where(ge, mid, lo), jnp.where(ge, hi, mid)

    lo, hi = lax.fori_loop(0, N_BISECT, bisect_body, (lo0, hi0))

    v0 = max_le(hi)
    c0 = count_ge(v0)

    def walk_cond(carry):
        _, cnt, it = carry
        pending = jnp.where((cnt < kf) & big, 1.0, 0.0)
        return (jnp.max(pending) > 0.0) & (it < topk + 2)

    def walk_body(carry):
        v, cnt, it = carry
        v = jnp.where((cnt < kf) & big, max_lt(v), v)
        return v, count_ge(v), it + 1

    tau, cnt, _ = lax.while_loop(walk_cond, walk_body, (v0, c0, jnp.int32(0)))
    tau = jnp.where(small, -jnp.finfo(F32).max, tau)

    tied = (cnt > kf) & big

    @pl.when(jnp.max(jnp.where(tied, 1.0, 0.0)) > 0.0)
    def _():
        need = kf - count_gt(tau)
        n_keys = sc_ref.shape[0] * TK

        def kpos_f(j):
            return (lax.broadcasted_iota(jnp.int32, (TK, TQ), 0) + j * TK).astype(F32)

        def count_eq_le(pos):
            return col_reduce(lambda j: jnp.where((sc_ref[j] == tau) & (kpos_f(j) <= pos), 1.0, 0.0),
                              jnp.add, 0.0, jnp.sum)

        def pos_body(_, carry):
            lo_p, hi_p = carry
            mid = jnp.floor((lo_p + hi_p) * 0.5)
            ok = count_eq_le(mid) >= need
            return jnp.where(ok, lo_p, mid), jnp.where(ok, mid, hi_p)

        n_steps = int(math.ceil(math.log2(n_keys))) + 1
        _, cut = lax.fori_loop(0, n_steps, pos_body,
                               (jnp.full((1, TQ), -1.0, F32), jnp.full((1, TQ), float(n_keys), F32)))

        def drop_body(j, _):
            t = sc_ref[j]
            sc_ref[j] = jnp.where(tied & (t == tau) & (kpos_f(j) > cut), -jnp.inf, t)
            return 0

        lax.fori_loop(0, nt, drop_body, 0)

    acc_ref[...] = jnp.zeros(acc_ref.shape, F32)
    qlt = qlt_ref[...]

    def attn_tiles(js, m, with_bias):
        s_alls = [_dot(c_ref[j], qlt) for j in js]
        for g, (j, s_all) in enumerate(zip(js, s_alls)):
            p_ref = p_refs[g]
            sel = sc_ref[j] >= tau
            m_new, alpha = [], []
            for h in range(H_A):
                cols = slice(h * TQ, (h + 1) * TQ)
                s = s_all[:, cols]
                if with_bias:
                    s = s + _bias_tile(bias_ref, h, i, 1, j)
                s = jnp.where(sel, s, NEG_INF)
                m_old = m[:, cols]
                m_h = jnp.maximum(m_old, jnp.max(_fold_rows(s, jnp.maximum), axis=0, keepdims=True))
                p_ref[:, cols] = jnp.exp2(s - m_h).astype(BF16)
                alpha.append(jnp.exp2(m_old - m_h))
                m_new.append(m_h)
            acc_ref[...] = acc_ref[...] * jnp.concatenate(alpha, axis=1) + _dot(ctx_ref[j], p_ref[...])
            m = jnp.concatenate(m_new, axis=1)
        return m

    m = jnp.full((1, H_A * TQ), NEG_INF, F32)
    m = _grouped_loop(0, n_far, m, lambda js, mm: attn_tiles(js, mm, False), GROUPS)
    _grouped_loop(n_far, nt, m, lambda js, mm: attn_tiles(js, mm, True), GROUPS_NEAR)

    outs = []
    for h in range(H_A):
        cols = slice(h * TQ, (h + 1) * TQ)
        o_t = (acc_ref[0:D_LAT, cols] / acc_ref[D_LAT:D_LAT + 1, cols]).astype(BF16)
        outs.append(_dot(wuvt_ref[h], o_t))
    for pr in range(H_A // 2):
        pair = jnp.concatenate([outs[2 * pr], outs[2 * pr + 1]], axis=0)
        o_ref[:, pr * LANES:(pr + 1) * LANES] = pair.T.astype(BF16)


def _with_ones_rows(xt):
    return jnp.concatenate([xt, jnp.ones(xt.shape[:-2] + (ONES_ROWS, xt.shape[-1]), xt.dtype)], axis=-2)


def _dsa(qi, wi, qlat, ki, ckv, bias_a, w_uv, b, s):
    nq, nt = s // TQ, s // TK
    topk = min(TOPK_MAX, s // 4)
    qit = jnp.transpose(qi.reshape(b * nq, TQ, H_IDX, DH_IDX), (0, 3, 2, 1)).reshape(b * nq, DH_IDX, H_IDX * TQ)
    qlt = jnp.transpose(qlat.reshape(b * nq, TQ, H_A, D_LAT), (0, 3, 2, 1)).reshape(b * nq, D_LAT, H_A * TQ)
    wit = jnp.swapaxes(wi.reshape(b * nq, TQ, H_IDX), 1, 2)
    k4 = ki.reshape(b, nt, TK, DH_IDX)
    c4 = ckv.reshape(b, nt, TK, D_LAT)
    ctx = _with_ones_rows(jnp.swapaxes(c4, 2, 3))
    wuvt = jnp.swapaxes(w_uv, 1, 2).astype(BF16)
    qblk = lambda bb, i: (bb * nq + i, 0, 0)
    kv4 = lambda bb, i: (bb, 0, 0, 0)
    return pl.pallas_call(
        functools.partial(_dsa_kernel, topk=topk),
        out_shape=jax.ShapeDtypeStruct((b * s, H_A * DH_A), BF16),
        grid=(b, nq),
        in_specs=[
            pl.BlockSpec((None, DH_IDX, H_IDX * TQ), qblk),
            pl.BlockSpec((None, H_IDX, TQ), qblk),
            pl.BlockSpec((None, D_LAT, H_A * TQ), qblk),
            pl.BlockSpec((None, nt, TK, DH_IDX), kv4),
            pl.BlockSpec((None, nt, TK, D_LAT), kv4),
            pl.BlockSpec((None, nt, D_LAT + ONES_ROWS, TK), kv4),
            pl.BlockSpec((H_A * N_BT, LANES, LANES), lambda bb, i: (0, 0, 0)),
            pl.BlockSpec((H_A, DH_A, D_LAT), lambda bb, i: (0, 0, 0)),
        ],
        out_specs=pl.BlockSpec((TQ, H_A * DH_A), lambda bb, i: (bb * nq + i, 0)),
        scratch_shapes=[
            pltpu.VMEM((nt, TK, TQ), F32),
            pltpu.VMEM((D_LAT + ONES_ROWS, H_A * TQ), F32),
        ] + [pltpu.VMEM((TK, H_A * TQ), BF16)] * max(GROUPS),
        compiler_params=pltpu.CompilerParams(vmem_limit_bytes=VMEM_LIMIT),
        name="dsa_attention",
    )(qit, wit, qlt, k4, c4, ctx, bias_a, wuvt)


def _diff_kernel(qt_ref, k_ref, vtx_ref, bias_ref, lam_ref, sub_ref, o_ref, acc_ref, *p_refs, lam_init):
    h = pl.program_id(1)
    i = pl.program_id(2)
    tq = TQ_DIFF
    n_qb = tq // LANES
    t0 = i * tq
    nt = (t0 + tq + TK - 1) // TK
    n_far = jnp.maximum(i * n_qb - N_SAT + 1, 0) // SUB
    qt = qt_ref[...]
    top = lax.broadcasted_iota(jnp.int32, qt.shape, 0) < DH_B
    q2 = jnp.concatenate([jnp.where(top, qt, jnp.zeros_like(qt)), jnp.where(top, jnp.zeros_like(qt), qt)], axis=1)
    qpos = lax.broadcasted_iota(jnp.int32, (1, tq), 1) + t0
    limit = (qpos // CHUNK + 1) * CHUNK
    acc_ref[...] = jnp.zeros(acc_ref.shape, F32)

    def tiles(js, m, near):
        s2s = [_dot(k_ref[j], q2) for j in js]
        for g, (j, s2) in enumerate(zip(js, s2s)):
            p_ref = p_refs[g]
            if near:
                bias = _bias_tile(bias_ref, h, i * n_qb, n_qb, j)
                adm = (lax.broadcasted_iota(jnp.int32, (TK, tq), 0) + j * TK) < limit
            m_new, alpha = [], []
            for mp in range(2):
                cols = slice(mp * tq, (mp + 1) * tq)
                s = s2[:, cols]
                if near:
                    s = jnp.where(adm, s + bias, NEG_INF)
                m_old = m[:, cols]
                m_h = jnp.maximum(m_old, jnp.max(_fold_rows(s, jnp.maximum), axis=0, keepdims=True))
                p_ref[:, cols] = jnp.exp2(s - m_h).astype(BF16)
                alpha.append(jnp.exp2(m_old - m_h))
                m_new.append(m_h)
            acc_ref[...] = acc_ref[...] * jnp.concatenate(alpha, axis=1) + _dot(vtx_ref[j], p_ref[...])
            m = jnp.concatenate(m_new, axis=1)
        return m

    m = jnp.full((1, 2 * tq), NEG_INF, F32)
    m = _grouped_loop(0, n_far, m, lambda js, mm: tiles(js, mm, False), GROUPS)
    _grouped_loop(n_far, nt, m, lambda js, mm: tiles(js, mm, True), GROUPS_NEAR)

    lf = lam_ref[...]
    lam = (jnp.exp(jnp.sum(lf[0:1, :] * lf[1:2, :], axis=1, keepdims=True))
           - jnp.exp(jnp.sum(lf[2:3, :] * lf[3:4, :], axis=1, keepdims=True)) + lam_init)
    dv = 2 * DH_B
    o = (acc_ref[0:dv, 0:tq] / acc_ref[dv:dv + 1, 0:tq]
         - lam * (acc_ref[0:dv, tq:2 * tq] / acc_ref[dv:dv + 1, tq:2 * tq]))
    o = o * lax.rsqrt(jnp.mean(o * o, axis=0, keepdims=True) + EPS) * sub_ref[...] * (1.0 - lam_init)
    o_ref[...] = o.T.astype(BF16)


def _diff(qb, kb, vb, bias_b, b_lambda, b_subln, lam_init, b, s):
    tq = TQ_DIFF
    nq, nt = s // tq, s // TK
    dv = 2 * DH_B
    qt = jnp.transpose(qb.reshape(b, nq, tq, H_B, dv), (0, 3, 1, 4, 2))
    k4 = kb.reshape(b, nt, TK, H_B * dv)
    vtx = _with_ones_rows(jnp.transpose(vb.reshape(b, nt, TK, H_B, dv), (0, 3, 1, 4, 2)))
    return pl.pallas_call(
        functools.partial(_diff_kernel, lam_init=lam_init),
        out_shape=jax.ShapeDtypeStruct((b * s, H_B * dv), BF16),
        grid=(b, H_B, nq),
        in_specs=[
            pl.BlockSpec((None, None, None, dv, tq), lambda bb, h, i: (bb, h, i, 0, 0)),
            pl.BlockSpec((None, nt, TK, dv), lambda bb, h, i: (bb, 0, 0, h)),
            pl.BlockSpec((None, None, nt, dv + ONES_ROWS, TK), lambda bb, h, i: (bb, h, 0, 0, 0)),
            pl.BlockSpec((H_B * N_BT, LANES, LANES), lambda bb, h, i: (0, 0, 0)),
            pl.BlockSpec((4, DH_B), lambda bb, h, i: (0, 0)),
            pl.BlockSpec((dv, 1), lambda bb, h, i: (0, 0)),
        ],
        out_specs=pl.BlockSpec((tq, dv), lambda bb, h, i: (bb * nq + i, h)),
        scratch_shapes=[
            pltpu.VMEM((dv + ONES_ROWS, 2 * tq), F32),
        ] + [pltpu.VMEM((TK, 2 * tq), BF16)] * max(GROUPS),
        compiler_params=pltpu.CompilerParams(vmem_limit_bytes=VMEM_LIMIT),
        name="diff_attention",
    )(qt, k4, vtx, bias_b, b_lambda.astype(F32), b_subln.reshape(dv, 1).astype(F32))


def _swa_kernel(qt_ref, ka_ref, kb_ref, vta_ref, vtb_ref, bias_ref, sink_ref, o_ref):
    i = pl.program_id(1)
    pad_pen = jnp.where(i == 0, NEG_INF, 0.0)
    for kv in range(H_C_KV):
        qt = qt_ref[kv]
        s = jnp.concatenate([_dot(ka_ref[kv], qt) + pad_pen, _dot(kb_ref[kv], qt)], axis=0) + bias_ref[kv]
        sink = sink_ref[kv]
        m = jnp.maximum(jnp.max(s, axis=0, keepdims=True), sink)
        p = jnp.exp2(s - m).astype(BF16)
        acc = _dot(vta_ref[kv], p[0:TQ, :]) + _dot(vtb_ref[kv], p[TQ:2 * TQ, :])
        o = acc[0:DH_C, :] / (acc[DH_C:DH_C + 1, :] + jnp.exp2(sink - m))
        for pr in range(G_C // 2):
            pair = jnp.concatenate([o[:, (2 * pr) * TQ:(2 * pr + 1) * TQ], o[:, (2 * pr + 1) * TQ:(2 * pr + 2) * TQ]],
                                   axis=0)
            col = (kv * G_C + 2 * pr) * DH_C
            o_ref[:, col:col + LANES] = pair.T.astype(BF16)


def _swa(q, k, v, table_c, sinks, b, s):
    assert WINDOW == TQ and TQ % CHUNK == 0
    nq = s // TQ
    length = 3 * TQ + 1
    m = np.arange(length)
    diff = np.where(m < 2 * TQ + 1, m, m - length)
    u = jnp.take(table_c.astype(F32), jnp.asarray(_t5_bucket_np(diff - TQ)), axis=0) * LOG2E
    bias = _toeplitz(jnp.moveaxis(u, -1, 0), TQ, 2 * TQ)
    rblk = np.arange(TQ)[:, None] // CHUNK + WINDOW // CHUNK
    cblk = np.arange(2 * TQ)[None, :] // CHUNK
    valid = (cblk <= rblk) & (cblk >= rblk - WINDOW // CHUNK)
    bias = jnp.where(jnp.asarray(valid)[None], bias, NEG_INF)
    bias_t = jnp.transpose(bias.reshape(H_C_KV, G_C, TQ, 2 * TQ), (0, 3, 1, 2)).reshape(H_C_KV, 2 * TQ, G_C * TQ)
    sink_row = jnp.repeat(sinks.astype(F32).reshape(H_C_KV, 1, G_C) * LOG2E, TQ, axis=2)

    qt = jnp.transpose(q.reshape(b * nq, TQ, H_C_KV, G_C, DH_C), (0, 2, 4, 3, 1)).reshape(
        b * nq, H_C_KV, DH_C, G_C * TQ)
    kp = jnp.pad(k.reshape(b, s, H_C_KV, DH_C), ((0, 0), (TQ, 0), (0, 0), (0, 0)))
    vp = jnp.pad(v.reshape(b, s, H_C_KV, DH_C), ((0, 0), (TQ, 0), (0, 0), (0, 0)))
    k4 = jnp.swapaxes(kp, 1, 2)
    vtx = _with_ones_rows(jnp.transpose(vp, (0, 2, 3, 1)))
    dvx = DH_C + ONES_ROWS
    return pl.pallas_call(
        _swa_kernel,
        out_shape=jax.ShapeDtypeStruct((b * s, H_C * DH_C), BF16),
        grid=(b, nq),
        in_specs=[
            pl.BlockSpec((None, H_C_KV, DH_C, G_C * TQ), lambda bb, i: (bb * nq + i, 0, 0, 0)),
            pl.BlockSpec((None, H_C_KV, TQ, DH_C), lambda bb, i: (bb, 0, i, 0)),
            pl.BlockSpec((None, H_C_KV, TQ, DH_C), lambda bb, i: (bb, 0, i + 1, 0)),
            pl.BlockSpec((None, H_C_KV, dvx, TQ), lambda bb, i: (bb, 0, 0, i)),
            pl.BlockSpec((None, H_C_KV, dvx, TQ), lambda bb, i: (bb, 0, 0, i + 1)),
            pl.BlockSpec((H_C_KV, 2 * TQ, G_C * TQ), lambda bb, i: (0, 0, 0)),
            pl.BlockSpec((H_C_KV, 1, G_C * TQ), lambda bb, i: (0, 0, 0)),
        ],
        out_specs=pl.BlockSpec((TQ, H_C * DH_C), lambda bb, i: (bb * nq + i, 0)),
        compiler_params=pltpu.CompilerParams(vmem_limit_bytes=VMEM_LIMIT),
        name="swa_attention",
    )(qt, k4, k4, vtx, vtx, bias_t, sink_row)


def _memkv_kernel(mem_ref, g_ref, w_ref, k_ref, v_ref):
    h = _rms(mem_ref[...], g_ref[...]).astype(BF16)
    kv = _dot(h, w_ref[...])
    k_ref[...] = kv[:, 0:H_X * DH_X].astype(BF16)
    v_ref[...] = kv[:, H_X * DH_X:].astype(BF16)


def _memkv(mem, g, wkv):
    b, m, _ = mem.shape
    n = H_X * DH_X
    return pl.pallas_call(
        _memkv_kernel,
        out_shape=[jax.ShapeDtypeStruct((b, m, n), BF16), jax.ShapeDtypeStruct((b, m, n), BF16)],
        grid=(b,),
        in_specs=[pl.BlockSpec((None, m, D_MODEL), lambda i: (i, 0, 0)),
                  pl.BlockSpec((1, D_MODEL), lambda i: (0, 0)),
                  pl.BlockSpec((D_MODEL, 2 * n), lambda i: (0, 0))],
        out_specs=[pl.BlockSpec((None, m, n), lambda i: (i, 0, 0)), pl.BlockSpec((None, m, n), lambda i: (i, 0, 0))],
        compiler_params=pltpu.CompilerParams(vmem_limit_bytes=VMEM_LIMIT),
        name="mem_kv",
    )(mem, g.reshape(1, D_MODEL), wkv.astype(BF16))


def _lane_half_mask(rows, half):
    lane = lax.broadcasted_iota(jnp.int32, (rows, LANES), 1)
    return (lane < 64) if half == 0 else (lane >= 64)


def _tail_kernel(*refs, n_a):
    x_ref, g_ref = refs[0], refs[1]
    a_refs = refs[2:2 + n_a]
    wout_refs = refs[2 + n_a:2 + 2 * n_a]
    wq_ref, kt_ref, v_ref, wo_ref, w1_ref, w2_ref, o_ref = refs[2 + 2 * n_a:]
    tc = x_ref.shape[0] // TAIL_CHAINS
    chains = [slice(c * tc, (c + 1) * tc) for c in range(TAIL_CHAINS)]
    g = g_ref[...]

    ys = []
    for r in chains:
        y = _dot(a_refs[0][r, :], wout_refs[0][...])
        for a_ref, w_ref in zip(a_refs[1:], wout_refs[1:]):
            y = y + _dot(a_ref[r, :], w_ref[...])
        ys.append(y)
    xs = [x_ref[r, :] + _rms(y, g[1:2, :]) for r, y in zip(chains, ys)]

    hqs = [_rms(x, g[2:3, :]).astype(BF16) for x in xs]
    qs = [_dot(hq, wq_ref[...]) for hq in hqs]
    qs = [(q * (DH_X ** -0.5)).astype(BF16) for q in qs]
    pairs = [[] for _ in chains]
    for pr in range(H_X // 2):
        vp = v_ref[:, pr * LANES:(pr + 1) * LANES]
        outs = [jnp.zeros((tc, LANES), F32) for _ in chains]
        for half in range(2):
            qzs = [jnp.where(_lane_half_mask(tc, half), q[:, pr * LANES:(pr + 1) * LANES], jnp.zeros((tc, LANES), BF16))
                   for q in qs]
            ss = [_dot(qz, kt_ref[pr * LANES:(pr + 1) * LANES, :]) for qz in qzs]
            vz = jnp.where(_lane_half_mask(vp.shape[0], half), vp, jnp.zeros_like(vp))
            ps = []
            for sc in ss:
                e = jnp.exp(sc - jnp.max(sc, axis=1, keepdims=True))
                ps.append((e / jnp.sum(e, axis=1, keepdims=True)).astype(BF16))
            outs = [out + _dot(p, vz) for out, p in zip(outs, ps)]
        for c, out in enumerate(outs):
            pairs[c].append(out.astype(BF16))
    ys = [_dot(jnp.concatenate(pc, axis=1), wo_ref[...]) for pc in pairs]
    xs = [x + _rms(y, g[3:4, :]) for x, y in zip(xs, ys)]

    hms = [_rms(x, g[4:5, :]).astype(BF16) for x in xs]
    n_chunks = D_FF // FF_CHUNK
    ys = [jnp.zeros((tc, D_MODEL), F32) for _ in chains]
    a_cur = [_dot(hm, w1_ref[:, 0:FF_CHUNK]) for hm in hms]
    for k in range(n_chunks):
        if k + 1 < n_chunks:
            a_next = [_dot(hm, w1_ref[:, (k + 1) * FF_CHUNK:(k + 2) * FF_CHUNK]) for hm in hms]
        acts = [jnp.square(jnp.maximum(a, 0.0)).astype(BF16) for a in a_cur]
        ys = [y + _dot(act, w2_ref[k * FF_CHUNK:(k + 1) * FF_CHUNK, :]) for y, act in zip(ys, acts)]
        if k + 1 < n_chunks:
            a_cur = a_next
    for r, x, y in zip(chains, xs, ys):
        o_ref[r, :] = x + _rms(y, g[5:6, :])


def _tail(x2, g, a_list, wout_list, wq, kt, v, wo, w1, w2, b, s):
    t = x2.shape[0]
    tm = min(TM_TAIL, s)
    per_b = s // tm
    n_a = len(a_list)
    row = lambda i: (i, 0)
    full2 = lambda i: (0, 0)
    once = dict(pipeline_mode=pl.Buffered(1))
    in_specs = [pl.BlockSpec((tm, D_MODEL), row), pl.BlockSpec((6, D_MODEL), full2)]
    in_specs += [pl.BlockSpec((tm, a.shape[1]), row) for a in a_list]
    in_specs += [pl.BlockSpec(w.shape, full2, **once) for w in wout_list]
    in_specs += [
        pl.BlockSpec(wq.shape, full2, **once),
        pl.BlockSpec((None,) + kt.shape[1:], lambda i: (i // per_b, 0, 0)),
        pl.BlockSpec((None,) + v.shape[1:], lambda i: (i // per_b, 0, 0)),
        pl.BlockSpec(wo.shape, full2, **once),
        pl.BlockSpec(w1.shape, full2, **once),
        pl.BlockSpec(w2.shape, full2, **once),
    ]
    return pl.pallas_call(
        functools.partial(_tail_kernel, n_a=n_a),
        out_shape=jax.ShapeDtypeStruct((t, D_MODEL), F32),
        grid=(t // tm,),
        in_specs=in_specs,
        out_specs=pl.BlockSpec((tm, D_MODEL), row),
        compiler_params=pltpu.CompilerParams(vmem_limit_bytes=VMEM_LIMIT),
        name="tail",
    )(x2, g, *a_list, *wout_list, wq, kt, v, wo, w1, w2)


def kernel(x, mem, rel_bias_table, norm_g, ev_w_in, ev_a_kv_norm, ev_a_w_uk, ev_a_w_uv, ev_b_lambda, ev_b_subln, ev_w_out, od_w_in, od_sinks, od_w_out, xa_wq, xa_wkv, xa_wo, xa_mem_norm, mlp_w1, mlp_w2):
    b, s, d = x.shape
    depth = norm_g.shape[0]
    assert d == D_MODEL and s % TK == 0 and TK % TQ_DIFF == 0 and TQ_DIFF % TQ == 0
    x2 = x.reshape(b * s, d)
    bias_a = _causal_bias_tiles(rel_bias_table[:, :H_A])
    bias_b = _causal_bias_tiles(rel_bias_table[:, H_A:H_A + H_B])
    table_c = rel_bias_table[:, H_A + H_B:]
    for l in range(depth):
        g = norm_g[l].astype(F32)
        if l % 2 == 0:
            e = l // 2
            lam_init = 0.8 - 0.6 * math.exp(-0.3 * l)
            qlat, ckv, qi, qb, kb, vb, ki, wi = _proj_even(x2, g, ev_w_in[e], ev_a_kv_norm[e], ev_a_w_uk[e])
            oa = _dsa(qi, wi, qlat, ki, ckv, bias_a, ev_a_w_uv[e], b, s)
            ob = _diff(qb, kb, vb, bias_b, ev_b_lambda[e], ev_b_subln[e], lam_init, b, s)
            n_a = H_A * DH_A
            a_list = [oa, ob]
            wout_list = [ev_w_out[e][:n_a].astype(BF16), ev_w_out[e][n_a:].astype(BF16)]
        else:
            o = l // 2
            q, k, v = _proj_odd(x2, g, od_w_in[o])
            a_list = [_swa(q, k, v, table_c, od_sinks[o], b, s)]
            wout_list = [od_w_out[o].astype(BF16)]
        mk, mv = _memkv(mem, xa_mem_norm[l], xa_wkv[l])
        x2 = _tail(x2, g, a_list, wout_list, xa_wq[l].astype(BF16), jnp.swapaxes(mk, 1, 2), mv,
                   xa_wo[l].astype(BF16), mlp_w1[l].astype(BF16), mlp_w2[l].astype(BF16), b, s)
    return x2.reshape(b, s, d)
```

```python
import functools
import math

import numpy as np
import jax
import jax.numpy as jnp
from jax import lax
from jax.experimental import pallas as pl
from jax.experimental.pallas import tpu as pltpu

F32 = jnp.float32
BF16 = jnp.bfloat16

D_MODEL = 1024
CHUNK = 64
EPS = 1e-6
NEG_INF = -1e30
LOG2E = math.log2(math.e)

H_A, DH_A, D_LAT = 8, 64, 128
H_IDX, DH_IDX = 8, 64
TOPK_MAX = 256
H_B, DH_B = 4, 64
H_C, H_C_KV, DH_C = 16, 2, 64
G_C = H_C // H_C_KV
WINDOW = 128
H_X, DH_X = 4, 64
D_FF = 4 * D_MODEL
N_BUCKETS = 32
MAX_DIST = 1024

LANES = 128
FOLD_ROWS = 64
TQ = 128
TQ_DIFF = 256
TK = 512
SUB = TK // LANES
ONES_ROWS = 16
TM_PROJ = 512
TM_TAIL = 512
TAIL_CHAINS = 2
FF_CHUNK = 1024
VMEM_LIMIT = 56 * 1024 * 1024
N_BISECT = 14
GROUPS = (4, 2, 1)
GROUPS_NEAR = (2, 1)
GROUPS_PASS = (4, 1)


def _rms(x, g):
    return x * lax.rsqrt(jnp.mean(x * x, axis=-1, keepdims=True) + EPS) * g


def _dot(a, b):
    return jnp.dot(a, b, preferred_element_type=F32)


def _t5_bucket_np(rel):
    nb = N_BUCKETS // 2
    max_exact = nb // 2
    n = np.abs(rel)
    nf = np.maximum(n, 1).astype(np.float32)
    large = max_exact + (np.log(nf / max_exact) / math.log(MAX_DIST / max_exact) * (nb - max_exact)).astype(np.int32)
    large = np.minimum(large, nb - 1)
    return np.where(rel > 0, nb, 0) + np.where(n < max_exact, n, large)


def _sat_blocks():
    d = 1
    while _t5_bucket_np(np.array([-(d * LANES - (LANES - 1))]))[0] != N_BUCKETS // 2 - 1:
        d += 1
    return d


N_SAT = _sat_blocks()
N_BT = N_SAT + 1


def _toeplitz(u, rows, cols):
    length = u.shape[-1]
    flat = jnp.tile(u, (1,) * (u.ndim - 1) + (rows,))[..., :rows * (length - 1)]
    return flat.reshape(u.shape[:-1] + (rows, length - 1))[..., :cols]


def _causal_bias_tiles(table_cols):
    length = 2 * LANES + 1
    m = np.arange(length)
    diff = np.where(m < LANES + 1, m, m - length)
    rel = -diff[None, :] - (np.arange(N_BT) * LANES)[:, None]
    tab = table_cols.astype(F32)
    u = jnp.take(tab, jnp.asarray(_t5_bucket_np(rel)), axis=0)
    u = (u - tab[N_BUCKETS // 2 - 1][None, None, :]) * LOG2E
    tiles = _toeplitz(jnp.moveaxis(u, -1, 0), LANES, LANES)
    return tiles.reshape(table_cols.shape[1] * N_BT, LANES, LANES)


def _fold_rows(x, op):
    acc = x[0:FOLD_ROWS, :]
    for r in range(1, x.shape[0] // FOLD_ROWS):
        acc = op(acc, x[r * FOLD_ROWS:(r + 1) * FOLD_ROWS, :])
    return acc


def _grouped_loop(lo, hi, carry, fn, groups):
    for g in groups:
        n = (hi - lo) // g
        carry = lax.fori_loop(0, n, lambda u, c, lo=lo, g=g: fn([lo + u * g + k for k in range(g)], c), carry)
        lo = lo + n * g
    return carry


def _bias_tile(bias_ref, h, qb0, n_qb, j):
    rows = []
    for c in range(SUB):
        cols = [bias_ref[h * N_BT + jnp.clip(qb0 + r - (j * SUB + c), 0, N_SAT)] for r in range(n_qb)]
        rows.append(cols[0] if n_qb == 1 else jnp.concatenate(cols, axis=1))
    return jnp.concatenate(rows, axis=0)


def _ones_rows(n):
    return jnp.ones((ONES_ROWS, n), BF16)


def _proj_even_kernel(x_ref, g_ref, w_ref, kvn_ref, wuk_ref,
                      qit_ref, wit_ref, qlt_ref, ki_ref, c_ref, ctx_ref, qbt_ref, kb_ref, vtx_ref):
    tm = x_ref.shape[0]
    h = _rms(x_ref[...], g_ref[0:1, :]).astype(BF16)
    z = _dot(h, w_ref[...])
    qa = z[:, 0:512].astype(BF16)
    ckv = _rms(z[:, 512:640], kvn_ref[...])
    c_ref[...] = ckv.astype(BF16)
    kb_ref[...] = z[:, 1664:2176].astype(BF16)
    ki_ref[...] = z[:, 2688:2688 + DH_IDX].astype(BF16)
    ctx_ref[0, D_LAT:, :] = _ones_rows(tm)
    for hb in range(H_B):
        vtx_ref[0, hb, 2 * DH_B:, :] = _ones_rows(tm)
    for hh in range(H_A):
        ql = _dot(qa[:, hh * DH_A:(hh + 1) * DH_A], wuk_ref[hh]) * (DH_A ** -0.5 * LOG2E)
        for sb in range(tm // TQ):
            qlt_ref[sb, :, hh * TQ:(hh + 1) * TQ] = ql[sb * TQ:(sb + 1) * TQ, :].T.astype(BF16)
    per_diff = TQ_DIFF // LANES
    for sb in range(tm // LANES):
        rows = slice(sb * LANES, (sb + 1) * LANES)
        for pr in range(H_IDX // 2):
            t = z[rows, 640 + pr * LANES:640 + (pr + 1) * LANES].T
            qit_ref[sb, :, (2 * pr) * TQ:(2 * pr + 1) * TQ] = t[0:DH_IDX, :].astype(BF16)
            qit_ref[sb, :, (2 * pr + 1) * TQ:(2 * pr + 2) * TQ] = t[DH_IDX:, :].astype(BF16)
        wit_ref[sb] = z[rows, 2816:2816 + LANES].T[0:H_IDX, :]
        ctx_ref[0, 0:D_LAT, rows] = ckv[rows, :].T.astype(BF16)
        for hb in range(H_B):
            qb = z[rows, 1152 + hb * LANES:1152 + (hb + 1) * LANES] * (DH_B ** -0.5 * LOG2E)
            lanes = slice((sb % per_diff) * LANES, (sb % per_diff + 1) * LANES)
            qbt_ref[sb // per_diff, hb, :, lanes] = qb.T.astype(BF16)
            vtx_ref[0, hb, 0:2 * DH_B, rows] = z[rows, 2176 + hb * LANES:2176 + (hb + 1) * LANES].T.astype(BF16)


def _proj_even(x2, g, w_in, kvn, w_uk):
    t = x2.shape[0]
    tm = TK
    assert TQ == LANES and t % tm == 0
    pad = jnp.zeros((D_MODEL, LANES - DH_IDX), w_in.dtype)
    cols = [w_in[:, 0:512], w_in[:, 512:640], w_in[:, 640:1152], w_in[:, 1224:1736], w_in[:, 1736:2248],
            w_in[:, 2248:2760], w_in[:, 1152:1216], pad, w_in[:, 1216:1224],
            jnp.zeros((D_MODEL, LANES - H_IDX), w_in.dtype)]
    w = jnp.concatenate(cols, axis=1).astype(BF16)
    n = w.shape[1]
    dv = 2 * DH_B
    row = lambda i: (i, 0)
    full2 = lambda i: (0, 0)
    lead3 = lambda i: (i, 0, 0)
    lead4 = lambda i: (i, 0, 0, 0)
    outs = [
        ((t // TQ, DH_IDX, H_IDX * TQ), BF16, (tm // TQ, DH_IDX, H_IDX * TQ), lead3),
        ((t // TQ, H_IDX, TQ), F32, (tm // TQ, H_IDX, TQ), lead3),
        ((t // TQ, D_LAT, H_A * TQ), BF16, (tm // TQ, D_LAT, H_A * TQ), lead3),
        ((t, DH_IDX), BF16, (tm, DH_IDX), row),
        ((t, D_LAT), BF16, (tm, D_LAT), row),
        ((t // TK, D_LAT + ONES_ROWS, TK), BF16, (1, D_LAT + ONES_ROWS, TK), lead3),
        ((t // TQ_DIFF, H_B, dv, TQ_DIFF), BF16, (tm // TQ_DIFF, H_B, dv, TQ_DIFF), lead4),
        ((t, H_B * dv), BF16, (tm, H_B * dv), row),
        ((t // TK, H_B, dv + ONES_ROWS, TK), BF16, (1, H_B, dv + ONES_ROWS, TK), lead4),
    ]
    return pl.pallas_call(
        _proj_even_kernel,
        out_shape=[jax.ShapeDtypeStruct(shape, dt) for shape, dt, _, _ in outs],
        grid=(t // tm,),
        in_specs=[
            pl.BlockSpec((tm, D_MODEL), row),
            pl.BlockSpec((6, D_MODEL), full2),
            pl.BlockSpec((D_MODEL, n), full2),
            pl.BlockSpec((1, D_LAT), full2),
            pl.BlockSpec((H_A, DH_A, D_LAT), lambda i: (0, 0, 0)),
        ],
        out_specs=[pl.BlockSpec(block, imap) for _, _, block, imap in outs],
        compiler_params=pltpu.CompilerParams(vmem_limit_bytes=VMEM_LIMIT),
        name="proj_even",
    )(x2, g, w, kvn.reshape(1, D_LAT), w_uk.astype(BF16))


def _proj_odd_kernel(x_ref, g_ref, w_ref, qt_ref, k_ref, vtx_ref):
    tm = x_ref.shape[0]
    h = _rms(x_ref[...], g_ref[0:1, :]).astype(BF16)
    z = _dot(h, w_ref[...])
    for kv in range(H_C_KV):
        k_ref[kv] = z[:, 1024 + kv * DH_C:1024 + (kv + 1) * DH_C].astype(BF16)
    for sb in range(tm // TQ):
        rows = slice(sb * TQ, (sb + 1) * TQ)
        for pr in range(H_C // 2):
            t = (z[rows, pr * LANES:(pr + 1) * LANES] * (DH_C ** -0.5 * LOG2E)).T
            for half in range(2):
                hd = 2 * pr + half
                kv, gq = hd // G_C, hd % G_C
                qt_ref[sb, kv, :, gq * TQ:(gq + 1) * TQ] = t[half * DH_C:(half + 1) * DH_C, :].astype(BF16)
        vt = z[rows, 1152:1280].T
        for kv in range(H_C_KV):
            vtx_ref[sb, kv, 0:DH_C, :] = vt[kv * DH_C:(kv + 1) * DH_C, :].astype(BF16)
            vtx_ref[sb, kv, DH_C:, :] = _ones_rows(TQ)


def _proj_odd(x2, g, w_in):
    t = x2.shape[0]
    tm = min(TM_PROJ, t)
    assert TQ == LANES
    row = lambda i: (i, 0)
    full2 = lambda i: (0, 0)
    lead4 = lambda i: (i, 0, 0, 0)
    dvx = DH_C + ONES_ROWS
    return pl.pallas_call(
        _proj_odd_kernel,
        out_shape=[jax.ShapeDtypeStruct((t // TQ, H_C_KV, DH_C, G_C * TQ), BF16),
                   jax.ShapeDtypeStruct((H_C_KV, t, DH_C), BF16),
                   jax.ShapeDtypeStruct((t // TQ, H_C_KV, dvx, TQ), BF16)],
        grid=(t // tm,),
        in_specs=[pl.BlockSpec((tm, D_MODEL), row),
                  pl.BlockSpec((6, D_MODEL), full2),
                  pl.BlockSpec((D_MODEL, 1280), full2)],
        out_specs=[pl.BlockSpec((tm // TQ, H_C_KV, DH_C, G_C * TQ), lead4),
                   pl.BlockSpec((H_C_KV, tm, DH_C), lambda i: (0, i, 0)),
                   pl.BlockSpec((tm // TQ, H_C_KV, dvx, TQ), lead4)],
        compiler_params=pltpu.CompilerParams(vmem_limit_bytes=VMEM_LIMIT),
        name="proj_odd",
    )(x2, g, w_in.astype(BF16))


def _dsa_kernel(qit_ref, wit_ref, qlt_ref, k_ref, c_ref, ctx_ref, bias_ref, wuvt_ref, o_ref,
                sc_ref, acc_ref, *p_refs, topk):
    i = pl.program_id(1)
    t0 = i * TQ
    nt = (t0 + TQ + TK - 1) // TK
    n_far = jnp.maximum(i - N_SAT + 1, 0) // SUB
    kf = float(topk)

    qpos = lax.broadcasted_iota(jnp.int32, (1, TQ), 1) + t0
    limit = (qpos // CHUNK + 1) * CHUNK
    small = limit <= topk
    big = jnp.logical_not(small)
    w = wit_ref[...]
    qit = qit_ref[...]

    def score_tiles(js, carry):
        rmax, rmin = carry
        zs = [_dot(k_ref[j], qit) for j in js]
        for j, z in zip(js, zs):
            sc = jnp.maximum(z[:, 0:TQ], 0.0) * w[0:1, :]
            for h in range(1, H_IDX):
                sc = sc + jnp.maximum(z[:, h * TQ:(h + 1) * TQ], 0.0) * w[h:h + 1, :]
            kpos = lax.broadcasted_iota(jnp.int32, (TK, TQ), 0) + j * TK
            adm = kpos < limit
            sc_ref[j] = jnp.where(adm, sc, -jnp.inf)
            rmax = jnp.maximum(rmax, _fold_rows(jnp.where(adm, sc, -jnp.inf), jnp.maximum))
            rmin = jnp.minimum(rmin, _fold_rows(jnp.where(adm, sc, jnp.inf), jnp.minimum))
        return rmax, rmin

    rmax, rmin = _grouped_loop(
        0, nt, (jnp.full((FOLD_ROWS, TQ), -jnp.inf, F32), jnp.full((FOLD_ROWS, TQ), jnp.inf, F32)),
        score_tiles, GROUPS)
    hi0 = jnp.max(rmax, axis=0, keepdims=True)
    lo0 = jnp.min(rmin, axis=0, keepdims=True)

    def col_reduce(tile_fn, op, init, final):
        def fn(js, acc):
            for j in js:
                acc = op(acc, _fold_rows(tile_fn(j), op))
            return acc
        acc = _grouped_loop(0, nt, jnp.full((FOLD_ROWS, TQ), init, F32), fn, GROUPS_PASS)
        return final(acc, axis=0, keepdims=True)

    def count_ge(thr):
        return col_reduce(lambda j: jnp.where(sc_ref[j] >= thr, 1.0, 0.0), jnp.add, 0.0, jnp.sum)

    def count_gt(thr):
        return col_reduce(lambda j: jnp.where(sc_ref[j] > thr, 1.0, 0.0), jnp.add, 0.0, jnp.sum)

    def max_le(thr):
        def tile(j):
            t = sc_ref[j]
            return jnp.where(t <= thr, t, -jnp.inf)
        return col_reduce(tile, jnp.maximum, -jnp.inf, jnp.max)

    def max_lt(thr):
        def tile(j):
            t = sc_ref[j]
            return jnp.where(t < thr, t, -jnp.inf)
        return col_reduce(tile, jnp.maximum, -jnp.inf, jnp.max)

    def bisect_body(_, carry):
        lo, hi = carry
        mid = lo * 0.5 + hi * 0.5
        ge = count_ge(mid) >= kf
        return jnp.where(ge, mid, lo), jnp.where(ge, hi, mid)

    lo, hi = lax.fori_loop(0, N_BISECT, bisect_body, (lo0, hi0))

    v0 = max_le(hi)
    c0 = count_ge(v0)

    def walk_cond(carry):
        _, cnt, it = carry
        pending = jnp.where((cnt < kf) & big, 1.0, 0.0)
        return (jnp.max(pending) > 0.0) & (it < topk + 2)

    def walk_body(carry):
        v, cnt, it = carry
        v = jnp.where((cnt < kf) & big, max_lt(v), v)
        return v, count_ge(v), it + 1

    tau, cnt, _ = lax.while_loop(walk_cond, walk_body, (v0, c0, jnp.int32(0)))
    tau = jnp.where(small, -jnp.finfo(F32).max, tau)

    tied = (cnt > kf) & big

    @pl.when(jnp.max(jnp.where(tied, 1.0, 0.0)) > 0.0)
    def _():
        need = kf - count_gt(tau)
        n_keys = sc_ref.shape[0] * TK

        def kpos_f(j):
            return (lax.broadcasted_iota(jnp.int32, (TK, TQ), 0) + j * TK).astype(F32)

        def count_eq_le(pos):
            return col_reduce(lambda j: jnp.where((sc_ref[j] == tau) & (kpos_f(j) <= pos), 1.0, 0.0),
                              jnp.add, 0.0, jnp.sum)

        def pos_body(_, carry):
            lo_p, hi_p = carry
            mid = jnp.floor((lo_p + hi_p) * 0.5)
            ok = count_eq_le(mid) >= need
            return jnp.where(ok, lo_p, mid), jnp.where(ok, mid, hi_p)

        n_steps = int(math.ceil(math.log2(n_keys))) + 1
        _, cut = lax.fori_loop(0, n_steps, pos_body,
                               (jnp.full((1, TQ), -1.0, F32), jnp.full((1, TQ), float(n_keys), F32)))

        def drop_body(j, _):
            t = sc_ref[j]
            sc_ref[j] = jnp.where(tied & (t == tau) & (kpos_f(j) > cut), -jnp.inf, t)
            return 0

        lax.fori_loop(0, nt, drop_body, 0)

    acc_ref[...] = jnp.zeros(acc_ref.shape, F32)
    qlt = qlt_ref[...]

    def attn_tiles(js, m, with_bias):
        s_alls = [_dot(c_ref[j], qlt) for j in js]
        for g, (j, s_all) in enumerate(zip(js, s_alls)):
            p_ref = p_refs[g]
            sel = sc_ref[j] >= tau
            m_new, alpha = [], []
            for h in range(H_A):
                cols = slice(h * TQ, (h + 1) * TQ)
                s = s_all[:, cols]
                if with_bias:
                    s = s + _bias_tile(bias_ref, h, i, 1, j)
                s = jnp.where(sel, s, NEG_INF)
                m_old = m[:, cols]
                m_h = jnp.maximum(m_old, jnp.max(_fold_rows(s, jnp.maximum), axis=0, keepdims=True))
                p_ref[:, cols] = jnp.exp2(s - m_h).astype(BF16)
                alpha.append(jnp.exp2(m_old - m_h))
                m_new.append(m_h)
            acc_ref[...] = acc_ref[...] * jnp.concatenate(alpha, axis=1) + _dot(ctx_ref[j], p_ref[...])
            m = jnp.concatenate(m_new, axis=1)
        return m

    m = jnp.full((1, H_A * TQ), NEG_INF, F32)
    m = _grouped_loop(0, n_far, m, lambda js, mm: attn_tiles(js, mm, False), GROUPS)
    _grouped_loop(n_far, nt, m, lambda js, mm: attn_tiles(js, mm, True), GROUPS_NEAR)

    outs = []
    for h in range(H_A):
        cols = slice(h * TQ, (h + 1) * TQ)
        o_t = (acc_ref[0:D_LAT, cols] / acc_ref[D_LAT:D_LAT + 1, cols]).astype(BF16)
        outs.append(_dot(wuvt_ref[h], o_t))
    for pr in range(H_A // 2):
        pair = jnp.concatenate([outs[2 * pr], outs[2 * pr + 1]], axis=0)
        o_ref[:, pr * LANES:(pr + 1) * LANES] = pair.T.astype(BF16)


def _dsa(qit, wit, qlt, ki, ckv, ctx, bias_a, w_uv, b, s):
    nq, nt = s // TQ, s // TK
    topk = min(TOPK_MAX, s // 4)
    k4 = ki.reshape(b, nt, TK, DH_IDX)
    c4 = ckv.reshape(b, nt, TK, D_LAT)
    ctx = ctx.reshape(b, nt, D_LAT + ONES_ROWS, TK)
    wuvt = jnp.swapaxes(w_uv, 1, 2).astype(BF16)
    qblk = lambda bb, i: (bb * nq + i, 0, 0)
    kv4 = lambda bb, i: (bb, 0, 0, 0)
    return pl.pallas_call(
        functools.partial(_dsa_kernel, topk=topk),
        out_shape=jax.ShapeDtypeStruct((b * s, H_A * DH_A), BF16),
        grid=(b, nq),
        in_specs=[
            pl.BlockSpec((None, DH_IDX, H_IDX * TQ), qblk),
            pl.BlockSpec((None, H_IDX, TQ), qblk),
            pl.BlockSpec((None, D_LAT, H_A * TQ), qblk),
            pl.BlockSpec((None, nt, TK, DH_IDX), kv4),
            pl.BlockSpec((None, nt, TK, D_LAT), kv4),
            pl.BlockSpec((None, nt, D_LAT + ONES_ROWS, TK), kv4),
            pl.BlockSpec((H_A * N_BT, LANES, LANES), lambda bb, i: (0, 0, 0)),
            pl.BlockSpec((H_A, DH_A, D_LAT), lambda bb, i: (0, 0, 0)),
        ],
        out_specs=pl.BlockSpec((TQ, H_A * DH_A), lambda bb, i: (bb * nq + i, 0)),
        scratch_shapes=[
            pltpu.VMEM((nt, TK, TQ), F32),
            pltpu.VMEM((D_LAT + ONES_ROWS, H_A * TQ), F32),
        ] + [pltpu.VMEM((TK, H_A * TQ), BF16)] * max(GROUPS),
        compiler_params=pltpu.CompilerParams(vmem_limit_bytes=VMEM_LIMIT),
        name="dsa_attention",
    )(qit, wit, qlt, k4, c4, ctx, bias_a, wuvt)


def _diff_kernel(qt_ref, k_ref, vtx_ref, bias_ref, lam_ref, sub_ref, o_ref, acc_ref, *p_refs, lam_init):
    h = pl.program_id(1)
    i = pl.program_id(2)
    tq = TQ_DIFF
    n_qb = tq // LANES
    t0 = i * tq
    nt = (t0 + tq + TK - 1) // TK
    n_far = jnp.maximum(i * n_qb - N_SAT + 1, 0) // SUB
    qt = qt_ref[...]
    top = lax.broadcasted_iota(jnp.int32, qt.shape, 0) < DH_B
    q2 = jnp.concatenate([jnp.where(top, qt, jnp.zeros_like(qt)), jnp.where(top, jnp.zeros_like(qt), qt)], axis=1)
    qpos = lax.broadcasted_iota(jnp.int32, (1, tq), 1) + t0
    limit = (qpos // CHUNK + 1) * CHUNK
    acc_ref[...] = jnp.zeros(acc_ref.shape, F32)

    def tiles(js, m, near):
        s2s = [_dot(k_ref[j], q2) for j in js]
        for g, (j, s2) in enumerate(zip(js, s2s)):
            p_ref = p_refs[g]
            if near:
                bias = _bias_tile(bias_ref, h, i * n_qb, n_qb, j)
                adm = (lax.broadcasted_iota(jnp.int32, (TK, tq), 0) + j * TK) < limit
            m_new, alpha = [], []
            for mp in range(2):
                cols = slice(mp * tq, (mp + 1) * tq)
                s = s2[:, cols]
                if near:
                    s = jnp.where(adm, s + bias, NEG_INF)
                m_old = m[:, cols]
                m_h = jnp.maximum(m_old, jnp.max(_fold_rows(s, jnp.maximum), axis=0, keepdims=True))
                p_ref[:, cols] = jnp.exp2(s - m_h).astype(BF16)
                alpha.append(jnp.exp2(m_old - m_h))
                m_new.append(m_h)
            acc_ref[...] = acc_ref[...] * jnp.concatenate(alpha, axis=1) + _dot(vtx_ref[j], p_ref[...])
            m = jnp.concatenate(m_new, axis=1)
        return m

    m = jnp.full((1, 2 * tq), NEG_INF, F32)
    m = _grouped_loop(0, n_far, m, lambda js, mm: tiles(js, mm, False), GROUPS)
    _grouped_loop(n_far, nt, m, lambda js, mm: tiles(js, mm, True), GROUPS_NEAR)

    lf = lam_ref[...]
    lam = (jnp.exp(jnp.sum(lf[0:1, :] * lf[1:2, :], axis=1, keepdims=True))
           - jnp.exp(jnp.sum(lf[2:3, :] * lf[3:4, :], axis=1, keepdims=True)) + lam_init)
    dv = 2 * DH_B
    o = (acc_ref[0:dv, 0:tq] / acc_ref[dv:dv + 1, 0:tq]
         - lam * (acc_ref[0:dv, tq:2 * tq] / acc_ref[dv:dv + 1, tq:2 * tq]))
    o = o * lax.rsqrt(jnp.mean(o * o, axis=0, keepdims=True) + EPS) * sub_ref[...] * (1.0 - lam_init)
    o_ref[...] = o.T.astype(BF16)


def _diff(qbt, kb, vtx, bias_b, b_lambda, b_subln, lam_init, b, s):
    tq = TQ_DIFF
    nq, nt = s // tq, s // TK
    dv = 2 * DH_B
    k4 = kb.reshape(b, nt, TK, H_B * dv)
    vtx = vtx.reshape(b, nt, H_B, dv + ONES_ROWS, TK)
    return pl.pallas_call(
        functools.partial(_diff_kernel, lam_init=lam_init),
        out_shape=jax.ShapeDtypeStruct((b * s, H_B * dv), BF16),
        grid=(b, H_B, nq),
        in_specs=[
            pl.BlockSpec((None, None, dv, tq), lambda bb, h, i: (bb * nq + i, h, 0, 0)),
            pl.BlockSpec((None, nt, TK, dv), lambda bb, h, i: (bb, 0, 0, h)),
            pl.BlockSpec((None, nt, None, dv + ONES_ROWS, TK), lambda bb, h, i: (bb, 0, h, 0, 0)),
            pl.BlockSpec((H_B * N_BT, LANES, LANES), lambda bb, h, i: (0, 0, 0)),
            pl.BlockSpec((4, DH_B), lambda bb, h, i: (0, 0)),
            pl.BlockSpec((dv, 1), lambda bb, h, i: (0, 0)),
        ],
        out_specs=pl.BlockSpec((tq, dv), lambda bb, h, i: (bb * nq + i, h)),
        scratch_shapes=[
            pltpu.VMEM((dv + ONES_ROWS, 2 * tq), F32),
        ] + [pltpu.VMEM((TK, 2 * tq), BF16)] * max(GROUPS),
        compiler_params=pltpu.CompilerParams(vmem_limit_bytes=VMEM_LIMIT),
        name="diff_attention",
    )(qbt, k4, vtx, bias_b, b_lambda.astype(F32), b_subln.reshape(dv, 1).astype(F32))


def _swa_kernel(qt_ref, ka_ref, kb_ref, vta_ref, vtb_ref, bias_ref, sink_ref, o_ref):
    i = pl.program_id(1)
    pad_pen = jnp.where(i == 0, NEG_INF, 0.0)
    for kv in range(H_C_KV):
        qt = qt_ref[kv]
        s = jnp.concatenate([_dot(ka_ref[kv], qt) + pad_pen, _dot(kb_ref[kv], qt)], axis=0) + bias_ref[kv]
        sink = sink_ref[kv]
        m = jnp.maximum(jnp.max(s, axis=0, keepdims=True), sink)
        p = jnp.exp2(s - m).astype(BF16)
        acc = _dot(vta_ref[kv], p[0:TQ, :]) + _dot(vtb_ref[kv], p[TQ:2 * TQ, :])
        o = acc[0:DH_C, :] / (acc[DH_C:DH_C + 1, :] + jnp.exp2(sink - m))
        for pr in range(G_C // 2):
            pair = jnp.concatenate([o[:, (2 * pr) * TQ:(2 * pr + 1) * TQ], o[:, (2 * pr + 1) * TQ:(2 * pr + 2) * TQ]],
                                   axis=0)
            col = (kv * G_C + 2 * pr) * DH_C
            o_ref[:, col:col + LANES] = pair.T.astype(BF16)


def _swa(qt, k, vtx, table_c, sinks, b, s):
    assert WINDOW == TQ and TQ % CHUNK == 0
    nq = s // TQ
    length = 3 * TQ + 1
    m = np.arange(length)
    diff = np.where(m < 2 * TQ + 1, m, m - length)
    u = jnp.take(table_c.astype(F32), jnp.asarray(_t5_bucket_np(diff - TQ)), axis=0) * LOG2E
    bias = _toeplitz(jnp.moveaxis(u, -1, 0), TQ, 2 * TQ)
    rblk = np.arange(TQ)[:, None] // CHUNK + WINDOW // CHUNK
    cblk = np.arange(2 * TQ)[None, :] // CHUNK
    valid = (cblk <= rblk) & (cblk >= rblk - WINDOW // CHUNK)
    bias = jnp.where(jnp.asarray(valid)[None], bias, NEG_INF)
    bias_t = jnp.transpose(bias.reshape(H_C_KV, G_C, TQ, 2 * TQ), (0, 3, 1, 2)).reshape(H_C_KV, 2 * TQ, G_C * TQ)
    sink_row = jnp.repeat(sinks.astype(F32).reshape(H_C_KV, 1, G_C) * LOG2E, TQ, axis=2)

    dvx = DH_C + ONES_ROWS
    prev = lambda bb, i: bb * nq + jnp.maximum(i - 1, 0)
    return pl.pallas_call(
        _swa_kernel,
        out_shape=jax.ShapeDtypeStruct((b * s, H_C * DH_C), BF16),
        grid=(b, nq),
        in_specs=[
            pl.BlockSpec((None, H_C_KV, DH_C, G_C * TQ), lambda bb, i: (bb * nq + i, 0, 0, 0)),
            pl.BlockSpec((H_C_KV, TQ, DH_C), lambda bb, i: (0, prev(bb, i), 0)),
            pl.BlockSpec((H_C_KV, TQ, DH_C), lambda bb, i: (0, bb * nq + i, 0)),
            pl.BlockSpec((None, H_C_KV, dvx, TQ), lambda bb, i: (prev(bb, i), 0, 0, 0)),
            pl.BlockSpec((None, H_C_KV, dvx, TQ), lambda bb, i: (bb * nq + i, 0, 0, 0)),
            pl.BlockSpec((H_C_KV, 2 * TQ, G_C * TQ), lambda bb, i: (0, 0, 0)),
            pl.BlockSpec((H_C_KV, 1, G_C * TQ), lambda bb, i: (0, 0, 0)),
        ],
        out_specs=pl.BlockSpec((TQ, H_C * DH_C), lambda bb, i: (bb * nq + i, 0)),
        compiler_params=pltpu.CompilerParams(vmem_limit_bytes=VMEM_LIMIT),
        name="swa_attention",
    )(qt, k, k, vtx, vtx, bias_t, sink_row)


def _memkv_kernel(mem_ref, g_ref, w_ref, k_ref, v_ref):
    h = _rms(mem_ref[...], g_ref[...]).astype(BF16)
    kv = _dot(h, w_ref[...])
    k_ref[...] = kv[:, 0:H_X * DH_X].astype(BF16)
    v_ref[...] = kv[:, H_X * DH_X:].astype(BF16)


def _memkv(mem, g, wkv):
    b, m, _ = mem.shape
    n = H_X * DH_X
    return pl.pallas_call(
        _memkv_kernel,
        out_shape=[jax.ShapeDtypeStruct((b, m, n), BF16), jax.ShapeDtypeStruct((b, m, n), BF16)],
        grid=(b,),
        in_specs=[pl.BlockSpec((None, m, D_MODEL), lambda i: (i, 0, 0)),
                  pl.BlockSpec((1, D_MODEL), lambda i: (0, 0)),
                  pl.BlockSpec((D_MODEL, 2 * n), lambda i: (0, 0))],
        out_specs=[pl.BlockSpec((None, m, n), lambda i: (i, 0, 0)), pl.BlockSpec((None, m, n), lambda i: (i, 0, 0))],
        compiler_params=pltpu.CompilerParams(vmem_limit_bytes=VMEM_LIMIT),
        name="mem_kv",
    )(mem, g.reshape(1, D_MODEL), wkv.astype(BF16))


def _lane_half_mask(rows, half):
    lane = lax.broadcasted_iota(jnp.int32, (rows, LANES), 1)
    return (lane < 64) if half == 0 else (lane >= 64)


def _tail_kernel(*refs, n_a):
    x_ref, g_ref = refs[0], refs[1]
    a_refs = refs[2:2 + n_a]
    wout_refs = refs[2 + n_a:2 + 2 * n_a]
    wq_ref, kt_ref, v_ref, wo_ref, w1_ref, w2_ref, o_ref = refs[2 + 2 * n_a:]
    tc = x_ref.shape[0] // TAIL_CHAINS
    chains = [slice(c * tc, (c + 1) * tc) for c in range(TAIL_CHAINS)]
    g = g_ref[...]

    ys = []
    for r in chains:
        y = _dot(a_refs[0][r, :], wout_refs[0][...])
        for a_ref, w_ref in zip(a_refs[1:], wout_refs[1:]):
            y = y + _dot(a_ref[r, :], w_ref[...])
        ys.append(y)
    xs = [x_ref[r, :] + _rms(y, g[1:2, :]) for r, y in zip(chains, ys)]

    hqs = [_rms(x, g[2:3, :]).astype(BF16) for x in xs]
    qs = [_dot(hq, wq_ref[...]) for hq in hqs]
    qs = [(q * (DH_X ** -0.5)).astype(BF16) for q in qs]
    pairs = [[] for _ in chains]
    for pr in range(H_X // 2):
        vp = v_ref[:, pr * LANES:(pr + 1) * LANES]
        outs = [jnp.zeros((tc, LANES), F32) for _ in chains]
        for half in range(2):
            qzs = [jnp.where(_lane_half_mask(tc, half), q[:, pr * LANES:(pr + 1) * LANES], jnp.zeros((tc, LANES), BF16))
                   for q in qs]
            ss = [_dot(qz, kt_ref[pr * LANES:(pr + 1) * LANES, :]) for qz in qzs]
            vz = jnp.where(_lane_half_mask(vp.shape[0], half), vp, jnp.zeros_like(vp))
            ps = []
            for sc in ss:
                e = jnp.exp(sc - jnp.max(sc, axis=1, keepdims=True))
                ps.append((e / jnp.sum(e, axis=1, keepdims=True)).astype(BF16))
            outs = [out + _dot(p, vz) for out, p in zip(outs, ps)]
        for c, out in enumerate(outs):
            pairs[c].append(out.astype(BF16))
    ys = [_dot(jnp.concatenate(pc, axis=1), wo_ref[...]) for pc in pairs]
    xs = [x + _rms(y, g[3:4, :]) for x, y in zip(xs, ys)]

    hms = [_rms(x, g[4:5, :]).astype(BF16) for x in xs]
    n_chunks = D_FF // FF_CHUNK
    ys = [jnp.zeros((tc, D_MODEL), F32) for _ in chains]
    a_cur = [_dot(hm, w1_ref[:, 0:FF_CHUNK]) for hm in hms]
    for k in range(n_chunks):
        if k + 1 < n_chunks:
            a_next = [_dot(hm, w1_ref[:, (k + 1) * FF_CHUNK:(k + 2) * FF_CHUNK]) for hm in hms]
        acts = [jnp.square(jnp.maximum(a, 0.0)).astype(BF16) for a in a_cur]
        ys = [y + _dot(act, w2_ref[k * FF_CHUNK:(k + 1) * FF_CHUNK, :]) for y, act in zip(ys, acts)]
        if k + 1 < n_chunks:
            a_cur = a_next
    for r, x, y in zip(chains, xs, ys):
        o_ref[r, :] = x + _rms(y, g[5:6, :])


def _tail(x2, g, a_list, wout_list, wq, kt, v, wo, w1, w2, b, s):
    t = x2.shape[0]
    tm = min(TM_TAIL, s)
    per_b = s // tm
    n_a = len(a_list)
    row = lambda i: (i, 0)
    full2 = lambda i: (0, 0)
    once = dict(pipeline_mode=pl.Buffered(1))
    in_specs = [pl.BlockSpec((tm, D_MODEL), row), pl.BlockSpec((6, D_MODEL), full2)]
    in_specs += [pl.BlockSpec((tm, a.shape[1]), row) for a in a_list]
    in_specs += [pl.BlockSpec(w.shape, full2, **once) for w in wout_list]
    in_specs += [
        pl.BlockSpec(wq.shape, full2, **once),
        pl.BlockSpec((None,) + kt.shape[1:], lambda i: (i // per_b, 0, 0)),
        pl.BlockSpec((None,) + v.shape[1:], lambda i: (i // per_b, 0, 0)),
        pl.BlockSpec(wo.shape, full2, **once),
        pl.BlockSpec(w1.shape, full2, **once),
        pl.BlockSpec(w2.shape, full2, **once),
    ]
    return pl.pallas_call(
        functools.partial(_tail_kernel, n_a=n_a),
        out_shape=jax.ShapeDtypeStruct((t, D_MODEL), F32),
        grid=(t // tm,),
        in_specs=in_specs,
        out_specs=pl.BlockSpec((tm, D_MODEL), row),
        compiler_params=pltpu.CompilerParams(vmem_limit_bytes=VMEM_LIMIT),
        name="tail",
    )(x2, g, *a_list, *wout_list, wq, kt, v, wo, w1, w2)


def kernel(x, mem, rel_bias_table, norm_g, ev_w_in, ev_a_kv_norm, ev_a_w_uk, ev_a_w_uv, ev_b_lambda, ev_b_subln, ev_w_out, od_w_in, od_sinks, od_w_out, xa_wq, xa_wkv, xa_wo, xa_mem_norm, mlp_w1, mlp_w2):
    b, s, d = x.shape
    depth = norm_g.shape[0]
    assert d == D_MODEL and s % TK == 0 and TK % TQ_DIFF == 0 and TQ_DIFF % TQ == 0
    x2 = x.reshape(b * s, d)
    bias_a = _causal_bias_tiles(rel_bias_table[:, :H_A])
    bias_b = _causal_bias_tiles(rel_bias_table[:, H_A:H_A + H_B])
    table_c = rel_bias_table[:, H_A + H_B:]
    for l in range(depth):
        g = norm_g[l].astype(F32)
        if l % 2 == 0:
            e = l // 2
            lam_init = 0.8 - 0.6 * math.exp(-0.3 * l)
            qit, wit, qlt, ki, ckv, ctx, qbt, kb, vtx = _proj_even(x2, g, ev_w_in[e], ev_a_kv_norm[e], ev_a_w_uk[e])
            oa = _dsa(qit, wit, qlt, ki, ckv, ctx, bias_a, ev_a_w_uv[e], b, s)
            ob = _diff(qbt, kb, vtx, bias_b, ev_b_lambda[e], ev_b_subln[e], lam_init, b, s)
            n_a = H_A * DH_A
            a_list = [oa, ob]
            wout_list = [ev_w_out[e][:n_a].astype(BF16), ev_w_out[e][n_a:].astype(BF16)]
        else:
            o = l // 2
            qt, k, vtx = _proj_odd(x2, g, od_w_in[o])
            a_list = [_swa(qt, k, vtx, table_c, od_sinks[o], b, s)]
            wout_list = [od_w_out[o].astype(BF16)]
        mk, mv = _memkv(mem, xa_mem_norm[l], xa_wkv[l])
        x2 = _tail(x2, g, a_list, wout_list, xa_wq[l].astype(BF16), jnp.swapaxes(mk, 1, 2), mv,
                   xa_wo[l].astype(BF16), mlp_w1[l].astype(BF16), mlp_w2[l].astype(BF16), b, s)
    return x2.reshape(b, s, d)
```

```python
import functools
import math

import numpy as np
import jax
import jax.numpy as jnp
from jax import lax
from jax.experimental import pallas as pl
from jax.experimental.pallas import tpu as pltpu

F32 = jnp.float32
BF16 = jnp.bfloat16

D_MODEL = 1024
CHUNK = 64
EPS = 1e-6
NEG_INF = -1e30
LOG2E = math.log2(math.e)

H_A, DH_A, D_LAT = 8, 64, 128
H_IDX, DH_IDX = 8, 64
TOPK_MAX = 256
H_B, DH_B = 4, 64
H_C, H_C_KV, DH_C = 16, 2, 64
G_C = H_C // H_C_KV
WINDOW = 128
H_X, DH_X = 4, 64
D_FF = 4 * D_MODEL
N_BUCKETS = 32
MAX_DIST = 1024

LANES = 128
FOLD_ROWS = 64
TQ = 128
TQ_DIFF = 256
TK = 512
SUB = TK // LANES
ONES_ROWS = 16
TM_PROJ = 512
TM_TAIL = 512
TAIL_CHAINS = 2
FF_CHUNK = 1024
VMEM_LIMIT = 56 * 1024 * 1024
N_BISECT = 14
GROUPS = (4, 2, 1)
GROUPS_NEAR = (2, 1)
GROUPS_PASS = (4, 1)


def _rms(x, g):
    return x * lax.rsqrt(jnp.mean(x * x, axis=-1, keepdims=True) + EPS) * g


def _dot(a, b):
    return jnp.dot(a, b, preferred_element_type=F32)


def _t5_bucket_np(rel):
    nb = N_BUCKETS // 2
    max_exact = nb // 2
    n = np.abs(rel)
    nf = np.maximum(n, 1).astype(np.float32)
    large = max_exact + (np.log(nf / max_exact) / math.log(MAX_DIST / max_exact) * (nb - max_exact)).astype(np.int32)
    large = np.minimum(large, nb - 1)
    return np.where(rel > 0, nb, 0) + np.where(n < max_exact, n, large)


def _sat_blocks():
    d = 1
    while _t5_bucket_np(np.array([-(d * LANES - (LANES - 1))]))[0] != N_BUCKETS // 2 - 1:
        d += 1
    return d


N_SAT = _sat_blocks()
N_BT = N_SAT + 1


def _toeplitz(u, rows, cols):
    length = u.shape[-1]
    flat = jnp.tile(u, (1,) * (u.ndim - 1) + (rows,))[..., :rows * (length - 1)]
    return flat.reshape(u.shape[:-1] + (rows, length - 1))[..., :cols]


def _causal_bias_tiles(table_cols):
    length = 2 * LANES + 1
    m = np.arange(length)
    diff = np.where(m < LANES + 1, m, m - length)
    rel = -diff[None, :] - (np.arange(N_BT) * LANES)[:, None]
    tab = table_cols.astype(F32)
    u = jnp.take(tab, jnp.asarray(_t5_bucket_np(rel)), axis=0)
    u = (u - tab[N_BUCKETS // 2 - 1][None, None, :]) * LOG2E
    tiles = _toeplitz(jnp.moveaxis(u, -1, 0), LANES, LANES)
    return tiles.reshape(table_cols.shape[1] * N_BT, LANES, LANES)


def _fold_rows(x, op):
    acc = x[0:FOLD_ROWS, :]
    for r in range(1, x.shape[0] // FOLD_ROWS):
        acc = op(acc, x[r * FOLD_ROWS:(r + 1) * FOLD_ROWS, :])
    return acc


def _grouped_loop(lo, hi, carry, fn, groups):
    for g in groups:
        n = (hi - lo) // g
        carry = lax.fori_loop(0, n, lambda u, c, lo=lo, g=g: fn([lo + u * g + k for k in range(g)], c), carry)
        lo = lo + n * g
    return carry


def _bias_tile(bias_ref, h, qb0, n_qb, j):
    rows = []
    for c in range(SUB):
        cols = [bias_ref[h * N_BT + jnp.clip(qb0 + r - (j * SUB + c), 0, N_SAT)] for r in range(n_qb)]
        rows.append(cols[0] if n_qb == 1 else jnp.concatenate(cols, axis=1))
    return jnp.concatenate(rows, axis=0)


def _ones_rows(n):
    return jnp.ones((ONES_ROWS, n), BF16)


def _proj_even_kernel(x_ref, g_ref, w_ref, kvn_ref, wuk_ref,
                      qit_ref, wit_ref, qlt_ref, ki_ref, c_ref, ctx_ref, qbt_ref, kb_ref, vtx_ref):
    tm = x_ref.shape[0]
    h = _rms(x_ref[...], g_ref[0:1, :]).astype(BF16)
    z = _dot(h, w_ref[...])
    qa = z[:, 0:512].astype(BF16)
    ckv = _rms(z[:, 512:640], kvn_ref[...])
    c_ref[...] = ckv.astype(BF16)
    kb_ref[...] = z[:, 1664:2176].astype(BF16)
    ki_ref[...] = z[:, 2688:2688 + DH_IDX].astype(BF16)
    ctx_ref[0, D_LAT:, :] = _ones_rows(tm)
    for hb in range(H_B):
        vtx_ref[0, hb, 2 * DH_B:, :] = _ones_rows(tm)
    for hh in range(H_A):
        ql = _dot(qa[:, hh * DH_A:(hh + 1) * DH_A], wuk_ref[hh]) * (DH_A ** -0.5 * LOG2E)
        for sb in range(tm // TQ):
            qlt_ref[sb, :, hh * TQ:(hh + 1) * TQ] = ql[sb * TQ:(sb + 1) * TQ, :].T.astype(BF16)
    per_diff = TQ_DIFF // LANES
    for sb in range(tm // LANES):
        rows = slice(sb * LANES, (sb + 1) * LANES)
        for pr in range(H_IDX // 2):
            t = z[rows, 640 + pr * LANES:640 + (pr + 1) * LANES].T
            qit_ref[sb, :, (2 * pr) * TQ:(2 * pr + 1) * TQ] = t[0:DH_IDX, :].astype(BF16)
            qit_ref[sb, :, (2 * pr + 1) * TQ:(2 * pr + 2) * TQ] = t[DH_IDX:, :].astype(BF16)
        wit_ref[sb] = z[rows, 2816:2816 + LANES].T[0:H_IDX, :]
        ctx_ref[0, 0:D_LAT, rows] = ckv[rows, :].T.astype(BF16)
        for hb in range(H_B):
            qb = z[rows, 1152 + hb * LANES:1152 + (hb + 1) * LANES] * (DH_B ** -0.5 * LOG2E)
            lanes = slice((sb % per_diff) * LANES, (sb % per_diff + 1) * LANES)
            qbt_ref[sb // per_diff, hb, :, lanes] = qb.T.astype(BF16)
            vtx_ref[0, hb, 0:2 * DH_B, rows] = z[rows, 2176 + hb * LANES:2176 + (hb + 1) * LANES].T.astype(BF16)


def _proj_even(x2, g, w_in, kvn, w_uk):
    t = x2.shape[0]
    tm = TK
    assert TQ == LANES and t % tm == 0
    pad = jnp.zeros((D_MODEL, LANES - DH_IDX), w_in.dtype)
    cols = [w_in[:, 0:512], w_in[:, 512:640], w_in[:, 640:1152], w_in[:, 1224:1736], w_in[:, 1736:2248],
            w_in[:, 2248:2760], w_in[:, 1152:1216], pad, w_in[:, 1216:1224],
            jnp.zeros((D_MODEL, LANES - H_IDX), w_in.dtype)]
    w = jnp.concatenate(cols, axis=1).astype(BF16)
    n = w.shape[1]
    dv = 2 * DH_B
    row = lambda i: (i, 0)
    full2 = lambda i: (0, 0)
    lead3 = lambda i: (i, 0, 0)
    lead4 = lambda i: (i, 0, 0, 0)
    outs = [
        ((t // TQ, DH_IDX, H_IDX * TQ), BF16, (tm // TQ, DH_IDX, H_IDX * TQ), lead3),
        ((t // TQ, H_IDX, TQ), F32, (tm // TQ, H_IDX, TQ), lead3),
        ((t // TQ, D_LAT, H_A * TQ), BF16, (tm // TQ, D_LAT, H_A * TQ), lead3),
        ((t, DH_IDX), BF16, (tm, DH_IDX), row),
        ((t, D_LAT), BF16, (tm, D_LAT), row),
        ((t // TK, D_LAT + ONES_ROWS, TK), BF16, (1, D_LAT + ONES_ROWS, TK), lead3),
        ((t // TQ_DIFF, H_B, dv, TQ_DIFF), BF16, (tm // TQ_DIFF, H_B, dv, TQ_DIFF), lead4),
        ((t, H_B * dv), BF16, (tm, H_B * dv), row),
        ((t // TK, H_B, dv + ONES_ROWS, TK), BF16, (1, H_B, dv + ONES_ROWS, TK), lead4),
    ]
    return pl.pallas_call(
        _proj_even_kernel,
        out_shape=[jax.ShapeDtypeStruct(shape, dt) for shape, dt, _, _ in outs],
        grid=(t // tm,),
        in_specs=[
            pl.BlockSpec((tm, D_MODEL), row),
            pl.BlockSpec((6, D_MODEL), full2),
            pl.BlockSpec((D_MODEL, n), full2),
            pl.BlockSpec((1, D_LAT), full2),
            pl.BlockSpec((H_A, DH_A, D_LAT), lambda i: (0, 0, 0)),
        ],
        out_specs=[pl.BlockSpec(block, imap) for _, _, block, imap in outs],
        compiler_params=pltpu.CompilerParams(vmem_limit_bytes=VMEM_LIMIT),
        name="proj_even",
    )(x2, g, w, kvn.reshape(1, D_LAT), w_uk.astype(BF16))


def _proj_odd_kernel(x_ref, g_ref, w_ref, qt_ref, k_ref, vtx_ref):
    tm = x_ref.shape[0]
    h = _rms(x_ref[...], g_ref[0:1, :]).astype(BF16)
    z = _dot(h, w_ref[...])
    for kv in range(H_C_KV):
        k_ref[kv] = z[:, 1024 + kv * DH_C:1024 + (kv + 1) * DH_C].astype(BF16)
    for sb in range(tm // TQ):
        rows = slice(sb * TQ, (sb + 1) * TQ)
        for pr in range(H_C // 2):
            t = (z[rows, pr * LANES:(pr + 1) * LANES] * (DH_C ** -0.5 * LOG2E)).T
            for half in range(2):
                hd = 2 * pr + half
                kv, gq = hd // G_C, hd % G_C
                qt_ref[sb, kv, :, gq * TQ:(gq + 1) * TQ] = t[half * DH_C:(half + 1) * DH_C, :].astype(BF16)
        vt = z[rows, 1152:1280].T
        for kv in range(H_C_KV):
            vtx_ref[sb, kv, 0:DH_C, :] = vt[kv * DH_C:(kv + 1) * DH_C, :].astype(BF16)
            vtx_ref[sb, kv, DH_C:, :] = _ones_rows(TQ)


def _proj_odd(x2, g, w_in):
    t = x2.shape[0]
    tm = min(TM_PROJ, t)
    assert TQ == LANES
    row = lambda i: (i, 0)
    full2 = lambda i: (0, 0)
    lead4 = lambda i: (i, 0, 0, 0)
    dvx = DH_C + ONES_ROWS
    return pl.pallas_call(
        _proj_odd_kernel,
        out_shape=[jax.ShapeDtypeStruct((t // TQ, H_C_KV, DH_C, G_C * TQ), BF16),
                   jax.ShapeDtypeStruct((H_C_KV, t, DH_C), BF16),
                   jax.ShapeDtypeStruct((t // TQ, H_C_KV, dvx, TQ), BF16)],
        grid=(t // tm,),
        in_specs=[pl.BlockSpec((tm, D_MODEL), row),
                  pl.BlockSpec((6, D_MODEL), full2),
                  pl.BlockSpec((D_MODEL, 1280), full2)],
        out_specs=[pl.BlockSpec((tm // TQ, H_C_KV, DH_C, G_C * TQ), lead4),
                   pl.BlockSpec((H_C_KV, tm, DH_C), lambda i: (0, i, 0)),
                   pl.BlockSpec((tm // TQ, H_C_KV, dvx, TQ), lead4)],
        compiler_params=pltpu.CompilerParams(vmem_limit_bytes=VMEM_LIMIT),
        name="proj_odd",
    )(x2, g, w_in.astype(BF16))


def _dsa_kernel(qit_ref, wit_ref, qlt_ref, k_ref, c_ref, ctx_ref, bias_ref, wuvt_ref, o_ref,
                sc_ref, acc_ref, *p_refs, topk):
    i = pl.program_id(1)
    t0 = i * TQ
    nt = (t0 + TQ + TK - 1) // TK
    n_far = jnp.maximum(i - N_SAT + 1, 0) // SUB
    kf = float(topk)

    qpos = lax.broadcasted_iota(jnp.int32, (1, TQ), 1) + t0
    limit = (qpos // CHUNK + 1) * CHUNK
    small = limit <= topk
    big = jnp.logical_not(small)
    w = wit_ref[...]
    qit = qit_ref[...]

    def score_tiles(js, carry):
        rmax, rmin = carry
        zs = [_dot(k_ref[j], qit) for j in js]
        for j, z in zip(js, zs):
            sc = jnp.maximum(z[:, 0:TQ], 0.0) * w[0:1, :]
            for h in range(1, H_IDX):
                sc = sc + jnp.maximum(z[:, h * TQ:(h + 1) * TQ], 0.0) * w[h:h + 1, :]
            kpos = lax.broadcasted_iota(jnp.int32, (TK, TQ), 0) + j * TK
            adm = kpos < limit
            sc_ref[j] = jnp.where(adm, sc, -jnp.inf)
            rmax = jnp.maximum(rmax, _fold_rows(jnp.where(adm, sc, -jnp.inf), jnp.maximum))
            rmin = jnp.minimum(rmin, _fold_rows(jnp.where(adm, sc, jnp.inf), jnp.minimum))
        return rmax, rmin

    rmax, rmin = _grouped_loop(
        0, nt, (jnp.full((FOLD_ROWS, TQ), -jnp.inf, F32), jnp.full((FOLD_ROWS, TQ), jnp.inf, F32)),
        score_tiles, GROUPS)
    rmax, rmin = _grouped_loop(0, nt, (rmax, rmin), score_tiles, GROUPS)
    hi0 = jnp.max(rmax, axis=0, keepdims=True)
    lo0 = jnp.min(rmin, axis=0, keepdims=True)

    def col_reduce(tile_fn, op, init, final):
        def fn(js, acc):
            for j in js:
                acc = op(acc, _fold_rows(tile_fn(j), op))
            return acc
        acc = _grouped_loop(0, nt, jnp.full((FOLD_ROWS, TQ), init, F32), fn, GROUPS_PASS)
        return final(acc, axis=0, keepdims=True)

    def count_ge(thr):
        return col_reduce(lambda j: jnp.where(sc_ref[j] >= thr, 1.0, 0.0), jnp.add, 0.0, jnp.sum)

    def count_gt(thr):
        return col_reduce(lambda j: jnp.where(sc_ref[j] > thr, 1.0, 0.0), jnp.add, 0.0, jnp.sum)

    def max_le(thr):
        def tile(j):
            t = sc_ref[j]
            return jnp.where(t <= thr, t, -jnp.inf)
        return col_reduce(tile, jnp.maximum, -jnp.inf, jnp.max)

    def max_lt(thr):
        def tile(j):
            t = sc_ref[j]
            return jnp.where(t < thr, t, -jnp.inf)
        return col_reduce(tile, jnp.maximum, -jnp.inf, jnp.max)

    def bisect_body(_, carry):
        lo, hi = carry
        mid = lo * 0.5 + hi * 0.5
        ge = count_ge(mid) >= kf
        return jnp.where(ge, mid, lo), jnp.where(ge, hi, mid)

    lo, hi = lax.fori_loop(0, N_BISECT, bisect_body, (lo0, hi0))

    v0 = max_le(hi)
    c0 = count_ge(v0)

    def walk_cond(carry):
        _, cnt, it = carry
        pending = jnp.where((cnt < kf) & big, 1.0, 0.0)
        return (jnp.max(pending) > 0.0) & (it < topk + 2)

    def walk_body(carry):
        v, cnt, it = carry
        v = jnp.where((cnt < kf) & big, max_lt(v), v)
        return v, count_ge(v), it + 1

    tau, cnt, _ = lax.while_loop(walk_cond, walk_body, (v0, c0, jnp.int32(0)))
    tau = jnp.where(small, -jnp.finfo(F32).max, tau)

    tied = (cnt > kf) & big

    @pl.when(jnp.max(jnp.where(tied, 1.0, 0.0)) > 0.0)
    def _():
        need = kf - count_gt(tau)
        n_keys = sc_ref.shape[0] * TK

        def kpos_f(j):
            return (lax.broadcasted_iota(jnp.int32, (TK, TQ), 0) + j * TK).astype(F32)

        def count_eq_le(pos):
            return col_reduce(lambda j: jnp.where((sc_ref[j] == tau) & (kpos_f(j) <= pos), 1.0, 0.0),
                              jnp.add, 0.0, jnp.sum)

        def pos_body(_, carry):
            lo_p, hi_p = carry
            mid = jnp.floor((lo_p + hi_p) * 0.5)
            ok = count_eq_le(mid) >= need
            return jnp.where(ok, lo_p, mid), jnp.where(ok, mid, hi_p)

        n_steps = int(math.ceil(math.log2(n_keys))) + 1
        _, cut = lax.fori_loop(0, n_steps, pos_body,
                               (jnp.full((1, TQ), -1.0, F32), jnp.full((1, TQ), float(n_keys), F32)))

        def drop_body(j, _):
            t = sc_ref[j]
            sc_ref[j] = jnp.where(tied & (t == tau) & (kpos_f(j) > cut), -jnp.inf, t)
            return 0

        lax.fori_loop(0, nt, drop_body, 0)

    acc_ref[...] = jnp.zeros(acc_ref.shape, F32)
    qlt = qlt_ref[...]

    def attn_tiles(js, m, with_bias):
        s_alls = [_dot(c_ref[j], qlt) for j in js]
        for g, (j, s_all) in enumerate(zip(js, s_alls)):
            p_ref = p_refs[g]
            sel = sc_ref[j] >= tau
            m_new, alpha = [], []
            for h in range(H_A):
                cols = slice(h * TQ, (h + 1) * TQ)
                s = s_all[:, cols]
                if with_bias:
                    s = s + _bias_tile(bias_ref, h, i, 1, j)
                s = jnp.where(sel, s, NEG_INF)
                m_old = m[:, cols]
                m_h = jnp.maximum(m_old, jnp.max(_fold_rows(s, jnp.maximum), axis=0, keepdims=True))
                p_ref[:, cols] = jnp.exp2(s - m_h).astype(BF16)
                alpha.append(jnp.exp2(m_old - m_h))
                m_new.append(m_h)
            acc_ref[...] = acc_ref[...] * jnp.concatenate(alpha, axis=1) + _dot(ctx_ref[j], p_ref[...])
            m = jnp.concatenate(m_new, axis=1)
        return m

    m = jnp.full((1, H_A * TQ), NEG_INF, F32)
    m = _grouped_loop(0, n_far, m, lambda js, mm: attn_tiles(js, mm, False), GROUPS)
    _grouped_loop(n_far, nt, m, lambda js, mm: attn_tiles(js, mm, True), GROUPS_NEAR)

    outs = []
    for h in range(H_A):
        cols = slice(h * TQ, (h + 1) * TQ)
        o_t = (acc_ref[0:D_LAT, cols] / acc_ref[D_LAT:D_LAT + 1, cols]).astype(BF16)
        outs.append(_dot(wuvt_ref[h], o_t))
    for pr in range(H_A // 2):
        pair = jnp.concatenate([outs[2 * pr], outs[2 * pr + 1]], axis=0)
        o_ref[:, pr * LANES:(pr + 1) * LANES] = pair.T.astype(BF16)


def _dsa(qit, wit, qlt, ki, ckv, ctx, bias_a, w_uv, b, s):
    nq, nt = s // TQ, s // TK
    topk = min(TOPK_MAX, s // 4)
    k4 = ki.reshape(b, nt, TK, DH_IDX)
    c4 = ckv.reshape(b, nt, TK, D_LAT)
    ctx = ctx.reshape(b, nt, D_LAT + ONES_ROWS, TK)
    wuvt = jnp.swapaxes(w_uv, 1, 2).astype(BF16)
    qblk = lambda bb, i: (bb * nq + i, 0, 0)
    kv4 = lambda bb, i: (bb, 0, 0, 0)
    return pl.pallas_call(
        functools.partial(_dsa_kernel, topk=topk),
        out_shape=jax.ShapeDtypeStruct((b * s, H_A * DH_A), BF16),
        grid=(b, nq),
        in_specs=[
            pl.BlockSpec((None, DH_IDX, H_IDX * TQ), qblk),
            pl.BlockSpec((None, H_IDX, TQ), qblk),
            pl.BlockSpec((None, D_LAT, H_A * TQ), qblk),
            pl.BlockSpec((None, nt, TK, DH_IDX), kv4),
            pl.BlockSpec((None, nt, TK, D_LAT), kv4),
            pl.BlockSpec((None, nt, D_LAT + ONES_ROWS, TK), kv4),
            pl.BlockSpec((H_A * N_BT, LANES, LANES), lambda bb, i: (0, 0, 0)),
            pl.BlockSpec((H_A, DH_A, D_LAT), lambda bb, i: (0, 0, 0)),
        ],
        out_specs=pl.BlockSpec((TQ, H_A * DH_A), lambda bb, i: (bb * nq + i, 0)),
        scratch_shapes=[
            pltpu.VMEM((nt, TK, TQ), F32),
            pltpu.VMEM((D_LAT + ONES_ROWS, H_A * TQ), F32),
        ] + [pltpu.VMEM((TK, H_A * TQ), BF16)] * max(GROUPS),
        compiler_params=pltpu.CompilerParams(vmem_limit_bytes=VMEM_LIMIT),
        name="dsa_attention",
    )(qit, wit, qlt, k4, c4, ctx, bias_a, wuvt)


def _diff_kernel(qt_ref, k_ref, vtx_ref, bias_ref, lam_ref, sub_ref, o_ref, acc_ref, *p_refs, lam_init):
    h = pl.program_id(1)
    i = pl.program_id(2)
    tq = TQ_DIFF
    n_qb = tq // LANES
    t0 = i * tq
    nt = (t0 + tq + TK - 1) // TK
    n_far = jnp.maximum(i * n_qb - N_SAT + 1, 0) // SUB
    qt = qt_ref[...]
    top = lax.broadcasted_iota(jnp.int32, qt.shape, 0) < DH_B
    q2 = jnp.concatenate([jnp.where(top, qt, jnp.zeros_like(qt)), jnp.where(top, jnp.zeros_like(qt), qt)], axis=1)
    qpos = lax.broadcasted_iota(jnp.int32, (1, tq), 1) + t0
    limit = (qpos // CHUNK + 1) * CHUNK
    acc_ref[...] = jnp.zeros(acc_ref.shape, F32)

    def tiles(js, m, near):
        s2s = [_dot(k_ref[j], q2) for j in js]
        for g, (j, s2) in enumerate(zip(js, s2s)):
            p_ref = p_refs[g]
            if near:
                bias = _bias_tile(bias_ref, h, i * n_qb, n_qb, j)
                adm = (lax.broadcasted_iota(jnp.int32, (TK, tq), 0) + j * TK) < limit
            m_new, alpha = [], []
            for mp in range(2):
                cols = slice(mp * tq, (mp + 1) * tq)
                s = s2[:, cols]
                if near:
                    s = jnp.where(adm, s + bias, NEG_INF)
                m_old = m[:, cols]
                m_h = jnp.maximum(m_old, jnp.max(_fold_rows(s, jnp.maximum), axis=0, keepdims=True))
                p_ref[:, cols] = jnp.exp2(s - m_h).astype(BF16)
                alpha.append(jnp.exp2(m_old - m_h))
                m_new.append(m_h)
            acc_ref[...] = acc_ref[...] * jnp.concatenate(alpha, axis=1) + _dot(vtx_ref[j], p_ref[...])
            m = jnp.concatenate(m_new, axis=1)
        return m

    m = jnp.full((1, 2 * tq), NEG_INF, F32)
    m = _grouped_loop(0, n_far, m, lambda js, mm: tiles(js, mm, False), GROUPS)
    _grouped_loop(n_far, nt, m, lambda js, mm: tiles(js, mm, True), GROUPS_NEAR)

    lf = lam_ref[...]
    lam = (jnp.exp(jnp.sum(lf[0:1, :] * lf[1:2, :], axis=1, keepdims=True))
           - jnp.exp(jnp.sum(lf[2:3, :] * lf[3:4, :], axis=1, keepdims=True)) + lam_init)
    dv = 2 * DH_B
    o = (acc_ref[0:dv, 0:tq] / acc_ref[dv:dv + 1, 0:tq]
         - lam * (acc_ref[0:dv, tq:2 * tq] / acc_ref[dv:dv + 1, tq:2 * tq]))
    o = o * lax.rsqrt(jnp.mean(o * o, axis=0, keepdims=True) + EPS) * sub_ref[...] * (1.0 - lam_init)
    o_ref[...] = o.T.astype(BF16)


def _diff(qbt, kb, vtx, bias_b, b_lambda, b_subln, lam_init, b, s):
    tq = TQ_DIFF
    nq, nt = s // tq, s // TK
    dv = 2 * DH_B
    k4 = kb.reshape(b, nt, TK, H_B * dv)
    vtx = vtx.reshape(b, nt, H_B, dv + ONES_ROWS, TK)
    return pl.pallas_call(
        functools.partial(_diff_kernel, lam_init=lam_init),
        out_shape=jax.ShapeDtypeStruct((b * s, H_B * dv), BF16),
        grid=(b, H_B, nq),
        in_specs=[
            pl.BlockSpec((None, None, dv, tq), lambda bb, h, i: (bb * nq + i, h, 0, 0)),
            pl.BlockSpec((None, nt, TK, dv), lambda bb, h, i: (bb, 0, 0, h)),
            pl.BlockSpec((None, nt, None, dv + ONES_ROWS, TK), lambda bb, h, i: (bb, 0, h, 0, 0)),
            pl.BlockSpec((H_B * N_BT, LANES, LANES), lambda bb, h, i: (0, 0, 0)),
            pl.BlockSpec((4, DH_B), lambda bb, h, i: (0, 0)),
            pl.BlockSpec((dv, 1), lambda bb, h, i: (0, 0)),
        ],
        out_specs=pl.BlockSpec((tq, dv), lambda bb, h, i: (bb * nq + i, h)),
        scratch_shapes=[
            pltpu.VMEM((dv + ONES_ROWS, 2 * tq), F32),
        ] + [pltpu.VMEM((TK, 2 * tq), BF16)] * max(GROUPS),
        compiler_params=pltpu.CompilerParams(vmem_limit_bytes=VMEM_LIMIT),
        name="diff_attention",
    )(qbt, k4, vtx, bias_b, b_lambda.astype(F32), b_subln.reshape(dv, 1).astype(F32))


def _swa_kernel(qt_ref, ka_ref, kb_ref, vta_ref, vtb_ref, bias_ref, sink_ref, o_ref):
    i = pl.program_id(1)
    pad_pen = jnp.where(i == 0, NEG_INF, 0.0)
    for kv in range(H_C_KV):
        qt = qt_ref[kv]
        s = jnp.concatenate([_dot(ka_ref[kv], qt) + pad_pen, _dot(kb_ref[kv], qt)], axis=0) + bias_ref[kv]
        sink = sink_ref[kv]
        m = jnp.maximum(jnp.max(s, axis=0, keepdims=True), sink)
        p = jnp.exp2(s - m).astype(BF16)
        acc = _dot(vta_ref[kv], p[0:TQ, :]) + _dot(vtb_ref[kv], p[TQ:2 * TQ, :])
        o = acc[0:DH_C, :] / (acc[DH_C:DH_C + 1, :] + jnp.exp2(sink - m))
        for pr in range(G_C // 2):
            pair = jnp.concatenate([o[:, (2 * pr) * TQ:(2 * pr + 1) * TQ], o[:, (2 * pr + 1) * TQ:(2 * pr + 2) * TQ]],
                                   axis=0)
            col = (kv * G_C + 2 * pr) * DH_C
            o_ref[:, col:col + LANES] = pair.T.astype(BF16)


def _swa(qt, k, vtx, table_c, sinks, b, s):
    assert WINDOW == TQ and TQ % CHUNK == 0
    nq = s // TQ
    length = 3 * TQ + 1
    m = np.arange(length)
    diff = np.where(m < 2 * TQ + 1, m, m - length)
    u = jnp.take(table_c.astype(F32), jnp.asarray(_t5_bucket_np(diff - TQ)), axis=0) * LOG2E
    bias = _toeplitz(jnp.moveaxis(u, -1, 0), TQ, 2 * TQ)
    rblk = np.arange(TQ)[:, None] // CHUNK + WINDOW // CHUNK
    cblk = np.arange(2 * TQ)[None, :] // CHUNK
    valid = (cblk <= rblk) & (cblk >= rblk - WINDOW // CHUNK)
    bias = jnp.where(jnp.asarray(valid)[None], bias, NEG_INF)
    bias_t = jnp.transpose(bias.reshape(H_C_KV, G_C, TQ, 2 * TQ), (0, 3, 1, 2)).reshape(H_C_KV, 2 * TQ, G_C * TQ)
    sink_row = jnp.repeat(sinks.astype(F32).reshape(H_C_KV, 1, G_C) * LOG2E, TQ, axis=2)

    dvx = DH_C + ONES_ROWS
    prev = lambda bb, i: bb * nq + jnp.maximum(i - 1, 0)
    return pl.pallas_call(
        _swa_kernel,
        out_shape=jax.ShapeDtypeStruct((b * s, H_C * DH_C), BF16),
        grid=(b, nq),
        in_specs=[
            pl.BlockSpec((None, H_C_KV, DH_C, G_C * TQ), lambda bb, i: (bb * nq + i, 0, 0, 0)),
            pl.BlockSpec((H_C_KV, TQ, DH_C), lambda bb, i: (0, prev(bb, i), 0)),
            pl.BlockSpec((H_C_KV, TQ, DH_C), lambda bb, i: (0, bb * nq + i, 0)),
            pl.BlockSpec((None, H_C_KV, dvx, TQ), lambda bb, i: (prev(bb, i), 0, 0, 0)),
            pl.BlockSpec((None, H_C_KV, dvx, TQ), lambda bb, i: (bb * nq + i, 0, 0, 0)),
            pl.BlockSpec((H_C_KV, 2 * TQ, G_C * TQ), lambda bb, i: (0, 0, 0)),
            pl.BlockSpec((H_C_KV, 1, G_C * TQ), lambda bb, i: (0, 0, 0)),
        ],
        out_specs=pl.BlockSpec((TQ, H_C * DH_C), lambda bb, i: (bb * nq + i, 0)),
        compiler_params=pltpu.CompilerParams(vmem_limit_bytes=VMEM_LIMIT),
        name="swa_attention",
    )(qt, k, k, vtx, vtx, bias_t, sink_row)


def _memkv_kernel(mem_ref, g_ref, w_ref, k_ref, v_ref):
    h = _rms(mem_ref[...], g_ref[...]).astype(BF16)
    kv = _dot(h, w_ref[...])
    k_ref[...] = kv[:, 0:H_X * DH_X].astype(BF16)
    v_ref[...] = kv[:, H_X * DH_X:].astype(BF16)


def _memkv(mem, g, wkv):
    b, m, _ = mem.shape
    n = H_X * DH_X
    return pl.pallas_call(
        _memkv_kernel,
        out_shape=[jax.ShapeDtypeStruct((b, m, n), BF16), jax.ShapeDtypeStruct((b, m, n), BF16)],
        grid=(b,),
        in_specs=[pl.BlockSpec((None, m, D_MODEL), lambda i: (i, 0, 0)),
                  pl.BlockSpec((1, D_MODEL), lambda i: (0, 0)),
                  pl.BlockSpec((D_MODEL, 2 * n), lambda i: (0, 0))],
        out_specs=[pl.BlockSpec((None, m, n), lambda i: (i, 0, 0)), pl.BlockSpec((None, m, n), lambda i: (i, 0, 0))],
        compiler_params=pltpu.CompilerParams(vmem_limit_bytes=VMEM_LIMIT),
        name="mem_kv",
    )(mem, g.reshape(1, D_MODEL), wkv.astype(BF16))


def _lane_half_mask(rows, half):
    lane = lax.broadcasted_iota(jnp.int32, (rows, LANES), 1)
    return (lane < 64) if half == 0 else (lane >= 64)


def _tail_kernel(*refs, n_a):
    x_ref, g_ref = refs[0], refs[1]
    a_refs = refs[2:2 + n_a]
    wout_refs = refs[2 + n_a:2 + 2 * n_a]
    wq_ref, kt_ref, v_ref, wo_ref, w1_ref, w2_ref, o_ref = refs[2 + 2 * n_a:]
    tc = x_ref.shape[0] // TAIL_CHAINS
    chains = [slice(c * tc, (c + 1) * tc) for c in range(TAIL_CHAINS)]
    g = g_ref[...]

    ys = []
    for r in chains:
        y = _dot(a_refs[0][r, :], wout_refs[0][...])
        for a_ref, w_ref in zip(a_refs[1:], wout_refs[1:]):
            y = y + _dot(a_ref[r, :], w_ref[...])
        ys.append(y)
    xs = [x_ref[r, :] + _rms(y, g[1:2, :]) for r, y in zip(chains, ys)]

    hqs = [_rms(x, g[2:3, :]).astype(BF16) for x in xs]
    qs = [_dot(hq, wq_ref[...]) for hq in hqs]
    qs = [(q * (DH_X ** -0.5)).astype(BF16) for q in qs]
    pairs = [[] for _ in chains]
    for pr in range(H_X // 2):
        vp = v_ref[:, pr * LANES:(pr + 1) * LANES]
        outs = [jnp.zeros((tc, LANES), F32) for _ in chains]
        for half in range(2):
            qzs = [jnp.where(_lane_half_mask(tc, half), q[:, pr * LANES:(pr + 1) * LANES], jnp.zeros((tc, LANES), BF16))
                   for q in qs]
            ss = [_dot(qz, kt_ref[pr * LANES:(pr + 1) * LANES, :]) for qz in qzs]
            vz = jnp.where(_lane_half_mask(vp.shape[0], half), vp, jnp.zeros_like(vp))
            ps = []
            for sc in ss:
                e = jnp.exp(sc - jnp.max(sc, axis=1, keepdims=True))
                ps.append((e / jnp.sum(e, axis=1, keepdims=True)).astype(BF16))
            outs = [out + _dot(p, vz) for out, p in zip(outs, ps)]
        for c, out in enumerate(outs):
            pairs[c].append(out.astype(BF16))
    ys = [_dot(jnp.concatenate(pc, axis=1), wo_ref[...]) for pc in pairs]
    xs = [x + _rms(y, g[3:4, :]) for x, y in zip(xs, ys)]

    hms = [_rms(x, g[4:5, :]).astype(BF16) for x in xs]
    n_chunks = D_FF // FF_CHUNK
    ys = [jnp.zeros((tc, D_MODEL), F32) for _ in chains]
    a_cur = [_dot(hm, w1_ref[:, 0:FF_CHUNK]) for hm in hms]
    for k in range(n_chunks):
        if k + 1 < n_chunks:
            a_next = [_dot(hm, w1_ref[:, (k + 1) * FF_CHUNK:(k + 2) * FF_CHUNK]) for hm in hms]
        acts = [jnp.square(jnp.maximum(a, 0.0)).astype(BF16) for a in a_cur]
        ys = [y + _dot(act, w2_ref[k * FF_CHUNK:(k + 1) * FF_CHUNK, :]) for y, act in zip(ys, acts)]
        if k + 1 < n_chunks:
            a_cur = a_next
    for r, x, y in zip(chains, xs, ys):
        o_ref[r, :] = x + _rms(y, g[5:6, :])


def _tail(x2, g, a_list, wout_list, wq, kt, v, wo, w1, w2, b, s):
    t = x2.shape[0]
    tm = min(TM_TAIL, s)
    per_b = s // tm
    n_a = len(a_list)
    row = lambda i: (i, 0)
    full2 = lambda i: (0, 0)
    once = dict(pipeline_mode=pl.Buffered(1))
    in_specs = [pl.BlockSpec((tm, D_MODEL), row), pl.BlockSpec((6, D_MODEL), full2)]
    in_specs += [pl.BlockSpec((tm, a.shape[1]), row) for a in a_list]
    in_specs += [pl.BlockSpec(w.shape, full2, **once) for w in wout_list]
    in_specs += [
        pl.BlockSpec(wq.shape, full2, **once),
        pl.BlockSpec((None,) + kt.shape[1:], lambda i: (i // per_b, 0, 0)),
        pl.BlockSpec((None,) + v.shape[1:], lambda i: (i // per_b, 0, 0)),
        pl.BlockSpec(wo.shape, full2, **once),
        pl.BlockSpec(w1.shape, full2, **once),
        pl.BlockSpec(w2.shape, full2, **once),
    ]
    return pl.pallas_call(
        functools.partial(_tail_kernel, n_a=n_a),
        out_shape=jax.ShapeDtypeStruct((t, D_MODEL), F32),
        grid=(t // tm,),
        in_specs=in_specs,
        out_specs=pl.BlockSpec((tm, D_MODEL), row),
        compiler_params=pltpu.CompilerParams(vmem_limit_bytes=VMEM_LIMIT),
        name="tail",
    )(x2, g, *a_list, *wout_list, wq, kt, v, wo, w1, w2)


def kernel(x, mem, rel_bias_table, norm_g, ev_w_in, ev_a_kv_norm, ev_a_w_uk, ev_a_w_uv, ev_b_lambda, ev_b_subln, ev_w_out, od_w_in, od_sinks, od_w_out, xa_wq, xa_wkv, xa_wo, xa_mem_norm, mlp_w1, mlp_w2):
    b, s, d = x.shape
    depth = norm_g.shape[0]
    assert d == D_MODEL and s % TK == 0 and TK % TQ_DIFF == 0 and TQ_DIFF % TQ == 0
    x2 = x.reshape(b * s, d)
    bias_a = _causal_bias_tiles(rel_bias_table[:, :H_A])
    bias_b = _causal_bias_tiles(rel_bias_table[:, H_A:H_A + H_B])
    table_c = rel_bias_table[:, H_A + H_B:]
    for l in range(depth):
        g = norm_g[l].astype(F32)
        if l % 2 == 0:
            e = l // 2
            lam_init = 0.8 - 0.6 * math.exp(-0.3 * l)
            qit, wit, qlt, ki, ckv, ctx, qbt, kb, vtx = _proj_even(x2, g, ev_w_in[e], ev_a_kv_norm[e], ev_a_w_uk[e])
            oa = _dsa(qit, wit, qlt, ki, ckv, ctx, bias_a, ev_a_w_uv[e], b, s)
            ob = _diff(qbt, kb, vtx, bias_b, ev_b_lambda[e], ev_b_subln[e], lam_init, b, s)
            n_a = H_A * DH_A
            a_list = [oa, ob]
            wout_list = [ev_w_out[e][:n_a].astype(BF16), ev_w_out[e][n_a:].astype(BF16)]
        else:
            o = l // 2
            qt, k, vtx = _proj_odd(x2, g, od_w_in[o])
            a_list = [_swa(qt, k, vtx, table_c, od_sinks[o], b, s)]
            wout_list = [od_w_out[o].astype(BF16)]
        mk, mv = _memkv(mem, xa_mem_norm[l], xa_wkv[l])
        x2 = _tail(x2, g, a_list, wout_list, xa_wq[l].astype(BF16), jnp.swapaxes(mk, 1, 2), mv,
                   xa_wo[l].astype(BF16), mlp_w1[l].astype(BF16), mlp_w2[l].astype(BF16), b, s)
    return x2.reshape(b, s, d)
```

```python
import functools
import math

import numpy as np
import jax
import jax.numpy as jnp
from jax import lax
from jax.experimental import pallas as pl
from jax.experimental.pallas import tpu as pltpu

F32 = jnp.float32
BF16 = jnp.bfloat16

D_MODEL = 1024
CHUNK = 64
EPS = 1e-6
NEG_INF = -1e30
LOG2E = math.log2(math.e)

H_A, DH_A, D_LAT = 8, 64, 128
H_IDX, DH_IDX = 8, 64
TOPK_MAX = 256
H_B, DH_B = 4, 64
H_C, H_C_KV, DH_C = 16, 2, 64
G_C = H_C // H_C_KV
WINDOW = 128
H_X, DH_X = 4, 64
D_FF = 4 * D_MODEL
N_BUCKETS = 32
MAX_DIST = 1024

LANES = 128
FOLD_ROWS = 64
TQ = 128
TQ_DIFF = 512
TK = 512
SUB = TK // LANES
ONES_ROWS = 16
TM_PROJ = 512
TM_TAIL = 512
TAIL_CHAINS = 2
FF_CHUNK = 1024
VMEM_LIMIT = 56 * 1024 * 1024
N_BISECT = 14
GROUPS = (4, 2, 1)
GROUPS_NEAR = (2, 1)
GROUPS_PASS = (4, 1)


def _rms(x, g):
    return x * lax.rsqrt(jnp.mean(x * x, axis=-1, keepdims=True) + EPS) * g


def _dot(a, b):
    return jnp.dot(a, b, preferred_element_type=F32)


def _t5_bucket_np(rel):
    nb = N_BUCKETS // 2
    max_exact = nb // 2
    n = np.abs(rel)
    nf = np.maximum(n, 1).astype(np.float32)
    large = max_exact + (np.log(nf / max_exact) / math.log(MAX_DIST / max_exact) * (nb - max_exact)).astype(np.int32)
    large = np.minimum(large, nb - 1)
    return np.where(rel > 0, nb, 0) + np.where(n < max_exact, n, large)


def _sat_blocks():
    d = 1
    while _t5_bucket_np(np.array([-(d * LANES - (LANES - 1))]))[0] != N_BUCKETS // 2 - 1:
        d += 1
    return d


N_SAT = _sat_blocks()
N_BT = N_SAT + 1


def _toeplitz(u, rows, cols):
    length = u.shape[-1]
    flat = jnp.tile(u, (1,) * (u.ndim - 1) + (rows,))[..., :rows * (length - 1)]
    return flat.reshape(u.shape[:-1] + (rows, length - 1))[..., :cols]


def _causal_bias_tiles(table_cols):
    length = 2 * LANES + 1
    m = np.arange(length)
    diff = np.where(m < LANES + 1, m, m - length)
    rel = -diff[None, :] - (np.arange(N_BT) * LANES)[:, None]
    tab = table_cols.astype(F32)
    u = jnp.take(tab, jnp.asarray(_t5_bucket_np(rel)), axis=0)
    u = (u - tab[N_BUCKETS // 2 - 1][None, None, :]) * LOG2E
    tiles = _toeplitz(jnp.moveaxis(u, -1, 0), LANES, LANES)
    return tiles.reshape(table_cols.shape[1] * N_BT, LANES, LANES)


def _fold_rows(x, op):
    acc = x[0:FOLD_ROWS, :]
    for r in range(1, x.shape[0] // FOLD_ROWS):
        acc = op(acc, x[r * FOLD_ROWS:(r + 1) * FOLD_ROWS, :])
    return acc


def _grouped_loop(lo, hi, carry, fn, groups):
    for g in groups:
        n = (hi - lo) // g
        carry = lax.fori_loop(0, n, lambda u, c, lo=lo, g=g: fn([lo + u * g + k for k in range(g)], c), carry)
        lo = lo + n * g
    return carry


def _bias_tile(bias_ref, h, qb0, n_qb, j):
    rows = []
    for c in range(SUB):
        cols = [bias_ref[h * N_BT + jnp.clip(qb0 + r - (j * SUB + c), 0, N_SAT)] for r in range(n_qb)]
        rows.append(cols[0] if n_qb == 1 else jnp.concatenate(cols, axis=1))
    return jnp.concatenate(rows, axis=0)


def _ones_rows(n):
    return jnp.ones((ONES_ROWS, n), BF16)


def _proj_even_kernel(x_ref, g_ref, w_ref, kvn_ref, wuk_ref,
                      qit_ref, wit_ref, qlt_ref, ki_ref, c_ref, ctx_ref, qbt_ref, kb_ref, vtx_ref):
    tm = x_ref.shape[0]
    h = _rms(x_ref[...], g_ref[0:1, :]).astype(BF16)
    z = _dot(h, w_ref[...])
    qa = z[:, 0:512].astype(BF16)
    ckv = _rms(z[:, 512:640], kvn_ref[...])
    c_ref[...] = ckv.astype(BF16)
    kb_ref[...] = z[:, 1664:2176].astype(BF16)
    ki_ref[...] = z[:, 2688:2688 + DH_IDX].astype(BF16)
    ctx_ref[0, D_LAT:, :] = _ones_rows(tm)
    for hb in range(H_B):
        vtx_ref[0, hb, 2 * DH_B:, :] = _ones_rows(tm)
    for hh in range(H_A):
        ql = _dot(qa[:, hh * DH_A:(hh + 1) * DH_A], wuk_ref[hh]) * (DH_A ** -0.5 * LOG2E)
        for sb in range(tm // TQ):
            qlt_ref[sb, :, hh * TQ:(hh + 1) * TQ] = ql[sb * TQ:(sb + 1) * TQ, :].T.astype(BF16)
    per_diff = TQ_DIFF // LANES
    for sb in range(tm // LANES):
        rows = slice(sb * LANES, (sb + 1) * LANES)
        for pr in range(H_IDX // 2):
            t = z[rows, 640 + pr * LANES:640 + (pr + 1) * LANES].T
            qit_ref[sb, :, (2 * pr) * TQ:(2 * pr + 1) * TQ] = t[0:DH_IDX, :].astype(BF16)
            qit_ref[sb, :, (2 * pr + 1) * TQ:(2 * pr + 2) * TQ] = t[DH_IDX:, :].astype(BF16)
        wit_ref[sb] = z[rows, 2816:2816 + LANES].T[0:H_IDX, :]
        ctx_ref[0, 0:D_LAT, rows] = ckv[rows, :].T.astype(BF16)
        for hb in range(H_B):
            qb = z[rows, 1152 + hb * LANES:1152 + (hb + 1) * LANES] * (DH_B ** -0.5 * LOG2E)
            lanes = slice((sb % per_diff) * LANES, (sb % per_diff + 1) * LANES)
            qbt_ref[sb // per_diff, hb, :, lanes] = qb.T.astype(BF16)
            vtx_ref[0, hb, 0:2 * DH_B, rows] = z[rows, 2176 + hb * LANES:2176 + (hb + 1) * LANES].T.astype(BF16)


def _proj_even(x2, g, w_in, kvn, w_uk):
    t = x2.shape[0]
    tm = TK
    assert TQ == LANES and t % tm == 0
    pad = jnp.zeros((D_MODEL, LANES - DH_IDX), w_in.dtype)
    cols = [w_in[:, 0:512], w_in[:, 512:640], w_in[:, 640:1152], w_in[:, 1224:1736], w_in[:, 1736:2248],
            w_in[:, 2248:2760], w_in[:, 1152:1216], pad, w_in[:, 1216:1224],
            jnp.zeros((D_MODEL, LANES - H_IDX), w_in.dtype)]
    w = jnp.concatenate(cols, axis=1).astype(BF16)
    n = w.shape[1]
    dv = 2 * DH_B
    row = lambda i: (i, 0)
    full2 = lambda i: (0, 0)
    lead3 = lambda i: (i, 0, 0)
    lead4 = lambda i: (i, 0, 0, 0)
    outs = [
        ((t // TQ, DH_IDX, H_IDX * TQ), BF16, (tm // TQ, DH_IDX, H_IDX * TQ), lead3),
        ((t // TQ, H_IDX, TQ), F32, (tm // TQ, H_IDX, TQ), lead3),
        ((t // TQ, D_LAT, H_A * TQ), BF16, (tm // TQ, D_LAT, H_A * TQ), lead3),
        ((t, DH_IDX), BF16, (tm, DH_IDX), row),
        ((t, D_LAT), BF16, (tm, D_LAT), row),
        ((t // TK, D_LAT + ONES_ROWS, TK), BF16, (1, D_LAT + ONES_ROWS, TK), lead3),
        ((t // TQ_DIFF, H_B, dv, TQ_DIFF), BF16, (tm // TQ_DIFF, H_B, dv, TQ_DIFF), lead4),
        ((t, H_B * dv), BF16, (tm, H_B * dv), row),
        ((t // TK, H_B, dv + ONES_ROWS, TK), BF16, (1, H_B, dv + ONES_ROWS, TK), lead4),
    ]
    return pl.pallas_call(
        _proj_even_kernel,
        out_shape=[jax.ShapeDtypeStruct(shape, dt) for shape, dt, _, _ in outs],
        grid=(t // tm,),
        in_specs=[
            pl.BlockSpec((tm, D_MODEL), row),
            pl.BlockSpec((6, D_MODEL), full2),
            pl.BlockSpec((D_MODEL, n), full2),
            pl.BlockSpec((1, D_LAT), full2),
            pl.BlockSpec((H_A, DH_A, D_LAT), lambda i: (0, 0, 0)),
        ],
        out_specs=[pl.BlockSpec(block, imap) for _, _, block, imap in outs],
        compiler_params=pltpu.CompilerParams(vmem_limit_bytes=VMEM_LIMIT),
        name="proj_even",
    )(x2, g, w, kvn.reshape(1, D_LAT), w_uk.astype(BF16))


def _proj_odd_kernel(x_ref, g_ref, w_ref, qt_ref, k_ref, vtx_ref):
    tm = x_ref.shape[0]
    h = _rms(x_ref[...], g_ref[0:1, :]).astype(BF16)
    z = _dot(h, w_ref[...])
    for kv in range(H_C_KV):
        k_ref[kv] = z[:, 1024 + kv * DH_C:1024 + (kv + 1) * DH_C].astype(BF16)
    for sb in range(tm // TQ):
        rows = slice(sb * TQ, (sb + 1) * TQ)
        for pr in range(H_C // 2):
            t = (z[rows, pr * LANES:(pr + 1) * LANES] * (DH_C ** -0.5 * LOG2E)).T
            for half in range(2):
                hd = 2 * pr + half
                kv, gq = hd // G_C, hd % G_C
                qt_ref[sb, kv, :, gq * TQ:(gq + 1) * TQ] = t[half * DH_C:(half + 1) * DH_C, :].astype(BF16)
        vt = z[rows, 1152:1280].T
        for kv in range(H_C_KV):
            vtx_ref[sb, kv, 0:DH_C, :] = vt[kv * DH_C:(kv + 1) * DH_C, :].astype(BF16)
            vtx_ref[sb, kv, DH_C:, :] = _ones_rows(TQ)


def _proj_odd(x2, g, w_in):
    t = x2.shape[0]
    tm = min(TM_PROJ, t)
    assert TQ == LANES
    row = lambda i: (i, 0)
    full2 = lambda i: (0, 0)
    lead4 = lambda i: (i, 0, 0, 0)
    dvx = DH_C + ONES_ROWS
    return pl.pallas_call(
        _proj_odd_kernel,
        out_shape=[jax.ShapeDtypeStruct((t // TQ, H_C_KV, DH_C, G_C * TQ), BF16),
                   jax.ShapeDtypeStruct((H_C_KV, t, DH_C), BF16),
                   jax.ShapeDtypeStruct((t // TQ, H_C_KV, dvx, TQ), BF16)],
        grid=(t // tm,),
        in_specs=[pl.BlockSpec((tm, D_MODEL), row),
                  pl.BlockSpec((6, D_MODEL), full2),
                  pl.BlockSpec((D_MODEL, 1280), full2)],
        out_specs=[pl.BlockSpec((tm // TQ, H_C_KV, DH_C, G_C * TQ), lead4),
                   pl.BlockSpec((H_C_KV, tm, DH_C), lambda i: (0, i, 0)),
                   pl.BlockSpec((tm // TQ, H_C_KV, dvx, TQ), lead4)],
        compiler_params=pltpu.CompilerParams(vmem_limit_bytes=VMEM_LIMIT),
        name="proj_odd",
    )(x2, g, w_in.astype(BF16))


def _dsa_kernel(qit_ref, wit_ref, qlt_ref, k_ref, c_ref, ctx_ref, bias_ref, wuvt_ref, o_ref,
                sc_ref, acc_ref, *p_refs, topk):
    i = pl.program_id(1)
    t0 = i * TQ
    nt = (t0 + TQ + TK - 1) // TK
    n_far = jnp.maximum(i - N_SAT + 1, 0) // SUB
    kf = float(topk)

    qpos = lax.broadcasted_iota(jnp.int32, (1, TQ), 1) + t0
    limit = (qpos // CHUNK + 1) * CHUNK
    small = limit <= topk
    big = jnp.logical_not(small)
    w = wit_ref[...]
    qit = qit_ref[...]

    def score_tiles(js, carry):
        rmax, rmin = carry
        zs = [_dot(k_ref[j], qit) for j in js]
        for j, z in zip(js, zs):
            sc = jnp.maximum(z[:, 0:TQ], 0.0) * w[0:1, :]
            for h in range(1, H_IDX):
                sc = sc + jnp.maximum(z[:, h * TQ:(h + 1) * TQ], 0.0) * w[h:h + 1, :]
            kpos = lax.broadcasted_iota(jnp.int32, (TK, TQ), 0) + j * TK
            adm = kpos < limit
            sc_ref[j] = jnp.where(adm, sc, -jnp.inf)
            rmax = jnp.maximum(rmax, _fold_rows(jnp.where(adm, sc, -jnp.inf), jnp.maximum))
            rmin = jnp.minimum(rmin, _fold_rows(jnp.where(adm, sc, jnp.inf), jnp.minimum))
        return rmax, rmin

    rmax, rmin = _grouped_loop(
        0, nt, (jnp.full((FOLD_ROWS, TQ), -jnp.inf, F32), jnp.full((FOLD_ROWS, TQ), jnp.inf, F32)),
        score_tiles, GROUPS)
    hi0 = jnp.max(rmax, axis=0, keepdims=True)
    lo0 = jnp.min(rmin, axis=0, keepdims=True)

    def col_reduce(tile_fn, op, init, final):
        def fn(js, acc):
            for j in js:
                acc = op(acc, _fold_rows(tile_fn(j), op))
            return acc
        acc = _grouped_loop(0, nt, jnp.full((FOLD_ROWS, TQ), init, F32), fn, GROUPS_PASS)
        return final(acc, axis=0, keepdims=True)

    def count_ge(thr):
        return col_reduce(lambda j: jnp.where(sc_ref[j] >= thr, 1.0, 0.0), jnp.add, 0.0, jnp.sum)

    def count_gt(thr):
        return col_reduce(lambda j: jnp.where(sc_ref[j] > thr, 1.0, 0.0), jnp.add, 0.0, jnp.sum)

    def max_le(thr):
        def tile(j):
            t = sc_ref[j]
            return jnp.where(t <= thr, t, -jnp.inf)
        return col_reduce(tile, jnp.maximum, -jnp.inf, jnp.max)

    def max_lt(thr):
        def tile(j):
            t = sc_ref[j]
            return jnp.where(t < thr, t, -jnp.inf)
        return col_reduce(tile, jnp.maximum, -jnp.inf, jnp.max)

    def bisect_body(_, carry):
        lo, hi = carry
        mid = lo * 0.5 + hi * 0.5
        ge = count_ge(mid) >= kf
        return jnp.where(ge, mid, lo), jnp.where(ge, hi, mid)

    lo, hi = lax.fori_loop(0, N_BISECT, bisect_body, (lo0, hi0))

    v0 = max_le(hi)
    c0 = count_ge(v0)

    def walk_cond(carry):
        _, cnt, it = carry
        pending = jnp.where((cnt < kf) & big, 1.0, 0.0)
        return (jnp.max(pending) > 0.0) & (it < topk + 2)

    def walk_body(carry):
        v, cnt, it = carry
        v = jnp.where((cnt < kf) & big, max_lt(v), v)
        return v, count_ge(v), it + 1

    tau, cnt, _ = lax.while_loop(walk_cond, walk_body, (v0, c0, jnp.int32(0)))
    tau = jnp.where(small, -jnp.finfo(F32).max, tau)

    tied = (cnt > kf) & big

    @pl.when(jnp.max(jnp.where(tied, 1.0, 0.0)) > 0.0)
    def _():
        need = kf - count_gt(tau)
        n_keys = sc_ref.shape[0] * TK

        def kpos_f(j):
            return (lax.broadcasted_iota(jnp.int32, (TK, TQ), 0) + j * TK).astype(F32)

        def count_eq_le(pos):
            return col_reduce(lambda j: jnp.where((sc_ref[j] == tau) & (kpos_f(j) <= pos), 1.0, 0.0),
                              jnp.add, 0.0, jnp.sum)

        def pos_body(_, carry):
            lo_p, hi_p = carry
            mid = jnp.floor((lo_p + hi_p) * 0.5)
            ok = count_eq_le(mid) >= need
            return jnp.where(ok, lo_p, mid), jnp.where(ok, mid, hi_p)

        n_steps = int(math.ceil(math.log2(n_keys))) + 1
        _, cut = lax.fori_loop(0, n_steps, pos_body,
                               (jnp.full((1, TQ), -1.0, F32), jnp.full((1, TQ), float(n_keys), F32)))

        def drop_body(j, _):
            t = sc_ref[j]
            sc_ref[j] = jnp.where(tied & (t == tau) & (kpos_f(j) > cut), -jnp.inf, t)
            return 0

        lax.fori_loop(0, nt, drop_body, 0)

    acc_ref[...] = jnp.zeros(acc_ref.shape, F32)
    qlt = qlt_ref[...]

    def attn_tiles(js, m, with_bias):
        s_alls = [_dot(c_ref[j], qlt) for j in js]
        for g, (j, s_all) in enumerate(zip(js, s_alls)):
            p_ref = p_refs[g]
            sel = sc_ref[j] >= tau
            m_new, alpha = [], []
            for h in range(H_A):
                cols = slice(h * TQ, (h + 1) * TQ)
                s = s_all[:, cols]
                if with_bias:
                    s = s + _bias_tile(bias_ref, h, i, 1, j)
                s = jnp.where(sel, s, NEG_INF)
                m_old = m[:, cols]
                m_h = jnp.maximum(m_old, jnp.max(_fold_rows(s, jnp.maximum), axis=0, keepdims=True))
                p_ref[:, cols] = jnp.exp2(s - m_h).astype(BF16)
                alpha.append(jnp.exp2(m_old - m_h))
                m_new.append(m_h)
            acc_ref[...] = acc_ref[...] * jnp.concatenate(alpha, axis=1) + _dot(ctx_ref[j], p_ref[...])
            m = jnp.concatenate(m_new, axis=1)
        return m

    m = jnp.full((1, H_A * TQ), NEG_INF, F32)
    m = _grouped_loop(0, n_far, m, lambda js, mm: attn_tiles(js, mm, False), GROUPS)
    _grouped_loop(n_far, nt, m, lambda js, mm: attn_tiles(js, mm, True), GROUPS_NEAR)

    outs = []
    for h in range(H_A):
        cols = slice(h * TQ, (h + 1) * TQ)
        o_t = (acc_ref[0:D_LAT, cols] / acc_ref[D_LAT:D_LAT + 1, cols]).astype(BF16)
        outs.append(_dot(wuvt_ref[h], o_t))
    for pr in range(H_A // 2):
        pair = jnp.concatenate([outs[2 * pr], outs[2 * pr + 1]], axis=0)
        o_ref[:, pr * LANES:(pr + 1) * LANES] = pair.T.astype(BF16)


def _dsa(qit, wit, qlt, ki, ckv, ctx, bias_a, w_uv, b, s):
    nq, nt = s // TQ, s // TK
    topk = min(TOPK_MAX, s // 4)
    k4 = ki.reshape(b, nt, TK, DH_IDX)
    c4 = ckv.reshape(b, nt, TK, D_LAT)
    ctx = ctx.reshape(b, nt, D_LAT + ONES_ROWS, TK)
    wuvt = jnp.swapaxes(w_uv, 1, 2).astype(BF16)
    qblk = lambda bb, i: (bb * nq + i, 0, 0)
    kv4 = lambda bb, i: (bb, 0, 0, 0)
    return pl.pallas_call(
        functools.partial(_dsa_kernel, topk=topk),
        out_shape=jax.ShapeDtypeStruct((b * s, H_A * DH_A), BF16),
        grid=(b, nq),
        in_specs=[
            pl.BlockSpec((None, DH_IDX, H_IDX * TQ), qblk),
            pl.BlockSpec((None, H_IDX, TQ), qblk),
            pl.BlockSpec((None, D_LAT, H_A * TQ), qblk),
            pl.BlockSpec((None, nt, TK, DH_IDX), kv4),
            pl.BlockSpec((None, nt, TK, D_LAT), kv4),
            pl.BlockSpec((None, nt, D_LAT + ONES_ROWS, TK), kv4),
            pl.BlockSpec((H_A * N_BT, LANES, LANES), lambda bb, i: (0, 0, 0)),
            pl.BlockSpec((H_A, DH_A, D_LAT), lambda bb, i: (0, 0, 0)),
        ],
        out_specs=pl.BlockSpec((TQ, H_A * DH_A), lambda bb, i: (bb * nq + i, 0)),
        scratch_shapes=[
            pltpu.VMEM((nt, TK, TQ), F32),
            pltpu.VMEM((D_LAT + ONES_ROWS, H_A * TQ), F32),
        ] + [pltpu.VMEM((TK, H_A * TQ), BF16)] * max(GROUPS),
        compiler_params=pltpu.CompilerParams(vmem_limit_bytes=VMEM_LIMIT),
        name="dsa_attention",
    )(qit, wit, qlt, k4, c4, ctx, bias_a, wuvt)


def _diff_kernel(qt_ref, k_ref, vtx_ref, bias_ref, lam_ref, sub_ref, o_ref, acc_ref, *p_refs, lam_init):
    h = pl.program_id(1)
    i = pl.program_id(2)
    tq = TQ_DIFF
    n_qb = tq // LANES
    t0 = i * tq
    nt = (t0 + tq + TK - 1) // TK
    n_far = jnp.maximum(i * n_qb - N_SAT + 1, 0) // SUB
    qt = qt_ref[...]
    top = lax.broadcasted_iota(jnp.int32, qt.shape, 0) < DH_B
    q2 = jnp.concatenate([jnp.where(top, qt, jnp.zeros_like(qt)), jnp.where(top, jnp.zeros_like(qt), qt)], axis=1)
    qpos = lax.broadcasted_iota(jnp.int32, (1, tq), 1) + t0
    limit = (qpos // CHUNK + 1) * CHUNK
    acc_ref[...] = jnp.zeros(acc_ref.shape, F32)

    def tiles(js, m, near):
        s2s = [_dot(k_ref[j], q2) for j in js]
        for g, (j, s2) in enumerate(zip(js, s2s)):
            p_ref = p_refs[g]
            if near:
                bias = _bias_tile(bias_ref, h, i * n_qb, n_qb, j)
                adm = (lax.broadcasted_iota(jnp.int32, (TK, tq), 0) + j * TK) < limit
            m_new, alpha = [], []
            for mp in range(2):
                cols = slice(mp * tq, (mp + 1) * tq)
                s = s2[:, cols]
                if near:
                    s = jnp.where(adm, s + bias, NEG_INF)
                m_old = m[:, cols]
                m_h = jnp.maximum(m_old, jnp.max(_fold_rows(s, jnp.maximum), axis=0, keepdims=True))
                p_ref[:, cols] = jnp.exp2(s - m_h).astype(BF16)
                alpha.append(jnp.exp2(m_old - m_h))
                m_new.append(m_h)
            acc_ref[...] = acc_ref[...] * jnp.concatenate(alpha, axis=1) + _dot(vtx_ref[j], p_ref[...])
            m = jnp.concatenate(m_new, axis=1)
        return m

    m = jnp.full((1, 2 * tq), NEG_INF, F32)
    m = _grouped_loop(0, n_far, m, lambda js, mm: tiles(js, mm, False), GROUPS)
    _grouped_loop(n_far, nt, m, lambda js, mm: tiles(js, mm, True), GROUPS_NEAR)

    lf = lam_ref[...]
    lam = (jnp.exp(jnp.sum(lf[0:1, :] * lf[1:2, :], axis=1, keepdims=True))
           - jnp.exp(jnp.sum(lf[2:3, :] * lf[3:4, :], axis=1, keepdims=True)) + lam_init)
    dv = 2 * DH_B
    o = (acc_ref[0:dv, 0:tq] / acc_ref[dv:dv + 1, 0:tq]
         - lam * (acc_ref[0:dv, tq:2 * tq] / acc_ref[dv:dv + 1, tq:2 * tq]))
    o = o * lax.rsqrt(jnp.mean(o * o, axis=0, keepdims=True) + EPS) * sub_ref[...] * (1.0 - lam_init)
    o_ref[...] = o.T.astype(BF16)


def _diff(qbt, kb, vtx, bias_b, b_lambda, b_subln, lam_init, b, s):
    tq = TQ_DIFF
    nq, nt = s // tq, s // TK
    dv = 2 * DH_B
    k4 = kb.reshape(b, nt, TK, H_B * dv)
    vtx = vtx.reshape(b, nt, H_B, dv + ONES_ROWS, TK)
    return pl.pallas_call(
        functools.partial(_diff_kernel, lam_init=lam_init),
        out_shape=jax.ShapeDtypeStruct((b * s, H_B * dv), BF16),
        grid=(b, H_B, nq),
        in_specs=[
            pl.BlockSpec((None, None, dv, tq), lambda bb, h, i: (bb * nq + i, h, 0, 0)),
            pl.BlockSpec((None, nt, TK, dv), lambda bb, h, i: (bb, 0, 0, h)),
            pl.BlockSpec((None, nt, None, dv + ONES_ROWS, TK), lambda bb, h, i: (bb, 0, h, 0, 0)),
            pl.BlockSpec((H_B * N_BT, LANES, LANES), lambda bb, h, i: (0, 0, 0)),
            pl.BlockSpec((4, DH_B), lambda bb, h, i: (0, 0)),
            pl.BlockSpec((dv, 1), lambda bb, h, i: (0, 0)),
        ],
        out_specs=pl.BlockSpec((tq, dv), lambda bb, h, i: (bb * nq + i, h)),
        scratch_shapes=[
            pltpu.VMEM((dv + ONES_ROWS, 2 * tq), F32),
        ] + [pltpu.VMEM((TK, 2 * tq), BF16)] * max(GROUPS),
        compiler_params=pltpu.CompilerParams(vmem_limit_bytes=VMEM_LIMIT),
        name="diff_attention",
    )(qbt, k4, vtx, bias_b, b_lambda.astype(F32), b_subln.reshape(dv, 1).astype(F32))


def _swa_kernel(qt_ref, ka_ref, kb_ref, vta_ref, vtb_ref, bias_ref, sink_ref, o_ref):
    i = pl.program_id(1)
    pad_pen = jnp.where(i == 0, NEG_INF, 0.0)
    for kv in range(H_C_KV):
        qt = qt_ref[kv]
        s = jnp.concatenate([_dot(ka_ref[kv], qt) + pad_pen, _dot(kb_ref[kv], qt)], axis=0) + bias_ref[kv]
        sink = sink_ref[kv]
        m = jnp.maximum(jnp.max(s, axis=0, keepdims=True), sink)
        p = jnp.exp2(s - m).astype(BF16)
        acc = _dot(vta_ref[kv], p[0:TQ, :]) + _dot(vtb_ref[kv], p[TQ:2 * TQ, :])
        o = acc[0:DH_C, :] / (acc[DH_C:DH_C + 1, :] + jnp.exp2(sink - m))
        for pr in range(G_C // 2):
            pair = jnp.concatenate([o[:, (2 * pr) * TQ:(2 * pr + 1) * TQ], o[:, (2 * pr + 1) * TQ:(2 * pr + 2) * TQ]],
                                   axis=0)
            col = (kv * G_C + 2 * pr) * DH_C
            o_ref[:, col:col + LANES] = pair.T.astype(BF16)


def _swa(qt, k, vtx, table_c, sinks, b, s):
    assert WINDOW == TQ and TQ % CHUNK == 0
    nq = s // TQ
    length = 3 * TQ + 1
    m = np.arange(length)
    diff = np.where(m < 2 * TQ + 1, m, m - length)
    u = jnp.take(table_c.astype(F32), jnp.asarray(_t5_bucket_np(diff - TQ)), axis=0) * LOG2E
    bias = _toeplitz(jnp.moveaxis(u, -1, 0), TQ, 2 * TQ)
    rblk = np.arange(TQ)[:, None] // CHUNK + WINDOW // CHUNK
    cblk = np.arange(2 * TQ)[None, :] // CHUNK
    valid = (cblk <= rblk) & (cblk >= rblk - WINDOW // CHUNK)
    bias = jnp.where(jnp.asarray(valid)[None], bias, NEG_INF)
    bias_t = jnp.transpose(bias.reshape(H_C_KV, G_C, TQ, 2 * TQ), (0, 3, 1, 2)).reshape(H_C_KV, 2 * TQ, G_C * TQ)
    sink_row = jnp.repeat(sinks.astype(F32).reshape(H_C_KV, 1, G_C) * LOG2E, TQ, axis=2)

    dvx = DH_C + ONES_ROWS
    prev = lambda bb, i: bb * nq + jnp.maximum(i - 1, 0)
    return pl.pallas_call(
        _swa_kernel,
        out_shape=jax.ShapeDtypeStruct((b * s, H_C * DH_C), BF16),
        grid=(b, nq),
        in_specs=[
            pl.BlockSpec((None, H_C_KV, DH_C, G_C * TQ), lambda bb, i: (bb * nq + i, 0, 0, 0)),
            pl.BlockSpec((H_C_KV, TQ, DH_C), lambda bb, i: (0, prev(bb, i), 0)),
            pl.BlockSpec((H_C_KV, TQ, DH_C), lambda bb, i: (0, bb * nq + i, 0)),
            pl.BlockSpec((None, H_C_KV, dvx, TQ), lambda bb, i: (prev(bb, i), 0, 0, 0)),
            pl.BlockSpec((None, H_C_KV, dvx, TQ), lambda bb, i: (bb * nq + i, 0, 0, 0)),
            pl.BlockSpec((H_C_KV, 2 * TQ, G_C * TQ), lambda bb, i: (0, 0, 0)),
            pl.BlockSpec((H_C_KV, 1, G_C * TQ), lambda bb, i: (0, 0, 0)),
        ],
        out_specs=pl.BlockSpec((TQ, H_C * DH_C), lambda bb, i: (bb * nq + i, 0)),
        compiler_params=pltpu.CompilerParams(vmem_limit_bytes=VMEM_LIMIT),
        name="swa_attention",
    )(qt, k, k, vtx, vtx, bias_t, sink_row)


def _memkv_kernel(mem_ref, g_ref, w_ref, k_ref, v_ref):
    h = _rms(mem_ref[...], g_ref[...]).astype(BF16)
    kv = _dot(h, w_ref[...])
    k_ref[...] = kv[:, 0:H_X * DH_X].astype(BF16)
    v_ref[...] = kv[:, H_X * DH_X:].astype(BF16)


def _memkv(mem, g, wkv):
    b, m, _ = mem.shape
    n = H_X * DH_X
    return pl.pallas_call(
        _memkv_kernel,
        out_shape=[jax.ShapeDtypeStruct((b, m, n), BF16), jax.ShapeDtypeStruct((b, m, n), BF16)],
        grid=(b,),
        in_specs=[pl.BlockSpec((None, m, D_MODEL), lambda i: (i, 0, 0)),
                  pl.BlockSpec((1, D_MODEL), lambda i: (0, 0)),
                  pl.BlockSpec((D_MODEL, 2 * n), lambda i: (0, 0))],
        out_specs=[pl.BlockSpec((None, m, n), lambda i: (i, 0, 0)), pl.BlockSpec((None, m, n), lambda i: (i, 0, 0))],
        compiler_params=pltpu.CompilerParams(vmem_limit_bytes=VMEM_LIMIT),
        name="mem_kv",
    )(mem, g.reshape(1, D_MODEL), wkv.astype(BF16))


def _lane_half_mask(rows, half):
    lane = lax.broadcasted_iota(jnp.int32, (rows, LANES), 1)
    return (lane < 64) if half == 0 else (lane >= 64)


def _tail_kernel(*refs, n_a):
    x_ref, g_ref = refs[0], refs[1]
    a_refs = refs[2:2 + n_a]
    wout_refs = refs[2 + n_a:2 + 2 * n_a]
    wq_ref, kt_ref, v_ref, wo_ref, w1_ref, w2_ref, o_ref = refs[2 + 2 * n_a:]
    tc = x_ref.shape[0] // TAIL_CHAINS
    chains = [slice(c * tc, (c + 1) * tc) for c in range(TAIL_CHAINS)]
    g = g_ref[...]

    ys = []
    for r in chains:
        y = _dot(a_refs[0][r, :], wout_refs[0][...])
        for a_ref, w_ref in zip(a_refs[1:], wout_refs[1:]):
            y = y + _dot(a_ref[r, :], w_ref[...])
        ys.append(y)
    xs = [x_ref[r, :] + _rms(y, g[1:2, :]) for r, y in zip(chains, ys)]

    hqs = [_rms(x, g[2:3, :]).astype(BF16) for x in xs]
    qs = [_dot(hq, wq_ref[...]) for hq in hqs]
    qs = [(q * (DH_X ** -0.5)).astype(BF16) for q in qs]
    pairs = [[] for _ in chains]
    for pr in range(H_X // 2):
        vp = v_ref[:, pr * LANES:(pr + 1) * LANES]
        outs = [jnp.zeros((tc, LANES), F32) for _ in chains]
        for half in range(2):
            qzs = [jnp.where(_lane_half_mask(tc, half), q[:, pr * LANES:(pr + 1) * LANES], jnp.zeros((tc, LANES), BF16))
                   for q in qs]
            ss = [_dot(qz, kt_ref[pr * LANES:(pr + 1) * LANES, :]) for qz in qzs]
            vz = jnp.where(_lane_half_mask(vp.shape[0], half), vp, jnp.zeros_like(vp))
            ps = []
            for sc in ss:
                e = jnp.exp(sc - jnp.max(sc, axis=1, keepdims=True))
                ps.append((e / jnp.sum(e, axis=1, keepdims=True)).astype(BF16))
            outs = [out + _dot(p, vz) for out, p in zip(outs, ps)]
        for c, out in enumerate(outs):
            pairs[c].append(out.astype(BF16))
    ys = [_dot(jnp.concatenate(pc, axis=1), wo_ref[...]) for pc in pairs]
    xs = [x + _rms(y, g[3:4, :]) for x, y in zip(xs, ys)]

    hms = [_rms(x, g[4:5, :]).astype(BF16) for x in xs]
    n_chunks = D_FF // FF_CHUNK
    ys = [jnp.zeros((tc, D_MODEL), F32) for _ in chains]
    a_cur = [_dot(hm, w1_ref[:, 0:FF_CHUNK]) for hm in hms]
    for k in range(n_chunks):
        if k + 1 < n_chunks:
            a_next = [_dot(hm, w1_ref[:, (k + 1) * FF_CHUNK:(k + 2) * FF_CHUNK]) for hm in hms]
        acts = [jnp.square(jnp.maximum(a, 0.0)).astype(BF16) for a in a_cur]
        ys = [y + _dot(act, w2_ref[k * FF_CHUNK:(k + 1) * FF_CHUNK, :]) for y, act in zip(ys, acts)]
        if k + 1 < n_chunks:
            a_cur = a_next
    for r, x, y in zip(chains, xs, ys):
        o_ref[r, :] = x + _rms(y, g[5:6, :])


def _tail(x2, g, a_list, wout_list, wq, kt, v, wo, w1, w2, b, s):
    t = x2.shape[0]
    tm = min(TM_TAIL, s)
    per_b = s // tm
    n_a = len(a_list)
    row = lambda i: (i, 0)
    full2 = lambda i: (0, 0)
    once = dict(pipeline_mode=pl.Buffered(1))
    in_specs = [pl.BlockSpec((tm, D_MODEL), row), pl.BlockSpec((6, D_MODEL), full2)]
    in_specs += [pl.BlockSpec((tm, a.shape[1]), row) for a in a_list]
    in_specs += [pl.BlockSpec(w.shape, full2, **once) for w in wout_list]
    in_specs += [
        pl.BlockSpec(wq.shape, full2, **once),
        pl.BlockSpec((None,) + kt.shape[1:], lambda i: (i // per_b, 0, 0)),
        pl.BlockSpec((None,) + v.shape[1:], lambda i: (i // per_b, 0, 0)),
        pl.BlockSpec(wo.shape, full2, **once),
        pl.BlockSpec(w1.shape, full2, **once),
        pl.BlockSpec(w2.shape, full2, **once),
    ]
    return pl.pallas_call(
        functools.partial(_tail_kernel, n_a=n_a),
        out_shape=jax.ShapeDtypeStruct((t, D_MODEL), F32),
        grid=(t // tm,),
        in_specs=in_specs,
        out_specs=pl.BlockSpec((tm, D_MODEL), row),
        compiler_params=pltpu.CompilerParams(vmem_limit_bytes=VMEM_LIMIT),
        name="tail",
    )(x2, g, *a_list, *wout_list, wq, kt, v, wo, w1, w2)


def kernel(x, mem, rel_bias_table, norm_g, ev_w_in, ev_a_kv_norm, ev_a_w_uk, ev_a_w_uv, ev_b_lambda, ev_b_subln, ev_w_out, od_w_in, od_sinks, od_w_out, xa_wq, xa_wkv, xa_wo, xa_mem_norm, mlp_w1, mlp_w2):
    b, s, d = x.shape
    depth = norm_g.shape[0]
    assert d == D_MODEL and s % TK == 0 and TK % TQ_DIFF == 0 and TQ_DIFF % TQ == 0
    x2 = x.reshape(b * s, d)
    bias_a = _causal_bias_tiles(rel_bias_table[:, :H_A])
    bias_b = _causal_bias_tiles(rel_bias_table[:, H_A:H_A + H_B])
    table_c = rel_bias_table[:, H_A + H_B:]
    for l in range(depth):
        g = norm_g[l].astype(F32)
        if l % 2 == 0:
            e = l // 2
            lam_init = 0.8 - 0.6 * math.exp(-0.3 * l)
            qit, wit, qlt, ki, ckv, ctx, qbt, kb, vtx = _proj_even(x2, g, ev_w_in[e], ev_a_kv_norm[e], ev_a_w_uk[e])
            oa = _dsa(qit, wit, qlt, ki, ckv, ctx, bias_a, ev_a_w_uv[e], b, s)
            ob = _diff(qbt, kb, vtx, bias_b, ev_b_lambda[e], ev_b_subln[e], lam_init, b, s)
            n_a = H_A * DH_A
            a_list = [oa, ob]
            wout_list = [ev_w_out[e][:n_a].astype(BF16), ev_w_out[e][n_a:].astype(BF16)]
        else:
            o = l // 2
            qt, k, vtx = _proj_odd(x2, g, od_w_in[o])
            a_list = [_swa(qt, k, vtx, table_c, od_sinks[o], b, s)]
            wout_list = [od_w_out[o].astype(BF16)]
        mk, mv = _memkv(mem, xa_mem_norm[l], xa_wkv[l])
        x2 = _tail(x2, g, a_list, wout_list, xa_wq[l].astype(BF16), jnp.swapaxes(mk, 1, 2), mv,
                   xa_wo[l].astype(BF16), mlp_w1[l].astype(BF16), mlp_w2[l].astype(BF16), b, s)
    return x2.reshape(b, s, d)
```

```python
import functools
import math

import numpy as np
import jax
import jax.numpy as jnp
from jax import lax
from jax.experimental import pallas as pl
from jax.experimental.pallas import tpu as pltpu

F32 = jnp.float32
BF16 = jnp.bfloat16

D_MODEL = 1024
CHUNK = 64
EPS = 1e-6
NEG_INF = -1e30
LOG2E = math.log2(math.e)

H_A, DH_A, D_LAT = 8, 64, 128
H_IDX, DH_IDX = 8, 64
TOPK_MAX = 256
H_B, DH_B = 4, 64
H_C, H_C_KV, DH_C = 16, 2, 64
G_C = H_C // H_C_KV
WINDOW = 128
H_X, DH_X = 4, 64
D_FF = 4 * D_MODEL
N_BUCKETS = 32
MAX_DIST = 1024

LANES = 128
FOLD_ROWS = 64
TQ = 128
TQ_DSA = 256
TQ_DIFF = 512
TK = 512
SUB = TK // LANES
ONES_ROWS = 16
TM_PROJ = 512
TM_TAIL = 512
TAIL_CHAINS = 2
FF_CHUNK = 1024
VMEM_LIMIT = 56 * 1024 * 1024
N_BISECT = 14
GROUPS = (4, 2, 1)
GROUPS_NEAR = (2, 1)
GROUPS_DSA = (2, 1)
GROUPS_PASS = (2, 1)


def _rms(x, g):
    return x * lax.rsqrt(jnp.mean(x * x, axis=-1, keepdims=True) + EPS) * g


def _dot(a, b):
    return jnp.dot(a, b, preferred_element_type=F32)


def _t5_bucket_np(rel):
    nb = N_BUCKETS // 2
    max_exact = nb // 2
    n = np.abs(rel)
    nf = np.maximum(n, 1).astype(np.float32)
    large = max_exact + (np.log(nf / max_exact) / math.log(MAX_DIST / max_exact) * (nb - max_exact)).astype(np.int32)
    large = np.minimum(large, nb - 1)
    return np.where(rel > 0, nb, 0) + np.where(n < max_exact, n, large)


def _sat_blocks():
    d = 1
    while _t5_bucket_np(np.array([-(d * LANES - (LANES - 1))]))[0] != N_BUCKETS // 2 - 1:
        d += 1
    return d


N_SAT = _sat_blocks()
N_BT = N_SAT + 1


def _toeplitz(u, rows, cols):
    length = u.shape[-1]
    flat = jnp.tile(u, (1,) * (u.ndim - 1) + (rows,))[..., :rows * (length - 1)]
    return flat.reshape(u.shape[:-1] + (rows, length - 1))[..., :cols]


def _causal_bias_tiles(table_cols):
    length = 2 * LANES + 1
    m = np.arange(length)
    diff = np.where(m < LANES + 1, m, m - length)
    rel = -diff[None, :] - (np.arange(N_BT) * LANES)[:, None]
    tab = table_cols.astype(F32)
    u = jnp.take(tab, jnp.asarray(_t5_bucket_np(rel)), axis=0)
    u = (u - tab[N_BUCKETS // 2 - 1][None, None, :]) * LOG2E
    tiles = _toeplitz(jnp.moveaxis(u, -1, 0), LANES, LANES)
    return tiles.reshape(table_cols.shape[1] * N_BT, LANES, LANES)


def _fold_rows(x, op):
    acc = x[0:FOLD_ROWS, :]
    for r in range(1, x.shape[0] // FOLD_ROWS):
        acc = op(acc, x[r * FOLD_ROWS:(r + 1) * FOLD_ROWS, :])
    return acc


def _grouped_loop(lo, hi, carry, fn, groups):
    for g in groups:
        n = (hi - lo) // g
        carry = lax.fori_loop(0, n, lambda u, c, lo=lo, g=g: fn([lo + u * g + k for k in range(g)], c), carry)
        lo = lo + n * g
    return carry


def _bias_tile(bias_ref, h, qb0, n_qb, j):
    rows = []
    for c in range(SUB):
        cols = [bias_ref[h * N_BT + jnp.clip(qb0 + r - (j * SUB + c), 0, N_SAT)] for r in range(n_qb)]
        rows.append(cols[0] if n_qb == 1 else jnp.concatenate(cols, axis=1))
    return jnp.concatenate(rows, axis=0)


def _ones_rows(n):
    return jnp.ones((ONES_ROWS, n), BF16)


def _proj_even_kernel(x_ref, g_ref, w_ref, kvn_ref, wuk_ref,
                      qit_ref, wit_ref, qlt_ref, ki_ref, c_ref, ctx_ref, qbt_ref, kb_ref, vtx_ref):
    tm = x_ref.shape[0]
    h = _rms(x_ref[...], g_ref[0:1, :]).astype(BF16)
    z = _dot(h, w_ref[...])
    qa = z[:, 0:512].astype(BF16)
    ckv = _rms(z[:, 512:640], kvn_ref[...])
    c_ref[...] = ckv.astype(BF16)
    kb_ref[...] = z[:, 1664:2176].astype(BF16)
    ki_ref[...] = z[:, 2688:2688 + DH_IDX].astype(BF16)
    ctx_ref[0, D_LAT:, :] = _ones_rows(tm)
    for hb in range(H_B):
        vtx_ref[0, hb, 2 * DH_B:, :] = _ones_rows(tm)
    per_q, per_diff = TQ_DSA // LANES, TQ_DIFF // LANES
    for hh in range(H_A):
        ql = _dot(qa[:, hh * DH_A:(hh + 1) * DH_A], wuk_ref[hh]) * (DH_A ** -0.5 * LOG2E)
        for sb in range(tm // LANES):
            off = hh * TQ_DSA + (sb % per_q) * LANES
            qlt_ref[sb // per_q, :, off:off + LANES] = ql[sb * LANES:(sb + 1) * LANES, :].T.astype(BF16)
    for sb in range(tm // LANES):
        rows = slice(sb * LANES, (sb + 1) * LANES)
        blk, sub = sb // per_q, (sb % per_q) * LANES
        for pr in range(H_IDX // 2):
            t = z[rows, 640 + pr * LANES:640 + (pr + 1) * LANES].T
            for half in range(2):
                off = (2 * pr + half) * TQ_DSA + sub
                qit_ref[blk, :, off:off + LANES] = t[half * DH_IDX:(half + 1) * DH_IDX, :].astype(BF16)
        wit_ref[blk, :, sub:sub + LANES] = z[rows, 2816:2816 + LANES].T[0:H_IDX, :]
        ctx_ref[0, 0:D_LAT, rows] = ckv[rows, :].T.astype(BF16)
        for hb in range(H_B):
            qb = z[rows, 1152 + hb * LANES:1152 + (hb + 1) * LANES] * (DH_B ** -0.5 * LOG2E)
            lanes = slice((sb % per_diff) * LANES, (sb % per_diff + 1) * LANES)
            qbt_ref[sb // per_diff, hb, :, lanes] = qb.T.astype(BF16)
            vtx_ref[0, hb, 0:2 * DH_B, rows] = z[rows, 2176 + hb * LANES:2176 + (hb + 1) * LANES].T.astype(BF16)


def _proj_even(x2, g, w_in, kvn, w_uk):
    t = x2.shape[0]
    tm = TK
    assert t % tm == 0 and tm % TQ_DSA == 0 and tm % TQ_DIFF == 0
    pad = jnp.zeros((D_MODEL, LANES - DH_IDX), w_in.dtype)
    cols = [w_in[:, 0:512], w_in[:, 512:640], w_in[:, 640:1152], w_in[:, 1224:1736], w_in[:, 1736:2248],
            w_in[:, 2248:2760], w_in[:, 1152:1216], pad, w_in[:, 1216:1224],
            jnp.zeros((D_MODEL, LANES - H_IDX), w_in.dtype)]
    w = jnp.concatenate(cols, axis=1).astype(BF16)
    n = w.shape[1]
    dv = 2 * DH_B
    row = lambda i: (i, 0)
    full2 = lambda i: (0, 0)
    lead3 = lambda i: (i, 0, 0)
    lead4 = lambda i: (i, 0, 0, 0)
    outs = [
        ((t // TQ_DSA, DH_IDX, H_IDX * TQ_DSA), BF16, (tm // TQ_DSA, DH_IDX, H_IDX * TQ_DSA), lead3),
        ((t // TQ_DSA, H_IDX, TQ_DSA), F32, (tm // TQ_DSA, H_IDX, TQ_DSA), lead3),
        ((t // TQ_DSA, D_LAT, H_A * TQ_DSA), BF16, (tm // TQ_DSA, D_LAT, H_A * TQ_DSA), lead3),
        ((t, DH_IDX), BF16, (tm, DH_IDX), row),
        ((t, D_LAT), BF16, (tm, D_LAT), row),
        ((t // TK, D_LAT + ONES_ROWS, TK), BF16, (1, D_LAT + ONES_ROWS, TK), lead3),
        ((t // TQ_DIFF, H_B, dv, TQ_DIFF), BF16, (tm // TQ_DIFF, H_B, dv, TQ_DIFF), lead4),
        ((t, H_B * dv), BF16, (tm, H_B * dv), row),
        ((t // TK, H_B, dv + ONES_ROWS, TK), BF16, (1, H_B, dv + ONES_ROWS, TK), lead4),
    ]
    return pl.pallas_call(
        _proj_even_kernel,
        out_shape=[jax.ShapeDtypeStruct(shape, dt) for shape, dt, _, _ in outs],
        grid=(t // tm,),
        in_specs=[
            pl.BlockSpec((tm, D_MODEL), row),
            pl.BlockSpec((6, D_MODEL), full2),
            pl.BlockSpec((D_MODEL, n), full2),
            pl.BlockSpec((1, D_LAT), full2),
            pl.BlockSpec((H_A, DH_A, D_LAT), lambda i: (0, 0, 0)),
        ],
        out_specs=[pl.BlockSpec(block, imap) for _, _, block, imap in outs],
        compiler_params=pltpu.CompilerParams(vmem_limit_bytes=VMEM_LIMIT),
        name="proj_even",
    )(x2, g, w, kvn.reshape(1, D_LAT), w_uk.astype(BF16))


def _proj_odd_kernel(x_ref, g_ref, w_ref, qt_ref, k_ref, vtx_ref):
    tm = x_ref.shape[0]
    h = _rms(x_ref[...], g_ref[0:1, :]).astype(BF16)
    z = _dot(h, w_ref[...])
    for kv in range(H_C_KV):
        k_ref[kv] = z[:, 1024 + kv * DH_C:1024 + (kv + 1) * DH_C].astype(BF16)
    for sb in range(tm // TQ):
        rows = slice(sb * TQ, (sb + 1) * TQ)
        for pr in range(H_C // 2):
            t = (z[rows, pr * LANES:(pr + 1) * LANES] * (DH_C ** -0.5 * LOG2E)).T
            for half in range(2):
                hd = 2 * pr + half
                kv, gq = hd // G_C, hd % G_C
                qt_ref[sb, kv, :, gq * TQ:(gq + 1) * TQ] = t[half * DH_C:(half + 1) * DH_C, :].astype(BF16)
        vt = z[rows, 1152:1280].T
        for kv in range(H_C_KV):
            vtx_ref[sb, kv, 0:DH_C, :] = vt[kv * DH_C:(kv + 1) * DH_C, :].astype(BF16)
            vtx_ref[sb, kv, DH_C:, :] = _ones_rows(TQ)


def _proj_odd(x2, g, w_in):
    t = x2.shape[0]
    tm = min(TM_PROJ, t)
    assert TQ == LANES
    row = lambda i: (i, 0)
    full2 = lambda i: (0, 0)
    lead4 = lambda i: (i, 0, 0, 0)
    dvx = DH_C + ONES_ROWS
    return pl.pallas_call(
        _proj_odd_kernel,
        out_shape=[jax.ShapeDtypeStruct((t // TQ, H_C_KV, DH_C, G_C * TQ), BF16),
                   jax.ShapeDtypeStruct((H_C_KV, t, DH_C), BF16),
                   jax.ShapeDtypeStruct((t // TQ, H_C_KV, dvx, TQ), BF16)],
        grid=(t // tm,),
        in_specs=[pl.BlockSpec((tm, D_MODEL), row),
                  pl.BlockSpec((6, D_MODEL), full2),
                  pl.BlockSpec((D_MODEL, 1280), full2)],
        out_specs=[pl.BlockSpec((tm // TQ, H_C_KV, DH_C, G_C * TQ), lead4),
                   pl.BlockSpec((H_C_KV, tm, DH_C), lambda i: (0, i, 0)),
                   pl.BlockSpec((tm // TQ, H_C_KV, dvx, TQ), lead4)],
        compiler_params=pltpu.CompilerParams(vmem_limit_bytes=VMEM_LIMIT),
        name="proj_odd",
    )(x2, g, w_in.astype(BF16))


def _dsa_kernel(qit_ref, wit_ref, qlt_ref, k_ref, c_ref, ctx_ref, bias_ref, wuvt_ref, o_ref,
                sc_ref, acc_ref, *p_refs, topk):
    TQ = TQ_DSA
    i = pl.program_id(1)
    t0 = i * TQ
    nt = (t0 + TQ + TK - 1) // TK
    n_qb = TQ // LANES
    n_far = jnp.maximum(i * n_qb - N_SAT + 1, 0) // SUB
    kf = float(topk)

    qpos = lax.broadcasted_iota(jnp.int32, (1, TQ), 1) + t0
    limit = (qpos // CHUNK + 1) * CHUNK
    small = limit <= topk
    big = jnp.logical_not(small)
    w = wit_ref[...]
    qit = qit_ref[...]

    def score_tiles(js, carry):
        rmax, rmin = carry
        zs = [_dot(k_ref[j], qit) for j in js]
        for j, z in zip(js, zs):
            sc = jnp.maximum(z[:, 0:TQ], 0.0) * w[0:1, :]
            for h in range(1, H_IDX):
                sc = sc + jnp.maximum(z[:, h * TQ:(h + 1) * TQ], 0.0) * w[h:h + 1, :]
            kpos = lax.broadcasted_iota(jnp.int32, (TK, TQ), 0) + j * TK
            adm = kpos < limit
            sc_ref[j] = jnp.where(adm, sc, -jnp.inf)
            rmax = jnp.maximum(rmax, _fold_rows(jnp.where(adm, sc, -jnp.inf), jnp.maximum))
            rmin = jnp.minimum(rmin, _fold_rows(jnp.where(adm, sc, jnp.inf), jnp.minimum))
        return rmax, rmin

    rmax, rmin = _grouped_loop(
        0, nt, (jnp.full((FOLD_ROWS, TQ), -jnp.inf, F32), jnp.full((FOLD_ROWS, TQ), jnp.inf, F32)),
        score_tiles, GROUPS_DSA)
    hi0 = jnp.max(rmax, axis=0, keepdims=True)
    lo0 = jnp.min(rmin, axis=0, keepdims=True)

    def col_reduce(tile_fn, op, init, final):
        def fn(js, acc):
            for j in js:
                acc = op(acc, _fold_rows(tile_fn(j), op))
            return acc
        acc = _grouped_loop(0, nt, jnp.full((FOLD_ROWS, TQ), init, F32), fn, GROUPS_PASS)
        return final(acc, axis=0, keepdims=True)

    def count_ge(thr):
        return col_reduce(lambda j: jnp.where(sc_ref[j] >= thr, 1.0, 0.0), jnp.add, 0.0, jnp.sum)

    def count_gt(thr):
        return col_reduce(lambda j: jnp.where(sc_ref[j] > thr, 1.0, 0.0), jnp.add, 0.0, jnp.sum)

    def max_le(thr):
        def tile(j):
            t = sc_ref[j]
            return jnp.where(t <= thr, t, -jnp.inf)
        return col_reduce(tile, jnp.maximum, -jnp.inf, jnp.max)

    def max_lt(thr):
        def tile(j):
            t = sc_ref[j]
            return jnp.where(t < thr, t, -jnp.inf)
        return col_reduce(tile, jnp.maximum, -jnp.inf, jnp.max)

    def bisect_body(_, carry):
        lo, hi = carry
        mid = lo * 0.5 + hi * 0.5
        ge = count_ge(mid) >= kf
        return jnp.where(ge, mid, lo), jnp.where(ge, hi, mid)

    lo, hi = lax.fori_loop(0, N_BISECT, bisect_body, (lo0, hi0))

    v0 = max_le(hi)
    c0 = count_ge(v0)

    def walk_cond(carry):
        _, cnt, it = carry
        pending = jnp.where((cnt < kf) & big, 1.0, 0.0)
        return (jnp.max(pending) > 0.0) & (it < topk + 2)

    def walk_body(carry):
        v, cnt, it = carry
        v = jnp.where((cnt < kf) & big, max_lt(v), v)
        return v, count_ge(v), it + 1

    tau, cnt, _ = lax.while_loop(walk_cond, walk_body, (v0, c0, jnp.int32(0)))
    tau = jnp.where(small, -jnp.finfo(F32).max, tau)

    tied = (cnt > kf) & big

    @pl.when(jnp.max(jnp.where(tied, 1.0, 0.0)) > 0.0)
    def _():
        need = kf - count_gt(tau)
        n_keys = sc_ref.shape[0] * TK

        def kpos_f(j):
            return (lax.broadcasted_iota(jnp.int32, (TK, TQ), 0) + j * TK).astype(F32)

        def count_eq_le(pos):
            return col_reduce(lambda j: jnp.where((sc_ref[j] == tau) & (kpos_f(j) <= pos), 1.0, 0.0),
                              jnp.add, 0.0, jnp.sum)

        def pos_body(_, carry):
            lo_p, hi_p = carry
            mid = jnp.floor((lo_p + hi_p) * 0.5)
            ok = count_eq_le(mid) >= need
            return jnp.where(ok, lo_p, mid), jnp.where(ok, mid, hi_p)

        n_steps = int(math.ceil(math.log2(n_keys))) + 1
        _, cut = lax.fori_loop(0, n_steps, pos_body,
                               (jnp.full((1, TQ), -1.0, F32), jnp.full((1, TQ), float(n_keys), F32)))

        def drop_body(j, _):
            t = sc_ref[j]
            sc_ref[j] = jnp.where(tied & (t == tau) & (kpos_f(j) > cut), -jnp.inf, t)
            return 0

        lax.fori_loop(0, nt, drop_body, 0)

    acc_ref[...] = jnp.zeros(acc_ref.shape, F32)
    qlt = qlt_ref[...]

    def attn_tiles(js, m, with_bias):
        s_alls = [_dot(c_ref[j], qlt) for j in js]
        for g, (j, s_all) in enumerate(zip(js, s_alls)):
            p_ref = p_refs[g]
            sel = sc_ref[j] >= tau
            m_new, alpha = [], []
            for h in range(H_A):
                cols = slice(h * TQ, (h + 1) * TQ)
                s = s_all[:, cols]
                if with_bias:
                    s = s + _bias_tile(bias_ref, h, i * n_qb, n_qb, j)
                s = jnp.where(sel, s, NEG_INF)
                m_old = m[:, cols]
                m_h = jnp.maximum(m_old, jnp.max(_fold_rows(s, jnp.maximum), axis=0, keepdims=True))
                p_ref[:, cols] = jnp.exp2(s - m_h).astype(BF16)
                alpha.append(jnp.exp2(m_old - m_h))
                m_new.append(m_h)
            acc_ref[...] = acc_ref[...] * jnp.concatenate(alpha, axis=1) + _dot(ctx_ref[j], p_ref[...])
            m = jnp.concatenate(m_new, axis=1)
        return m

    m = jnp.full((1, H_A * TQ), NEG_INF, F32)
    m = _grouped_loop(0, n_far, m, lambda js, mm: attn_tiles(js, mm, False), GROUPS_DSA)
    _grouped_loop(n_far, nt, m, lambda js, mm: attn_tiles(js, mm, True), GROUPS_NEAR)

    outs = []
    for h in range(H_A):
        cols = slice(h * TQ, (h + 1) * TQ)
        o_t = (acc_ref[0:D_LAT, cols] / acc_ref[D_LAT:D_LAT + 1, cols]).astype(BF16)
        outs.append(_dot(wuvt_ref[h], o_t))
    for pr in range(H_A // 2):
        pair = jnp.concatenate([outs[2 * pr], outs[2 * pr + 1]], axis=0)
        o_ref[:, pr * LANES:(pr + 1) * LANES] = pair.T.astype(BF16)


def _dsa(qit, wit, qlt, ki, ckv, ctx, bias_a, w_uv, b, s):
    TQ = TQ_DSA
    nq, nt = s // TQ, s // TK
    topk = min(TOPK_MAX, s // 4)
    k4 = ki.reshape(b, nt, TK, DH_IDX)
    c4 = ckv.reshape(b, nt, TK, D_LAT)
    ctx = ctx.reshape(b, nt, D_LAT + ONES_ROWS, TK)
    wuvt = jnp.swapaxes(w_uv, 1, 2).astype(BF16)
    qblk = lambda bb, i: (bb * nq + i, 0, 0)
    kv4 = lambda bb, i: (bb, 0, 0, 0)
    once = dict(pipeline_mode=pl.Buffered(1))
    return pl.pallas_call(
        functools.partial(_dsa_kernel, topk=topk),
        out_shape=jax.ShapeDtypeStruct((b * s, H_A * DH_A), BF16),
        grid=(b, nq),
        in_specs=[
            pl.BlockSpec((None, DH_IDX, H_IDX * TQ), qblk),
            pl.BlockSpec((None, H_IDX, TQ), qblk),
            pl.BlockSpec((None, D_LAT, H_A * TQ), qblk),
            pl.BlockSpec((None, nt, TK, DH_IDX), kv4, **once),
            pl.BlockSpec((None, nt, TK, D_LAT), kv4, **once),
            pl.BlockSpec((None, nt, D_LAT + ONES_ROWS, TK), kv4, **once),
            pl.BlockSpec((H_A * N_BT, LANES, LANES), lambda bb, i: (0, 0, 0), **once),
            pl.BlockSpec((H_A, DH_A, D_LAT), lambda bb, i: (0, 0, 0), **once),
        ],
        out_specs=pl.BlockSpec((TQ, H_A * DH_A), lambda bb, i: (bb * nq + i, 0)),
        scratch_shapes=[
            pltpu.VMEM((nt, TK, TQ), F32),
            pltpu.VMEM((D_LAT + ONES_ROWS, H_A * TQ), F32),
        ] + [pltpu.VMEM((TK, H_A * TQ), BF16)] * max(GROUPS_DSA),
        compiler_params=pltpu.CompilerParams(vmem_limit_bytes=VMEM_LIMIT),
        name="dsa_attention",
    )(qit, wit, qlt, k4, c4, ctx, bias_a, wuvt)


def _diff_kernel(qt_ref, k_ref, vtx_ref, bias_ref, lam_ref, sub_ref, o_ref, acc_ref, *p_refs, lam_init):
    h = pl.program_id(1)
    i = pl.program_id(2)
    tq = TQ_DIFF
    n_qb = tq // LANES
    t0 = i * tq
    nt = (t0 + tq + TK - 1) // TK
    n_far = jnp.maximum(i * n_qb - N_SAT + 1, 0) // SUB
    qt = qt_ref[...]
    top = lax.broadcasted_iota(jnp.int32, qt.shape, 0) < DH_B
    q2 = jnp.concatenate([jnp.where(top, qt, jnp.zeros_like(qt)), jnp.where(top, jnp.zeros_like(qt), qt)], axis=1)
    qpos = lax.broadcasted_iota(jnp.int32, (1, tq), 1) + t0
    limit = (qpos // CHUNK + 1) * CHUNK
    acc_ref[...] = jnp.zeros(acc_ref.shape, F32)

    def tiles(js, m, near):
        s2s = [_dot(k_ref[j], q2) for j in js]
        for g, (j, s2) in enumerate(zip(js, s2s)):
            p_ref = p_refs[g]
            if near:
                bias = _bias_tile(bias_ref, h, i * n_qb, n_qb, j)
                adm = (lax.broadcasted_iota(jnp.int32, (TK, tq), 0) + j * TK) < limit
            m_new, alpha = [], []
            for mp in range(2):
                cols = slice(mp * tq, (mp + 1) * tq)
                s = s2[:, cols]
                if near:
                    s = jnp.where(adm, s + bias, NEG_INF)
                m_old = m[:, cols]
                m_h = jnp.maximum(m_old, jnp.max(_fold_rows(s, jnp.maximum), axis=0, keepdims=True))
                p_ref[:, cols] = jnp.exp2(s - m_h).astype(BF16)
                alpha.append(jnp.exp2(m_old - m_h))
                m_new.append(m_h)
            acc_ref[...] = acc_ref[...] * jnp.concatenate(alpha, axis=1) + _dot(vtx_ref[j], p_ref[...])
            m = jnp.concatenate(m_new, axis=1)
        return m

    m = jnp.full((1, 2 * tq), NEG_INF, F32)
    m = _grouped_loop(0, n_far, m, lambda js, mm: tiles(js, mm, False), GROUPS)
    _grouped_loop(n_far, nt, m, lambda js, mm: tiles(js, mm, True), GROUPS_NEAR)

    lf = lam_ref[...]
    lam = (jnp.exp(jnp.sum(lf[0:1, :] * lf[1:2, :], axis=1, keepdims=True))
           - jnp.exp(jnp.sum(lf[2:3, :] * lf[3:4, :], axis=1, keepdims=True)) + lam_init)
    dv = 2 * DH_B
    o = (acc_ref[0:dv, 0:tq] / acc_ref[dv:dv + 1, 0:tq]
         - lam * (acc_ref[0:dv, tq:2 * tq] / acc_ref[dv:dv + 1, tq:2 * tq]))
    o = o * lax.rsqrt(jnp.mean(o * o, axis=0, keepdims=True) + EPS) * sub_ref[...] * (1.0 - lam_init)
    o_ref[...] = o.T.astype(BF16)


def _diff(qbt, kb, vtx, bias_b, b_lambda, b_subln, lam_init, b, s):
    tq = TQ_DIFF
    nq, nt = s // tq, s // TK
    dv = 2 * DH_B
    k4 = kb.reshape(b, nt, TK, H_B * dv)
    vtx = vtx.reshape(b, nt, H_B, dv + ONES_ROWS, TK)
    return pl.pallas_call(
        functools.partial(_diff_kernel, lam_init=lam_init),
        out_shape=jax.ShapeDtypeStruct((b * s, H_B * dv), BF16),
        grid=(b, H_B, nq),
        in_specs=[
            pl.BlockSpec((None, None, dv, tq), lambda bb, h, i: (bb * nq + i, h, 0, 0)),
            pl.BlockSpec((None, nt, TK, dv), lambda bb, h, i: (bb, 0, 0, h)),
            pl.BlockSpec((None, nt, None, dv + ONES_ROWS, TK), lambda bb, h, i: (bb, 0, h, 0, 0)),
            pl.BlockSpec((H_B * N_BT, LANES, LANES), lambda bb, h, i: (0, 0, 0)),
            pl.BlockSpec((4, DH_B), lambda bb, h, i: (0, 0)),
            pl.BlockSpec((dv, 1), lambda bb, h, i: (0, 0)),
        ],
        out_specs=pl.BlockSpec((tq, dv), lambda bb, h, i: (bb * nq + i, h)),
        scratch_shapes=[
            pltpu.VMEM((dv + ONES_ROWS, 2 * tq), F32),
        ] + [pltpu.VMEM((TK, 2 * tq), BF16)] * max(GROUPS),
        compiler_params=pltpu.CompilerParams(vmem_limit_bytes=VMEM_LIMIT),
        name="diff_attention",
    )(qbt, k4, vtx, bias_b, b_lambda.astype(F32), b_subln.reshape(dv, 1).astype(F32))


def _swa_kernel(qt_ref, ka_ref, kb_ref, vta_ref, vtb_ref, bias_ref, sink_ref, o_ref):
    i = pl.program_id(1)
    pad_pen = jnp.where(i == 0, NEG_INF, 0.0)
    for kv in range(H_C_KV):
        qt = qt_ref[kv]
        s = jnp.concatenate([_dot(ka_ref[kv], qt) + pad_pen, _dot(kb_ref[kv], qt)], axis=0) + bias_ref[kv]
        sink = sink_ref[kv]
        m = jnp.maximum(jnp.max(s, axis=0, keepdims=True), sink)
        p = jnp.exp2(s - m).astype(BF16)
        acc = _dot(vta_ref[kv], p[0:TQ, :]) + _dot(vtb_ref[kv], p[TQ:2 * TQ, :])
        o = acc[0:DH_C, :] / (acc[DH_C:DH_C + 1, :] + jnp.exp2(sink - m))
        for pr in range(G_C // 2):
            pair = jnp.concatenate([o[:, (2 * pr) * TQ:(2 * pr + 1) * TQ], o[:, (2 * pr + 1) * TQ:(2 * pr + 2) * TQ]],
                                   axis=0)
            col = (kv * G_C + 2 * pr) * DH_C
            o_ref[:, col:col + LANES] = pair.T.astype(BF16)


def _swa(qt, k, vtx, table_c, sinks, b, s):
    assert WINDOW == TQ and TQ % CHUNK == 0
    nq = s // TQ
    length = 3 * TQ + 1
    m = np.arange(length)
    diff = np.where(m < 2 * TQ + 1, m, m - length)
    u = jnp.take(table_c.astype(F32), jnp.asarray(_t5_bucket_np(diff - TQ)), axis=0) * LOG2E
    bias = _toeplitz(jnp.moveaxis(u, -1, 0), TQ, 2 * TQ)
    rblk = np.arange(TQ)[:, None] // CHUNK + WINDOW // CHUNK
    cblk = np.arange(2 * TQ)[None, :] // CHUNK
    valid = (cblk <= rblk) & (cblk >= rblk - WINDOW // CHUNK)
    bias = jnp.where(jnp.asarray(valid)[None], bias, NEG_INF)
    bias_t = jnp.transpose(bias.reshape(H_C_KV, G_C, TQ, 2 * TQ), (0, 3, 1, 2)).reshape(H_C_KV, 2 * TQ, G_C * TQ)
    sink_row = jnp.repeat(sinks.astype(F32).reshape(H_C_KV, 1, G_C) * LOG2E, TQ, axis=2)

    dvx = DH_C + ONES_ROWS
    prev = lambda bb, i: bb * nq + jnp.maximum(i - 1, 0)
    return pl.pallas_call(
        _swa_kernel,
        out_shape=jax.ShapeDtypeStruct((b * s, H_C * DH_C), BF16),
        grid=(b, nq),
        in_specs=[
            pl.BlockSpec((None, H_C_KV, DH_C, G_C * TQ), lambda bb, i: (bb * nq + i, 0, 0, 0)),
            pl.BlockSpec((H_C_KV, TQ, DH_C), lambda bb, i: (0, prev(bb, i), 0)),
            pl.BlockSpec((H_C_KV, TQ, DH_C), lambda bb, i: (0, bb * nq + i, 0)),
            pl.BlockSpec((None, H_C_KV, dvx, TQ), lambda bb, i: (prev(bb, i), 0, 0, 0)),
            pl.BlockSpec((None, H_C_KV, dvx, TQ), lambda bb, i: (bb * nq + i, 0, 0, 0)),
            pl.BlockSpec((H_C_KV, 2 * TQ, G_C * TQ), lambda bb, i: (0, 0, 0)),
            pl.BlockSpec((H_C_KV, 1, G_C * TQ), lambda bb, i: (0, 0, 0)),
        ],
        out_specs=pl.BlockSpec((TQ, H_C * DH_C), lambda bb, i: (bb * nq + i, 0)),
        compiler_params=pltpu.CompilerParams(vmem_limit_bytes=VMEM_LIMIT),
        name="swa_attention",
    )(qt, k, k, vtx, vtx, bias_t, sink_row)


def _memkv_kernel(mem_ref, g_ref, w_ref, k_ref, v_ref):
    h = _rms(mem_ref[...], g_ref[...]).astype(BF16)
    kv = _dot(h, w_ref[...])
    k_ref[...] = kv[:, 0:H_X * DH_X].astype(BF16)
    v_ref[...] = kv[:, H_X * DH_X:].astype(BF16)


def _memkv(mem, g, wkv):
    b, m, _ = mem.shape
    n = H_X * DH_X
    return pl.pallas_call(
        _memkv_kernel,
        out_shape=[jax.ShapeDtypeStruct((b, m, n), BF16), jax.ShapeDtypeStruct((b, m, n), BF16)],
        grid=(b,),
        in_specs=[pl.BlockSpec((None, m, D_MODEL), lambda i: (i, 0, 0)),
                  pl.BlockSpec((1, D_MODEL), lambda i: (0, 0)),
                  pl.BlockSpec((D_MODEL, 2 * n), lambda i: (0, 0))],
        out_specs=[pl.BlockSpec((None, m, n), lambda i: (i, 0, 0)), pl.BlockSpec((None, m, n), lambda i: (i, 0, 0))],
        compiler_params=pltpu.CompilerParams(vmem_limit_bytes=VMEM_LIMIT),
        name="mem_kv",
    )(mem, g.reshape(1, D_MODEL), wkv.astype(BF16))


def _lane_half_mask(rows, half):
    lane = lax.broadcasted_iota(jnp.int32, (rows, LANES), 1)
    return (lane < 64) if half == 0 else (lane >= 64)


def _tail_kernel(*refs, n_a):
    x_ref, g_ref = refs[0], refs[1]
    a_refs = refs[2:2 + n_a]
    wout_refs = refs[2 + n_a:2 + 2 * n_a]
    wq_ref, kt_ref, v_ref, wo_ref, w1_ref, w2_ref, o_ref = refs[2 + 2 * n_a:]
    tc = x_ref.shape[0] // TAIL_CHAINS
    chains = [slice(c * tc, (c + 1) * tc) for c in range(TAIL_CHAINS)]
    g = g_ref[...]

    ys = []
    for r in chains:
        y = _dot(a_refs[0][r, :], wout_refs[0][...])
        for a_ref, w_ref in zip(a_refs[1:], wout_refs[1:]):
            y = y + _dot(a_ref[r, :], w_ref[...])
        ys.append(y)
    xs = [x_ref[r, :] + _rms(y, g[1:2, :]) for r, y in zip(chains, ys)]

    hqs = [_rms(x, g[2:3, :]).astype(BF16) for x in xs]
    qs = [_dot(hq, wq_ref[...]) for hq in hqs]
    qs = [(q * (DH_X ** -0.5)).astype(BF16) for q in qs]
    pairs = [[] for _ in chains]
    for pr in range(H_X // 2):
        vp = v_ref[:, pr * LANES:(pr + 1) * LANES]
        outs = [jnp.zeros((tc, LANES), F32) for _ in chains]
        for half in range(2):
            qzs = [jnp.where(_lane_half_mask(tc, half), q[:, pr * LANES:(pr + 1) * LANES], jnp.zeros((tc, LANES), BF16))
                   for q in qs]
            ss = [_dot(qz, kt_ref[pr * LANES:(pr + 1) * LANES, :]) for qz in qzs]
            vz = jnp.where(_lane_half_mask(vp.shape[0], half), vp, jnp.zeros_like(vp))
            ps = []
            for sc in ss:
                e = jnp.exp(sc - jnp.max(sc, axis=1, keepdims=True))
                ps.append((e / jnp.sum(e, axis=1, keepdims=True)).astype(BF16))
            outs = [out + _dot(p, vz) for out, p in zip(outs, ps)]
        for c, out in enumerate(outs):
            pairs[c].append(out.astype(BF16))
    ys = [_dot(jnp.concatenate(pc, axis=1), wo_ref[...]) for pc in pairs]
    xs = [x + _rms(y, g[3:4, :]) for x, y in zip(xs, ys)]

    hms = [_rms(x, g[4:5, :]).astype(BF16) for x in xs]
    n_chunks = D_FF // FF_CHUNK
    ys = [jnp.zeros((tc, D_MODEL), F32) for _ in chains]
    a_cur = [_dot(hm, w1_ref[:, 0:FF_CHUNK]) for hm in hms]
    for k in range(n_chunks):
        if k + 1 < n_chunks:
            a_next = [_dot(hm, w1_ref[:, (k + 1) * FF_CHUNK:(k + 2) * FF_CHUNK]) for hm in hms]
        acts = [jnp.square(jnp.maximum(a, 0.0)).astype(BF16) for a in a_cur]
        ys = [y + _dot(act, w2_ref[k * FF_CHUNK:(k + 1) * FF_CHUNK, :]) for y, act in zip(ys, acts)]
        if k + 1 < n_chunks:
            a_cur = a_next
    for r, x, y in zip(chains, xs, ys):
        o_ref[r, :] = x + _rms(y, g[5:6, :])


def _tail(x2, g, a_list, wout_list, wq, kt, v, wo, w1, w2, b, s):
    t = x2.shape[0]
    tm = min(TM_TAIL, s)
    per_b = s // tm
    n_a = len(a_list)
    row = lambda i: (i, 0)
    full2 = lambda i: (0, 0)
    once = dict(pipeline_mode=pl.Buffered(1))
    in_specs = [pl.BlockSpec((tm, D_MODEL), row), pl.BlockSpec((6, D_MODEL), full2)]
    in_specs += [pl.BlockSpec((tm, a.shape[1]), row) for a in a_list]
    in_specs += [pl.BlockSpec(w.shape, full2, **once) for w in wout_list]
    in_specs += [
        pl.BlockSpec(wq.shape, full2, **once),
        pl.BlockSpec((None,) + kt.shape[1:], lambda i: (i // per_b, 0, 0)),
        pl.BlockSpec((None,) + v.shape[1:], lambda i: (i // per_b, 0, 0)),
        pl.BlockSpec(wo.shape, full2, **once),
        pl.BlockSpec(w1.shape, full2, **once),
        pl.BlockSpec(w2.shape, full2, **once),
    ]
    return pl.pallas_call(
        functools.partial(_tail_kernel, n_a=n_a),
        out_shape=jax.ShapeDtypeStruct((t, D_MODEL), F32),
        grid=(t // tm,),
        in_specs=in_specs,
        out_specs=pl.BlockSpec((tm, D_MODEL), row),
        compiler_params=pltpu.CompilerParams(vmem_limit_bytes=VMEM_LIMIT),
        name="tail",
    )(x2, g, *a_list, *wout_list, wq, kt, v, wo, w1, w2)


def kernel(x, mem, rel_bias_table, norm_g, ev_w_in, ev_a_kv_norm, ev_a_w_uk, ev_a_w_uv, ev_b_lambda, ev_b_subln, ev_w_out, od_w_in, od_sinks, od_w_out, xa_wq, xa_wkv, xa_wo, xa_mem_norm, mlp_w1, mlp_w2):
    b, s, d = x.shape
    depth = norm_g.shape[0]
    assert d == D_MODEL and s % TK == 0 and TK % TQ_DIFF == 0 and TK % TQ_DSA == 0 and TQ_DSA % LANES == 0
    x2 = x.reshape(b * s, d)
    bias_a = _causal_bias_tiles(rel_bias_table[:, :H_A])
    bias_b = _causal_bias_tiles(rel_bias_table[:, H_A:H_A + H_B])
    table_c = rel_bias_table[:, H_A + H_B:]
    for l in range(depth):
        g = norm_g[l].astype(F32)
        if l % 2 == 0:
            e = l // 2
            lam_init = 0.8 - 0.6 * math.exp(-0.3 * l)
            qit, wit, qlt, ki, ckv, ctx, qbt, kb, vtx = _proj_even(x2, g, ev_w_in[e], ev_a_kv_norm[e], ev_a_w_uk[e])
            oa = _dsa(qit, wit, qlt, ki, ckv, ctx, bias_a, ev_a_w_uv[e], b, s)
            ob = _diff(qbt, kb, vtx, bias_b, ev_b_lambda[e], ev_b_subln[e], lam_init, b, s)
            n_a = H_A * DH_A
            a_list = [oa, ob]
            wout_list = [ev_w_out[e][:n_a].astype(BF16), ev_w_out[e][n_a:].astype(BF16)]
        else:
            o = l // 2
            qt, k, vtx = _proj_odd(x2, g, od_w_in[o])
            a_list = [_swa(qt, k, vtx, table_c, od_sinks[o], b, s)]
            wout_list = [od_w_out[o].astype(BF16)]
        mk, mv = _memkv(mem, xa_mem_norm[l], xa_wkv[l])
        x2 = _tail(x2, g, a_list, wout_list, xa_wq[l].astype(BF16), jnp.swapaxes(mk, 1, 2), mv,
                   xa_wo[l].astype(BF16), mlp_w1[l].astype(BF16), mlp_w2[l].astype(BF16), b, s)
    return x2.reshape(b, s, d)
```

```python
import functools
import math

import numpy as np
import jax
import jax.numpy as jnp
from jax import lax
from jax.experimental import pallas as pl
from jax.experimental.pallas import tpu as pltpu

F32 = jnp.float32
BF16 = jnp.bfloat16

D_MODEL = 1024
CHUNK = 64
EPS = 1e-6
NEG_INF = -1e30
LOG2E = math.log2(math.e)

H_A, DH_A, D_LAT = 8, 64, 128
H_IDX, DH_IDX = 8, 64
TOPK_MAX = 256
H_B, DH_B = 4, 64
H_C, H_C_KV, DH_C = 16, 2, 64
G_C = H_C // H_C_KV
WINDOW = 128
H_X, DH_X = 4, 64
D_FF = 4 * D_MODEL
N_BUCKETS = 32
MAX_DIST = 1024

LANES = 128
FOLD_ROWS = 64
TQ = 128
TQ_DSA = 128
TQ_DIFF = 512
TK = 512
SUB = TK // LANES
ONES_ROWS = 16
TM_PROJ = 512
TM_TAIL = 512
TAIL_CHAINS = 2
FF_CHUNK = 1024
VMEM_LIMIT = 56 * 1024 * 1024
N_BISECT = 14
GROUPS = (4, 2, 1)
GROUPS_NEAR = (2, 1)
GROUPS_DSA = (4, 2, 1)
GROUPS_PASS = (4, 1)


def _rms(x, g):
    return x * lax.rsqrt(jnp.mean(x * x, axis=-1, keepdims=True) + EPS) * g


def _dot(a, b):
    return jnp.dot(a, b, preferred_element_type=F32)


def _t5_bucket_np(rel):
    nb = N_BUCKETS // 2
    max_exact = nb // 2
    n = np.abs(rel)
    nf = np.maximum(n, 1).astype(np.float32)
    large = max_exact + (np.log(nf / max_exact) / math.log(MAX_DIST / max_exact) * (nb - max_exact)).astype(np.int32)
    large = np.minimum(large, nb - 1)
    return np.where(rel > 0, nb, 0) + np.where(n < max_exact, n, large)


def _sat_blocks():
    d = 1
    while _t5_bucket_np(np.array([-(d * LANES - (LANES - 1))]))[0] != N_BUCKETS // 2 - 1:
        d += 1
    return d


N_SAT = _sat_blocks()
N_BT = N_SAT + 1


def _toeplitz(u, rows, cols):
    length = u.shape[-1]
    flat = jnp.tile(u, (1,) * (u.ndim - 1) + (rows,))[..., :rows * (length - 1)]
    return flat.reshape(u.shape[:-1] + (rows, length - 1))[..., :cols]


def _causal_bias_tiles(table_cols):
    length = 2 * LANES + 1
    m = np.arange(length)
    diff = np.where(m < LANES + 1, m, m - length)
    rel = -diff[None, :] - (np.arange(N_BT) * LANES)[:, None]
    tab = table_cols.astype(F32)
    u = jnp.take(tab, jnp.asarray(_t5_bucket_np(rel)), axis=0)
    u = (u - tab[N_BUCKETS // 2 - 1][None, None, :]) * LOG2E
    tiles = _toeplitz(jnp.moveaxis(u, -1, 0), LANES, LANES)
    return tiles.reshape(table_cols.shape[1] * N_BT, LANES, LANES)


def _fold_rows(x, op):
    acc = x[0:FOLD_ROWS, :]
    for r in range(1, x.shape[0] // FOLD_ROWS):
        acc = op(acc, x[r * FOLD_ROWS:(r + 1) * FOLD_ROWS, :])
    return acc


def _grouped_loop(lo, hi, carry, fn, groups):
    for g in groups:
        n = (hi - lo) // g
        carry = lax.fori_loop(0, n, lambda u, c, lo=lo, g=g: fn([lo + u * g + k for k in range(g)], c), carry)
        lo = lo + n * g
    return carry


def _bias_tile(bias_ref, h, qb0, n_qb, j):
    rows = []
    for c in range(SUB):
        cols = [bias_ref[h * N_BT + jnp.clip(qb0 + r - (j * SUB + c), 0, N_SAT)] for r in range(n_qb)]
        rows.append(cols[0] if n_qb == 1 else jnp.concatenate(cols, axis=1))
    return jnp.concatenate(rows, axis=0)


def _ones_rows(n):
    return jnp.ones((ONES_ROWS, n), BF16)


def _proj_even_kernel(x_ref, g_ref, w_ref, kvn_ref, wuk_ref,
                      qit_ref, wit_ref, qlt_ref, ki_ref, c_ref, ctx_ref, qbt_ref, kb_ref, vtx_ref):
    tm = x_ref.shape[0]
    h = _rms(x_ref[...], g_ref[0:1, :]).astype(BF16)
    z = _dot(h, w_ref[...])
    qa = z[:, 0:512].astype(BF16)
    ckv = _rms(z[:, 512:640], kvn_ref[...])
    c_ref[...] = ckv.astype(BF16)
    kb_ref[...] = z[:, 1664:2176].astype(BF16)
    ki_ref[...] = z[:, 2688:2688 + DH_IDX].astype(BF16)
    ctx_ref[0, D_LAT:, :] = _ones_rows(tm)
    for hb in range(H_B):
        vtx_ref[0, hb, 2 * DH_B:, :] = _ones_rows(tm)
    per_q, per_diff = TQ_DSA // LANES, TQ_DIFF // LANES
    for hh in range(H_A):
        ql = _dot(qa[:, hh * DH_A:(hh + 1) * DH_A], wuk_ref[hh]) * (DH_A ** -0.5 * LOG2E)
        for sb in range(tm // LANES):
            off = hh * TQ_DSA + (sb % per_q) * LANES
            qlt_ref[sb // per_q, :, off:off + LANES] = ql[sb * LANES:(sb + 1) * LANES, :].T.astype(BF16)
    for sb in range(tm // LANES):
        rows = slice(sb * LANES, (sb + 1) * LANES)
        blk, sub = sb // per_q, (sb % per_q) * LANES
        for pr in range(H_IDX // 2):
            t = z[rows, 640 + pr * LANES:640 + (pr + 1) * LANES].T
            for half in range(2):
                off = (2 * pr + half) * TQ_DSA + sub
                qit_ref[blk, :, off:off + LANES] = t[half * DH_IDX:(half + 1) * DH_IDX, :].astype(BF16)
        wit_ref[blk, :, sub:sub + LANES] = z[rows, 2816:2816 + LANES].T[0:H_IDX, :]
        ctx_ref[0, 0:D_LAT, rows] = ckv[rows, :].T.astype(BF16)
        for hb in range(H_B):
            qb = z[rows, 1152 + hb * LANES:1152 + (hb + 1) * LANES] * (DH_B ** -0.5 * LOG2E)
            lanes = slice((sb % per_diff) * LANES, (sb % per_diff + 1) * LANES)
            qbt_ref[sb // per_diff, hb, :, lanes] = qb.T.astype(BF16)
            vtx_ref[0, hb, 0:2 * DH_B, rows] = z[rows, 2176 + hb * LANES:2176 + (hb + 1) * LANES].T.astype(BF16)


def _proj_even(x2, g, w_in, kvn, w_uk):
    t = x2.shape[0]
    tm = TK
    assert t % tm == 0 and tm % TQ_DSA == 0 and tm % TQ_DIFF == 0
    pad = jnp.zeros((D_MODEL, LANES - DH_IDX), w_in.dtype)
    cols = [w_in[:, 0:512], w_in[:, 512:640], w_in[:, 640:1152], w_in[:, 1224:1736], w_in[:, 1736:2248],
            w_in[:, 2248:2760], w_in[:, 1152:1216], pad, w_in[:, 1216:1224],
            jnp.zeros((D_MODEL, LANES - H_IDX), w_in.dtype)]
    w = jnp.concatenate(cols, axis=1).astype(BF16)
    n = w.shape[1]
    dv = 2 * DH_B
    row = lambda i: (i, 0)
    full2 = lambda i: (0, 0)
    lead3 = lambda i: (i, 0, 0)
    lead4 = lambda i: (i, 0, 0, 0)
    outs = [
        ((t // TQ_DSA, DH_IDX, H_IDX * TQ_DSA), BF16, (tm // TQ_DSA, DH_IDX, H_IDX * TQ_DSA), lead3),
        ((t // TQ_DSA, H_IDX, TQ_DSA), F32, (tm // TQ_DSA, H_IDX, TQ_DSA), lead3),
        ((t // TQ_DSA, D_LAT, H_A * TQ_DSA), BF16, (tm // TQ_DSA, D_LAT, H_A * TQ_DSA), lead3),
        ((t, DH_IDX), BF16, (tm, DH_IDX), row),
        ((t, D_LAT), BF16, (tm, D_LAT), row),
        ((t // TK, D_LAT + ONES_ROWS, TK), BF16, (1, D_LAT + ONES_ROWS, TK), lead3),
        ((t // TQ_DIFF, H_B, dv, TQ_DIFF), BF16, (tm // TQ_DIFF, H_B, dv, TQ_DIFF), lead4),
        ((t, H_B * dv), BF16, (tm, H_B * dv), row),
        ((t // TK, H_B, dv + ONES_ROWS, TK), BF16, (1, H_B, dv + ONES_ROWS, TK), lead4),
    ]
    return pl.pallas_call(
        _proj_even_kernel,
        out_shape=[jax.ShapeDtypeStruct(shape, dt) for shape, dt, _, _ in outs],
        grid=(t // tm,),
        in_specs=[
            pl.BlockSpec((tm, D_MODEL), row),
            pl.BlockSpec((6, D_MODEL), full2),
            pl.BlockSpec((D_MODEL, n), full2),
            pl.BlockSpec((1, D_LAT), full2),
            pl.BlockSpec((H_A, DH_A, D_LAT), lambda i: (0, 0, 0)),
        ],
        out_specs=[pl.BlockSpec(block, imap) for _, _, block, imap in outs],
        compiler_params=pltpu.CompilerParams(vmem_limit_bytes=VMEM_LIMIT),
        name="proj_even",
    )(x2, g, w, kvn.reshape(1, D_LAT), w_uk.astype(BF16))


def _proj_odd_kernel(x_ref, g_ref, w_ref, qt_ref, k_ref, vtx_ref):
    tm = x_ref.shape[0]
    h = _rms(x_ref[...], g_ref[0:1, :]).astype(BF16)
    z = _dot(h, w_ref[...])
    for kv in range(H_C_KV):
        k_ref[kv] = z[:, 1024 + kv * DH_C:1024 + (kv + 1) * DH_C].astype(BF16)
    for sb in range(tm // TQ):
        rows = slice(sb * TQ, (sb + 1) * TQ)
        for pr in range(H_C // 2):
            t = (z[rows, pr * LANES:(pr + 1) * LANES] * (DH_C ** -0.5 * LOG2E)).T
            for half in range(2):
                hd = 2 * pr + half
                kv, gq = hd // G_C, hd % G_C
                qt_ref[sb, kv, :, gq * TQ:(gq + 1) * TQ] = t[half * DH_C:(half + 1) * DH_C, :].astype(BF16)
        vt = z[rows, 1152:1280].T
        for kv in range(H_C_KV):
            vtx_ref[sb, kv, 0:DH_C, :] = vt[kv * DH_C:(kv + 1) * DH_C, :].astype(BF16)
            vtx_ref[sb, kv, DH_C:, :] = _ones_rows(TQ)


def _proj_odd(x2, g, w_in):
    t = x2.shape[0]
    tm = min(TM_PROJ, t)
    assert TQ == LANES
    row = lambda i: (i, 0)
    full2 = lambda i: (0, 0)
    lead4 = lambda i: (i, 0, 0, 0)
    dvx = DH_C + ONES_ROWS
    return pl.pallas_call(
        _proj_odd_kernel,
        out_shape=[jax.ShapeDtypeStruct((t // TQ, H_C_KV, DH_C, G_C * TQ), BF16),
                   jax.ShapeDtypeStruct((H_C_KV, t, DH_C), BF16),
                   jax.ShapeDtypeStruct((t // TQ, H_C_KV, dvx, TQ), BF16)],
        grid=(t // tm,),
        in_specs=[pl.BlockSpec((tm, D_MODEL), row),
                  pl.BlockSpec((6, D_MODEL), full2),
                  pl.BlockSpec((D_MODEL, 1280), full2)],
        out_specs=[pl.BlockSpec((tm // TQ, H_C_KV, DH_C, G_C * TQ), lead4),
                   pl.BlockSpec((H_C_KV, tm, DH_C), lambda i: (0, i, 0)),
                   pl.BlockSpec((tm // TQ, H_C_KV, dvx, TQ), lead4)],
        compiler_params=pltpu.CompilerParams(vmem_limit_bytes=VMEM_LIMIT),
        name="proj_odd",
    )(x2, g, w_in.astype(BF16))


def _dsa_kernel(qit_ref, wit_ref, qlt_ref, k_ref, c_ref, ctx_ref, bias_ref, wuvt_ref, o_ref,
                sc_ref, acc_ref, *p_refs, topk):
    TQ = TQ_DSA
    i = pl.program_id(1)
    t0 = i * TQ
    nt = (t0 + TQ + TK - 1) // TK
    n_qb = TQ // LANES
    n_far = jnp.maximum(i * n_qb - N_SAT + 1, 0) // SUB
    kf = float(topk)

    qpos = lax.broadcasted_iota(jnp.int32, (1, TQ), 1) + t0
    limit = (qpos // CHUNK + 1) * CHUNK
    small = limit <= topk
    big = jnp.logical_not(small)
    w = wit_ref[...]
    qit = qit_ref[...]

    def score_tiles(js, carry):
        rmax, rmin = carry
        zs = [_dot(k_ref[j], qit) for j in js]
        for j, z in zip(js, zs):
            sc = jnp.maximum(z[:, 0:TQ], 0.0) * w[0:1, :]
            for h in range(1, H_IDX):
                sc = sc + jnp.maximum(z[:, h * TQ:(h + 1) * TQ], 0.0) * w[h:h + 1, :]
            kpos = lax.broadcasted_iota(jnp.int32, (TK, TQ), 0) + j * TK
            adm = kpos < limit
            sc_ref[j] = jnp.where(adm, sc, -jnp.inf)
            rmax = jnp.maximum(rmax, _fold_rows(jnp.where(adm, sc, -jnp.inf), jnp.maximum))
            rmin = jnp.minimum(rmin, _fold_rows(jnp.where(adm, sc, jnp.inf), jnp.minimum))
        return rmax, rmin

    rmax, rmin = _grouped_loop(
        0, nt, (jnp.full((FOLD_ROWS, TQ), -jnp.inf, F32), jnp.full((FOLD_ROWS, TQ), jnp.inf, F32)),
        score_tiles, GROUPS_DSA)
    hi0 = jnp.max(rmax, axis=0, keepdims=True)
    lo0 = jnp.min(rmin, axis=0, keepdims=True)

    def col_reduce(tile_fn, op, init, final):
        def fn(js, acc):
            for j in js:
                acc = op(acc, _fold_rows(tile_fn(j), op))
            return acc
        acc = _grouped_loop(0, nt, jnp.full((FOLD_ROWS, TQ), init, F32), fn, GROUPS_PASS)
        return final(acc, axis=0, keepdims=True)

    def count_ge(thr):
        return col_reduce(lambda j: jnp.where(sc_ref[j] >= thr, 1.0, 0.0), jnp.add, 0.0, jnp.sum)

    def max_le(thr):
        def tile(j):
            t = sc_ref[j]
            return jnp.where(t <= thr, t, -jnp.inf)
        return col_reduce(tile, jnp.maximum, -jnp.inf, jnp.max)

    def max_lt(thr):
        def tile(j):
            t = sc_ref[j]
            return jnp.where(t < thr, t, -jnp.inf)
        return col_reduce(tile, jnp.maximum, -jnp.inf, jnp.max)

    def bisect_body(_, carry):
        lo, hi = carry
        mid = lo * 0.5 + hi * 0.5
        ge = count_ge(mid) >= kf
        return jnp.where(ge, mid, lo), jnp.where(ge, hi, mid)

    lo, hi = lax.fori_loop(0, N_BISECT, bisect_body, (lo0, hi0))

    v0 = max_le(hi)
    c0 = count_ge(v0)

    def walk_cond(carry):
        _, cnt, it = carry
        pending = jnp.where((cnt < kf) & big, 1.0, 0.0)
        return (jnp.max(pending) > 0.0) & (it < topk + 2)

    def walk_body(carry):
        v, cnt, it = carry
        v = jnp.where((cnt < kf) & big, max_lt(v), v)
        return v, count_ge(v), it + 1

    tau, cnt, _ = lax.while_loop(walk_cond, walk_body, (v0, c0, jnp.int32(0)))
    tau = jnp.where(small, -jnp.finfo(F32).max, tau)

    tied = (cnt > kf) & big

    @pl.when(jnp.max(jnp.where(tied, 1.0, 0.0)) > 0.0)
    def _():
        excess = jnp.where(tied, cnt - kf, 0.0)
        kr = lax.broadcasted_iota(jnp.int32, (TK, TK), 0)
        kc = lax.broadcasted_iota(jnp.int32, (TK, TK), 1)
        upper = jnp.where(kc >= kr, 1.0, 0.0).astype(BF16)

        def drop_body(u, later):
            j = nt - 1 - u
            t = sc_ref[j]
            e = (t == tau) & tied
            behind = _dot(upper, jnp.where(e, 1.0, 0.0).astype(BF16)) + later
            sc_ref[j] = jnp.where(e & (behind <= excess), -jnp.inf, t)
            return behind[0:1, :]

        lax.fori_loop(0, nt, drop_body, jnp.zeros((1, TQ), F32))

    acc_ref[...] = jnp.zeros(acc_ref.shape, F32)
    qlt = qlt_ref[...]

    def attn_tiles(js, m, with_bias):
        s_alls = [_dot(c_ref[j], qlt) for j in js]
        for g, (j, s_all) in enumerate(zip(js, s_alls)):
            p_ref = p_refs[g]
            sel = sc_ref[j] >= tau
            m_new, alpha = [], []
            for h in range(H_A):
                cols = slice(h * TQ, (h + 1) * TQ)
                s = s_all[:, cols]
                if with_bias:
                    s = s + _bias_tile(bias_ref, h, i * n_qb, n_qb, j)
                s = jnp.where(sel, s, NEG_INF)
                m_old = m[:, cols]
                m_h = jnp.maximum(m_old, jnp.max(_fold_rows(s, jnp.maximum), axis=0, keepdims=True))
                p_ref[:, cols] = jnp.exp2(s - m_h).astype(BF16)
                alpha.append(jnp.exp2(m_old - m_h))
                m_new.append(m_h)
            acc_ref[...] = acc_ref[...] * jnp.concatenate(alpha, axis=1) + _dot(ctx_ref[j], p_ref[...])
            m = jnp.concatenate(m_new, axis=1)
        return m

    m = jnp.full((1, H_A * TQ), NEG_INF, F32)
    m = _grouped_loop(0, n_far, m, lambda js, mm: attn_tiles(js, mm, False), GROUPS_DSA)
    _grouped_loop(n_far, nt, m, lambda js, mm: attn_tiles(js, mm, True), GROUPS_NEAR)

    outs = []
    for h in range(H_A):
        cols = slice(h * TQ, (h + 1) * TQ)
        o_t = (acc_ref[0:D_LAT, cols] / acc_ref[D_LAT:D_LAT + 1, cols]).astype(BF16)
        outs.append(_dot(wuvt_ref[h], o_t))
    for pr in range(H_A // 2):
        pair = jnp.concatenate([outs[2 * pr], outs[2 * pr + 1]], axis=0)
        o_ref[:, pr * LANES:(pr + 1) * LANES] = pair.T.astype(BF16)


def _dsa(qit, wit, qlt, ki, ckv, ctx, bias_a, w_uv, b, s):
    TQ = TQ_DSA
    nq, nt = s // TQ, s // TK
    topk = min(TOPK_MAX, s // 4)
    k4 = ki.reshape(b, nt, TK, DH_IDX)
    c4 = ckv.reshape(b, nt, TK, D_LAT)
    ctx = ctx.reshape(b, nt, D_LAT + ONES_ROWS, TK)
    wuvt = jnp.swapaxes(w_uv, 1, 2).astype(BF16)
    qblk = lambda bb, i: (bb * nq + i, 0, 0)
    kv4 = lambda bb, i: (bb, 0, 0, 0)
    once = dict(pipeline_mode=pl.Buffered(1))
    return pl.pallas_call(
        functools.partial(_dsa_kernel, topk=topk),
        out_shape=jax.ShapeDtypeStruct((b * s, H_A * DH_A), BF16),
        grid=(b, nq),
        in_specs=[
            pl.BlockSpec((None, DH_IDX, H_IDX * TQ), qblk),
            pl.BlockSpec((None, H_IDX, TQ), qblk),
            pl.BlockSpec((None, D_LAT, H_A * TQ), qblk),
            pl.BlockSpec((None, nt, TK, DH_IDX), kv4, **once),
            pl.BlockSpec((None, nt, TK, D_LAT), kv4, **once),
            pl.BlockSpec((None, nt, D_LAT + ONES_ROWS, TK), kv4, **once),
            pl.BlockSpec((H_A * N_BT, LANES, LANES), lambda bb, i: (0, 0, 0), **once),
            pl.BlockSpec((H_A, DH_A, D_LAT), lambda bb, i: (0, 0, 0), **once),
        ],
        out_specs=pl.BlockSpec((TQ, H_A * DH_A), lambda bb, i: (bb * nq + i, 0)),
        scratch_shapes=[
            pltpu.VMEM((nt, TK, TQ), F32),
            pltpu.VMEM((D_LAT + ONES_ROWS, H_A * TQ), F32),
        ] + [pltpu.VMEM((TK, H_A * TQ), BF16)] * max(GROUPS_DSA),
        compiler_params=pltpu.CompilerParams(vmem_limit_bytes=VMEM_LIMIT),
        name="dsa_attention",
    )(qit, wit, qlt, k4, c4, ctx, bias_a, wuvt)


def _diff_kernel(qt_ref, k_ref, vtx_ref, bias_ref, lam_ref, sub_ref, o_ref, acc_ref, *p_refs, lam_init):
    h = pl.program_id(1)
    i = pl.program_id(2)
    tq = TQ_DIFF
    n_qb = tq // LANES
    t0 = i * tq
    nt = (t0 + tq + TK - 1) // TK
    n_far = jnp.maximum(i * n_qb - N_SAT + 1, 0) // SUB
    qt = qt_ref[...]
    top = lax.broadcasted_iota(jnp.int32, qt.shape, 0) < DH_B
    q2 = jnp.concatenate([jnp.where(top, qt, jnp.zeros_like(qt)), jnp.where(top, jnp.zeros_like(qt), qt)], axis=1)
    qpos = lax.broadcasted_iota(jnp.int32, (1, tq), 1) + t0
    limit = (qpos // CHUNK + 1) * CHUNK
    acc_ref[...] = jnp.zeros(acc_ref.shape, F32)

    def tiles(js, m, near):
        s2s = [_dot(k_ref[j], q2) for j in js]
        for g, (j, s2) in enumerate(zip(js, s2s)):
            p_ref = p_refs[g]
            if near:
                bias = _bias_tile(bias_ref, h, i * n_qb, n_qb, j)
                adm = (lax.broadcasted_iota(jnp.int32, (TK, tq), 0) + j * TK) < limit
            m_new, alpha = [], []
            for mp in range(2):
                cols = slice(mp * tq, (mp + 1) * tq)
                s = s2[:, cols]
                if near:
                    s = jnp.where(adm, s + bias, NEG_INF)
                m_old = m[:, cols]
                m_h = jnp.maximum(m_old, jnp.max(_fold_rows(s, jnp.maximum), axis=0, keepdims=True))
                p_ref[:, cols] = jnp.exp2(s - m_h).astype(BF16)
                alpha.append(jnp.exp2(m_old - m_h))
                m_new.append(m_h)
            acc_ref[...] = acc_ref[...] * jnp.concatenate(alpha, axis=1) + _dot(vtx_ref[j], p_ref[...])
            m = jnp.concatenate(m_new, axis=1)
        return m

    m = jnp.full((1, 2 * tq), NEG_INF, F32)
    m = _grouped_loop(0, n_far, m, lambda js, mm: tiles(js, mm, False), GROUPS)
    _grouped_loop(n_far, nt, m, lambda js, mm: tiles(js, mm, True), GROUPS_NEAR)

    lf = lam_ref[...]
    lam = (jnp.exp(jnp.sum(lf[0:1, :] * lf[1:2, :], axis=1, keepdims=True))
           - jnp.exp(jnp.sum(lf[2:3, :] * lf[3:4, :], axis=1, keepdims=True)) + lam_init)
    dv = 2 * DH_B
    o = (acc_ref[0:dv, 0:tq] / acc_ref[dv:dv + 1, 0:tq]
         - lam * (acc_ref[0:dv, tq:2 * tq] / acc_ref[dv:dv + 1, tq:2 * tq]))
    o = o * lax.rsqrt(jnp.mean(o * o, axis=0, keepdims=True) + EPS) * sub_ref[...] * (1.0 - lam_init)
    o_ref[...] = o.T.astype(BF16)


def _diff(qbt, kb, vtx, bias_b, b_lambda, b_subln, lam_init, b, s):
    tq = TQ_DIFF
    nq, nt = s // tq, s // TK
    dv = 2 * DH_B
    k4 = kb.reshape(b, nt, TK, H_B * dv)
    vtx = vtx.reshape(b, nt, H_B, dv + ONES_ROWS, TK)
    return pl.pallas_call(
        functools.partial(_diff_kernel, lam_init=lam_init),
        out_shape=jax.ShapeDtypeStruct((b * s, H_B * dv), BF16),
        grid=(b, H_B, nq),
        in_specs=[
            pl.BlockSpec((None, None, dv, tq), lambda bb, h, i: (bb * nq + i, h, 0, 0)),
            pl.BlockSpec((None, nt, TK, dv), lambda bb, h, i: (bb, 0, 0, h)),
            pl.BlockSpec((None, nt, None, dv + ONES_ROWS, TK), lambda bb, h, i: (bb, 0, h, 0, 0)),
            pl.BlockSpec((H_B * N_BT, LANES, LANES), lambda bb, h, i: (0, 0, 0)),
            pl.BlockSpec((4, DH_B), lambda bb, h, i: (0, 0)),
            pl.BlockSpec((dv, 1), lambda bb, h, i: (0, 0)),
        ],
        out_specs=pl.BlockSpec((tq, dv), lambda bb, h, i: (bb * nq + i, h)),
        scratch_shapes=[
            pltpu.VMEM((dv + ONES_ROWS, 2 * tq), F32),
        ] + [pltpu.VMEM((TK, 2 * tq), BF16)] * max(GROUPS),
        compiler_params=pltpu.CompilerParams(vmem_limit_bytes=VMEM_LIMIT),
        name="diff_attention",
    )(qbt, k4, vtx, bias_b, b_lambda.astype(F32), b_subln.reshape(dv, 1).astype(F32))


def _swa_kernel(qt_ref, ka_ref, kb_ref, vta_ref, vtb_ref, bias_ref, sink_ref, o_ref):
    i = pl.program_id(1)
    pad_pen = jnp.where(i == 0, NEG_INF, 0.0)
    for kv in range(H_C_KV):
        qt = qt_ref[kv]
        s = jnp.concatenate([_dot(ka_ref[kv], qt) + pad_pen, _dot(kb_ref[kv], qt)], axis=0) + bias_ref[kv]
        sink = sink_ref[kv]
        m = jnp.maximum(jnp.max(s, axis=0, keepdims=True), sink)
        p = jnp.exp2(s - m).astype(BF16)
        acc = _dot(vta_ref[kv], p[0:TQ, :]) + _dot(vtb_ref[kv], p[TQ:2 * TQ, :])
        o = acc[0:DH_C, :] / (acc[DH_C:DH_C + 1, :] + jnp.exp2(sink - m))
        for pr in range(G_C // 2):
            pair = jnp.concatenate([o[:, (2 * pr) * TQ:(2 * pr + 1) * TQ], o[:, (2 * pr + 1) * TQ:(2 * pr + 2) * TQ]],
                                   axis=0)
            col = (kv * G_C + 2 * pr) * DH_C
            o_ref[:, col:col + LANES] = pair.T.astype(BF16)


def _swa(qt, k, vtx, table_c, sinks, b, s):
    assert WINDOW == TQ and TQ % CHUNK == 0
    nq = s // TQ
    length = 3 * TQ + 1
    m = np.arange(length)
    diff = np.where(m < 2 * TQ + 1, m, m - length)
    u = jnp.take(table_c.astype(F32), jnp.asarray(_t5_bucket_np(diff - TQ)), axis=0) * LOG2E
    bias = _toeplitz(jnp.moveaxis(u, -1, 0), TQ, 2 * TQ)
    rblk = np.arange(TQ)[:, None] // CHUNK + WINDOW // CHUNK
    cblk = np.arange(2 * TQ)[None, :] // CHUNK
    valid = (cblk <= rblk) & (cblk >= rblk - WINDOW // CHUNK)
    bias = jnp.where(jnp.asarray(valid)[None], bias, NEG_INF)
    bias_t = jnp.transpose(bias.reshape(H_C_KV, G_C, TQ, 2 * TQ), (0, 3, 1, 2)).reshape(H_C_KV, 2 * TQ, G_C * TQ)
    sink_row = jnp.repeat(sinks.astype(F32).reshape(H_C_KV, 1, G_C) * LOG2E, TQ, axis=2)

    dvx = DH_C + ONES_ROWS
    prev = lambda bb, i: bb * nq + jnp.maximum(i - 1, 0)
    return pl.pallas_call(
        _swa_kernel,
        out_shape=jax.ShapeDtypeStruct((b * s, H_C * DH_C), BF16),
        grid=(b, nq),
        in_specs=[
            pl.BlockSpec((None, H_C_KV, DH_C, G_C * TQ), lambda bb, i: (bb * nq + i, 0, 0, 0)),
            pl.BlockSpec((H_C_KV, TQ, DH_C), lambda bb, i: (0, prev(bb, i), 0)),
            pl.BlockSpec((H_C_KV, TQ, DH_C), lambda bb, i: (0, bb * nq + i, 0)),
            pl.BlockSpec((None, H_C_KV, dvx, TQ), lambda bb, i: (prev(bb, i), 0, 0, 0)),
            pl.BlockSpec((None, H_C_KV, dvx, TQ), lambda bb, i: (bb * nq + i, 0, 0, 0)),
            pl.BlockSpec((H_C_KV, 2 * TQ, G_C * TQ), lambda bb, i: (0, 0, 0)),
            pl.BlockSpec((H_C_KV, 1, G_C * TQ), lambda bb, i: (0, 0, 0)),
        ],
        out_specs=pl.BlockSpec((TQ, H_C * DH_C), lambda bb, i: (bb * nq + i, 0)),
        compiler_params=pltpu.CompilerParams(vmem_limit_bytes=VMEM_LIMIT),
        name="swa_attention",
    )(qt, k, k, vtx, vtx, bias_t, sink_row)


def _memkv_kernel(mem_ref, g_ref, w_ref, k_ref, v_ref):
    h = _rms(mem_ref[...], g_ref[...]).astype(BF16)
    kv = _dot(h, w_ref[...])
    k_ref[...] = kv[:, 0:H_X * DH_X].astype(BF16)
    v_ref[...] = kv[:, H_X * DH_X:].astype(BF16)


def _memkv(mem, g, wkv):
    b, m, _ = mem.shape
    n = H_X * DH_X
    return pl.pallas_call(
        _memkv_kernel,
        out_shape=[jax.ShapeDtypeStruct((b, m, n), BF16), jax.ShapeDtypeStruct((b, m, n), BF16)],
        grid=(b,),
        in_specs=[pl.BlockSpec((None, m, D_MODEL), lambda i: (i, 0, 0)),
                  pl.BlockSpec((1, D_MODEL), lambda i: (0, 0)),
                  pl.BlockSpec((D_MODEL, 2 * n), lambda i: (0, 0))],
        out_specs=[pl.BlockSpec((None, m, n), lambda i: (i, 0, 0)), pl.BlockSpec((None, m, n), lambda i: (i, 0, 0))],
        compiler_params=pltpu.CompilerParams(vmem_limit_bytes=VMEM_LIMIT),
        name="mem_kv",
    )(mem, g.reshape(1, D_MODEL), wkv.astype(BF16))


def _lane_half_mask(rows, half):
    lane = lax.broadcasted_iota(jnp.int32, (rows, LANES), 1)
    return (lane < 64) if half == 0 else (lane >= 64)


def _tail_kernel(*refs, n_a):
    x_ref, g_ref = refs[0], refs[1]
    a_refs = refs[2:2 + n_a]
    wout_refs = refs[2 + n_a:2 + 2 * n_a]
    wq_ref, kt_ref, v_ref, wo_ref, w1_ref, w2_ref, o_ref = refs[2 + 2 * n_a:]
    tc = x_ref.shape[0] // TAIL_CHAINS
    chains = [slice(c * tc, (c + 1) * tc) for c in range(TAIL_CHAINS)]
    g = g_ref[...]

    ys = []
    for r in chains:
        y = _dot(a_refs[0][r, :], wout_refs[0][...])
        for a_ref, w_ref in zip(a_refs[1:], wout_refs[1:]):
            y = y + _dot(a_ref[r, :], w_ref[...])
        ys.append(y)
    xs = [x_ref[r, :] + _rms(y, g[1:2, :]) for r, y in zip(chains, ys)]

    hqs = [_rms(x, g[2:3, :]).astype(BF16) for x in xs]
    qs = [_dot(hq, wq_ref[...]) for hq in hqs]
    qs = [(q * (DH_X ** -0.5)).astype(BF16) for q in qs]
    pairs = [[] for _ in chains]
    for pr in range(H_X // 2):
        vp = v_ref[:, pr * LANES:(pr + 1) * LANES]
        outs = [jnp.zeros((tc, LANES), F32) for _ in chains]
        for half in range(2):
            qzs = [jnp.where(_lane_half_mask(tc, half), q[:, pr * LANES:(pr + 1) * LANES], jnp.zeros((tc, LANES), BF16))
                   for q in qs]
            ss = [_dot(qz, kt_ref[pr * LANES:(pr + 1) * LANES, :]) for qz in qzs]
            vz = jnp.where(_lane_half_mask(vp.shape[0], half), vp, jnp.zeros_like(vp))
            ps = []
            for sc in ss:
                e = jnp.exp(sc - jnp.max(sc, axis=1, keepdims=True))
                ps.append((e / jnp.sum(e, axis=1, keepdims=True)).astype(BF16))
            outs = [out + _dot(p, vz) for out, p in zip(outs, ps)]
        for c, out in enumerate(outs):
            pairs[c].append(out.astype(BF16))
    ys = [_dot(jnp.concatenate(pc, axis=1), wo_ref[...]) for pc in pairs]
    xs = [x + _rms(y, g[3:4, :]) for x, y in zip(xs, ys)]

    hms = [_rms(x, g[4:5, :]).astype(BF16) for x in xs]
    n_chunks = D_FF // FF_CHUNK
    ys = [jnp.zeros((tc, D_MODEL), F32) for _ in chains]
    a_cur = [_dot(hm, w1_ref[:, 0:FF_CHUNK]) for hm in hms]
    for k in range(n_chunks):
        if k + 1 < n_chunks:
            a_next = [_dot(hm, w1_ref[:, (k + 1) * FF_CHUNK:(k + 2) * FF_CHUNK]) for hm in hms]
        acts = [jnp.square(jnp.maximum(a, 0.0)).astype(BF16) for a in a_cur]
        ys = [y + _dot(act, w2_ref[k * FF_CHUNK:(k + 1) * FF_CHUNK, :]) for y, act in zip(ys, acts)]
        if k + 1 < n_chunks:
            a_cur = a_next
    for r, x, y in zip(chains, xs, ys):
        o_ref[r, :] = x + _rms(y, g[5:6, :])


def _tail(x2, g, a_list, wout_list, wq, kt, v, wo, w1, w2, b, s):
    t = x2.shape[0]
    tm = min(TM_TAIL, s)
    per_b = s // tm
    n_a = len(a_list)
    row = lambda i: (i, 0)
    full2 = lambda i: (0, 0)
    once = dict(pipeline_mode=pl.Buffered(1))
    in_specs = [pl.BlockSpec((tm, D_MODEL), row), pl.BlockSpec((6, D_MODEL), full2)]
    in_specs += [pl.BlockSpec((tm, a.shape[1]), row) for a in a_list]
    in_specs += [pl.BlockSpec(w.shape, full2, **once) for w in wout_list]
    in_specs += [
        pl.BlockSpec(wq.shape, full2, **once),
        pl.BlockSpec((None,) + kt.shape[1:], lambda i: (i // per_b, 0, 0)),
        pl.BlockSpec((None,) + v.shape[1:], lambda i: (i // per_b, 0, 0)),
        pl.BlockSpec(wo.shape, full2, **once),
        pl.BlockSpec(w1.shape, full2, **once),
        pl.BlockSpec(w2.shape, full2, **once),
    ]
    return pl.pallas_call(
        functools.partial(_tail_kernel, n_a=n_a),
        out_shape=jax.ShapeDtypeStruct((t, D_MODEL), F32),
        grid=(t // tm,),
        in_specs=in_specs,
        out_specs=pl.BlockSpec((tm, D_MODEL), row),
        compiler_params=pltpu.CompilerParams(vmem_limit_bytes=VMEM_LIMIT),
        name="tail",
    )(x2, g, *a_list, *wout_list, wq, kt, v, wo, w1, w2)


def kernel(x, mem, rel_bias_table, norm_g, ev_w_in, ev_a_kv_norm, ev_a_w_uk, ev_a_w_uv, ev_b_lambda, ev_b_subln, ev_w_out, od_w_in, od_sinks, od_w_out, xa_wq, xa_wkv, xa_wo, xa_mem_norm, mlp_w1, mlp_w2):
    b, s, d = x.shape
    depth = norm_g.shape[0]
    assert d == D_MODEL and s % TK == 0 and TK % TQ_DIFF == 0 and TK % TQ_DSA == 0 and TQ_DSA % LANES == 0
    x2 = x.reshape(b * s, d)
    bias_a = _causal_bias_tiles(rel_bias_table[:, :H_A])
    bias_b = _causal_bias_tiles(rel_bias_table[:, H_A:H_A + H_B])
    table_c = rel_bias_table[:, H_A + H_B:]
    for l in range(depth):
        g = norm_g[l].astype(F32)
        if l % 2 == 0:
            e = l // 2
            lam_init = 0.8 - 0.6 * math.exp(-0.3 * l)
            qit, wit, qlt, ki, ckv, ctx, qbt, kb, vtx = _proj_even(x2, g, ev_w_in[e], ev_a_kv_norm[e], ev_a_w_uk[e])
            oa = _dsa(qit, wit, qlt, ki, ckv, ctx, bias_a, ev_a_w_uv[e], b, s)
            ob = _diff(qbt, kb, vtx, bias_b, ev_b_lambda[e], ev_b_subln[e], lam_init, b, s)
            n_a = H_A * DH_A
            a_list = [oa, ob]
            wout_list = [ev_w_out[e][:n_a].astype(BF16), ev_w_out[e][n_a:].astype(BF16)]
        else:
            o = l // 2
            qt, k, vtx = _proj_odd(x2, g, od_w_in[o])
            a_list = [_swa(qt, k, vtx, table_c, od_sinks[o], b, s)]
            wout_list = [od_w_out[o].astype(BF16)]
        mk, mv = _memkv(mem, xa_mem_norm[l], xa_wkv[l])
        x2 = _tail(x2, g, a_list, wout_list, xa_wq[l].astype(BF16), jnp.swapaxes(mk, 1, 2), mv,
                   xa_wo[l].astype(BF16), mlp_w1[l].astype(BF16), mlp_w2[l].astype(BF16), b, s)
    return x2.reshape(b, s, d)
```

```python
import functools
import math

import numpy as np
import jax
import jax.numpy as jnp
from jax import lax
from jax.experimental import pallas as pl
from jax.experimental.pallas import tpu as pltpu

F32 = jnp.float32
BF16 = jnp.bfloat16

D_MODEL = 1024
CHUNK = 64
EPS = 1e-6
NEG_INF = -1e30
LOG2E = math.log2(math.e)

H_A, DH_A, D_LAT = 8, 64, 128
H_IDX, DH_IDX = 8, 64
TOPK_MAX = 256
H_B, DH_B = 4, 64
H_C, H_C_KV, DH_C = 16, 2, 64
G_C = H_C // H_C_KV
WINDOW = 128
H_X, DH_X = 4, 64
D_FF = 4 * D_MODEL
N_BUCKETS = 32
MAX_DIST = 1024

LANES = 128
FOLD_ROWS = 64
TQ = 128
TQ_DSA = 128
TQ_DIFF = 512
TK = 512
SUB = TK // LANES
ONES_ROWS = 16
TM_PROJ = 512
TM_TAIL = 512
TAIL_CHAINS = 2
FF_CHUNK = 1024
VMEM_LIMIT = 56 * 1024 * 1024
N_BISECT = 14
GROUPS = (4, 2, 1)
GROUPS_NEAR = (2, 1)
GROUPS_DSA = (4, 2, 1)
GROUPS_PASS = (4, 1)


def _rms(x, g):
    return x * lax.rsqrt(jnp.mean(x * x, axis=-1, keepdims=True) + EPS) * g


def _dot(a, b):
    return jnp.dot(a, b, preferred_element_type=F32)


def _t5_bucket_np(rel):
    nb = N_BUCKETS // 2
    max_exact = nb // 2
    n = np.abs(rel)
    nf = np.maximum(n, 1).astype(np.float32)
    large = max_exact + (np.log(nf / max_exact) / math.log(MAX_DIST / max_exact) * (nb - max_exact)).astype(np.int32)
    large = np.minimum(large, nb - 1)
    return np.where(rel > 0, nb, 0) + np.where(n < max_exact, n, large)


def _sat_blocks():
    d = 1
    while _t5_bucket_np(np.array([-(d * LANES - (LANES - 1))]))[0] != N_BUCKETS // 2 - 1:
        d += 1
    return d


N_SAT = _sat_blocks()
N_BT = N_SAT + 1


def _toeplitz(u, rows, cols):
    length = u.shape[-1]
    flat = jnp.tile(u, (1,) * (u.ndim - 1) + (rows,))[..., :rows * (length - 1)]
    return flat.reshape(u.shape[:-1] + (rows, length - 1))[..., :cols]


def _causal_bias_tiles(table_cols):
    length = 2 * LANES + 1
    m = np.arange(length)
    diff = np.where(m < LANES + 1, m, m - length)
    rel = -diff[None, :] - (np.arange(N_BT) * LANES)[:, None]
    tab = table_cols.astype(F32)
    u = jnp.take(tab, jnp.asarray(_t5_bucket_np(rel)), axis=0)
    u = (u - tab[N_BUCKETS // 2 - 1][None, None, :]) * LOG2E
    tiles = _toeplitz(jnp.moveaxis(u, -1, 0), LANES, LANES)
    return tiles.reshape(table_cols.shape[1] * N_BT, LANES, LANES)


def _fold_rows(x, op):
    acc = x[0:FOLD_ROWS, :]
    for r in range(1, x.shape[0] // FOLD_ROWS):
        acc = op(acc, x[r * FOLD_ROWS:(r + 1) * FOLD_ROWS, :])
    return acc


def _grouped_loop(lo, hi, carry, fn, groups):
    for g in groups:
        n = (hi - lo) // g
        carry = lax.fori_loop(0, n, lambda u, c, lo=lo, g=g: fn([lo + u * g + k for k in range(g)], c), carry)
        lo = lo + n * g
    return carry


def _bias_tile(bias_ref, h, qb0, n_qb, j):
    rows = []
    for c in range(SUB):
        cols = [bias_ref[h * N_BT + jnp.clip(qb0 + r - (j * SUB + c), 0, N_SAT)] for r in range(n_qb)]
        rows.append(cols[0] if n_qb == 1 else jnp.concatenate(cols, axis=1))
    return jnp.concatenate(rows, axis=0)


def _ones_rows(n):
    return jnp.ones((ONES_ROWS, n), BF16)


def _proj_even_kernel(x_ref, g_ref, w_ref, kvn_ref, wuk_ref,
                      qit_ref, wit_ref, qlt_ref, ki_ref, c_ref, ctx_ref, qbt_ref, kb_ref, vtx_ref):
    tm = x_ref.shape[0]
    h = _rms(x_ref[...], g_ref[0:1, :]).astype(BF16)
    z = _dot(h, w_ref[...])
    qa = z[:, 0:512].astype(BF16)
    ckv = _rms(z[:, 512:640], kvn_ref[...])
    c_ref[...] = ckv.astype(BF16)
    kb_ref[...] = z[:, 1664:2176].astype(BF16)
    ki_ref[...] = z[:, 2688:2688 + DH_IDX].astype(BF16)
    ctx_ref[0, D_LAT:, :] = _ones_rows(tm)
    for hb in range(H_B):
        vtx_ref[0, hb, 2 * DH_B:, :] = _ones_rows(tm)
    per_q, per_diff = TQ_DSA // LANES, TQ_DIFF // LANES
    for hh in range(H_A):
        ql = _dot(qa[:, hh * DH_A:(hh + 1) * DH_A], wuk_ref[hh]) * (DH_A ** -0.5 * LOG2E)
        for sb in range(tm // LANES):
            off = hh * TQ_DSA + (sb % per_q) * LANES
            qlt_ref[sb // per_q, :, off:off + LANES] = ql[sb * LANES:(sb + 1) * LANES, :].T.astype(BF16)
    for sb in range(tm // LANES):
        rows = slice(sb * LANES, (sb + 1) * LANES)
        blk, sub = sb // per_q, (sb % per_q) * LANES
        for pr in range(H_IDX // 2):
            t = z[rows, 640 + pr * LANES:640 + (pr + 1) * LANES].T
            for half in range(2):
                off = (2 * pr + half) * TQ_DSA + sub
                qit_ref[blk, :, off:off + LANES] = t[half * DH_IDX:(half + 1) * DH_IDX, :].astype(BF16)
        wit_ref[blk, :, sub:sub + LANES] = z[rows, 2816:2816 + LANES].T[0:H_IDX, :]
        ctx_ref[0, 0:D_LAT, rows] = ckv[rows, :].T.astype(BF16)
        for hb in range(H_B):
            qb = z[rows, 1152 + hb * LANES:1152 + (hb + 1) * LANES] * (DH_B ** -0.5 * LOG2E)
            lanes = slice((sb % per_diff) * LANES, (sb % per_diff + 1) * LANES)
            qbt_ref[sb // per_diff, hb, :, lanes] = qb.T.astype(BF16)
            vtx_ref[0, hb, 0:2 * DH_B, rows] = z[rows, 2176 + hb * LANES:2176 + (hb + 1) * LANES].T.astype(BF16)


def _proj_even(x2, g, w_in, kvn, w_uk):
    t = x2.shape[0]
    tm = TK
    assert t % tm == 0 and tm % TQ_DSA == 0 and tm % TQ_DIFF == 0
    pad = jnp.zeros((D_MODEL, LANES - DH_IDX), w_in.dtype)
    cols = [w_in[:, 0:512], w_in[:, 512:640], w_in[:, 640:1152], w_in[:, 1224:1736], w_in[:, 1736:2248],
            w_in[:, 2248:2760], w_in[:, 1152:1216], pad, w_in[:, 1216:1224],
            jnp.zeros((D_MODEL, LANES - H_IDX), w_in.dtype)]
    w = jnp.concatenate(cols, axis=1).astype(BF16)
    n = w.shape[1]
    dv = 2 * DH_B
    row = lambda i: (i, 0)
    full2 = lambda i: (0, 0)
    lead3 = lambda i: (i, 0, 0)
    lead4 = lambda i: (i, 0, 0, 0)
    outs = [
        ((t // TQ_DSA, DH_IDX, H_IDX * TQ_DSA), BF16, (tm // TQ_DSA, DH_IDX, H_IDX * TQ_DSA), lead3),
        ((t // TQ_DSA, H_IDX, TQ_DSA), F32, (tm // TQ_DSA, H_IDX, TQ_DSA), lead3),
        ((t // TQ_DSA, D_LAT, H_A * TQ_DSA), BF16, (tm // TQ_DSA, D_LAT, H_A * TQ_DSA), lead3),
        ((t, DH_IDX), BF16, (tm, DH_IDX), row),
        ((t, D_LAT), BF16, (tm, D_LAT), row),
        ((t // TK, D_LAT + ONES_ROWS, TK), BF16, (1, D_LAT + ONES_ROWS, TK), lead3),
        ((t // TQ_DIFF, H_B, dv, TQ_DIFF), BF16, (tm // TQ_DIFF, H_B, dv, TQ_DIFF), lead4),
        ((t, H_B * dv), BF16, (tm, H_B * dv), row),
        ((t // TK, H_B, dv + ONES_ROWS, TK), BF16, (1, H_B, dv + ONES_ROWS, TK), lead4),
    ]
    return pl.pallas_call(
        _proj_even_kernel,
        out_shape=[jax.ShapeDtypeStruct(shape, dt) for shape, dt, _, _ in outs],
        grid=(t // tm,),
        in_specs=[
            pl.BlockSpec((tm, D_MODEL), row),
            pl.BlockSpec((6, D_MODEL), full2),
            pl.BlockSpec((D_MODEL, n), full2),
            pl.BlockSpec((1, D_LAT), full2),
            pl.BlockSpec((H_A, DH_A, D_LAT), lambda i: (0, 0, 0)),
        ],
        out_specs=[pl.BlockSpec(block, imap) for _, _, block, imap in outs],
        compiler_params=pltpu.CompilerParams(vmem_limit_bytes=VMEM_LIMIT),
        name="proj_even",
    )(x2, g, w, kvn.reshape(1, D_LAT), w_uk.astype(BF16))


def _proj_odd_kernel(x_ref, g_ref, w_ref, qt_ref, k_ref, vtx_ref):
    tm = x_ref.shape[0]
    h = _rms(x_ref[...], g_ref[0:1, :]).astype(BF16)
    z = _dot(h, w_ref[...])
    for kv in range(H_C_KV):
        k_ref[kv] = z[:, 1024 + kv * DH_C:1024 + (kv + 1) * DH_C].astype(BF16)
    for sb in range(tm // TQ):
        rows = slice(sb * TQ, (sb + 1) * TQ)
        for pr in range(H_C // 2):
            t = (z[rows, pr * LANES:(pr + 1) * LANES] * (DH_C ** -0.5 * LOG2E)).T
            for half in range(2):
                hd = 2 * pr + half
                kv, gq = hd // G_C, hd % G_C
                qt_ref[sb, kv, :, gq * TQ:(gq + 1) * TQ] = t[half * DH_C:(half + 1) * DH_C, :].astype(BF16)
        vt = z[rows, 1152:1280].T
        for kv in range(H_C_KV):
            vtx_ref[sb, kv, 0:DH_C, :] = vt[kv * DH_C:(kv + 1) * DH_C, :].astype(BF16)
            vtx_ref[sb, kv, DH_C:, :] = _ones_rows(TQ)


def _proj_odd(x2, g, w_in):
    t = x2.shape[0]
    tm = min(TM_PROJ, t)
    assert TQ == LANES
    row = lambda i: (i, 0)
    full2 = lambda i: (0, 0)
    lead4 = lambda i: (i, 0, 0, 0)
    dvx = DH_C + ONES_ROWS
    return pl.pallas_call(
        _proj_odd_kernel,
        out_shape=[jax.ShapeDtypeStruct((t // TQ, H_C_KV, DH_C, G_C * TQ), BF16),
                   jax.ShapeDtypeStruct((H_C_KV, t, DH_C), BF16),
                   jax.ShapeDtypeStruct((t // TQ, H_C_KV, dvx, TQ), BF16)],
        grid=(t // tm,),
        in_specs=[pl.BlockSpec((tm, D_MODEL), row),
                  pl.BlockSpec((6, D_MODEL), full2),
                  pl.BlockSpec((D_MODEL, 1280), full2)],
        out_specs=[pl.BlockSpec((tm // TQ, H_C_KV, DH_C, G_C * TQ), lead4),
                   pl.BlockSpec((H_C_KV, tm, DH_C), lambda i: (0, i, 0)),
                   pl.BlockSpec((tm // TQ, H_C_KV, dvx, TQ), lead4)],
        compiler_params=pltpu.CompilerParams(vmem_limit_bytes=VMEM_LIMIT),
        name="proj_odd",
    )(x2, g, w_in.astype(BF16))


def _dsa_kernel(qit_ref, wit_ref, qlt_ref, k_ref, c_ref, ctx_ref, bias_ref, wuvt_ref, o_ref,
                sc_ref, acc_ref, *p_refs, topk):
    TQ = TQ_DSA
    i = pl.program_id(1)
    t0 = i * TQ
    nt = (t0 + TQ + TK - 1) // TK
    n_qb = TQ // LANES
    n_far = jnp.maximum(i * n_qb - N_SAT + 1, 0) // SUB
    kf = float(topk)

    qpos = lax.broadcasted_iota(jnp.int32, (1, TQ), 1) + t0
    limit = (qpos // CHUNK + 1) * CHUNK
    small = limit <= topk
    big = jnp.logical_not(small)
    w = wit_ref[...]
    qit = qit_ref[...]

    def score_tiles(js, carry):
        rmax, rmin = carry
        zs = [_dot(k_ref[j], qit) for j in js]
        for j, z in zip(js, zs):
            sc = jnp.maximum(z[:, 0:TQ], 0.0) * w[0:1, :]
            for h in range(1, H_IDX):
                sc = sc + jnp.maximum(z[:, h * TQ:(h + 1) * TQ], 0.0) * w[h:h + 1, :]
            kpos = lax.broadcasted_iota(jnp.int32, (TK, TQ), 0) + j * TK
            adm = kpos < limit
            sc_ref[j] = jnp.where(adm, sc, -jnp.inf)
            rmax = jnp.maximum(rmax, _fold_rows(jnp.where(adm, sc, -jnp.inf), jnp.maximum))
            rmin = jnp.minimum(rmin, _fold_rows(jnp.where(adm, sc, jnp.inf), jnp.minimum))
        return rmax, rmin

    rmax, rmin = _grouped_loop(
        0, nt, (jnp.full((FOLD_ROWS, TQ), -jnp.inf, F32), jnp.full((FOLD_ROWS, TQ), jnp.inf, F32)),
        score_tiles, GROUPS_DSA)
    hi0 = jnp.max(rmax, axis=0, keepdims=True)
    lo0 = jnp.min(rmin, axis=0, keepdims=True)

    def col_reduce(tile_fn, op, init, final):
        def fn(js, acc):
            for j in js:
                acc = op(acc, _fold_rows(tile_fn(j), op))
            return acc
        acc = _grouped_loop(0, nt, jnp.full((FOLD_ROWS, TQ), init, F32), fn, GROUPS_PASS)
        return final(acc, axis=0, keepdims=True)

    def count_ge(thr):
        return col_reduce(lambda j: jnp.where(sc_ref[j] >= thr, 1.0, 0.0), jnp.add, 0.0, jnp.sum)

    def max_le(thr):
        def tile(j):
            t = sc_ref[j]
            return jnp.where(t <= thr, t, -jnp.inf)
        return col_reduce(tile, jnp.maximum, -jnp.inf, jnp.max)

    def max_lt(thr):
        def tile(j):
            t = sc_ref[j]
            return jnp.where(t < thr, t, -jnp.inf)
        return col_reduce(tile, jnp.maximum, -jnp.inf, jnp.max)

    def bisect_body(_, carry):
        lo, hi = carry
        mid = lo * 0.5 + hi * 0.5
        ge = count_ge(mid) >= kf
        return jnp.where(ge, mid, lo), jnp.where(ge, hi, mid)

    lo, hi = lax.fori_loop(0, N_BISECT, bisect_body, (lo0, hi0))

    v0 = max_le(hi)
    c0 = count_ge(v0)

    def walk_cond(carry):
        _, cnt, it = carry
        pending = jnp.where((cnt < kf) & big, 1.0, 0.0)
        return (jnp.max(pending) > 0.0) & (it < topk + 2)

    def walk_body(carry):
        v, cnt, it = carry
        v = jnp.where((cnt < kf) & big, max_lt(v), v)
        return v, count_ge(v), it + 1

    tau, cnt, _ = lax.while_loop(walk_cond, walk_body, (v0, c0, jnp.int32(0)))
    tau = jnp.where(small, -jnp.finfo(F32).max, tau)

    tied = (cnt > kf) & big

    @pl.when(jnp.max(jnp.where(tied, 1.0, 0.0)) > 0.0)
    def _():
        excess = jnp.where(tied, cnt - kf, 0.0)
        kr = lax.broadcasted_iota(jnp.int32, (TK, TK), 0)
        kc = lax.broadcasted_iota(jnp.int32, (TK, TK), 1)
        upper = jnp.where(kc >= kr, 1.0, 0.0).astype(BF16)

        def drop_body(u, later):
            j = nt - 1 - u
            t = sc_ref[j]
            e = (t == tau) & tied
            behind = _dot(upper, jnp.where(e, 1.0, 0.0).astype(BF16)) + later
            sc_ref[j] = jnp.where(e & (behind <= excess), -jnp.inf, t)
            return behind[0:1, :]

        lax.fori_loop(0, nt, drop_body, jnp.zeros((1, TQ), F32))

    acc_ref[...] = jnp.zeros(acc_ref.shape, F32)
    qlt = qlt_ref[...]

    def attn_tiles(js, m, with_bias):
        s_alls = [_dot(c_ref[j], qlt) for j in js]
        for g, (j, s_all) in enumerate(zip(js, s_alls)):
            p_ref = p_refs[g]
            sel = sc_ref[j] >= tau
            m_new, alpha = [], []
            for h in range(H_A):
                cols = slice(h * TQ, (h + 1) * TQ)
                s = s_all[:, cols]
                if with_bias:
                    s = s + _bias_tile(bias_ref, h, i * n_qb, n_qb, j)
                s = jnp.where(sel, s, NEG_INF)
                m_old = m[:, cols]
                m_h = jnp.maximum(m_old, jnp.max(_fold_rows(s, jnp.maximum), axis=0, keepdims=True))
                p_ref[:, cols] = jnp.exp2(s - m_h).astype(BF16)
                alpha.append(jnp.exp2(m_old - m_h))
                m_new.append(m_h)
            acc_ref[...] = acc_ref[...] * jnp.concatenate(alpha, axis=1) + _dot(ctx_ref[j], p_ref[...])
            m = jnp.concatenate(m_new, axis=1)
        return m

    m = jnp.full((1, H_A * TQ), NEG_INF, F32)
    m = _grouped_loop(0, n_far, m, lambda js, mm: attn_tiles(js, mm, False), GROUPS_DSA)
    _grouped_loop(n_far, nt, m, lambda js, mm: attn_tiles(js, mm, True), GROUPS_NEAR)

    outs = []
    for h in range(H_A):
        cols = slice(h * TQ, (h + 1) * TQ)
        o_t = (acc_ref[0:D_LAT, cols] / acc_ref[D_LAT:D_LAT + 1, cols]).astype(BF16)
        outs.append(_dot(wuvt_ref[h], o_t))
    for pr in range(H_A // 2):
        pair = jnp.concatenate([outs[2 * pr], outs[2 * pr + 1]], axis=0)
        o_ref[:, pr * LANES:(pr + 1) * LANES] = pair.T.astype(BF16)


def _dsa(qit, wit, qlt, ki, ckv, ctx, bias_a, w_uv, b, s):
    TQ = TQ_DSA
    nq, nt = s // TQ, s // TK
    topk = min(TOPK_MAX, s // 4)
    k4 = ki.reshape(b, nt, TK, DH_IDX)
    c4 = ckv.reshape(b, nt, TK, D_LAT)
    ctx = ctx.reshape(b, nt, D_LAT + ONES_ROWS, TK)
    wuvt = jnp.swapaxes(w_uv, 1, 2).astype(BF16)
    qblk = lambda bb, i: (bb * nq + i, 0, 0)
    kv4 = lambda bb, i: (bb, 0, 0, 0)
    once = dict(pipeline_mode=pl.Buffered(1))
    return pl.pallas_call(
        functools.partial(_dsa_kernel, topk=topk),
        out_shape=jax.ShapeDtypeStruct((b * s, H_A * DH_A), BF16),
        grid=(b, nq),
        in_specs=[
            pl.BlockSpec((None, DH_IDX, H_IDX * TQ), qblk),
            pl.BlockSpec((None, H_IDX, TQ), qblk),
            pl.BlockSpec((None, D_LAT, H_A * TQ), qblk),
            pl.BlockSpec((None, nt, TK, DH_IDX), kv4, **once),
            pl.BlockSpec((None, nt, TK, D_LAT), kv4, **once),
            pl.BlockSpec((None, nt, D_LAT + ONES_ROWS, TK), kv4, **once),
            pl.BlockSpec((H_A * N_BT, LANES, LANES), lambda bb, i: (0, 0, 0), **once),
            pl.BlockSpec((H_A, DH_A, D_LAT), lambda bb, i: (0, 0, 0), **once),
        ],
        out_specs=pl.BlockSpec((TQ, H_A * DH_A), lambda bb, i: (bb * nq + i, 0)),
        scratch_shapes=[
            pltpu.VMEM((nt, TK, TQ), F32),
            pltpu.VMEM((D_LAT + ONES_ROWS, H_A * TQ), F32),
        ] + [pltpu.VMEM((TK, H_A * TQ), BF16)] * max(GROUPS_DSA),
        compiler_params=pltpu.CompilerParams(vmem_limit_bytes=VMEM_LIMIT),
        name="dsa_attention",
    )(qit, wit, qlt, k4, c4, ctx, bias_a, wuvt)


def _diff_kernel(qt_ref, k_ref, vtx_ref, bias_ref, lam_ref, sub_ref, o_ref, acc_ref, *p_refs, lam_init):
    h = pl.program_id(1)
    i = pl.program_id(2)
    tq = TQ_DIFF
    n_qb = tq // LANES
    t0 = i * tq
    nt = (t0 + tq + TK - 1) // TK
    n_far = jnp.maximum(i * n_qb - N_SAT + 1, 0) // SUB
    qt = qt_ref[...]
    top = lax.broadcasted_iota(jnp.int32, qt.shape, 0) < DH_B
    q2 = jnp.concatenate([jnp.where(top, qt, jnp.zeros_like(qt)), jnp.where(top, jnp.zeros_like(qt), qt)], axis=1)
    qpos = lax.broadcasted_iota(jnp.int32, (1, tq), 1) + t0
    limit = (qpos // CHUNK + 1) * CHUNK
    acc_ref[...] = jnp.zeros(acc_ref.shape, F32)

    def tiles(js, m, near):
        s2s = [_dot(k_ref[j], q2).astype(BF16) for j in js]
        for g, (j, s2) in enumerate(zip(js, s2s)):
            p_ref = p_refs[g]
            if near:
                bias = _bias_tile(bias_ref, h, i * n_qb, n_qb, j).astype(BF16)
                adm = (lax.broadcasted_iota(jnp.int32, (TK, tq), 0) + j * TK) < limit
            m_new, alpha = [], []
            for mp in range(2):
                cols = slice(mp * tq, (mp + 1) * tq)
                s = s2[:, cols]
                if near:
                    s = jnp.where(adm, s + bias, jnp.asarray(NEG_INF, BF16))
                m_old = m[:, cols]
                m_h = jnp.maximum(m_old, jnp.max(_fold_rows(s, jnp.maximum), axis=0, keepdims=True).astype(F32))
                p_ref[:, cols] = jnp.exp2(s - m_h.astype(BF16))
                alpha.append(jnp.exp2(m_old - m_h))
                m_new.append(m_h)
            acc_ref[...] = acc_ref[...] * jnp.concatenate(alpha, axis=1) + _dot(vtx_ref[j], p_ref[...])
            m = jnp.concatenate(m_new, axis=1)
        return m

    m = jnp.full((1, 2 * tq), NEG_INF, F32)
    m = _grouped_loop(0, n_far, m, lambda js, mm: tiles(js, mm, False), GROUPS)
    _grouped_loop(n_far, nt, m, lambda js, mm: tiles(js, mm, True), GROUPS_NEAR)

    lf = lam_ref[...]
    lam = (jnp.exp(jnp.sum(lf[0:1, :] * lf[1:2, :], axis=1, keepdims=True))
           - jnp.exp(jnp.sum(lf[2:3, :] * lf[3:4, :], axis=1, keepdims=True)) + lam_init)
    dv = 2 * DH_B
    o = (acc_ref[0:dv, 0:tq] / acc_ref[dv:dv + 1, 0:tq]
         - lam * (acc_ref[0:dv, tq:2 * tq] / acc_ref[dv:dv + 1, tq:2 * tq]))
    o = o * lax.rsqrt(jnp.mean(o * o, axis=0, keepdims=True) + EPS) * sub_ref[...] * (1.0 - lam_init)
    o_ref[...] = o.T.astype(BF16)


def _diff(qbt, kb, vtx, bias_b, b_lambda, b_subln, lam_init, b, s):
    tq = TQ_DIFF
    nq, nt = s // tq, s // TK
    dv = 2 * DH_B
    k4 = kb.reshape(b, nt, TK, H_B * dv)
    vtx = vtx.reshape(b, nt, H_B, dv + ONES_ROWS, TK)
    return pl.pallas_call(
        functools.partial(_diff_kernel, lam_init=lam_init),
        out_shape=jax.ShapeDtypeStruct((b * s, H_B * dv), BF16),
        grid=(b, H_B, nq),
        in_specs=[
            pl.BlockSpec((None, None, dv, tq), lambda bb, h, i: (bb * nq + i, h, 0, 0)),
            pl.BlockSpec((None, nt, TK, dv), lambda bb, h, i: (bb, 0, 0, h)),
            pl.BlockSpec((None, nt, None, dv + ONES_ROWS, TK), lambda bb, h, i: (bb, 0, h, 0, 0)),
            pl.BlockSpec((H_B * N_BT, LANES, LANES), lambda bb, h, i: (0, 0, 0)),
            pl.BlockSpec((4, DH_B), lambda bb, h, i: (0, 0)),
            pl.BlockSpec((dv, 1), lambda bb, h, i: (0, 0)),
        ],
        out_specs=pl.BlockSpec((tq, dv), lambda bb, h, i: (bb * nq + i, h)),
        scratch_shapes=[
            pltpu.VMEM((dv + ONES_ROWS, 2 * tq), F32),
        ] + [pltpu.VMEM((TK, 2 * tq), BF16)] * max(GROUPS),
        compiler_params=pltpu.CompilerParams(vmem_limit_bytes=VMEM_LIMIT),
        name="diff_attention",
    )(qbt, k4, vtx, bias_b, b_lambda.astype(F32), b_subln.reshape(dv, 1).astype(F32))


def _swa_kernel(qt_ref, ka_ref, kb_ref, vta_ref, vtb_ref, bias_ref, sink_ref, o_ref):
    i = pl.program_id(1)
    pad_pen = jnp.where(i == 0, NEG_INF, 0.0)
    for kv in range(H_C_KV):
        qt = qt_ref[kv]
        s = jnp.concatenate([_dot(ka_ref[kv], qt) + pad_pen, _dot(kb_ref[kv], qt)], axis=0) + bias_ref[kv]
        sink = sink_ref[kv]
        m = jnp.maximum(jnp.max(s, axis=0, keepdims=True), sink)
        p = jnp.exp2(s - m).astype(BF16)
        acc = _dot(vta_ref[kv], p[0:TQ, :]) + _dot(vtb_ref[kv], p[TQ:2 * TQ, :])
        o = acc[0:DH_C, :] / (acc[DH_C:DH_C + 1, :] + jnp.exp2(sink - m))
        for pr in range(G_C // 2):
            pair = jnp.concatenate([o[:, (2 * pr) * TQ:(2 * pr + 1) * TQ], o[:, (2 * pr + 1) * TQ:(2 * pr + 2) * TQ]],
                                   axis=0)
            col = (kv * G_C + 2 * pr) * DH_C
            o_ref[:, col:col + LANES] = pair.T.astype(BF16)


def _swa(qt, k, vtx, table_c, sinks, b, s):
    assert WINDOW == TQ and TQ % CHUNK == 0
    nq = s // TQ
    length = 3 * TQ + 1
    m = np.arange(length)
    diff = np.where(m < 2 * TQ + 1, m, m - length)
    u = jnp.take(table_c.astype(F32), jnp.asarray(_t5_bucket_np(diff - TQ)), axis=0) * LOG2E
    bias = _toeplitz(jnp.moveaxis(u, -1, 0), TQ, 2 * TQ)
    rblk = np.arange(TQ)[:, None] // CHUNK + WINDOW // CHUNK
    cblk = np.arange(2 * TQ)[None, :] // CHUNK
    valid = (cblk <= rblk) & (cblk >= rblk - WINDOW // CHUNK)
    bias = jnp.where(jnp.asarray(valid)[None], bias, NEG_INF)
    bias_t = jnp.transpose(bias.reshape(H_C_KV, G_C, TQ, 2 * TQ), (0, 3, 1, 2)).reshape(H_C_KV, 2 * TQ, G_C * TQ)
    sink_row = jnp.repeat(sinks.astype(F32).reshape(H_C_KV, 1, G_C) * LOG2E, TQ, axis=2)

    dvx = DH_C + ONES_ROWS
    prev = lambda bb, i: bb * nq + jnp.maximum(i - 1, 0)
    return pl.pallas_call(
        _swa_kernel,
        out_shape=jax.ShapeDtypeStruct((b * s, H_C * DH_C), BF16),
        grid=(b, nq),
        in_specs=[
            pl.BlockSpec((None, H_C_KV, DH_C, G_C * TQ), lambda bb, i: (bb * nq + i, 0, 0, 0)),
            pl.BlockSpec((H_C_KV, TQ, DH_C), lambda bb, i: (0, prev(bb, i), 0)),
            pl.BlockSpec((H_C_KV, TQ, DH_C), lambda bb, i: (0, bb * nq + i, 0)),
            pl.BlockSpec((None, H_C_KV, dvx, TQ), lambda bb, i: (prev(bb, i), 0, 0, 0)),
            pl.BlockSpec((None, H_C_KV, dvx, TQ), lambda bb, i: (bb * nq + i, 0, 0, 0)),
            pl.BlockSpec((H_C_KV, 2 * TQ, G_C * TQ), lambda bb, i: (0, 0, 0)),
            pl.BlockSpec((H_C_KV, 1, G_C * TQ), lambda bb, i: (0, 0, 0)),
        ],
        out_specs=pl.BlockSpec((TQ, H_C * DH_C), lambda bb, i: (bb * nq + i, 0)),
        compiler_params=pltpu.CompilerParams(vmem_limit_bytes=VMEM_LIMIT),
        name="swa_attention",
    )(qt, k, k, vtx, vtx, bias_t, sink_row)


def _memkv_kernel(mem_ref, g_ref, w_ref, k_ref, v_ref):
    h = _rms(mem_ref[...], g_ref[...]).astype(BF16)
    kv = _dot(h, w_ref[...])
    k_ref[...] = kv[:, 0:H_X * DH_X].astype(BF16)
    v_ref[...] = kv[:, H_X * DH_X:].astype(BF16)


def _memkv(mem, g, wkv):
    b, m, _ = mem.shape
    n = H_X * DH_X
    return pl.pallas_call(
        _memkv_kernel,
        out_shape=[jax.ShapeDtypeStruct((b, m, n), BF16), jax.ShapeDtypeStruct((b, m, n), BF16)],
        grid=(b,),
        in_specs=[pl.BlockSpec((None, m, D_MODEL), lambda i: (i, 0, 0)),
                  pl.BlockSpec((1, D_MODEL), lambda i: (0, 0)),
                  pl.BlockSpec((D_MODEL, 2 * n), lambda i: (0, 0))],
        out_specs=[pl.BlockSpec((None, m, n), lambda i: (i, 0, 0)), pl.BlockSpec((None, m, n), lambda i: (i, 0, 0))],
        compiler_params=pltpu.CompilerParams(vmem_limit_bytes=VMEM_LIMIT),
        name="mem_kv",
    )(mem, g.reshape(1, D_MODEL), wkv.astype(BF16))


def _lane_half_mask(rows, half):
    lane = lax.broadcasted_iota(jnp.int32, (rows, LANES), 1)
    return (lane < 64) if half == 0 else (lane >= 64)


def _tail_kernel(*refs, n_a):
    x_ref, g_ref = refs[0], refs[1]
    a_refs = refs[2:2 + n_a]
    wout_refs = refs[2 + n_a:2 + 2 * n_a]
    wq_ref, kt_ref, v_ref, wo_ref, w1_ref, w2_ref, o_ref = refs[2 + 2 * n_a:]
    tc = x_ref.shape[0] // TAIL_CHAINS
    chains = [slice(c * tc, (c + 1) * tc) for c in range(TAIL_CHAINS)]
    g = g_ref[...]

    ys = []
    for r in chains:
        y = _dot(a_refs[0][r, :], wout_refs[0][...])
        for a_ref, w_ref in zip(a_refs[1:], wout_refs[1:]):
            y = y + _dot(a_ref[r, :], w_ref[...])
        ys.append(y)
    xs = [x_ref[r, :] + _rms(y, g[1:2, :]) for r, y in zip(chains, ys)]

    hqs = [_rms(x, g[2:3, :]).astype(BF16) for x in xs]
    qs = [_dot(hq, wq_ref[...]) for hq in hqs]
    qs = [(q * (DH_X ** -0.5)).astype(BF16) for q in qs]
    pairs = [[] for _ in chains]
    for pr in range(H_X // 2):
        vp = v_ref[:, pr * LANES:(pr + 1) * LANES]
        outs = [jnp.zeros((tc, LANES), F32) for _ in chains]
        for half in range(2):
            qzs = [jnp.where(_lane_half_mask(tc, half), q[:, pr * LANES:(pr + 1) * LANES], jnp.zeros((tc, LANES), BF16))
                   for q in qs]
            ss = [_dot(qz, kt_ref[pr * LANES:(pr + 1) * LANES, :]) for qz in qzs]
            vz = jnp.where(_lane_half_mask(vp.shape[0], half), vp, jnp.zeros_like(vp))
            ps = []
            for sc in ss:
                e = jnp.exp(sc - jnp.max(sc, axis=1, keepdims=True))
                ps.append((e / jnp.sum(e, axis=1, keepdims=True)).astype(BF16))
            outs = [out + _dot(p, vz) for out, p in zip(outs, ps)]
        for c, out in enumerate(outs):
            pairs[c].append(out.astype(BF16))
    ys = [_dot(jnp.concatenate(pc, axis=1), wo_ref[...]) for pc in pairs]
    xs = [x + _rms(y, g[3:4, :]) for x, y in zip(xs, ys)]

    hms = [_rms(x, g[4:5, :]).astype(BF16) for x in xs]
    n_chunks = D_FF // FF_CHUNK
    ys = [jnp.zeros((tc, D_MODEL), F32) for _ in chains]
    a_cur = [_dot(hm, w1_ref[:, 0:FF_CHUNK]) for hm in hms]
    for k in range(n_chunks):
        if k + 1 < n_chunks:
            a_next = [_dot(hm, w1_ref[:, (k + 1) * FF_CHUNK:(k + 2) * FF_CHUNK]) for hm in hms]
        acts = [jnp.square(jnp.maximum(a, 0.0)).astype(BF16) for a in a_cur]
        ys = [y + _dot(act, w2_ref[k * FF_CHUNK:(k + 1) * FF_CHUNK, :]) for y, act in zip(ys, acts)]
        if k + 1 < n_chunks:
            a_cur = a_next
    for r, x, y in zip(chains, xs, ys):
        o_ref[r, :] = x + _rms(y, g[5:6, :])


def _tail(x2, g, a_list, wout_list, wq, kt, v, wo, w1, w2, b, s):
    t = x2.shape[0]
    tm = min(TM_TAIL, s)
    per_b = s // tm
    n_a = len(a_list)
    row = lambda i: (i, 0)
    full2 = lambda i: (0, 0)
    once = dict(pipeline_mode=pl.Buffered(1))
    in_specs = [pl.BlockSpec((tm, D_MODEL), row), pl.BlockSpec((6, D_MODEL), full2)]
    in_specs += [pl.BlockSpec((tm, a.shape[1]), row) for a in a_list]
    in_specs += [pl.BlockSpec(w.shape, full2, **once) for w in wout_list]
    in_specs += [
        pl.BlockSpec(wq.shape, full2, **once),
        pl.BlockSpec((None,) + kt.shape[1:], lambda i: (i // per_b, 0, 0)),
        pl.BlockSpec((None,) + v.shape[1:], lambda i: (i // per_b, 0, 0)),
        pl.BlockSpec(wo.shape, full2, **once),
        pl.BlockSpec(w1.shape, full2, **once),
        pl.BlockSpec(w2.shape, full2, **once),
    ]
    return pl.pallas_call(
        functools.partial(_tail_kernel, n_a=n_a),
        out_shape=jax.ShapeDtypeStruct((t, D_MODEL), F32),
        grid=(t // tm,),
        in_specs=in_specs,
        out_specs=pl.BlockSpec((tm, D_MODEL), row),
        compiler_params=pltpu.CompilerParams(vmem_limit_bytes=VMEM_LIMIT),
        name="tail",
    )(x2, g, *a_list, *wout_list, wq, kt, v, wo, w1, w2)


def kernel(x, mem, rel_bias_table, norm_g, ev_w_in, ev_a_kv_norm, ev_a_w_uk, ev_a_w_uv, ev_b_lambda, ev_b_subln, ev_w_out, od_w_in, od_sinks, od_w_out, xa_wq, xa_wkv, xa_wo, xa_mem_norm, mlp_w1, mlp_w2):
    b, s, d = x.shape
    depth = norm_g.shape[0]
    assert d == D_MODEL and s % TK == 0 and TK % TQ_DIFF == 0 and TK % TQ_DSA == 0 and TQ_DSA % LANES == 0
    x2 = x.reshape(b * s, d)
    bias_a = _causal_bias_tiles(rel_bias_table[:, :H_A])
    bias_b = _causal_bias_tiles(rel_bias_table[:, H_A:H_A + H_B])
    table_c = rel_bias_table[:, H_A + H_B:]
    for l in range(depth):
        g = norm_g[l].astype(F32)
        if l % 2 == 0:
            e = l // 2
            lam_init = 0.8 - 0.6 * math.exp(-0.3 * l)
            qit, wit, qlt, ki, ckv, ctx, qbt, kb, vtx = _proj_even(x2, g, ev_w_in[e], ev_a_kv_norm[e], ev_a_w_uk[e])
            oa = _dsa(qit, wit, qlt, ki, ckv, ctx, bias_a, ev_a_w_uv[e], b, s)
            ob = _diff(qbt, kb, vtx, bias_b, ev_b_lambda[e], ev_b_subln[e], lam_init, b, s)
            n_a = H_A * DH_A
            a_list = [oa, ob]
            wout_list = [ev_w_out[e][:n_a].astype(BF16), ev_w_out[e][n_a:].astype(BF16)]
        else:
            o = l // 2
            qt, k, vtx = _proj_odd(x2, g, od_w_in[o])
            a_list = [_swa(qt, k, vtx, table_c, od_sinks[o], b, s)]
            wout_list = [od_w_out[o].astype(BF16)]
        mk, mv = _memkv(mem, xa_mem_norm[l], xa_wkv[l])
        x2 = _tail(x2, g, a_list, wout_list, xa_wq[l].astype(BF16), jnp.swapaxes(mk, 1, 2), mv,
                   xa_wo[l].astype(BF16), mlp_w1[l].astype(BF16), mlp_w2[l].astype(BF16), b, s)
    return x2.reshape(b, s, d)
```

```python
import functools
import math

import numpy as np
import jax
import jax.numpy as jnp
from jax import lax
from jax.experimental import pallas as pl
from jax.experimental.pallas import tpu as pltpu

F32 = jnp.float32
BF16 = jnp.bfloat16

D_MODEL = 1024
CHUNK = 64
EPS = 1e-6
NEG_INF = -1e30
LOG2E = math.log2(math.e)

H_A, DH_A, D_LAT = 8, 64, 128
H_IDX, DH_IDX = 8, 64
TOPK_MAX = 256
H_B, DH_B = 4, 64
H_C, H_C_KV, DH_C = 16, 2, 64
G_C = H_C // H_C_KV
WINDOW = 128
H_X, DH_X = 4, 64
D_FF = 4 * D_MODEL
N_BUCKETS = 32
MAX_DIST = 1024

LANES = 128
FOLD_ROWS = 64
TQ = 128
TQ_DSA = 128
TQ_DIFF = 512
TK = 512
SUB = TK // LANES
ONES_ROWS = 16
TM_PROJ = 512
TM_TAIL = 512
TAIL_CHAINS = 2
FF_CHUNK = 1024
VMEM_LIMIT = 56 * 1024 * 1024
N_BISECT = 14
GROUPS = (4, 2, 1)
GROUPS_NEAR = (2, 1)
GROUPS_DSA = (4, 2, 1)
GROUPS_PASS = (4, 1)


def _rms(x, g):
    return x * lax.rsqrt(jnp.mean(x * x, axis=-1, keepdims=True) + EPS) * g


def _dot(a, b):
    return jnp.dot(a, b, preferred_element_type=F32)


def _t5_bucket_np(rel):
    nb = N_BUCKETS // 2
    max_exact = nb // 2
    n = np.abs(rel)
    nf = np.maximum(n, 1).astype(np.float32)
    large = max_exact + (np.log(nf / max_exact) / math.log(MAX_DIST / max_exact) * (nb - max_exact)).astype(np.int32)
    large = np.minimum(large, nb - 1)
    return np.where(rel > 0, nb, 0) + np.where(n < max_exact, n, large)


def _sat_blocks():
    d = 1
    while _t5_bucket_np(np.array([-(d * LANES - (LANES - 1))]))[0] != N_BUCKETS // 2 - 1:
        d += 1
    return d


N_SAT = _sat_blocks()
N_BT = N_SAT + 1


def _toeplitz(u, rows, cols):
    length = u.shape[-1]
    flat = jnp.tile(u, (1,) * (u.ndim - 1) + (rows,))[..., :rows * (length - 1)]
    return flat.reshape(u.shape[:-1] + (rows, length - 1))[..., :cols]


def _causal_bias_tiles(table_cols):
    length = 2 * LANES + 1
    m = np.arange(length)
    diff = np.where(m < LANES + 1, m, m - length)
    rel = -diff[None, :] - (np.arange(N_BT) * LANES)[:, None]
    tab = table_cols.astype(F32)
    u = jnp.take(tab, jnp.asarray(_t5_bucket_np(rel)), axis=0)
    u = (u - tab[N_BUCKETS // 2 - 1][None, None, :]) * LOG2E
    tiles = _toeplitz(jnp.moveaxis(u, -1, 0), LANES, LANES)
    return tiles.reshape(table_cols.shape[1] * N_BT, LANES, LANES)


def _fold_rows(x, op):
    acc = x[0:FOLD_ROWS, :]
    for r in range(1, x.shape[0] // FOLD_ROWS):
        acc = op(acc, x[r * FOLD_ROWS:(r + 1) * FOLD_ROWS, :])
    return acc


def _grouped_loop(lo, hi, carry, fn, groups):
    for g in groups:
        n = (hi - lo) // g
        carry = lax.fori_loop(0, n, lambda u, c, lo=lo, g=g: fn([lo + u * g + k for k in range(g)], c), carry)
        lo = lo + n * g
    return carry


def _bias_tile(bias_ref, h, qb0, n_qb, j):
    rows = []
    for c in range(SUB):
        cols = [bias_ref[h * N_BT + jnp.clip(qb0 + r - (j * SUB + c), 0, N_SAT)] for r in range(n_qb)]
        rows.append(cols[0] if n_qb == 1 else jnp.concatenate(cols, axis=1))
    return jnp.concatenate(rows, axis=0)


def _ones_rows(n):
    return jnp.ones((ONES_ROWS, n), BF16)


def _proj_even_kernel(x_ref, g_ref, w_ref, kvn_ref, wuk_ref,
                      qit_ref, wit_ref, qlt_ref, ki_ref, c_ref, ctx_ref, qbt_ref, kb_ref, vtx_ref):
    tm = x_ref.shape[0]
    h = _rms(x_ref[...], g_ref[0:1, :]).astype(BF16)
    z = _dot(h, w_ref[...])
    qa = z[:, 0:512].astype(BF16)
    ckv = _rms(z[:, 512:640], kvn_ref[...])
    c_ref[...] = ckv.astype(BF16)
    kb_ref[...] = z[:, 1664:2176].astype(BF16)
    ki_ref[...] = z[:, 2688:2688 + DH_IDX].astype(BF16)
    ctx_ref[0, D_LAT:, :] = _ones_rows(tm)
    for hb in range(H_B):
        vtx_ref[0, hb, 2 * DH_B:, :] = _ones_rows(tm)
    per_q, per_diff = TQ_DSA // LANES, TQ_DIFF // LANES
    for hh in range(H_A):
        ql = _dot(qa[:, hh * DH_A:(hh + 1) * DH_A], wuk_ref[hh]) * (DH_A ** -0.5 * LOG2E)
        for sb in range(tm // LANES):
            off = hh * TQ_DSA + (sb % per_q) * LANES
            qlt_ref[sb // per_q, :, off:off + LANES] = ql[sb * LANES:(sb + 1) * LANES, :].T.astype(BF16)
    for sb in range(tm // LANES):
        rows = slice(sb * LANES, (sb + 1) * LANES)
        blk, sub = sb // per_q, (sb % per_q) * LANES
        for pr in range(H_IDX // 2):
            t = z[rows, 640 + pr * LANES:640 + (pr + 1) * LANES].T
            for half in range(2):
                off = (2 * pr + half) * TQ_DSA + sub
                qit_ref[blk, :, off:off + LANES] = t[half * DH_IDX:(half + 1) * DH_IDX, :].astype(BF16)
        wit_ref[blk, :, sub:sub + LANES] = z[rows, 2816:2816 + LANES].T[0:H_IDX, :]
        ctx_ref[0, 0:D_LAT, rows] = ckv[rows, :].T.astype(BF16)
        for hb in range(H_B):
            qb = z[rows, 1152 + hb * LANES:1152 + (hb + 1) * LANES] * (DH_B ** -0.5 * LOG2E)
            lanes = slice((sb % per_diff) * LANES, (sb % per_diff + 1) * LANES)
            qbt_ref[sb // per_diff, hb, :, lanes] = qb.T.astype(BF16)
            vtx_ref[0, hb, 0:2 * DH_B, rows] = z[rows, 2176 + hb * LANES:2176 + (hb + 1) * LANES].T.astype(BF16)


def _proj_even(x2, g, w_in, kvn, w_uk):
    t = x2.shape[0]
    tm = TK
    assert t % tm == 0 and tm % TQ_DSA == 0 and tm % TQ_DIFF == 0
    pad = jnp.zeros((D_MODEL, LANES - DH_IDX), w_in.dtype)
    cols = [w_in[:, 0:512], w_in[:, 512:640], w_in[:, 640:1152], w_in[:, 1224:1736], w_in[:, 1736:2248],
            w_in[:, 2248:2760], w_in[:, 1152:1216], pad, w_in[:, 1216:1224],
            jnp.zeros((D_MODEL, LANES - H_IDX), w_in.dtype)]
    w = jnp.concatenate(cols, axis=1).astype(BF16)
    n = w.shape[1]
    dv = 2 * DH_B
    row = lambda i: (i, 0)
    full2 = lambda i: (0, 0)
    lead3 = lambda i: (i, 0, 0)
    lead4 = lambda i: (i, 0, 0, 0)
    outs = [
        ((t // TQ_DSA, DH_IDX, H_IDX * TQ_DSA), BF16, (tm // TQ_DSA, DH_IDX, H_IDX * TQ_DSA), lead3),
        ((t // TQ_DSA, H_IDX, TQ_DSA), F32, (tm // TQ_DSA, H_IDX, TQ_DSA), lead3),
        ((t // TQ_DSA, D_LAT, H_A * TQ_DSA), BF16, (tm // TQ_DSA, D_LAT, H_A * TQ_DSA), lead3),
        ((t, DH_IDX), BF16, (tm, DH_IDX), row),
        ((t, D_LAT), BF16, (tm, D_LAT), row),
        ((t // TK, D_LAT + ONES_ROWS, TK), BF16, (1, D_LAT + ONES_ROWS, TK), lead3),
        ((t // TQ_DIFF, H_B, dv, TQ_DIFF), BF16, (tm // TQ_DIFF, H_B, dv, TQ_DIFF), lead4),
        ((t, H_B * dv), BF16, (tm, H_B * dv), row),
        ((t // TK, H_B, dv + ONES_ROWS, TK), BF16, (1, H_B, dv + ONES_ROWS, TK), lead4),
    ]
    return pl.pallas_call(
        _proj_even_kernel,
        out_shape=[jax.ShapeDtypeStruct(shape, dt) for shape, dt, _, _ in outs],
        grid=(t // tm,),
        in_specs=[
            pl.BlockSpec((tm, D_MODEL), row),
            pl.BlockSpec((6, D_MODEL), full2),
            pl.BlockSpec((D_MODEL, n), full2),
            pl.BlockSpec((1, D_LAT), full2),
            pl.BlockSpec((H_A, DH_A, D_LAT), lambda i: (0, 0, 0)),
        ],
        out_specs=[pl.BlockSpec(block, imap) for _, _, block, imap in outs],
        compiler_params=pltpu.CompilerParams(vmem_limit_bytes=VMEM_LIMIT),
        name="proj_even",
    )(x2, g, w, kvn.reshape(1, D_LAT), w_uk.astype(BF16))


def _proj_odd_kernel(x_ref, g_ref, w_ref, qt_ref, k_ref, vtx_ref):
    tm = x_ref.shape[0]
    h = _rms(x_ref[...], g_ref[0:1, :]).astype(BF16)
    z = _dot(h, w_ref[...])
    for kv in range(H_C_KV):
        k_ref[kv] = z[:, 1024 + kv * DH_C:1024 + (kv + 1) * DH_C].astype(BF16)
    for sb in range(tm // TQ):
        rows = slice(sb * TQ, (sb + 1) * TQ)
        for pr in range(H_C // 2):
            t = (z[rows, pr * LANES:(pr + 1) * LANES] * (DH_C ** -0.5 * LOG2E)).T
            for half in range(2):
                hd = 2 * pr + half
                kv, gq = hd // G_C, hd % G_C
                qt_ref[sb, kv, :, gq * TQ:(gq + 1) * TQ] = t[half * DH_C:(half + 1) * DH_C, :].astype(BF16)
        vt = z[rows, 1152:1280].T
        for kv in range(H_C_KV):
            vtx_ref[sb, kv, 0:DH_C, :] = vt[kv * DH_C:(kv + 1) * DH_C, :].astype(BF16)
            vtx_ref[sb, kv, DH_C:, :] = _ones_rows(TQ)


def _proj_odd(x2, g, w_in):
    t = x2.shape[0]
    tm = min(TM_PROJ, t)
    assert TQ == LANES
    row = lambda i: (i, 0)
    full2 = lambda i: (0, 0)
    lead4 = lambda i: (i, 0, 0, 0)
    dvx = DH_C + ONES_ROWS
    return pl.pallas_call(
        _proj_odd_kernel,
        out_shape=[jax.ShapeDtypeStruct((t // TQ, H_C_KV, DH_C, G_C * TQ), BF16),
                   jax.ShapeDtypeStruct((H_C_KV, t, DH_C), BF16),
                   jax.ShapeDtypeStruct((t // TQ, H_C_KV, dvx, TQ), BF16)],
        grid=(t // tm,),
        in_specs=[pl.BlockSpec((tm, D_MODEL), row),
                  pl.BlockSpec((6, D_MODEL), full2),
                  pl.BlockSpec((D_MODEL, 1280), full2)],
        out_specs=[pl.BlockSpec((tm // TQ, H_C_KV, DH_C, G_C * TQ), lead4),
                   pl.BlockSpec((H_C_KV, tm, DH_C), lambda i: (0, i, 0)),
                   pl.BlockSpec((tm // TQ, H_C_KV, dvx, TQ), lead4)],
        compiler_params=pltpu.CompilerParams(vmem_limit_bytes=VMEM_LIMIT),
        name="proj_odd",
    )(x2, g, w_in.astype(BF16))


def _dsa_kernel(qit_ref, wit_ref, qlt_ref, k_ref, c_ref, ctx_ref, bias_ref, wuvt_ref, o_ref,
                sc_ref, acc_ref, *p_refs, topk):
    TQ = TQ_DSA
    i = pl.program_id(1)
    t0 = i * TQ
    nt = (t0 + TQ + TK - 1) // TK
    n_qb = TQ // LANES
    n_far = jnp.maximum(i * n_qb - N_SAT + 1, 0) // SUB
    kf = float(topk)

    qpos = lax.broadcasted_iota(jnp.int32, (1, TQ), 1) + t0
    limit = (qpos // CHUNK + 1) * CHUNK
    small = limit <= topk
    big = jnp.logical_not(small)
    w = wit_ref[...]
    qit = qit_ref[...]

    def score_tiles(js, carry):
        rmax, rmin = carry
        zs = [_dot(k_ref[j], qit) for j in js]
        for j, z in zip(js, zs):
            sc = jnp.maximum(z[:, 0:TQ], 0.0) * w[0:1, :]
            for h in range(1, H_IDX):
                sc = sc + jnp.maximum(z[:, h * TQ:(h + 1) * TQ], 0.0) * w[h:h + 1, :]
            kpos = lax.broadcasted_iota(jnp.int32, (TK, TQ), 0) + j * TK
            adm = kpos < limit
            sc_ref[j] = jnp.where(adm, sc, -jnp.inf)
            rmax = jnp.maximum(rmax, _fold_rows(jnp.where(adm, sc, -jnp.inf), jnp.maximum))
            rmin = jnp.minimum(rmin, _fold_rows(jnp.where(adm, sc, jnp.inf), jnp.minimum))
        return rmax, rmin

    rmax, rmin = _grouped_loop(
        0, nt, (jnp.full((FOLD_ROWS, TQ), -jnp.inf, F32), jnp.full((FOLD_ROWS, TQ), jnp.inf, F32)),
        score_tiles, GROUPS_DSA)
    hi0 = jnp.max(rmax, axis=0, keepdims=True)
    lo0 = jnp.min(rmin, axis=0, keepdims=True)

    def col_reduce(tile_fn, op, init, final):
        def fn(js, acc):
            for j in js:
                acc = op(acc, _fold_rows(tile_fn(j), op))
            return acc
        acc = _grouped_loop(0, nt, jnp.full((FOLD_ROWS, TQ), init, F32), fn, GROUPS_PASS)
        return final(acc, axis=0, keepdims=True)

    def count_ge(thr):
        return col_reduce(lambda j: jnp.where(sc_ref[j] >= thr, 1.0, 0.0), jnp.add, 0.0, jnp.sum)

    def max_le(thr):
        def tile(j):
            t = sc_ref[j]
            return jnp.where(t <= thr, t, -jnp.inf)
        return col_reduce(tile, jnp.maximum, -jnp.inf, jnp.max)

    def max_lt(thr):
        def tile(j):
            t = sc_ref[j]
            return jnp.where(t < thr, t, -jnp.inf)
        return col_reduce(tile, jnp.maximum, -jnp.inf, jnp.max)

    def bisect_body(_, carry):
        lo, hi = carry
        mid = lo * 0.5 + hi * 0.5
        ge = count_ge(mid) >= kf
        return jnp.where(ge, mid, lo), jnp.where(ge, hi, mid)

    lo, hi = lax.fori_loop(0, N_BISECT, bisect_body, (lo0, hi0))

    v0 = max_le(hi)
    c0 = count_ge(v0)

    def walk_cond(carry):
        _, cnt, it = carry
        pending = jnp.where((cnt < kf) & big, 1.0, 0.0)
        return (jnp.max(pending) > 0.0) & (it < topk + 2)

    def walk_body(carry):
        v, cnt, it = carry
        v = jnp.where((cnt < kf) & big, max_lt(v), v)
        return v, count_ge(v), it + 1

    tau, cnt, _ = lax.while_loop(walk_cond, walk_body, (v0, c0, jnp.int32(0)))
    tau = jnp.where(small, -jnp.finfo(F32).max, tau)

    tied = (cnt > kf) & big

    @pl.when(jnp.max(jnp.where(tied, 1.0, 0.0)) > 0.0)
    def _():
        excess = jnp.where(tied, cnt - kf, 0.0)
        kr = lax.broadcasted_iota(jnp.int32, (TK, TK), 0)
        kc = lax.broadcasted_iota(jnp.int32, (TK, TK), 1)
        upper = jnp.where(kc >= kr, 1.0, 0.0).astype(BF16)

        def drop_body(u, later):
            j = nt - 1 - u
            t = sc_ref[j]
            e = (t == tau) & tied
            behind = _dot(upper, jnp.where(e, 1.0, 0.0).astype(BF16)) + later
            sc_ref[j] = jnp.where(e & (behind <= excess), -jnp.inf, t)
            return behind[0:1, :]

        lax.fori_loop(0, nt, drop_body, jnp.zeros((1, TQ), F32))

    acc_ref[...] = jnp.zeros(acc_ref.shape, F32)
    qlt = qlt_ref[...]

    def attn_tiles(js, m, with_bias):
        s_alls = [_dot(c_ref[j], qlt).astype(BF16) for j in js]
        for g, (j, s_all) in enumerate(zip(js, s_alls)):
            p_ref = p_refs[g]
            pen = jnp.where(sc_ref[j] >= tau, 0.0, NEG_INF).astype(BF16)
            m_new, alpha = [], []
            for h in range(H_A):
                cols = slice(h * TQ, (h + 1) * TQ)
                s = s_all[:, cols] + pen
                if with_bias:
                    s = s + _bias_tile(bias_ref, h, i * n_qb, n_qb, j).astype(BF16)
                m_old = m[:, cols]
                m_h = jnp.maximum(m_old, jnp.max(_fold_rows(s, jnp.maximum), axis=0, keepdims=True).astype(F32))
                p_ref[:, cols] = jnp.exp2(s - m_h.astype(BF16))
                alpha.append(jnp.exp2(m_old - m_h))
                m_new.append(m_h)
            acc_ref[...] = acc_ref[...] * jnp.concatenate(alpha, axis=1) + _dot(ctx_ref[j], p_ref[...])
            m = jnp.concatenate(m_new, axis=1)
        return m

    m = jnp.full((1, H_A * TQ), NEG_INF, F32)
    m = _grouped_loop(0, n_far, m, lambda js, mm: attn_tiles(js, mm, False), GROUPS_DSA)
    _grouped_loop(n_far, nt, m, lambda js, mm: attn_tiles(js, mm, True), GROUPS_NEAR)

    outs = []
    for h in range(H_A):
        cols = slice(h * TQ, (h + 1) * TQ)
        o_t = (acc_ref[0:D_LAT, cols] / acc_ref[D_LAT:D_LAT + 1, cols]).astype(BF16)
        outs.append(_dot(wuvt_ref[h], o_t))
    for pr in range(H_A // 2):
        pair = jnp.concatenate([outs[2 * pr], outs[2 * pr + 1]], axis=0)
        o_ref[:, pr * LANES:(pr + 1) * LANES] = pair.T.astype(BF16)


def _dsa(qit, wit, qlt, ki, ckv, ctx, bias_a, w_uv, b, s):
    TQ = TQ_DSA
    nq, nt = s // TQ, s // TK
    topk = min(TOPK_MAX, s // 4)
    k4 = ki.reshape(b, nt, TK, DH_IDX)
    c4 = ckv.reshape(b, nt, TK, D_LAT)
    ctx = ctx.reshape(b, nt, D_LAT + ONES_ROWS, TK)
    wuvt = jnp.swapaxes(w_uv, 1, 2).astype(BF16)
    qblk = lambda bb, i: (bb * nq + i, 0, 0)
    kv4 = lambda bb, i: (bb, 0, 0, 0)
    once = dict(pipeline_mode=pl.Buffered(1))
    return pl.pallas_call(
        functools.partial(_dsa_kernel, topk=topk),
        out_shape=jax.ShapeDtypeStruct((b * s, H_A * DH_A), BF16),
        grid=(b, nq),
        in_specs=[
            pl.BlockSpec((None, DH_IDX, H_IDX * TQ), qblk),
            pl.BlockSpec((None, H_IDX, TQ), qblk),
            pl.BlockSpec((None, D_LAT, H_A * TQ), qblk),
            pl.BlockSpec((None, nt, TK, DH_IDX), kv4, **once),
            pl.BlockSpec((None, nt, TK, D_LAT), kv4, **once),
            pl.BlockSpec((None, nt, D_LAT + ONES_ROWS, TK), kv4, **once),
            pl.BlockSpec((H_A * N_BT, LANES, LANES), lambda bb, i: (0, 0, 0), **once),
            pl.BlockSpec((H_A, DH_A, D_LAT), lambda bb, i: (0, 0, 0), **once),
        ],
        out_specs=pl.BlockSpec((TQ, H_A * DH_A), lambda bb, i: (bb * nq + i, 0)),
        scratch_shapes=[
            pltpu.VMEM((nt, TK, TQ), F32),
            pltpu.VMEM((D_LAT + ONES_ROWS, H_A * TQ), F32),
        ] + [pltpu.VMEM((TK, H_A * TQ), BF16)] * max(GROUPS_DSA),
        compiler_params=pltpu.CompilerParams(vmem_limit_bytes=VMEM_LIMIT),
        name="dsa_attention",
    )(qit, wit, qlt, k4, c4, ctx, bias_a, wuvt)


def _diff_kernel(qt_ref, k_ref, vtx_ref, bias_ref, lam_ref, sub_ref, o_ref, acc_ref, *p_refs, lam_init):
    h = pl.program_id(1)
    i = pl.program_id(2)
    tq = TQ_DIFF
    n_qb = tq // LANES
    t0 = i * tq
    nt = (t0 + tq + TK - 1) // TK
    n_far = jnp.maximum(i * n_qb - N_SAT + 1, 0) // SUB
    qt = qt_ref[...]
    top = lax.broadcasted_iota(jnp.int32, qt.shape, 0) < DH_B
    q2 = jnp.concatenate([jnp.where(top, qt, jnp.zeros_like(qt)), jnp.where(top, jnp.zeros_like(qt), qt)], axis=1)
    qpos = lax.broadcasted_iota(jnp.int32, (1, tq), 1) + t0
    limit = (qpos // CHUNK + 1) * CHUNK
    acc_ref[...] = jnp.zeros(acc_ref.shape, F32)

    def tiles(js, m, near):
        s2s = [_dot(k_ref[j], q2).astype(BF16) for j in js]
        for g, (j, s2) in enumerate(zip(js, s2s)):
            p_ref = p_refs[g]
            if near:
                bias = _bias_tile(bias_ref, h, i * n_qb, n_qb, j).astype(BF16)
                adm = (lax.broadcasted_iota(jnp.int32, (TK, tq), 0) + j * TK) < limit
            m_new, alpha = [], []
            for mp in range(2):
                cols = slice(mp * tq, (mp + 1) * tq)
                s = s2[:, cols]
                if near:
                    s = jnp.where(adm, s + bias, jnp.asarray(NEG_INF, BF16))
                m_old = m[:, cols]
                m_h = jnp.maximum(m_old, jnp.max(_fold_rows(s, jnp.maximum), axis=0, keepdims=True).astype(F32))
                p_ref[:, cols] = jnp.exp2(s - m_h.astype(BF16))
                alpha.append(jnp.exp2(m_old - m_h))
                m_new.append(m_h)
            acc_ref[...] = acc_ref[...] * jnp.concatenate(alpha, axis=1) + _dot(vtx_ref[j], p_ref[...])
            m = jnp.concatenate(m_new, axis=1)
        return m

    m = jnp.full((1, 2 * tq), NEG_INF, F32)
    m = _grouped_loop(0, n_far, m, lambda js, mm: tiles(js, mm, False), GROUPS)
    _grouped_loop(n_far, nt, m, lambda js, mm: tiles(js, mm, True), GROUPS_NEAR)

    lf = lam_ref[...]
    lam = (jnp.exp(jnp.sum(lf[0:1, :] * lf[1:2, :], axis=1, keepdims=True))
           - jnp.exp(jnp.sum(lf[2:3, :] * lf[3:4, :], axis=1, keepdims=True)) + lam_init)
    dv = 2 * DH_B
    o = (acc_ref[0:dv, 0:tq] / acc_ref[dv:dv + 1, 0:tq]
         - lam * (acc_ref[0:dv, tq:2 * tq] / acc_ref[dv:dv + 1, tq:2 * tq]))
    o = o * lax.rsqrt(jnp.mean(o * o, axis=0, keepdims=True) + EPS) * sub_ref[...] * (1.0 - lam_init)
    o_ref[...] = o.T.astype(BF16)


def _diff(qbt, kb, vtx, bias_b, b_lambda, b_subln, lam_init, b, s):
    tq = TQ_DIFF
    nq, nt = s // tq, s // TK
    dv = 2 * DH_B
    k4 = kb.reshape(b, nt, TK, H_B * dv)
    vtx = vtx.reshape(b, nt, H_B, dv + ONES_ROWS, TK)
    return pl.pallas_call(
        functools.partial(_diff_kernel, lam_init=lam_init),
        out_shape=jax.ShapeDtypeStruct((b * s, H_B * dv), BF16),
        grid=(b, H_B, nq),
        in_specs=[
            pl.BlockSpec((None, None, dv, tq), lambda bb, h, i: (bb * nq + i, h, 0, 0)),
            pl.BlockSpec((None, nt, TK, dv), lambda bb, h, i: (bb, 0, 0, h)),
            pl.BlockSpec((None, nt, None, dv + ONES_ROWS, TK), lambda bb, h, i: (bb, 0, h, 0, 0)),
            pl.BlockSpec((H_B * N_BT, LANES, LANES), lambda bb, h, i: (0, 0, 0)),
            pl.BlockSpec((4, DH_B), lambda bb, h, i: (0, 0)),
            pl.BlockSpec((dv, 1), lambda bb, h, i: (0, 0)),
        ],
        out_specs=pl.BlockSpec((tq, dv), lambda bb, h, i: (bb * nq + i, h)),
        scratch_shapes=[
            pltpu.VMEM((dv + ONES_ROWS, 2 * tq), F32),
        ] + [pltpu.VMEM((TK, 2 * tq), BF16)] * max(GROUPS),
        compiler_params=pltpu.CompilerParams(vmem_limit_bytes=VMEM_LIMIT),
        name="diff_attention",
    )(qbt, k4, vtx, bias_b, b_lambda.astype(F32), b_subln.reshape(dv, 1).astype(F32))


def _swa_kernel(qt_ref, ka_ref, kb_ref, vta_ref, vtb_ref, bias_ref, sink_ref, o_ref):
    i = pl.program_id(1)
    pad_pen = jnp.where(i == 0, NEG_INF, 0.0)
    for kv in range(H_C_KV):
        qt = qt_ref[kv]
        s = jnp.concatenate([_dot(ka_ref[kv], qt) + pad_pen, _dot(kb_ref[kv], qt)], axis=0) + bias_ref[kv]
        sink = sink_ref[kv]
        m = jnp.maximum(jnp.max(s, axis=0, keepdims=True), sink)
        p = jnp.exp2(s - m).astype(BF16)
        acc = _dot(vta_ref[kv], p[0:TQ, :]) + _dot(vtb_ref[kv], p[TQ:2 * TQ, :])
        o = acc[0:DH_C, :] / (acc[DH_C:DH_C + 1, :] + jnp.exp2(sink - m))
        for pr in range(G_C // 2):
            pair = jnp.concatenate([o[:, (2 * pr) * TQ:(2 * pr + 1) * TQ], o[:, (2 * pr + 1) * TQ:(2 * pr + 2) * TQ]],
                                   axis=0)
            col = (kv * G_C + 2 * pr) * DH_C
            o_ref[:, col:col + LANES] = pair.T.astype(BF16)


def _swa(qt, k, vtx, table_c, sinks, b, s):
    assert WINDOW == TQ and TQ % CHUNK == 0
    nq = s // TQ
    length = 3 * TQ + 1
    m = np.arange(length)
    diff = np.where(m < 2 * TQ + 1, m, m - length)
    u = jnp.take(table_c.astype(F32), jnp.asarray(_t5_bucket_np(diff - TQ)), axis=0) * LOG2E
    bias = _toeplitz(jnp.moveaxis(u, -1, 0), TQ, 2 * TQ)
    rblk = np.arange(TQ)[:, None] // CHUNK + WINDOW // CHUNK
    cblk = np.arange(2 * TQ)[None, :] // CHUNK
    valid = (cblk <= rblk) & (cblk >= rblk - WINDOW // CHUNK)
    bias = jnp.where(jnp.asarray(valid)[None], bias, NEG_INF)
    bias_t = jnp.transpose(bias.reshape(H_C_KV, G_C, TQ, 2 * TQ), (0, 3, 1, 2)).reshape(H_C_KV, 2 * TQ, G_C * TQ)
    sink_row = jnp.repeat(sinks.astype(F32).reshape(H_C_KV, 1, G_C) * LOG2E, TQ, axis=2)

    dvx = DH_C + ONES_ROWS
    prev = lambda bb, i: bb * nq + jnp.maximum(i - 1, 0)
    return pl.pallas_call(
        _swa_kernel,
        out_shape=jax.ShapeDtypeStruct((b * s, H_C * DH_C), BF16),
        grid=(b, nq),
        in_specs=[
            pl.BlockSpec((None, H_C_KV, DH_C, G_C * TQ), lambda bb, i: (bb * nq + i, 0, 0, 0)),
            pl.BlockSpec((H_C_KV, TQ, DH_C), lambda bb, i: (0, prev(bb, i), 0)),
            pl.BlockSpec((H_C_KV, TQ, DH_C), lambda bb, i: (0, bb * nq + i, 0)),
            pl.BlockSpec((None, H_C_KV, dvx, TQ), lambda bb, i: (prev(bb, i), 0, 0, 0)),
            pl.BlockSpec((None, H_C_KV, dvx, TQ), lambda bb, i: (bb * nq + i, 0, 0, 0)),
            pl.BlockSpec((H_C_KV, 2 * TQ, G_C * TQ), lambda bb, i: (0, 0, 0)),
            pl.BlockSpec((H_C_KV, 1, G_C * TQ), lambda bb, i: (0, 0, 0)),
        ],
        out_specs=pl.BlockSpec((TQ, H_C * DH_C), lambda bb, i: (bb * nq + i, 0)),
        compiler_params=pltpu.CompilerParams(vmem_limit_bytes=VMEM_LIMIT),
        name="swa_attention",
    )(qt, k, k, vtx, vtx, bias_t, sink_row)


def _memkv_kernel(mem_ref, g_ref, w_ref, k_ref, v_ref):
    h = _rms(mem_ref[...], g_ref[...]).astype(BF16)
    kv = _dot(h, w_ref[...])
    k_ref[...] = kv[:, 0:H_X * DH_X].astype(BF16)
    v_ref[...] = kv[:, H_X * DH_X:].astype(BF16)


def _memkv(mem, g, wkv):
    b, m, _ = mem.shape
    n = H_X * DH_X
    return pl.pallas_call(
        _memkv_kernel,
        out_shape=[jax.ShapeDtypeStruct((b, m, n), BF16), jax.ShapeDtypeStruct((b, m, n), BF16)],
        grid=(b,),
        in_specs=[pl.BlockSpec((None, m, D_MODEL), lambda i: (i, 0, 0)),
                  pl.BlockSpec((1, D_MODEL), lambda i: (0, 0)),
                  pl.BlockSpec((D_MODEL, 2 * n), lambda i: (0, 0))],
        out_specs=[pl.BlockSpec((None, m, n), lambda i: (i, 0, 0)), pl.BlockSpec((None, m, n), lambda i: (i, 0, 0))],
        compiler_params=pltpu.CompilerParams(vmem_limit_bytes=VMEM_LIMIT),
        name="mem_kv",
    )(mem, g.reshape(1, D_MODEL), wkv.astype(BF16))


def _lane_half_mask(rows, half):
    lane = lax.broadcasted_iota(jnp.int32, (rows, LANES), 1)
    return (lane < 64) if half == 0 else (lane >= 64)


def _tail_kernel(*refs, n_a):
    x_ref, g_ref = refs[0], refs[1]
    a_refs = refs[2:2 + n_a]
    wout_refs = refs[2 + n_a:2 + 2 * n_a]
    wq_ref, kt_ref, v_ref, wo_ref, w1_ref, w2_ref, o_ref = refs[2 + 2 * n_a:]
    tc = x_ref.shape[0] // TAIL_CHAINS
    chains = [slice(c * tc, (c + 1) * tc) for c in range(TAIL_CHAINS)]
    g = g_ref[...]

    ys = []
    for r in chains:
        y = _dot(a_refs[0][r, :], wout_refs[0][...])
        for a_ref, w_ref in zip(a_refs[1:], wout_refs[1:]):
            y = y + _dot(a_ref[r, :], w_ref[...])
        ys.append(y)
    xs = [x_ref[r, :] + _rms(y, g[1:2, :]) for r, y in zip(chains, ys)]

    hqs = [_rms(x, g[2:3, :]).astype(BF16) for x in xs]
    qs = [_dot(hq, wq_ref[...]) for hq in hqs]
    qs = [(q * (DH_X ** -0.5)).astype(BF16) for q in qs]
    pairs = [[] for _ in chains]
    for pr in range(H_X // 2):
        vp = v_ref[:, pr * LANES:(pr + 1) * LANES]
        outs = [jnp.zeros((tc, LANES), F32) for _ in chains]
        for half in range(2):
            qzs = [jnp.where(_lane_half_mask(tc, half), q[:, pr * LANES:(pr + 1) * LANES], jnp.zeros((tc, LANES), BF16))
                   for q in qs]
            ss = [_dot(qz, kt_ref[pr * LANES:(pr + 1) * LANES, :]) for qz in qzs]
            vz = jnp.where(_lane_half_mask(vp.shape[0], half), vp, jnp.zeros_like(vp))
            ps = []
            for sc in ss:
                e = jnp.exp(sc - jnp.max(sc, axis=1, keepdims=True))
                ps.append((e / jnp.sum(e, axis=1, keepdims=True)).astype(BF16))
            outs = [out + _dot(p, vz) for out, p in zip(outs, ps)]
        for c, out in enumerate(outs):
            pairs[c].append(out.astype(BF16))
    ys = [_dot(jnp.concatenate(pc, axis=1), wo_ref[...]) for pc in pairs]
    xs = [x + _rms(y, g[3:4, :]) for x, y in zip(xs, ys)]

    hms = [_rms(x, g[4:5, :]).astype(BF16) for x in xs]
    n_chunks = D_FF // FF_CHUNK
    ys = [jnp.zeros((tc, D_MODEL), F32) for _ in chains]
    a_cur = [_dot(hm, w1_ref[:, 0:FF_CHUNK]) for hm in hms]
    for k in range(n_chunks):
        if k + 1 < n_chunks:
            a_next = [_dot(hm, w1_ref[:, (k + 1) * FF_CHUNK:(k + 2) * FF_CHUNK]) for hm in hms]
        acts = [jnp.square(jnp.maximum(a, 0.0)).astype(BF16) for a in a_cur]
        ys = [y + _dot(act, w2_ref[k * FF_CHUNK:(k + 1) * FF_CHUNK, :]) for y, act in zip(ys, acts)]
        if k + 1 < n_chunks:
            a_cur = a_next
    for r, x, y in zip(chains, xs, ys):
        o_ref[r, :] = x + _rms(y, g[5:6, :])


def _tail(x2, g, a_list, wout_list, wq, kt, v, wo, w1, w2, b, s):
    t = x2.shape[0]
    tm = min(TM_TAIL, s)
    per_b = s // tm
    n_a = len(a_list)
    row = lambda i: (i, 0)
    full2 = lambda i: (0, 0)
    once = dict(pipeline_mode=pl.Buffered(1))
    in_specs = [pl.BlockSpec((tm, D_MODEL), row), pl.BlockSpec((6, D_MODEL), full2)]
    in_specs += [pl.BlockSpec((tm, a.shape[1]), row) for a in a_list]
    in_specs += [pl.BlockSpec(w.shape, full2, **once) for w in wout_list]
    in_specs += [
        pl.BlockSpec(wq.shape, full2, **once),
        pl.BlockSpec((None,) + kt.shape[1:], lambda i: (i // per_b, 0, 0)),
        pl.BlockSpec((None,) + v.shape[1:], lambda i: (i // per_b, 0, 0)),
        pl.BlockSpec(wo.shape, full2, **once),
        pl.BlockSpec(w1.shape, full2, **once),
        pl.BlockSpec(w2.shape, full2, **once),
    ]
    return pl.pallas_call(
        functools.partial(_tail_kernel, n_a=n_a),
        out_shape=jax.ShapeDtypeStruct((t, D_MODEL), F32),
        grid=(t // tm,),
        in_specs=in_specs,
        out_specs=pl.BlockSpec((tm, D_MODEL), row),
        compiler_params=pltpu.CompilerParams(vmem_limit_bytes=VMEM_LIMIT),
        name="tail",
    )(x2, g, *a_list, *wout_list, wq, kt, v, wo, w1, w2)


def kernel(x, mem, rel_bias_table, norm_g, ev_w_in, ev_a_kv_norm, ev_a_w_uk, ev_a_w_uv, ev_b_lambda, ev_b_subln, ev_w_out, od_w_in, od_sinks, od_w_out, xa_wq, xa_wkv, xa_wo, xa_mem_norm, mlp_w1, mlp_w2):
    b, s, d = x.shape
    depth = norm_g.shape[0]
    assert d == D_MODEL and s % TK == 0 and TK % TQ_DIFF == 0 and TK % TQ_DSA == 0 and TQ_DSA % LANES == 0
    x2 = x.reshape(b * s, d)
    bias_a = _causal_bias_tiles(rel_bias_table[:, :H_A])
    bias_b = _causal_bias_tiles(rel_bias_table[:, H_A:H_A + H_B])
    table_c = rel_bias_table[:, H_A + H_B:]
    for l in range(depth):
        g = norm_g[l].astype(F32)
        if l % 2 == 0:
            e = l // 2
            lam_init = 0.8 - 0.6 * math.exp(-0.3 * l)
            qit, wit, qlt, ki, ckv, ctx, qbt, kb, vtx = _proj_even(x2, g, ev_w_in[e], ev_a_kv_norm[e], ev_a_w_uk[e])
            oa = _dsa(qit, wit, qlt, ki, ckv, ctx, bias_a, ev_a_w_uv[e], b, s)
            ob = _diff(qbt, kb, vtx, bias_b, ev_b_lambda[e], ev_b_subln[e], lam_init, b, s)
            n_a = H_A * DH_A
            a_list = [oa, ob]
            wout_list = [ev_w_out[e][:n_a].astype(BF16), ev_w_out[e][n_a:].astype(BF16)]
        else:
            o = l // 2
            qt, k, vtx = _proj_odd(x2, g, od_w_in[o])
            a_list = [_swa(qt, k, vtx, table_c, od_sinks[o], b, s)]
            wout_list = [od_w_out[o].astype(BF16)]
        mk, mv = _memkv(mem, xa_mem_norm[l], xa_wkv[l])
        x2 = _tail(x2, g, a_list, wout_list, xa_wq[l].astype(BF16), jnp.swapaxes(mk, 1, 2), mv,
                   xa_wo[l].astype(BF16), mlp_w1[l].astype(BF16), mlp_w2[l].astype(BF16), b, s)
    return x2.reshape(b, s, d)
```

```python
import functools
import math

import numpy as np
import jax
import jax.numpy as jnp
from jax import lax
from jax.experimental import pallas as pl
from jax.experimental.pallas import tpu as pltpu

F32 = jnp.float32
BF16 = jnp.bfloat16

D_MODEL = 1024
CHUNK = 64
EPS = 1e-6
NEG_INF = -1e30
LOG2E = math.log2(math.e)

H_A, DH_A, D_LAT = 8, 64, 128
H_IDX, DH_IDX = 8, 64
TOPK_MAX = 256
H_B, DH_B = 4, 64
H_C, H_C_KV, DH_C = 16, 2, 64
G_C = H_C // H_C_KV
WINDOW = 128
H_X, DH_X = 4, 64
D_FF = 4 * D_MODEL
N_BUCKETS = 32
MAX_DIST = 1024

LANES = 128
FOLD_ROWS = 64
TQ = 128
TQ_DSA = 128
TQ_DIFF = 512
TK = 512
SUB = TK // LANES
ONES_ROWS = 16
TM_PROJ = 512
TM_TAIL = 512
TAIL_CHAINS = 2
FF_CHUNK = 1024
VMEM_LIMIT = 56 * 1024 * 1024
N_BISECT = 14
GROUPS = (4, 2, 1)
GROUPS_NEAR = (3, 2, 1)
GROUPS_DSA = (8, 4, 2, 1)
GROUPS_PASS = (4, 1)


def _rms(x, g):
    return x * lax.rsqrt(jnp.mean(x * x, axis=-1, keepdims=True) + EPS) * g


def _dot(a, b):
    return jnp.dot(a, b, preferred_element_type=F32)


def _t5_bucket_np(rel):
    nb = N_BUCKETS // 2
    max_exact = nb // 2
    n = np.abs(rel)
    nf = np.maximum(n, 1).astype(np.float32)
    large = max_exact + (np.log(nf / max_exact) / math.log(MAX_DIST / max_exact) * (nb - max_exact)).astype(np.int32)
    large = np.minimum(large, nb - 1)
    return np.where(rel > 0, nb, 0) + np.where(n < max_exact, n, large)


def _sat_blocks():
    d = 1
    while _t5_bucket_np(np.array([-(d * LANES - (LANES - 1))]))[0] != N_BUCKETS // 2 - 1:
        d += 1
    return d


N_SAT = _sat_blocks()
N_BT = N_SAT + 1


def _toeplitz(u, rows, cols):
    length = u.shape[-1]
    flat = jnp.tile(u, (1,) * (u.ndim - 1) + (rows,))[..., :rows * (length - 1)]
    return flat.reshape(u.shape[:-1] + (rows, length - 1))[..., :cols]


def _causal_bias_tiles(table_cols):
    length = 2 * LANES + 1
    m = np.arange(length)
    diff = np.where(m < LANES + 1, m, m - length)
    rel = -diff[None, :] - (np.arange(N_BT) * LANES)[:, None]
    tab = table_cols.astype(F32)
    u = jnp.take(tab, jnp.asarray(_t5_bucket_np(rel)), axis=0)
    u = (u - tab[N_BUCKETS // 2 - 1][None, None, :]) * LOG2E
    tiles = _toeplitz(jnp.moveaxis(u, -1, 0), LANES, LANES)
    return tiles.reshape(table_cols.shape[1] * N_BT, LANES, LANES)


def _fold_rows(x, op):
    acc = x[0:FOLD_ROWS, :]
    for r in range(1, x.shape[0] // FOLD_ROWS):
        acc = op(acc, x[r * FOLD_ROWS:(r + 1) * FOLD_ROWS, :])
    return acc


def _grouped_loop(lo, hi, carry, fn, groups):
    for g in groups:
        n = (hi - lo) // g
        carry = lax.fori_loop(0, n, lambda u, c, lo=lo, g=g: fn([lo + u * g + k for k in range(g)], c), carry)
        lo = lo + n * g
    return carry


def _bias_tile(bias_ref, h, qb0, n_qb, j):
    rows = []
    for c in range(SUB):
        cols = [bias_ref[h * N_BT + jnp.clip(qb0 + r - (j * SUB + c), 0, N_SAT)] for r in range(n_qb)]
        rows.append(cols[0] if n_qb == 1 else jnp.concatenate(cols, axis=1))
    return jnp.concatenate(rows, axis=0)


def _ones_rows(n):
    return jnp.ones((ONES_ROWS, n), BF16)


def _proj_even_kernel(x_ref, g_ref, w_ref, kvn_ref, wuk_ref,
                      qit_ref, wit_ref, qlt_ref, ki_ref, c_ref, ctx_ref, qbt_ref, kb_ref, vtx_ref):
    tm = x_ref.shape[0]
    h = _rms(x_ref[...], g_ref[0:1, :]).astype(BF16)
    z = _dot(h, w_ref[...])
    qa = z[:, 0:512].astype(BF16)
    ckv = _rms(z[:, 512:640], kvn_ref[...])
    c_ref[...] = ckv.astype(BF16)
    kb_ref[...] = z[:, 1664:2176].astype(BF16)
    ki_ref[...] = z[:, 2688:2688 + DH_IDX].astype(BF16)
    ctx_ref[0, D_LAT:, :] = _ones_rows(tm)
    for hb in range(H_B):
        vtx_ref[0, hb, 2 * DH_B:, :] = _ones_rows(tm)
    per_q, per_diff = TQ_DSA // LANES, TQ_DIFF // LANES
    for hh in range(H_A):
        ql = _dot(qa[:, hh * DH_A:(hh + 1) * DH_A], wuk_ref[hh]) * (DH_A ** -0.5 * LOG2E)
        for sb in range(tm // LANES):
            off = hh * TQ_DSA + (sb % per_q) * LANES
            qlt_ref[sb // per_q, :, off:off + LANES] = ql[sb * LANES:(sb + 1) * LANES, :].T.astype(BF16)
    for sb in range(tm // LANES):
        rows = slice(sb * LANES, (sb + 1) * LANES)
        blk, sub = sb // per_q, (sb % per_q) * LANES
        for pr in range(H_IDX // 2):
            t = z[rows, 640 + pr * LANES:640 + (pr + 1) * LANES].T
            for half in range(2):
                off = (2 * pr + half) * TQ_DSA + sub
                qit_ref[blk, :, off:off + LANES] = t[half * DH_IDX:(half + 1) * DH_IDX, :].astype(BF16)
        wit_ref[blk, :, sub:sub + LANES] = z[rows, 2816:2816 + LANES].T[0:H_IDX, :]
        ctx_ref[0, 0:D_LAT, rows] = ckv[rows, :].T.astype(BF16)
        for hb in range(H_B):
            qb = z[rows, 1152 + hb * LANES:1152 + (hb + 1) * LANES] * (DH_B ** -0.5 * LOG2E)
            lanes = slice((sb % per_diff) * LANES, (sb % per_diff + 1) * LANES)
            qbt_ref[sb // per_diff, hb, :, lanes] = qb.T.astype(BF16)
            vtx_ref[0, hb, 0:2 * DH_B, rows] = z[rows, 2176 + hb * LANES:2176 + (hb + 1) * LANES].T.astype(BF16)


def _proj_even(x2, g, w_in, kvn, w_uk):
    t = x2.shape[0]
    tm = TK
    assert t % tm == 0 and tm % TQ_DSA == 0 and tm % TQ_DIFF == 0
    pad = jnp.zeros((D_MODEL, LANES - DH_IDX), w_in.dtype)
    cols = [w_in[:, 0:512], w_in[:, 512:640], w_in[:, 640:1152], w_in[:, 1224:1736], w_in[:, 1736:2248],
            w_in[:, 2248:2760], w_in[:, 1152:1216], pad, w_in[:, 1216:1224],
            jnp.zeros((D_MODEL, LANES - H_IDX), w_in.dtype)]
    w = jnp.concatenate(cols, axis=1).astype(BF16)
    n = w.shape[1]
    dv = 2 * DH_B
    row = lambda i: (i, 0)
    full2 = lambda i: (0, 0)
    lead3 = lambda i: (i, 0, 0)
    lead4 = lambda i: (i, 0, 0, 0)
    outs = [
        ((t // TQ_DSA, DH_IDX, H_IDX * TQ_DSA), BF16, (tm // TQ_DSA, DH_IDX, H_IDX * TQ_DSA), lead3),
        ((t // TQ_DSA, H_IDX, TQ_DSA), F32, (tm // TQ_DSA, H_IDX, TQ_DSA), lead3),
        ((t // TQ_DSA, D_LAT, H_A * TQ_DSA), BF16, (tm // TQ_DSA, D_LAT, H_A * TQ_DSA), lead3),
        ((t, DH_IDX), BF16, (tm, DH_IDX), row),
        ((t, D_LAT), BF16, (tm, D_LAT), row),
        ((t // TK, D_LAT + ONES_ROWS, TK), BF16, (1, D_LAT + ONES_ROWS, TK), lead3),
        ((t // TQ_DIFF, H_B, dv, TQ_DIFF), BF16, (tm // TQ_DIFF, H_B, dv, TQ_DIFF), lead4),
        ((t, H_B * dv), BF16, (tm, H_B * dv), row),
        ((t // TK, H_B, dv + ONES_ROWS, TK), BF16, (1, H_B, dv + ONES_ROWS, TK), lead4),
    ]
    return pl.pallas_call(
        _proj_even_kernel,
        out_shape=[jax.ShapeDtypeStruct(shape, dt) for shape, dt, _, _ in outs],
        grid=(t // tm,),
        in_specs=[
            pl.BlockSpec((tm, D_MODEL), row),
            pl.BlockSpec((6, D_MODEL), full2),
            pl.BlockSpec((D_MODEL, n), full2),
            pl.BlockSpec((1, D_LAT), full2),
            pl.BlockSpec((H_A, DH_A, D_LAT), lambda i: (0, 0, 0)),
        ],
        out_specs=[pl.BlockSpec(block, imap) for _, _, block, imap in outs],
        compiler_params=pltpu.CompilerParams(vmem_limit_bytes=VMEM_LIMIT),
        name="proj_even",
    )(x2, g, w, kvn.reshape(1, D_LAT), w_uk.astype(BF16))


def _proj_odd_kernel(x_ref, g_ref, w_ref, qt_ref, k_ref, vtx_ref):
    tm = x_ref.shape[0]
    h = _rms(x_ref[...], g_ref[0:1, :]).astype(BF16)
    z = _dot(h, w_ref[...])
    for kv in range(H_C_KV):
        k_ref[kv] = z[:, 1024 + kv * DH_C:1024 + (kv + 1) * DH_C].astype(BF16)
    for sb in range(tm // TQ):
        rows = slice(sb * TQ, (sb + 1) * TQ)
        for pr in range(H_C // 2):
            t = (z[rows, pr * LANES:(pr + 1) * LANES] * (DH_C ** -0.5 * LOG2E)).T
            for half in range(2):
                hd = 2 * pr + half
                kv, gq = hd // G_C, hd % G_C
                qt_ref[sb, kv, :, gq * TQ:(gq + 1) * TQ] = t[half * DH_C:(half + 1) * DH_C, :].astype(BF16)
        vt = z[rows, 1152:1280].T
        for kv in range(H_C_KV):
            vtx_ref[sb, kv, 0:DH_C, :] = vt[kv * DH_C:(kv + 1) * DH_C, :].astype(BF16)
            vtx_ref[sb, kv, DH_C:, :] = _ones_rows(TQ)


def _proj_odd(x2, g, w_in):
    t = x2.shape[0]
    tm = min(TM_PROJ, t)
    assert TQ == LANES
    row = lambda i: (i, 0)
    full2 = lambda i: (0, 0)
    lead4 = lambda i: (i, 0, 0, 0)
    dvx = DH_C + ONES_ROWS
    return pl.pallas_call(
        _proj_odd_kernel,
        out_shape=[jax.ShapeDtypeStruct((t // TQ, H_C_KV, DH_C, G_C * TQ), BF16),
                   jax.ShapeDtypeStruct((H_C_KV, t, DH_C), BF16),
                   jax.ShapeDtypeStruct((t // TQ, H_C_KV, dvx, TQ), BF16)],
        grid=(t // tm,),
        in_specs=[pl.BlockSpec((tm, D_MODEL), row),
                  pl.BlockSpec((6, D_MODEL), full2),
                  pl.BlockSpec((D_MODEL, 1280), full2)],
        out_specs=[pl.BlockSpec((tm // TQ, H_C_KV, DH_C, G_C * TQ), lead4),
                   pl.BlockSpec((H_C_KV, tm, DH_C), lambda i: (0, i, 0)),
                   pl.BlockSpec((tm // TQ, H_C_KV, dvx, TQ), lead4)],
        compiler_params=pltpu.CompilerParams(vmem_limit_bytes=VMEM_LIMIT),
        name="proj_odd",
    )(x2, g, w_in.astype(BF16))


def _dsa_kernel(qit_ref, wit_ref, qlt_ref, k_ref, c_ref, ctx_ref, bias_ref, wuvt_ref, o_ref,
                sc_ref, acc_ref, *p_refs, topk):
    TQ = TQ_DSA
    i = pl.program_id(1)
    t0 = i * TQ
    nt = (t0 + TQ + TK - 1) // TK
    n_qb = TQ // LANES
    n_far = jnp.maximum(i * n_qb - N_SAT + 1, 0) // SUB
    kf = float(topk)

    qpos = lax.broadcasted_iota(jnp.int32, (1, TQ), 1) + t0
    limit = (qpos // CHUNK + 1) * CHUNK
    small = limit <= topk
    big = jnp.logical_not(small)
    w = wit_ref[...]
    qit = qit_ref[...]

    def score_tiles(js, carry):
        rmax, rmin = carry
        zs = [_dot(k_ref[j], qit) for j in js]
        for j, z in zip(js, zs):
            sc = jnp.maximum(z[:, 0:TQ], 0.0) * w[0:1, :]
            for h in range(1, H_IDX):
                sc = sc + jnp.maximum(z[:, h * TQ:(h + 1) * TQ], 0.0) * w[h:h + 1, :]
            kpos = lax.broadcasted_iota(jnp.int32, (TK, TQ), 0) + j * TK
            adm = kpos < limit
            sc_ref[j] = jnp.where(adm, sc, -jnp.inf)
            rmax = jnp.maximum(rmax, _fold_rows(jnp.where(adm, sc, -jnp.inf), jnp.maximum))
            rmin = jnp.minimum(rmin, _fold_rows(jnp.where(adm, sc, jnp.inf), jnp.minimum))
        return rmax, rmin

    rmax, rmin = _grouped_loop(
        0, nt, (jnp.full((FOLD_ROWS, TQ), -jnp.inf, F32), jnp.full((FOLD_ROWS, TQ), jnp.inf, F32)),
        score_tiles, GROUPS_DSA)
    hi0 = jnp.max(rmax, axis=0, keepdims=True)
    lo0 = jnp.min(rmin, axis=0, keepdims=True)

    def col_reduce(tile_fn, op, init, final):
        def fn(js, acc):
            for j in js:
                acc = op(acc, _fold_rows(tile_fn(j), op))
            return acc
        acc = _grouped_loop(0, nt, jnp.full((FOLD_ROWS, TQ), init, F32), fn, GROUPS_PASS)
        return final(acc, axis=0, keepdims=True)

    def count_ge(thr):
        return col_reduce(lambda j: jnp.where(sc_ref[j] >= thr, 1.0, 0.0), jnp.add, 0.0, jnp.sum)

    def max_le(thr):
        def tile(j):
            t = sc_ref[j]
            return jnp.where(t <= thr, t, -jnp.inf)
        return col_reduce(tile, jnp.maximum, -jnp.inf, jnp.max)

    def max_lt(thr):
        def tile(j):
            t = sc_ref[j]
            return jnp.where(t < thr, t, -jnp.inf)
        return col_reduce(tile, jnp.maximum, -jnp.inf, jnp.max)

    def bisect_body(_, carry):
        lo, hi = carry
        mid = lo * 0.5 + hi * 0.5
        ge = count_ge(mid) >= kf
        return jnp.where(ge, mid, lo), jnp.where(ge, hi, mid)

    lo, hi = lax.fori_loop(0, N_BISECT, bisect_body, (lo0, hi0))

    v0 = max_le(hi)
    c0 = count_ge(v0)

    def walk_cond(carry):
        _, cnt, it = carry
        pending = jnp.where((cnt < kf) & big, 1.0, 0.0)
        return (jnp.max(pending) > 0.0) & (it < topk + 2)

    def walk_body(carry):
        v, cnt, it = carry
        v = jnp.where((cnt < kf) & big, max_lt(v), v)
        return v, count_ge(v), it + 1

    tau, cnt, _ = lax.while_loop(walk_cond, walk_body, (v0, c0, jnp.int32(0)))
    tau = jnp.where(small, -jnp.finfo(F32).max, tau)

    tied = (cnt > kf) & big

    @pl.when(jnp.max(jnp.where(tied, 1.0, 0.0)) > 0.0)
    def _():
        excess = jnp.where(tied, cnt - kf, 0.0)
        kr = lax.broadcasted_iota(jnp.int32, (TK, TK), 0)
        kc = lax.broadcasted_iota(jnp.int32, (TK, TK), 1)
        upper = jnp.where(kc >= kr, 1.0, 0.0).astype(BF16)

        def drop_body(u, later):
            j = nt - 1 - u
            t = sc_ref[j]
            e = (t == tau) & tied
            behind = _dot(upper, jnp.where(e, 1.0, 0.0).astype(BF16)) + later
            sc_ref[j] = jnp.where(e & (behind <= excess), -jnp.inf, t)
            return behind[0:1, :]

        lax.fori_loop(0, nt, drop_body, jnp.zeros((1, TQ), F32))

    acc_ref[...] = jnp.zeros(acc_ref.shape, F32)
    qlt = qlt_ref[...]

    def attn_tiles(js, m, with_bias):
        s_alls = [_dot(c_ref[j], qlt).astype(BF16) for j in js]
        for g, (j, s_all) in enumerate(zip(js, s_alls)):
            p_ref = p_refs[g]
            pen = jnp.where(sc_ref[j] >= tau, 0.0, NEG_INF).astype(BF16)
            m_new, alpha = [], []
            for h in range(H_A):
                cols = slice(h * TQ, (h + 1) * TQ)
                s = s_all[:, cols] + pen
                if with_bias:
                    s = s + _bias_tile(bias_ref, h, i * n_qb, n_qb, j).astype(BF16)
                m_old = m[:, cols]
                m_h = jnp.maximum(m_old, jnp.max(_fold_rows(s, jnp.maximum), axis=0, keepdims=True).astype(F32))
                p_ref[:, cols] = jnp.exp2(s - m_h.astype(BF16))
                alpha.append(jnp.exp2(m_old - m_h))
                m_new.append(m_h)
            acc_ref[...] = acc_ref[...] * jnp.concatenate(alpha, axis=1) + _dot(ctx_ref[j], p_ref[...])
            m = jnp.concatenate(m_new, axis=1)
        return m

    m = jnp.full((1, H_A * TQ), NEG_INF, F32)
    m = _grouped_loop(0, n_far, m, lambda js, mm: attn_tiles(js, mm, False), GROUPS_DSA)
    _grouped_loop(n_far, nt, m, lambda js, mm: attn_tiles(js, mm, True), GROUPS_NEAR)

    outs = []
    for h in range(H_A):
        cols = slice(h * TQ, (h + 1) * TQ)
        o_t = (acc_ref[0:D_LAT, cols] / acc_ref[D_LAT:D_LAT + 1, cols]).astype(BF16)
        outs.append(_dot(wuvt_ref[h], o_t))
    for pr in range(H_A // 2):
        pair = jnp.concatenate([outs[2 * pr], outs[2 * pr + 1]], axis=0)
        o_ref[:, pr * LANES:(pr + 1) * LANES] = pair.T.astype(BF16)


def _dsa(qit, wit, qlt, ki, ckv, ctx, bias_a, w_uv, b, s):
    TQ = TQ_DSA
    nq, nt = s // TQ, s // TK
    topk = min(TOPK_MAX, s // 4)
    k4 = ki.reshape(b, nt, TK, DH_IDX)
    c4 = ckv.reshape(b, nt, TK, D_LAT)
    ctx = ctx.reshape(b, nt, D_LAT + ONES_ROWS, TK)
    wuvt = jnp.swapaxes(w_uv, 1, 2).astype(BF16)
    qblk = lambda bb, i: (bb * nq + i, 0, 0)
    kv4 = lambda bb, i: (bb, 0, 0, 0)
    once = dict(pipeline_mode=pl.Buffered(1))
    return pl.pallas_call(
        functools.partial(_dsa_kernel, topk=topk),
        out_shape=jax.ShapeDtypeStruct((b * s, H_A * DH_A), BF16),
        grid=(b, nq),
        in_specs=[
            pl.BlockSpec((None, DH_IDX, H_IDX * TQ), qblk),
            pl.BlockSpec((None, H_IDX, TQ), qblk),
            pl.BlockSpec((None, D_LAT, H_A * TQ), qblk),
            pl.BlockSpec((None, nt, TK, DH_IDX), kv4, **once),
            pl.BlockSpec((None, nt, TK, D_LAT), kv4, **once),
            pl.BlockSpec((None, nt, D_LAT + ONES_ROWS, TK), kv4, **once),
            pl.BlockSpec((H_A * N_BT, LANES, LANES), lambda bb, i: (0, 0, 0), **once),
            pl.BlockSpec((H_A, DH_A, D_LAT), lambda bb, i: (0, 0, 0), **once),
        ],
        out_specs=pl.BlockSpec((TQ, H_A * DH_A), lambda bb, i: (bb * nq + i, 0)),
        scratch_shapes=[
            pltpu.VMEM((nt, TK, TQ), F32),
            pltpu.VMEM((D_LAT + ONES_ROWS, H_A * TQ), F32),
        ] + [pltpu.VMEM((TK, H_A * TQ), BF16)] * max(GROUPS_DSA),
        compiler_params=pltpu.CompilerParams(vmem_limit_bytes=VMEM_LIMIT),
        name="dsa_attention",
    )(qit, wit, qlt, k4, c4, ctx, bias_a, wuvt)


def _diff_kernel(qt_ref, k_ref, vtx_ref, bias_ref, lam_ref, sub_ref, o_ref, acc_ref, *p_refs, lam_init):
    h = pl.program_id(1)
    i = pl.program_id(2)
    tq = TQ_DIFF
    n_qb = tq // LANES
    t0 = i * tq
    nt = (t0 + tq + TK - 1) // TK
    n_far = jnp.maximum(i * n_qb - N_SAT + 1, 0) // SUB
    qt = qt_ref[...]
    top = lax.broadcasted_iota(jnp.int32, qt.shape, 0) < DH_B
    q2 = jnp.concatenate([jnp.where(top, qt, jnp.zeros_like(qt)), jnp.where(top, jnp.zeros_like(qt), qt)], axis=1)
    qpos = lax.broadcasted_iota(jnp.int32, (1, tq), 1) + t0
    limit = (qpos // CHUNK + 1) * CHUNK
    acc_ref[...] = jnp.zeros(acc_ref.shape, F32)

    def tiles(js, m, near):
        s2s = [_dot(k_ref[j], q2).astype(BF16) for j in js]
        for g, (j, s2) in enumerate(zip(js, s2s)):
            p_ref = p_refs[g]
            if near:
                bias = _bias_tile(bias_ref, h, i * n_qb, n_qb, j).astype(BF16)
                adm = (lax.broadcasted_iota(jnp.int32, (TK, tq), 0) + j * TK) < limit
            m_new, alpha = [], []
            for mp in range(2):
                cols = slice(mp * tq, (mp + 1) * tq)
                s = s2[:, cols]
                if near:
                    s = jnp.where(adm, s + bias, jnp.asarray(NEG_INF, BF16))
                m_old = m[:, cols]
                m_h = jnp.maximum(m_old, jnp.max(_fold_rows(s, jnp.maximum), axis=0, keepdims=True).astype(F32))
                p_ref[:, cols] = jnp.exp2(s - m_h.astype(BF16))
                alpha.append(jnp.exp2(m_old - m_h))
                m_new.append(m_h)
            acc_ref[...] = acc_ref[...] * jnp.concatenate(alpha, axis=1) + _dot(vtx_ref[j], p_ref[...])
            m = jnp.concatenate(m_new, axis=1)
        return m

    m = jnp.full((1, 2 * tq), NEG_INF, F32)
    m = _grouped_loop(0, n_far, m, lambda js, mm: tiles(js, mm, False), GROUPS)
    _grouped_loop(n_far, nt, m, lambda js, mm: tiles(js, mm, True), GROUPS_NEAR)

    lf = lam_ref[...]
    lam = (jnp.exp(jnp.sum(lf[0:1, :] * lf[1:2, :], axis=1, keepdims=True))
           - jnp.exp(jnp.sum(lf[2:3, :] * lf[3:4, :], axis=1, keepdims=True)) + lam_init)
    dv = 2 * DH_B
    o = (acc_ref[0:dv, 0:tq] / acc_ref[dv:dv + 1, 0:tq]
         - lam * (acc_ref[0:dv, tq:2 * tq] / acc_ref[dv:dv + 1, tq:2 * tq]))
    o = o * lax.rsqrt(jnp.mean(o * o, axis=0, keepdims=True) + EPS) * sub_ref[...] * (1.0 - lam_init)
    o_ref[...] = o.T.astype(BF16)


def _diff(qbt, kb, vtx, bias_b, b_lambda, b_subln, lam_init, b, s):
    tq = TQ_DIFF
    nq, nt = s // tq, s // TK
    dv = 2 * DH_B
    k4 = kb.reshape(b, nt, TK, H_B * dv)
    vtx = vtx.reshape(b, nt, H_B, dv + ONES_ROWS, TK)
    return pl.pallas_call(
        functools.partial(_diff_kernel, lam_init=lam_init),
        out_shape=jax.ShapeDtypeStruct((b * s, H_B * dv), BF16),
        grid=(b, H_B, nq),
        in_specs=[
            pl.BlockSpec((None, None, dv, tq), lambda bb, h, i: (bb * nq + i, h, 0, 0)),
            pl.BlockSpec((None, nt, TK, dv), lambda bb, h, i: (bb, 0, 0, h)),
            pl.BlockSpec((None, nt, None, dv + ONES_ROWS, TK), lambda bb, h, i: (bb, 0, h, 0, 0)),
            pl.BlockSpec((H_B * N_BT, LANES, LANES), lambda bb, h, i: (0, 0, 0)),
            pl.BlockSpec((4, DH_B), lambda bb, h, i: (0, 0)),
            pl.BlockSpec((dv, 1), lambda bb, h, i: (0, 0)),
        ],
        out_specs=pl.BlockSpec((tq, dv), lambda bb, h, i: (bb * nq + i, h)),
        scratch_shapes=[
            pltpu.VMEM((dv + ONES_ROWS, 2 * tq), F32),
        ] + [pltpu.VMEM((TK, 2 * tq), BF16)] * max(GROUPS),
        compiler_params=pltpu.CompilerParams(vmem_limit_bytes=VMEM_LIMIT),
        name="diff_attention",
    )(qbt, k4, vtx, bias_b, b_lambda.astype(F32), b_subln.reshape(dv, 1).astype(F32))


def _swa_kernel(qt_ref, ka_ref, kb_ref, vta_ref, vtb_ref, bias_ref, sink_ref, o_ref):
    i = pl.program_id(1)
    pad_pen = jnp.where(i == 0, NEG_INF, 0.0)
    for kv in range(H_C_KV):
        qt = qt_ref[kv]
        s = jnp.concatenate([_dot(ka_ref[kv], qt) + pad_pen, _dot(kb_ref[kv], qt)], axis=0) + bias_ref[kv]
        sink = sink_ref[kv]
        m = jnp.maximum(jnp.max(s, axis=0, keepdims=True), sink)
        p = jnp.exp2(s - m).astype(BF16)
        acc = _dot(vta_ref[kv], p[0:TQ, :]) + _dot(vtb_ref[kv], p[TQ:2 * TQ, :])
        o = acc[0:DH_C, :] / (acc[DH_C:DH_C + 1, :] + jnp.exp2(sink - m))
        for pr in range(G_C // 2):
            pair = jnp.concatenate([o[:, (2 * pr) * TQ:(2 * pr + 1) * TQ], o[:, (2 * pr + 1) * TQ:(2 * pr + 2) * TQ]],
                                   axis=0)
            col = (kv * G_C + 2 * pr) * DH_C
            o_ref[:, col:col + LANES] = pair.T.astype(BF16)


def _swa(qt, k, vtx, table_c, sinks, b, s):
    assert WINDOW == TQ and TQ % CHUNK == 0
    nq = s // TQ
    length = 3 * TQ + 1
    m = np.arange(length)
    diff = np.where(m < 2 * TQ + 1, m, m - length)
    u = jnp.take(table_c.astype(F32), jnp.asarray(_t5_bucket_np(diff - TQ)), axis=0) * LOG2E
    bias = _toeplitz(jnp.moveaxis(u, -1, 0), TQ, 2 * TQ)
    rblk = np.arange(TQ)[:, None] // CHUNK + WINDOW // CHUNK
    cblk = np.arange(2 * TQ)[None, :] // CHUNK
    valid = (cblk <= rblk) & (cblk >= rblk - WINDOW // CHUNK)
    bias = jnp.where(jnp.asarray(valid)[None], bias, NEG_INF)
    bias_t = jnp.transpose(bias.reshape(H_C_KV, G_C, TQ, 2 * TQ), (0, 3, 1, 2)).reshape(H_C_KV, 2 * TQ, G_C * TQ)
    sink_row = jnp.repeat(sinks.astype(F32).reshape(H_C_KV, 1, G_C) * LOG2E, TQ, axis=2)

    dvx = DH_C + ONES_ROWS
    prev = lambda bb, i: bb * nq + jnp.maximum(i - 1, 0)
    return pl.pallas_call(
        _swa_kernel,
        out_shape=jax.ShapeDtypeStruct((b * s, H_C * DH_C), BF16),
        grid=(b, nq),
        in_specs=[
            pl.BlockSpec((None, H_C_KV, DH_C, G_C * TQ), lambda bb, i: (bb * nq + i, 0, 0, 0)),
            pl.BlockSpec((H_C_KV, TQ, DH_C), lambda bb, i: (0, prev(bb, i), 0)),
            pl.BlockSpec((H_C_KV, TQ, DH_C), lambda bb, i: (0, bb * nq + i, 0)),
            pl.BlockSpec((None, H_C_KV, dvx, TQ), lambda bb, i: (prev(bb, i), 0, 0, 0)),
            pl.BlockSpec((None, H_C_KV, dvx, TQ), lambda bb, i: (bb * nq + i, 0, 0, 0)),
            pl.BlockSpec((H_C_KV, 2 * TQ, G_C * TQ), lambda bb, i: (0, 0, 0)),
            pl.BlockSpec((H_C_KV, 1, G_C * TQ), lambda bb, i: (0, 0, 0)),
        ],
        out_specs=pl.BlockSpec((TQ, H_C * DH_C), lambda bb, i: (bb * nq + i, 0)),
        compiler_params=pltpu.CompilerParams(vmem_limit_bytes=VMEM_LIMIT),
        name="swa_attention",
    )(qt, k, k, vtx, vtx, bias_t, sink_row)


def _memkv_kernel(mem_ref, g_ref, w_ref, k_ref, v_ref):
    h = _rms(mem_ref[...], g_ref[...]).astype(BF16)
    kv = _dot(h, w_ref[...])
    k_ref[...] = kv[:, 0:H_X * DH_X].astype(BF16)
    v_ref[...] = kv[:, H_X * DH_X:].astype(BF16)


def _memkv(mem, g, wkv):
    b, m, _ = mem.shape
    n = H_X * DH_X
    return pl.pallas_call(
        _memkv_kernel,
        out_shape=[jax.ShapeDtypeStruct((b, m, n), BF16), jax.ShapeDtypeStruct((b, m, n), BF16)],
        grid=(b,),
        in_specs=[pl.BlockSpec((None, m, D_MODEL), lambda i: (i, 0, 0)),
                  pl.BlockSpec((1, D_MODEL), lambda i: (0, 0)),
                  pl.BlockSpec((D_MODEL, 2 * n), lambda i: (0, 0))],
        out_specs=[pl.BlockSpec((None, m, n), lambda i: (i, 0, 0)), pl.BlockSpec((None, m, n), lambda i: (i, 0, 0))],
        compiler_params=pltpu.CompilerParams(vmem_limit_bytes=VMEM_LIMIT),
        name="mem_kv",
    )(mem, g.reshape(1, D_MODEL), wkv.astype(BF16))


def _lane_half_mask(rows, half):
    lane = lax.broadcasted_iota(jnp.int32, (rows, LANES), 1)
    return (lane < 64) if half == 0 else (lane >= 64)


def _tail_kernel(*refs, n_a):
    x_ref, g_ref = refs[0], refs[1]
    a_refs = refs[2:2 + n_a]
    wout_refs = refs[2 + n_a:2 + 2 * n_a]
    wq_ref, kt_ref, v_ref, wo_ref, w1_ref, w2_ref, o_ref = refs[2 + 2 * n_a:]
    tc = x_ref.shape[0] // TAIL_CHAINS
    chains = [slice(c * tc, (c + 1) * tc) for c in range(TAIL_CHAINS)]
    g = g_ref[...]

    ys = []
    for r in chains:
        y = _dot(a_refs[0][r, :], wout_refs[0][...])
        for a_ref, w_ref in zip(a_refs[1:], wout_refs[1:]):
            y = y + _dot(a_ref[r, :], w_ref[...])
        ys.append(y)
    xs = [x_ref[r, :] + _rms(y, g[1:2, :]) for r, y in zip(chains, ys)]

    hqs = [_rms(x, g[2:3, :]).astype(BF16) for x in xs]
    qs = [_dot(hq, wq_ref[...]) for hq in hqs]
    qs = [(q * (DH_X ** -0.5)).astype(BF16) for q in qs]
    pairs = [[] for _ in chains]
    for pr in range(H_X // 2):
        vp = v_ref[:, pr * LANES:(pr + 1) * LANES]
        outs = [jnp.zeros((tc, LANES), F32) for _ in chains]
        for half in range(2):
            qzs = [jnp.where(_lane_half_mask(tc, half), q[:, pr * LANES:(pr + 1) * LANES], jnp.zeros((tc, LANES), BF16))
                   for q in qs]
            ss = [_dot(qz, kt_ref[pr * LANES:(pr + 1) * LANES, :]) for qz in qzs]
            vz = jnp.where(_lane_half_mask(vp.shape[0], half), vp, jnp.zeros_like(vp))
            ps = []
            for sc in ss:
                e = jnp.exp(sc - jnp.max(sc, axis=1, keepdims=True))
                ps.append((e / jnp.sum(e, axis=1, keepdims=True)).astype(BF16))
            outs = [out + _dot(p, vz) for out, p in zip(outs, ps)]
        for c, out in enumerate(outs):
            pairs[c].append(out.astype(BF16))
    ys = [_dot(jnp.concatenate(pc, axis=1), wo_ref[...]) for pc in pairs]
    xs = [x + _rms(y, g[3:4, :]) for x, y in zip(xs, ys)]

    hms = [_rms(x, g[4:5, :]).astype(BF16) for x in xs]
    n_chunks = D_FF // FF_CHUNK
    ys = [jnp.zeros((tc, D_MODEL), F32) for _ in chains]
    a_cur = [_dot(hm, w1_ref[:, 0:FF_CHUNK]) for hm in hms]
    for k in range(n_chunks):
        if k + 1 < n_chunks:
            a_next = [_dot(hm, w1_ref[:, (k + 1) * FF_CHUNK:(k + 2) * FF_CHUNK]) for hm in hms]
        acts = [jnp.square(jnp.maximum(a, 0.0)).astype(BF16) for a in a_cur]
        ys = [y + _dot(act, w2_ref[k * FF_CHUNK:(k + 1) * FF_CHUNK, :]) for y, act in zip(ys, acts)]
        if k + 1 < n_chunks:
            a_cur = a_next
    for r, x, y in zip(chains, xs, ys):
        o_ref[r, :] = x + _rms(y, g[5:6, :])


def _tail(x2, g, a_list, wout_list, wq, kt, v, wo, w1, w2, b, s):
    t = x2.shape[0]
    tm = min(TM_TAIL, s)
    per_b = s // tm
    n_a = len(a_list)
    row = lambda i: (i, 0)
    full2 = lambda i: (0, 0)
    once = dict(pipeline_mode=pl.Buffered(1))
    in_specs = [pl.BlockSpec((tm, D_MODEL), row), pl.BlockSpec((6, D_MODEL), full2)]
    in_specs += [pl.BlockSpec((tm, a.shape[1]), row) for a in a_list]
    in_specs += [pl.BlockSpec(w.shape, full2, **once) for w in wout_list]
    in_specs += [
        pl.BlockSpec(wq.shape, full2, **once),
        pl.BlockSpec((None,) + kt.shape[1:], lambda i: (i // per_b, 0, 0)),
        pl.BlockSpec((None,) + v.shape[1:], lambda i: (i // per_b, 0, 0)),
        pl.BlockSpec(wo.shape, full2, **once),
        pl.BlockSpec(w1.shape, full2, **once),
        pl.BlockSpec(w2.shape, full2, **once),
    ]
    return pl.pallas_call(
        functools.partial(_tail_kernel, n_a=n_a),
        out_shape=jax.ShapeDtypeStruct((t, D_MODEL), F32),
        grid=(t // tm,),
        in_specs=in_specs,
        out_specs=pl.BlockSpec((tm, D_MODEL), row),
        compiler_params=pltpu.CompilerParams(vmem_limit_bytes=VMEM_LIMIT),
        name="tail",
    )(x2, g, *a_list, *wout_list, wq, kt, v, wo, w1, w2)


def kernel(x, mem, rel_bias_table, norm_g, ev_w_in, ev_a_kv_norm, ev_a_w_uk, ev_a_w_uv, ev_b_lambda, ev_b_subln, ev_w_out, od_w_in, od_sinks, od_w_out, xa_wq, xa_wkv, xa_wo, xa_mem_norm, mlp_w1, mlp_w2):
    b, s, d = x.shape
    depth = norm_g.shape[0]
    assert d == D_MODEL and s % TK == 0 and TK % TQ_DIFF == 0 and TK % TQ_DSA == 0 and TQ_DSA % LANES == 0
    x2 = x.reshape(b * s, d)
    bias_a = _causal_bias_tiles(rel_bias_table[:, :H_A])
    bias_b = _causal_bias_tiles(rel_bias_table[:, H_A:H_A + H_B])
    table_c = rel_bias_table[:, H_A + H_B:]
    for l in range(depth):
        g = norm_g[l].astype(F32)
        if l % 2 == 0:
            e = l // 2
            lam_init = 0.8 - 0.6 * math.exp(-0.3 * l)
            qit, wit, qlt, ki, ckv, ctx, qbt, kb, vtx = _proj_even(x2, g, ev_w_in[e], ev_a_kv_norm[e], ev_a_w_uk[e])
            oa = _dsa(qit, wit, qlt, ki, ckv, ctx, bias_a, ev_a_w_uv[e], b, s)
            ob = _diff(qbt, kb, vtx, bias_b, ev_b_lambda[e], ev_b_subln[e], lam_init, b, s)
            n_a = H_A * DH_A
            a_list = [oa, ob]
            wout_list = [ev_w_out[e][:n_a].astype(BF16), ev_w_out[e][n_a:].astype(BF16)]
        else:
            o = l // 2
            qt, k, vtx = _proj_odd(x2, g, od_w_in[o])
            a_list = [_swa(qt, k, vtx, table_c, od_sinks[o], b, s)]
            wout_list = [od_w_out[o].astype(BF16)]
        mk, mv = _memkv(mem, xa_mem_norm[l], xa_wkv[l])
        x2 = _tail(x2, g, a_list, wout_list, xa_wq[l].astype(BF16), jnp.swapaxes(mk, 1, 2), mv,
                   xa_wo[l].astype(BF16), mlp_w1[l].astype(BF16), mlp_w2[l].astype(BF16), b, s)
    return x2.reshape(b, s, d)
```

```python
import functools
import math

import numpy as np
import jax
import jax.numpy as jnp
from jax import lax
from jax.experimental import pallas as pl
from jax.experimental.pallas import tpu as pltpu

F32 = jnp.float32
BF16 = jnp.bfloat16

D_MODEL = 1024
CHUNK = 64
EPS = 1e-6
NEG_INF = -1e30
LOG2E = math.log2(math.e)

H_A, DH_A, D_LAT = 8, 64, 128
H_IDX, DH_IDX = 8, 64
TOPK_MAX = 256
H_B, DH_B = 4, 64
H_C, H_C_KV, DH_C = 16, 2, 64
G_C = H_C // H_C_KV
WINDOW = 128
H_X, DH_X = 4, 64
D_FF = 4 * D_MODEL
N_BUCKETS = 32
MAX_DIST = 1024

LANES = 128
FOLD_ROWS = 64
TQ = 128
TQ_DSA = 256
TQ_DIFF = 512
TK = 512
SUB = TK // LANES
ONES_ROWS = 16
TM_PROJ = 512
TM_TAIL = 1024
TAIL_CHAINS = 2
FF_CHUNK = 1024
VMEM_LIMIT = 56 * 1024 * 1024
N_BISECT = 16
GROUPS = (4, 2, 1)
GROUPS_NEAR = (3, 2, 1)
GROUPS_DSA = (4, 2, 1)
GROUPS_PASS = (4, 1)


def _rms(x, g):
    return x * lax.rsqrt(jnp.mean(x * x, axis=-1, keepdims=True) + EPS) * g


def _dot(a, b):
    return jnp.dot(a, b, preferred_element_type=F32)


def _t5_bucket_np(rel):
    nb = N_BUCKETS // 2
    max_exact = nb // 2
    n = np.abs(rel)
    nf = np.maximum(n, 1).astype(np.float32)
    large = max_exact + (np.log(nf / max_exact) / math.log(MAX_DIST / max_exact) * (nb - max_exact)).astype(np.int32)
    large = np.minimum(large, nb - 1)
    return np.where(rel > 0, nb, 0) + np.where(n < max_exact, n, large)


def _sat_blocks():
    d = 1
    while _t5_bucket_np(np.array([-(d * LANES - (LANES - 1))]))[0] != N_BUCKETS // 2 - 1:
        d += 1
    return d


N_SAT = _sat_blocks()
N_BT = N_SAT + 1


def _toeplitz(u, rows, cols):
    length = u.shape[-1]
    flat = jnp.tile(u, (1,) * (u.ndim - 1) + (rows,))[..., :rows * (length - 1)]
    return flat.reshape(u.shape[:-1] + (rows, length - 1))[..., :cols]


def _causal_bias_tiles(table_cols):
    length = 2 * LANES + 1
    m = np.arange(length)
    diff = np.where(m < LANES + 1, m, m - length)
    rel = -diff[None, :] - (np.arange(N_BT) * LANES)[:, None]
    tab = table_cols.astype(F32)
    u = jnp.take(tab, jnp.asarray(_t5_bucket_np(rel)), axis=0)
    u = (u - tab[N_BUCKETS // 2 - 1][None, None, :]) * LOG2E
    tiles = _toeplitz(jnp.moveaxis(u, -1, 0), LANES, LANES)
    return tiles.reshape(table_cols.shape[1] * N_BT, LANES, LANES)


def _fold_rows(x, op):
    acc = x[0:FOLD_ROWS, :]
    for r in range(1, x.shape[0] // FOLD_ROWS):
        acc = op(acc, x[r * FOLD_ROWS:(r + 1) * FOLD_ROWS, :])
    return acc


def _grouped_loop(lo, hi, carry, fn, groups):
    for g in groups:
        n = (hi - lo) // g
        carry = lax.fori_loop(0, n, lambda u, c, lo=lo, g=g: fn([lo + u * g + k for k in range(g)], c), carry)
        lo = lo + n * g
    return carry


def _bias_tile(bias_ref, h, qb0, n_qb, j):
    rows = []
    for c in range(SUB):
        cols = [bias_ref[h * N_BT + jnp.clip(qb0 + r - (j * SUB + c), 0, N_SAT)] for r in range(n_qb)]
        rows.append(cols[0] if n_qb == 1 else jnp.concatenate(cols, axis=1))
    return jnp.concatenate(rows, axis=0)


def _ones_rows(n):
    return jnp.ones((ONES_ROWS, n), BF16)


def _proj_even_kernel(x_ref, g_ref, w_ref, kvn_ref, wuk_ref,
                      qit_ref, wit_ref, qlt_ref, ki_ref, c_ref, ctx_ref, qbt_ref, kb_ref, vtx_ref):
    tm = x_ref.shape[0]
    h = _rms(x_ref[...], g_ref[0:1, :]).astype(BF16)
    z = _dot(h, w_ref[...])
    qa = z[:, 0:512].astype(BF16)
    ckv = _rms(z[:, 512:640], kvn_ref[...])
    c_ref[...] = ckv.astype(BF16)
    kb_ref[...] = z[:, 1664:2176].astype(BF16)
    ki_ref[...] = z[:, 2688:2688 + DH_IDX].astype(BF16)
    ctx_ref[0, D_LAT:, :] = _ones_rows(tm)
    for hb in range(H_B):
        vtx_ref[0, hb, 2 * DH_B:, :] = _ones_rows(tm)
    per_q, per_diff = TQ_DSA // LANES, TQ_DIFF // LANES
    for hh in range(H_A):
        ql = _dot(qa[:, hh * DH_A:(hh + 1) * DH_A], wuk_ref[hh]) * (DH_A ** -0.5 * LOG2E)
        for sb in range(tm // LANES):
            off = hh * TQ_DSA + (sb % per_q) * LANES
            qlt_ref[sb // per_q, :, off:off + LANES] = ql[sb * LANES:(sb + 1) * LANES, :].T.astype(BF16)
    for sb in range(tm // LANES):
        rows = slice(sb * LANES, (sb + 1) * LANES)
        blk, sub = sb // per_q, (sb % per_q) * LANES
        for pr in range(H_IDX // 2):
            t = z[rows, 640 + pr * LANES:640 + (pr + 1) * LANES].T
            for half in range(2):
                off = (2 * pr + half) * TQ_DSA + sub
                qit_ref[blk, :, off:off + LANES] = t[half * DH_IDX:(half + 1) * DH_IDX, :].astype(BF16)
        wit_ref[blk, :, sub:sub + LANES] = z[rows, 2816:2816 + LANES].T[0:H_IDX, :]
        ctx_ref[0, 0:D_LAT, rows] = ckv[rows, :].T.astype(BF16)
        for hb in range(H_B):
            qb = z[rows, 1152 + hb * LANES:1152 + (hb + 1) * LANES] * (DH_B ** -0.5 * LOG2E)
            lanes = slice((sb % per_diff) * LANES, (sb % per_diff + 1) * LANES)
            qbt_ref[sb // per_diff, hb, :, lanes] = qb.T.astype(BF16)
            vtx_ref[0, hb, 0:2 * DH_B, rows] = z[rows, 2176 + hb * LANES:2176 + (hb + 1) * LANES].T.astype(BF16)


def _proj_even(x2, g, w_in, kvn, w_uk):
    t = x2.shape[0]
    tm = TK
    assert t % tm == 0 and tm % TQ_DSA == 0 and tm % TQ_DIFF == 0
    pad = jnp.zeros((D_MODEL, LANES - DH_IDX), w_in.dtype)
    cols = [w_in[:, 0:512], w_in[:, 512:640], w_in[:, 640:1152], w_in[:, 1224:1736], w_in[:, 1736:2248],
            w_in[:, 2248:2760], w_in[:, 1152:1216], pad, w_in[:, 1216:1224],
            jnp.zeros((D_MODEL, LANES - H_IDX), w_in.dtype)]
    w = jnp.concatenate(cols, axis=1).astype(BF16)
    n = w.shape[1]
    dv = 2 * DH_B
    row = lambda i: (i, 0)
    full2 = lambda i: (0, 0)
    lead3 = lambda i: (i, 0, 0)
    lead4 = lambda i: (i, 0, 0, 0)
    outs = [
        ((t // TQ_DSA, DH_IDX, H_IDX * TQ_DSA), BF16, (tm // TQ_DSA, DH_IDX, H_IDX * TQ_DSA), lead3),
        ((t // TQ_DSA, H_IDX, TQ_DSA), F32, (tm // TQ_DSA, H_IDX, TQ_DSA), lead3),
        ((t // TQ_DSA, D_LAT, H_A * TQ_DSA), BF16, (tm // TQ_DSA, D_LAT, H_A * TQ_DSA), lead3),
        ((t, DH_IDX), BF16, (tm, DH_IDX), row),
        ((t, D_LAT), BF16, (tm, D_LAT), row),
        ((t // TK, D_LAT + ONES_ROWS, TK), BF16, (1, D_LAT + ONES_ROWS, TK), lead3),
        ((t // TQ_DIFF, H_B, dv, TQ_DIFF), BF16, (tm // TQ_DIFF, H_B, dv, TQ_DIFF), lead4),
        ((t, H_B * dv), BF16, (tm, H_B * dv), row),
        ((t // TK, H_B, dv + ONES_ROWS, TK), BF16, (1, H_B, dv + ONES_ROWS, TK), lead4),
    ]
    return pl.pallas_call(
        _proj_even_kernel,
        out_shape=[jax.ShapeDtypeStruct(shape, dt) for shape, dt, _, _ in outs],
        grid=(t // tm,),
        in_specs=[
            pl.BlockSpec((tm, D_MODEL), row),
            pl.BlockSpec((6, D_MODEL), full2),
            pl.BlockSpec((D_MODEL, n), full2),
            pl.BlockSpec((1, D_LAT), full2),
            pl.BlockSpec((H_A, DH_A, D_LAT), lambda i: (0, 0, 0)),
        ],
        out_specs=[pl.BlockSpec(block, imap) for _, _, block, imap in outs],
        compiler_params=pltpu.CompilerParams(vmem_limit_bytes=VMEM_LIMIT),
        name="proj_even",
    )(x2, g, w, kvn.reshape(1, D_LAT), w_uk.astype(BF16))


def _proj_odd_kernel(x_ref, g_ref, w_ref, qt_ref, k_ref, vtx_ref):
    tm = x_ref.shape[0]
    h = _rms(x_ref[...], g_ref[0:1, :]).astype(BF16)
    z = _dot(h, w_ref[...])
    for kv in range(H_C_KV):
        k_ref[kv] = z[:, 1024 + kv * DH_C:1024 + (kv + 1) * DH_C].astype(BF16)
    for sb in range(tm // TQ):
        rows = slice(sb * TQ, (sb + 1) * TQ)
        for pr in range(H_C // 2):
            t = (z[rows, pr * LANES:(pr + 1) * LANES] * (DH_C ** -0.5 * LOG2E)).T
            for half in range(2):
                hd = 2 * pr + half
                kv, gq = hd // G_C, hd % G_C
                qt_ref[sb, kv, :, gq * TQ:(gq + 1) * TQ] = t[half * DH_C:(half + 1) * DH_C, :].astype(BF16)
        vt = z[rows, 1152:1280].T
        for kv in range(H_C_KV):
            vtx_ref[sb, kv, 0:DH_C, :] = vt[kv * DH_C:(kv + 1) * DH_C, :].astype(BF16)
            vtx_ref[sb, kv, DH_C:, :] = _ones_rows(TQ)


def _proj_odd(x2, g, w_in):
    t = x2.shape[0]
    tm = min(TM_PROJ, t)
    assert TQ == LANES
    row = lambda i: (i, 0)
    full2 = lambda i: (0, 0)
    lead4 = lambda i: (i, 0, 0, 0)
    dvx = DH_C + ONES_ROWS
    return pl.pallas_call(
        _proj_odd_kernel,
        out_shape=[jax.ShapeDtypeStruct((t // TQ, H_C_KV, DH_C, G_C * TQ), BF16),
                   jax.ShapeDtypeStruct((H_C_KV, t, DH_C), BF16),
                   jax.ShapeDtypeStruct((t // TQ, H_C_KV, dvx, TQ), BF16)],
        grid=(t // tm,),
        in_specs=[pl.BlockSpec((tm, D_MODEL), row),
                  pl.BlockSpec((6, D_MODEL), full2),
                  pl.BlockSpec((D_MODEL, 1280), full2)],
        out_specs=[pl.BlockSpec((tm // TQ, H_C_KV, DH_C, G_C * TQ), lead4),
                   pl.BlockSpec((H_C_KV, tm, DH_C), lambda i: (0, i, 0)),
                   pl.BlockSpec((tm // TQ, H_C_KV, dvx, TQ), lead4)],
        compiler_params=pltpu.CompilerParams(vmem_limit_bytes=VMEM_LIMIT),
        name="proj_odd",
    )(x2, g, w_in.astype(BF16))


def _dsa_kernel(qit_ref, wit_ref, qlt_ref, k_ref, c_ref, ctx_ref, bias_ref, wuvt_ref, o_ref,
                sc_ref, acc_ref, *p_refs, topk):
    TQ = TQ_DSA
    i = pl.program_id(1)
    t0 = i * TQ
    nt = (t0 + TQ + TK - 1) // TK
    n_qb = TQ // LANES
    n_far = jnp.maximum(i * n_qb - N_SAT + 1, 0) // SUB
    kf = float(topk)

    qpos = lax.broadcasted_iota(jnp.int32, (1, TQ), 1) + t0
    limit = (qpos // CHUNK + 1) * CHUNK
    small = limit <= topk
    big = jnp.logical_not(small)
    w = wit_ref[...]
    qit = qit_ref[...]

    def score_tiles(js, carry):
        rmax, rmin = carry
        zs = [_dot(k_ref[j], qit) for j in js]
        for j, z in zip(js, zs):
            sc = jnp.maximum(z[:, 0:TQ], 0.0) * w[0:1, :]
            for h in range(1, H_IDX):
                sc = sc + jnp.maximum(z[:, h * TQ:(h + 1) * TQ], 0.0) * w[h:h + 1, :]
            kpos = lax.broadcasted_iota(jnp.int32, (TK, TQ), 0) + j * TK
            adm = kpos < limit
            sc_ref[j] = jnp.where(adm, sc, -jnp.inf)
            rmax = jnp.maximum(rmax, _fold_rows(jnp.where(adm, sc, -jnp.inf), jnp.maximum))
            rmin = jnp.minimum(rmin, _fold_rows(jnp.where(adm, sc, jnp.inf), jnp.minimum))
        return rmax, rmin

    rmax, rmin = _grouped_loop(
        0, nt, (jnp.full((FOLD_ROWS, TQ), -jnp.inf, F32), jnp.full((FOLD_ROWS, TQ), jnp.inf, F32)),
        score_tiles, GROUPS_DSA)
    hi0 = jnp.max(rmax, axis=0, keepdims=True)
    lo0 = jnp.min(rmin, axis=0, keepdims=True)

    def col_reduce(tile_fn, op, init, final):
        def fn(js, acc):
            for j in js:
                acc = op(acc, _fold_rows(tile_fn(j), op))
            return acc
        acc = _grouped_loop(0, nt, jnp.full((FOLD_ROWS, TQ), init, F32), fn, GROUPS_PASS)
        return final(acc, axis=0, keepdims=True)

    def count_ge(thr):
        return col_reduce(lambda j: jnp.where(sc_ref[j] >= thr, 1.0, 0.0), jnp.add, 0.0, jnp.sum)

    def max_le(thr):
        def tile(j):
            t = sc_ref[j]
            return jnp.where(t <= thr, t, -jnp.inf)
        return col_reduce(tile, jnp.maximum, -jnp.inf, jnp.max)

    def max_lt(thr):
        def tile(j):
            t = sc_ref[j]
            return jnp.where(t < thr, t, -jnp.inf)
        return col_reduce(tile, jnp.maximum, -jnp.inf, jnp.max)

    def bisect_body(_, carry):
        lo, hi = carry
        mid = lo * 0.5 + hi * 0.5
        ge = count_ge(mid) >= kf
        return jnp.where(ge, mid, lo), jnp.where(ge, hi, mid)

    lo, hi = lax.fori_loop(0, N_BISECT, bisect_body, (lo0, hi0))

    v0 = max_le(hi)
    c0 = count_ge(v0)

    def walk_cond(carry):
        _, cnt, it = carry
        pending = jnp.where((cnt < kf) & big, 1.0, 0.0)
        return (jnp.max(pending) > 0.0) & (it < topk + 2)

    def walk_body(carry):
        v, cnt, it = carry
        v = jnp.where((cnt < kf) & big, max_lt(v), v)
        return v, count_ge(v), it + 1

    tau, cnt, _ = lax.while_loop(walk_cond, walk_body, (v0, c0, jnp.int32(0)))
    tau = jnp.where(small, -jnp.finfo(F32).max, tau)

    tied = (cnt > kf) & big

    @pl.when(jnp.max(jnp.where(tied, 1.0, 0.0)) > 0.0)
    def _():
        excess = jnp.where(tied, cnt - kf, 0.0)
        kr = lax.broadcasted_iota(jnp.int32, (TK, TK), 0)
        kc = lax.broadcasted_iota(jnp.int32, (TK, TK), 1)
        upper = jnp.where(kc >= kr, 1.0, 0.0).astype(BF16)

        def drop_body(u, later):
            j = nt - 1 - u
            t = sc_ref[j]
            e = (t == tau) & tied
            behind = _dot(upper, jnp.where(e, 1.0, 0.0).astype(BF16)) + later
            sc_ref[j] = jnp.where(e & (behind <= excess), -jnp.inf, t)
            return behind[0:1, :]

        lax.fori_loop(0, nt, drop_body, jnp.zeros((1, TQ), F32))

    acc_ref[...] = jnp.zeros(acc_ref.shape, F32)
    qlt = qlt_ref[...]

    def attn_tiles(js, m, with_bias):
        s_alls = [_dot(c_ref[j], qlt).astype(BF16) for j in js]
        for g, (j, s_all) in enumerate(zip(js, s_alls)):
            p_ref = p_refs[g]
            pen = jnp.where(sc_ref[j] >= tau, 0.0, NEG_INF).astype(BF16)
            m_new, alpha = [], []
            for h in range(H_A):
                cols = slice(h * TQ, (h + 1) * TQ)
                s = s_all[:, cols] + pen
                if with_bias:
                    s = s + _bias_tile(bias_ref, h, i * n_qb, n_qb, j).astype(BF16)
                m_old = m[:, cols]
                m_h = jnp.maximum(m_old, jnp.max(_fold_rows(s, jnp.maximum), axis=0, keepdims=True).astype(F32))
                p_ref[:, cols] = jnp.exp2(s - m_h.astype(BF16))
                alpha.append(jnp.exp2(m_old - m_h))
                m_new.append(m_h)
            acc_ref[...] = acc_ref[...] * jnp.concatenate(alpha, axis=1) + _dot(ctx_ref[j], p_ref[...])
            m = jnp.concatenate(m_new, axis=1)
        return m

    m = jnp.full((1, H_A * TQ), NEG_INF, F32)
    m = _grouped_loop(0, n_far, m, lambda js, mm: attn_tiles(js, mm, False), GROUPS_DSA)
    _grouped_loop(n_far, nt, m, lambda js, mm: attn_tiles(js, mm, True), GROUPS_NEAR)

    outs = []
    for h in range(H_A):
        cols = slice(h * TQ, (h + 1) * TQ)
        o_t = (acc_ref[0:D_LAT, cols] / acc_ref[D_LAT:D_LAT + 1, cols]).astype(BF16)
        outs.append(_dot(wuvt_ref[h], o_t))
    for pr in range(H_A // 2):
        pair = jnp.concatenate([outs[2 * pr], outs[2 * pr + 1]], axis=0)
        o_ref[:, pr * LANES:(pr + 1) * LANES] = pair.T.astype(BF16)


def _dsa(qit, wit, qlt, ki, ckv, ctx, bias_a, w_uv, b, s):
    TQ = TQ_DSA
    nq, nt = s // TQ, s // TK
    topk = min(TOPK_MAX, s // 4)
    k4 = ki.reshape(b, nt, TK, DH_IDX)
    c4 = ckv.reshape(b, nt, TK, D_LAT)
    ctx = ctx.reshape(b, nt, D_LAT + ONES_ROWS, TK)
    wuvt = jnp.swapaxes(w_uv, 1, 2).astype(BF16)
    qblk = lambda bb, i: (bb * nq + i, 0, 0)
    kv4 = lambda bb, i: (bb, 0, 0, 0)
    once = dict(pipeline_mode=pl.Buffered(1))
    return pl.pallas_call(
        functools.partial(_dsa_kernel, topk=topk),
        out_shape=jax.ShapeDtypeStruct((b * s, H_A * DH_A), BF16),
        grid=(b, nq),
        in_specs=[
            pl.BlockSpec((None, DH_IDX, H_IDX * TQ), qblk),
            pl.BlockSpec((None, H_IDX, TQ), qblk),
            pl.BlockSpec((None, D_LAT, H_A * TQ), qblk),
            pl.BlockSpec((None, nt, TK, DH_IDX), kv4, **once),
            pl.BlockSpec((None, nt, TK, D_LAT), kv4, **once),
            pl.BlockSpec((None, nt, D_LAT + ONES_ROWS, TK), kv4, **once),
            pl.BlockSpec((H_A * N_BT, LANES, LANES), lambda bb, i: (0, 0, 0), **once),
            pl.BlockSpec((H_A, DH_A, D_LAT), lambda bb, i: (0, 0, 0), **once),
        ],
        out_specs=pl.BlockSpec((TQ, H_A * DH_A), lambda bb, i: (bb * nq + i, 0)),
        scratch_shapes=[
            pltpu.VMEM((nt, TK, TQ), F32),
            pltpu.VMEM((D_LAT + ONES_ROWS, H_A * TQ), F32),
        ] + [pltpu.VMEM((TK, H_A * TQ), BF16)] * max(GROUPS_DSA + GROUPS_NEAR),
        compiler_params=pltpu.CompilerParams(vmem_limit_bytes=VMEM_LIMIT),
        name="dsa_attention",
    )(qit, wit, qlt, k4, c4, ctx, bias_a, wuvt)


def _diff_kernel(qt_ref, k_ref, vtx_ref, bias_ref, lam_ref, sub_ref, o_ref, acc_ref, *p_refs, lam_init):
    h = pl.program_id(1)
    i = pl.program_id(2)
    tq = TQ_DIFF
    n_qb = tq // LANES
    t0 = i * tq
    nt = (t0 + tq + TK - 1) // TK
    n_far = jnp.maximum(i * n_qb - N_SAT + 1, 0) // SUB
    qt = qt_ref[...]
    top = lax.broadcasted_iota(jnp.int32, qt.shape, 0) < DH_B
    q2 = jnp.concatenate([jnp.where(top, qt, jnp.zeros_like(qt)), jnp.where(top, jnp.zeros_like(qt), qt)], axis=1)
    qpos = lax.broadcasted_iota(jnp.int32, (1, tq), 1) + t0
    limit = (qpos // CHUNK + 1) * CHUNK
    acc_ref[...] = jnp.zeros(acc_ref.shape, F32)

    def tiles(js, m, near):
        s2s = [_dot(k_ref[j], q2).astype(BF16) for j in js]
        for g, (j, s2) in enumerate(zip(js, s2s)):
            p_ref = p_refs[g]
            if near:
                bias = _bias_tile(bias_ref, h, i * n_qb, n_qb, j).astype(BF16)
                adm = (lax.broadcasted_iota(jnp.int32, (TK, tq), 0) + j * TK) < limit
            m_new, alpha = [], []
            for mp in range(2):
                cols = slice(mp * tq, (mp + 1) * tq)
                s = s2[:, cols]
                if near:
                    s = jnp.where(adm, s + bias, jnp.asarray(NEG_INF, BF16))
                m_old = m[:, cols]
                m_h = jnp.maximum(m_old, jnp.max(_fold_rows(s, jnp.maximum), axis=0, keepdims=True).astype(F32))
                p_ref[:, cols] = jnp.exp2(s - m_h.astype(BF16))
                alpha.append(jnp.exp2(m_old - m_h))
                m_new.append(m_h)
            acc_ref[...] = acc_ref[...] * jnp.concatenate(alpha, axis=1) + _dot(vtx_ref[j], p_ref[...])
            m = jnp.concatenate(m_new, axis=1)
        return m

    m = jnp.full((1, 2 * tq), NEG_INF, F32)
    m = _grouped_loop(0, n_far, m, lambda js, mm: tiles(js, mm, False), GROUPS)
    _grouped_loop(n_far, nt, m, lambda js, mm: tiles(js, mm, True), GROUPS_NEAR)

    lf = lam_ref[...]
    lam = (jnp.exp(jnp.sum(lf[0:1, :] * lf[1:2, :], axis=1, keepdims=True))
           - jnp.exp(jnp.sum(lf[2:3, :] * lf[3:4, :], axis=1, keepdims=True)) + lam_init)
    dv = 2 * DH_B
    o = (acc_ref[0:dv, 0:tq] / acc_ref[dv:dv + 1, 0:tq]
         - lam * (acc_ref[0:dv, tq:2 * tq] / acc_ref[dv:dv + 1, tq:2 * tq]))
    o = o * lax.rsqrt(jnp.mean(o * o, axis=0, keepdims=True) + EPS) * sub_ref[...] * (1.0 - lam_init)
    o_ref[...] = o.T.astype(BF16)


def _diff(qbt, kb, vtx, bias_b, b_lambda, b_subln, lam_init, b, s):
    tq = TQ_DIFF
    nq, nt = s // tq, s // TK
    dv = 2 * DH_B
    k4 = kb.reshape(b, nt, TK, H_B * dv)
    vtx = vtx.reshape(b, nt, H_B, dv + ONES_ROWS, TK)
    return pl.pallas_call(
        functools.partial(_diff_kernel, lam_init=lam_init),
        out_shape=jax.ShapeDtypeStruct((b * s, H_B * dv), BF16),
        grid=(b, H_B, nq),
        in_specs=[
            pl.BlockSpec((None, None, dv, tq), lambda bb, h, i: (bb * nq + i, h, 0, 0)),
            pl.BlockSpec((None, nt, TK, dv), lambda bb, h, i: (bb, 0, 0, h)),
            pl.BlockSpec((None, nt, None, dv + ONES_ROWS, TK), lambda bb, h, i: (bb, 0, h, 0, 0)),
            pl.BlockSpec((H_B * N_BT, LANES, LANES), lambda bb, h, i: (0, 0, 0)),
            pl.BlockSpec((4, DH_B), lambda bb, h, i: (0, 0)),
            pl.BlockSpec((dv, 1), lambda bb, h, i: (0, 0)),
        ],
        out_specs=pl.BlockSpec((tq, dv), lambda bb, h, i: (bb * nq + i, h)),
        scratch_shapes=[
            pltpu.VMEM((dv + ONES_ROWS, 2 * tq), F32),
        ] + [pltpu.VMEM((TK, 2 * tq), BF16)] * max(GROUPS),
        compiler_params=pltpu.CompilerParams(vmem_limit_bytes=VMEM_LIMIT),
        name="diff_attention",
    )(qbt, k4, vtx, bias_b, b_lambda.astype(F32), b_subln.reshape(dv, 1).astype(F32))


def _swa_kernel(qt_ref, ka_ref, kb_ref, vta_ref, vtb_ref, bias_ref, sink_ref, o_ref):
    i = pl.program_id(1)
    pad_pen = jnp.where(i == 0, NEG_INF, 0.0)
    for kv in range(H_C_KV):
        qt = qt_ref[kv]
        s = jnp.concatenate([_dot(ka_ref[kv], qt) + pad_pen, _dot(kb_ref[kv], qt)], axis=0) + bias_ref[kv]
        sink = sink_ref[kv]
        m = jnp.maximum(jnp.max(s, axis=0, keepdims=True), sink)
        p = jnp.exp2(s - m).astype(BF16)
        acc = _dot(vta_ref[kv], p[0:TQ, :]) + _dot(vtb_ref[kv], p[TQ:2 * TQ, :])
        o = acc[0:DH_C, :] / (acc[DH_C:DH_C + 1, :] + jnp.exp2(sink - m))
        for pr in range(G_C // 2):
            pair = jnp.concatenate([o[:, (2 * pr) * TQ:(2 * pr + 1) * TQ], o[:, (2 * pr + 1) * TQ:(2 * pr + 2) * TQ]],
                                   axis=0)
            col = (kv * G_C + 2 * pr) * DH_C
            o_ref[:, col:col + LANES] = pair.T.astype(BF16)


def _swa(qt, k, vtx, table_c, sinks, b, s):
    assert WINDOW == TQ and TQ % CHUNK == 0
    nq = s // TQ
    length = 3 * TQ + 1
    m = np.arange(length)
    diff = np.where(m < 2 * TQ + 1, m, m - length)
    u = jnp.take(table_c.astype(F32), jnp.asarray(_t5_bucket_np(diff - TQ)), axis=0) * LOG2E
    bias = _toeplitz(jnp.moveaxis(u, -1, 0), TQ, 2 * TQ)
    rblk = np.arange(TQ)[:, None] // CHUNK + WINDOW // CHUNK
    cblk = np.arange(2 * TQ)[None, :] // CHUNK
    valid = (cblk <= rblk) & (cblk >= rblk - WINDOW // CHUNK)
    bias = jnp.where(jnp.asarray(valid)[None], bias, NEG_INF)
    bias_t = jnp.transpose(bias.reshape(H_C_KV, G_C, TQ, 2 * TQ), (0, 3, 1, 2)).reshape(H_C_KV, 2 * TQ, G_C * TQ)
    sink_row = jnp.repeat(sinks.astype(F32).reshape(H_C_KV, 1, G_C) * LOG2E, TQ, axis=2)

    dvx = DH_C + ONES_ROWS
    prev = lambda bb, i: bb * nq + jnp.maximum(i - 1, 0)
    return pl.pallas_call(
        _swa_kernel,
        out_shape=jax.ShapeDtypeStruct((b * s, H_C * DH_C), BF16),
        grid=(b, nq),
        in_specs=[
            pl.BlockSpec((None, H_C_KV, DH_C, G_C * TQ), lambda bb, i: (bb * nq + i, 0, 0, 0)),
            pl.BlockSpec((H_C_KV, TQ, DH_C), lambda bb, i: (0, prev(bb, i), 0)),
            pl.BlockSpec((H_C_KV, TQ, DH_C), lambda bb, i: (0, bb * nq + i, 0)),
            pl.BlockSpec((None, H_C_KV, dvx, TQ), lambda bb, i: (prev(bb, i), 0, 0, 0)),
            pl.BlockSpec((None, H_C_KV, dvx, TQ), lambda bb, i: (bb * nq + i, 0, 0, 0)),
            pl.BlockSpec((H_C_KV, 2 * TQ, G_C * TQ), lambda bb, i: (0, 0, 0)),
            pl.BlockSpec((H_C_KV, 1, G_C * TQ), lambda bb, i: (0, 0, 0)),
        ],
        out_specs=pl.BlockSpec((TQ, H_C * DH_C), lambda bb, i: (bb * nq + i, 0)),
        compiler_params=pltpu.CompilerParams(vmem_limit_bytes=VMEM_LIMIT),
        name="swa_attention",
    )(qt, k, k, vtx, vtx, bias_t, sink_row)


def _memkv_kernel(mem_ref, g_ref, w_ref, k_ref, v_ref):
    h = _rms(mem_ref[...], g_ref[...]).astype(BF16)
    kv = _dot(h, w_ref[...])
    k_ref[...] = kv[:, 0:H_X * DH_X].astype(BF16)
    v_ref[...] = kv[:, H_X * DH_X:].astype(BF16)


def _memkv(mem, g, wkv):
    b, m, _ = mem.shape
    n = H_X * DH_X
    return pl.pallas_call(
        _memkv_kernel,
        out_shape=[jax.ShapeDtypeStruct((b, m, n), BF16), jax.ShapeDtypeStruct((b, m, n), BF16)],
        grid=(b,),
        in_specs=[pl.BlockSpec((None, m, D_MODEL), lambda i: (i, 0, 0)),
                  pl.BlockSpec((1, D_MODEL), lambda i: (0, 0)),
                  pl.BlockSpec((D_MODEL, 2 * n), lambda i: (0, 0))],
        out_specs=[pl.BlockSpec((None, m, n), lambda i: (i, 0, 0)), pl.BlockSpec((None, m, n), lambda i: (i, 0, 0))],
        compiler_params=pltpu.CompilerParams(vmem_limit_bytes=VMEM_LIMIT),
        name="mem_kv",
    )(mem, g.reshape(1, D_MODEL), wkv.astype(BF16))


def _lane_half_mask(rows, half):
    lane = lax.broadcasted_iota(jnp.int32, (rows, LANES), 1)
    return (lane < 64) if half == 0 else (lane >= 64)


def _tail_kernel(*refs, n_a):
    x_ref, g_ref = refs[0], refs[1]
    a_refs = refs[2:2 + n_a]
    wout_refs = refs[2 + n_a:2 + 2 * n_a]
    wq_ref, kt_ref, v_ref, wo_ref, w1_ref, w2_ref, o_ref = refs[2 + 2 * n_a:]
    tc = x_ref.shape[0] // TAIL_CHAINS
    chains = [slice(c * tc, (c + 1) * tc) for c in range(TAIL_CHAINS)]
    g = g_ref[...]

    ys = []
    for r in chains:
        y = _dot(a_refs[0][r, :], wout_refs[0][...])
        for a_ref, w_ref in zip(a_refs[1:], wout_refs[1:]):
            y = y + _dot(a_ref[r, :], w_ref[...])
        ys.append(y)
    xs = [x_ref[r, :] + _rms(y, g[1:2, :]) for r, y in zip(chains, ys)]

    hqs = [_rms(x, g[2:3, :]).astype(BF16) for x in xs]
    qs = [_dot(hq, wq_ref[...]) for hq in hqs]
    qs = [(q * (DH_X ** -0.5)).astype(BF16) for q in qs]
    pairs = [[] for _ in chains]
    for pr in range(H_X // 2):
        vp = v_ref[:, pr * LANES:(pr + 1) * LANES]
        outs = [jnp.zeros((tc, LANES), F32) for _ in chains]
        for half in range(2):
            qzs = [jnp.where(_lane_half_mask(tc, half), q[:, pr * LANES:(pr + 1) * LANES], jnp.zeros((tc, LANES), BF16))
                   for q in qs]
            ss = [_dot(qz, kt_ref[pr * LANES:(pr + 1) * LANES, :]) for qz in qzs]
            vz = jnp.where(_lane_half_mask(vp.shape[0], half), vp, jnp.zeros_like(vp))
            ps = []
            for sc in ss:
                e = jnp.exp(sc - jnp.max(sc, axis=1, keepdims=True))
                ps.append((e / jnp.sum(e, axis=1, keepdims=True)).astype(BF16))
            outs = [out + _dot(p, vz) for out, p in zip(outs, ps)]
        for c, out in enumerate(outs):
            pairs[c].append(out.astype(BF16))
    ys = [_dot(jnp.concatenate(pc, axis=1), wo_ref[...]) for pc in pairs]
    xs = [x + _rms(y, g[3:4, :]) for x, y in zip(xs, ys)]

    hms = [_rms(x, g[4:5, :]).astype(BF16) for x in xs]
    n_chunks = D_FF // FF_CHUNK
    ys = [jnp.zeros((tc, D_MODEL), F32) for _ in chains]
    a_cur = [_dot(hm, w1_ref[:, 0:FF_CHUNK]) for hm in hms]
    for k in range(n_chunks):
        if k + 1 < n_chunks:
            a_next = [_dot(hm, w1_ref[:, (k + 1) * FF_CHUNK:(k + 2) * FF_CHUNK]) for hm in hms]
        acts = [jnp.square(jnp.maximum(a, 0.0)).astype(BF16) for a in a_cur]
        ys = [y + _dot(act, w2_ref[k * FF_CHUNK:(k + 1) * FF_CHUNK, :]) for y, act in zip(ys, acts)]
        if k + 1 < n_chunks:
            a_cur = a_next
    for r, x, y in zip(chains, xs, ys):
        o_ref[r, :] = x + _rms(y, g[5:6, :])


def _tail(x2, g, a_list, wout_list, wq, kt, v, wo, w1, w2, b, s):
    t = x2.shape[0]
    tm = min(TM_TAIL, s)
    per_b = s // tm
    n_a = len(a_list)
    row = lambda i: (i, 0)
    full2 = lambda i: (0, 0)
    once = dict(pipeline_mode=pl.Buffered(1))
    in_specs = [pl.BlockSpec((tm, D_MODEL), row), pl.BlockSpec((6, D_MODEL), full2)]
    in_specs += [pl.BlockSpec((tm, a.shape[1]), row) for a in a_list]
    in_specs += [pl.BlockSpec(w.shape, full2, **once) for w in wout_list]
    in_specs += [
        pl.BlockSpec(wq.shape, full2, **once),
        pl.BlockSpec((None,) + kt.shape[1:], lambda i: (i // per_b, 0, 0)),
        pl.BlockSpec((None,) + v.shape[1:], lambda i: (i // per_b, 0, 0)),
        pl.BlockSpec(wo.shape, full2, **once),
        pl.BlockSpec(w1.shape, full2, **once),
        pl.BlockSpec(w2.shape, full2, **once),
    ]
    return pl.pallas_call(
        functools.partial(_tail_kernel, n_a=n_a),
        out_shape=jax.ShapeDtypeStruct((t, D_MODEL), F32),
        grid=(t // tm,),
        in_specs=in_specs,
        out_specs=pl.BlockSpec((tm, D_MODEL), row),
        compiler_params=pltpu.CompilerParams(vmem_limit_bytes=VMEM_LIMIT),
        name="tail",
    )(x2, g, *a_list, *wout_list, wq, kt, v, wo, w1, w2)


def kernel(x, mem, rel_bias_table, norm_g, ev_w_in, ev_a_kv_norm, ev_a_w_uk, ev_a_w_uv, ev_b_lambda, ev_b_subln, ev_w_out, od_w_in, od_sinks, od_w_out, xa_wq, xa_wkv, xa_wo, xa_mem_norm, mlp_w1, mlp_w2):
    b, s, d = x.shape
    depth = norm_g.shape[0]
    assert d == D_MODEL and s % TK == 0 and TK % TQ_DIFF == 0 and TK % TQ_DSA == 0 and TQ_DSA % LANES == 0
    x2 = x.reshape(b * s, d)
    bias_a = _causal_bias_tiles(rel_bias_table[:, :H_A])
    bias_b = _causal_bias_tiles(rel_bias_table[:, H_A:H_A + H_B])
    table_c = rel_bias_table[:, H_A + H_B:]
    for l in range(depth):
        g = norm_g[l].astype(F32)
        if l % 2 == 0:
            e = l // 2
            lam_init = 0.8 - 0.6 * math.exp(-0.3 * l)
            qit, wit, qlt, ki, ckv, ctx, qbt, kb, vtx = _proj_even(x2, g, ev_w_in[e], ev_a_kv_norm[e], ev_a_w_uk[e])
            oa = _dsa(qit, wit, qlt, ki, ckv, ctx, bias_a, ev_a_w_uv[e], b, s)
            ob = _diff(qbt, kb, vtx, bias_b, ev_b_lambda[e], ev_b_subln[e], lam_init, b, s)
            n_a = H_A * DH_A
            a_list = [oa, ob]
            wout_list = [ev_w_out[e][:n_a].astype(BF16), ev_w_out[e][n_a:].astype(BF16)]
        else:
            o = l // 2
            qt, k, vtx = _proj_odd(x2, g, od_w_in[o])
            a_list = [_swa(qt, k, vtx, table_c, od_sinks[o], b, s)]
            wout_list = [od_w_out[o].astype(BF16)]
        mk, mv = _memkv(mem, xa_mem_norm[l], xa_wkv[l])
        x2 = _tail(x2, g, a_list, wout_list, xa_wq[l].astype(BF16), jnp.swapaxes(mk, 1, 2), mv,
                   xa_wo[l].astype(BF16), mlp_w1[l].astype(BF16), mlp_w2[l].astype(BF16), b, s)
    return x2.reshape(b, s, d)
```

```python
import functools
import math

import numpy as np
import jax
import jax.numpy as jnp
from jax import lax
from jax.experimental import pallas as pl
from jax.experimental.pallas import tpu as pltpu

F32 = jnp.float32
BF16 = jnp.bfloat16

D_MODEL = 1024
CHUNK = 64
EPS = 1e-6
NEG_INF = -1e30
LOG2E = math.log2(math.e)

H_A, DH_A, D_LAT = 8, 64, 128
H_IDX, DH_IDX = 8, 64
TOPK_MAX = 256
H_B, DH_B = 4, 64
H_C, H_C_KV, DH_C = 16, 2, 64
G_C = H_C // H_C_KV
WINDOW = 128
H_X, DH_X = 4, 64
D_FF = 4 * D_MODEL
N_BUCKETS = 32
MAX_DIST = 1024

LANES = 128
FOLD_ROWS = 64
TQ = 128
TQ_DSA = 256
TQ_DIFF = 512
TK = 512
SUB = TK // LANES
ONES_ROWS = 16
TM_PROJ = 512
TM_TAIL = 1024
TAIL_CHAINS = 2
FF_CHUNK = 1024
VMEM_LIMIT = 56 * 1024 * 1024
N_COARSE = 9
N_BISECT = 7
GROUPS = (4, 2, 1)
GROUPS_NEAR = (3, 2, 1)
GROUPS_DSA = (4, 2, 1)
GROUPS_PASS = (4, 1)


def _rms(x, g):
    return x * lax.rsqrt(jnp.mean(x * x, axis=-1, keepdims=True) + EPS) * g


def _dot(a, b):
    return jnp.dot(a, b, preferred_element_type=F32)


def _t5_bucket_np(rel):
    nb = N_BUCKETS // 2
    max_exact = nb // 2
    n = np.abs(rel)
    nf = np.maximum(n, 1).astype(np.float32)
    large = max_exact + (np.log(nf / max_exact) / math.log(MAX_DIST / max_exact) * (nb - max_exact)).astype(np.int32)
    large = np.minimum(large, nb - 1)
    return np.where(rel > 0, nb, 0) + np.where(n < max_exact, n, large)


def _sat_blocks():
    d = 1
    while _t5_bucket_np(np.array([-(d * LANES - (LANES - 1))]))[0] != N_BUCKETS // 2 - 1:
        d += 1
    return d


N_SAT = _sat_blocks()
N_BT = N_SAT + 1


def _toeplitz(u, rows, cols):
    length = u.shape[-1]
    flat = jnp.tile(u, (1,) * (u.ndim - 1) + (rows,))[..., :rows * (length - 1)]
    return flat.reshape(u.shape[:-1] + (rows, length - 1))[..., :cols]


def _causal_bias_tiles(table_cols):
    length = 2 * LANES + 1
    m = np.arange(length)
    diff = np.where(m < LANES + 1, m, m - length)
    rel = -diff[None, :] - (np.arange(N_BT) * LANES)[:, None]
    tab = table_cols.astype(F32)
    u = jnp.take(tab, jnp.asarray(_t5_bucket_np(rel)), axis=0)
    u = (u - tab[N_BUCKETS // 2 - 1][None, None, :]) * LOG2E
    tiles = _toeplitz(jnp.moveaxis(u, -1, 0), LANES, LANES)
    return tiles.reshape(table_cols.shape[1] * N_BT, LANES, LANES)


def _fold_rows(x, op):
    acc = x[0:FOLD_ROWS, :]
    for r in range(1, x.shape[0] // FOLD_ROWS):
        acc = op(acc, x[r * FOLD_ROWS:(r + 1) * FOLD_ROWS, :])
    return acc


def _grouped_loop(lo, hi, carry, fn, groups):
    for g in groups:
        n = (hi - lo) // g
        carry = lax.fori_loop(0, n, lambda u, c, lo=lo, g=g: fn([lo + u * g + k for k in range(g)], c), carry)
        lo = lo + n * g
    return carry


def _bias_tile(bias_ref, h, qb0, n_qb, j):
    rows = []
    for c in range(SUB):
        cols = [bias_ref[h * N_BT + jnp.clip(qb0 + r - (j * SUB + c), 0, N_SAT)] for r in range(n_qb)]
        rows.append(cols[0] if n_qb == 1 else jnp.concatenate(cols, axis=1))
    return jnp.concatenate(rows, axis=0)


def _ones_rows(n):
    return jnp.ones((ONES_ROWS, n), BF16)


def _proj_even_kernel(x_ref, g_ref, w_ref, kvn_ref, wuk_ref,
                      qit_ref, wit_ref, qlt_ref, ki_ref, c_ref, ctx_ref, qbt_ref, kb_ref, vtx_ref):
    tm = x_ref.shape[0]
    h = _rms(x_ref[...], g_ref[0:1, :]).astype(BF16)
    z = _dot(h, w_ref[...])
    qa = z[:, 0:512].astype(BF16)
    ckv = _rms(z[:, 512:640], kvn_ref[...])
    c_ref[...] = ckv.astype(BF16)
    kb_ref[...] = z[:, 1664:2176].astype(BF16)
    ki_ref[...] = z[:, 2688:2688 + DH_IDX].astype(BF16)
    ctx_ref[0, D_LAT:, :] = _ones_rows(tm)
    for hb in range(H_B):
        vtx_ref[0, hb, 2 * DH_B:, :] = _ones_rows(tm)
    per_q, per_diff = TQ_DSA // LANES, TQ_DIFF // LANES
    for hh in range(H_A):
        ql = _dot(qa[:, hh * DH_A:(hh + 1) * DH_A], wuk_ref[hh]) * (DH_A ** -0.5 * LOG2E)
        for sb in range(tm // LANES):
            off = hh * TQ_DSA + (sb % per_q) * LANES
            qlt_ref[sb // per_q, :, off:off + LANES] = ql[sb * LANES:(sb + 1) * LANES, :].T.astype(BF16)
    for sb in range(tm // LANES):
        rows = slice(sb * LANES, (sb + 1) * LANES)
        blk, sub = sb // per_q, (sb % per_q) * LANES
        for pr in range(H_IDX // 2):
            t = z[rows, 640 + pr * LANES:640 + (pr + 1) * LANES].T
            for half in range(2):
                off = (2 * pr + half) * TQ_DSA + sub
                qit_ref[blk, :, off:off + LANES] = t[half * DH_IDX:(half + 1) * DH_IDX, :].astype(BF16)
        wit_ref[blk, :, sub:sub + LANES] = z[rows, 2816:2816 + LANES].T[0:H_IDX, :]
        ctx_ref[0, 0:D_LAT, rows] = ckv[rows, :].T.astype(BF16)
        for hb in range(H_B):
            qb = z[rows, 1152 + hb * LANES:1152 + (hb + 1) * LANES] * (DH_B ** -0.5 * LOG2E)
            lanes = slice((sb % per_diff) * LANES, (sb % per_diff + 1) * LANES)
            qbt_ref[sb // per_diff, hb, :, lanes] = qb.T.astype(BF16)
            vtx_ref[0, hb, 0:2 * DH_B, rows] = z[rows, 2176 + hb * LANES:2176 + (hb + 1) * LANES].T.astype(BF16)


def _proj_even(x2, g, w_in, kvn, w_uk):
    t = x2.shape[0]
    tm = TK
    assert t % tm == 0 and tm % TQ_DSA == 0 and tm % TQ_DIFF == 0
    pad = jnp.zeros((D_MODEL, LANES - DH_IDX), w_in.dtype)
    cols = [w_in[:, 0:512], w_in[:, 512:640], w_in[:, 640:1152], w_in[:, 1224:1736], w_in[:, 1736:2248],
            w_in[:, 2248:2760], w_in[:, 1152:1216], pad, w_in[:, 1216:1224],
            jnp.zeros((D_MODEL, LANES - H_IDX), w_in.dtype)]
    w = jnp.concatenate(cols, axis=1).astype(BF16)
    n = w.shape[1]
    dv = 2 * DH_B
    row = lambda i: (i, 0)
    full2 = lambda i: (0, 0)
    lead3 = lambda i: (i, 0, 0)
    lead4 = lambda i: (i, 0, 0, 0)
    outs = [
        ((t // TQ_DSA, DH_IDX, H_IDX * TQ_DSA), BF16, (tm // TQ_DSA, DH_IDX, H_IDX * TQ_DSA), lead3),
        ((t // TQ_DSA, H_IDX, TQ_DSA), F32, (tm // TQ_DSA, H_IDX, TQ_DSA), lead3),
        ((t // TQ_DSA, D_LAT, H_A * TQ_DSA), BF16, (tm // TQ_DSA, D_LAT, H_A * TQ_DSA), lead3),
        ((t, DH_IDX), BF16, (tm, DH_IDX), row),
        ((t, D_LAT), BF16, (tm, D_LAT), row),
        ((t // TK, D_LAT + ONES_ROWS, TK), BF16, (1, D_LAT + ONES_ROWS, TK), lead3),
        ((t // TQ_DIFF, H_B, dv, TQ_DIFF), BF16, (tm // TQ_DIFF, H_B, dv, TQ_DIFF), lead4),
        ((t, H_B * dv), BF16, (tm, H_B * dv), row),
        ((t // TK, H_B, dv + ONES_ROWS, TK), BF16, (1, H_B, dv + ONES_ROWS, TK), lead4),
    ]
    return pl.pallas_call(
        _proj_even_kernel,
        out_shape=[jax.ShapeDtypeStruct(shape, dt) for shape, dt, _, _ in outs],
        grid=(t // tm,),
        in_specs=[
            pl.BlockSpec((tm, D_MODEL), row),
            pl.BlockSpec((6, D_MODEL), full2),
            pl.BlockSpec((D_MODEL, n), full2),
            pl.BlockSpec((1, D_LAT), full2),
            pl.BlockSpec((H_A, DH_A, D_LAT), lambda i: (0, 0, 0)),
        ],
        out_specs=[pl.BlockSpec(block, imap) for _, _, block, imap in outs],
        compiler_params=pltpu.CompilerParams(vmem_limit_bytes=VMEM_LIMIT),
        name="proj_even",
    )(x2, g, w, kvn.reshape(1, D_LAT), w_uk.astype(BF16))


def _proj_odd_kernel(x_ref, g_ref, w_ref, qt_ref, k_ref, vtx_ref):
    tm = x_ref.shape[0]
    h = _rms(x_ref[...], g_ref[0:1, :]).astype(BF16)
    z = _dot(h, w_ref[...])
    for kv in range(H_C_KV):
        k_ref[kv] = z[:, 1024 + kv * DH_C:1024 + (kv + 1) * DH_C].astype(BF16)
    for sb in range(tm // TQ):
        rows = slice(sb * TQ, (sb + 1) * TQ)
        for pr in range(H_C // 2):
            t = (z[rows, pr * LANES:(pr + 1) * LANES] * (DH_C ** -0.5 * LOG2E)).T
            for half in range(2):
                hd = 2 * pr + half
                kv, gq = hd // G_C, hd % G_C
                qt_ref[sb, kv, :, gq * TQ:(gq + 1) * TQ] = t[half * DH_C:(half + 1) * DH_C, :].astype(BF16)
        vt = z[rows, 1152:1280].T
        for kv in range(H_C_KV):
            vtx_ref[sb, kv, 0:DH_C, :] = vt[kv * DH_C:(kv + 1) * DH_C, :].astype(BF16)
            vtx_ref[sb, kv, DH_C:, :] = _ones_rows(TQ)


def _proj_odd(x2, g, w_in):
    t = x2.shape[0]
    tm = min(TM_PROJ, t)
    assert TQ == LANES
    row = lambda i: (i, 0)
    full2 = lambda i: (0, 0)
    lead4 = lambda i: (i, 0, 0, 0)
    dvx = DH_C + ONES_ROWS
    return pl.pallas_call(
        _proj_odd_kernel,
        out_shape=[jax.ShapeDtypeStruct((t // TQ, H_C_KV, DH_C, G_C * TQ), BF16),
                   jax.ShapeDtypeStruct((H_C_KV, t, DH_C), BF16),
                   jax.ShapeDtypeStruct((t // TQ, H_C_KV, dvx, TQ), BF16)],
        grid=(t // tm,),
        in_specs=[pl.BlockSpec((tm, D_MODEL), row),
                  pl.BlockSpec((6, D_MODEL), full2),
                  pl.BlockSpec((D_MODEL, 1280), full2)],
        out_specs=[pl.BlockSpec((tm // TQ, H_C_KV, DH_C, G_C * TQ), lead4),
                   pl.BlockSpec((H_C_KV, tm, DH_C), lambda i: (0, i, 0)),
                   pl.BlockSpec((tm // TQ, H_C_KV, dvx, TQ), lead4)],
        compiler_params=pltpu.CompilerParams(vmem_limit_bytes=VMEM_LIMIT),
        name="proj_odd",
    )(x2, g, w_in.astype(BF16))


def _dsa_kernel(qit_ref, wit_ref, qlt_ref, k_ref, c_ref, ctx_ref, bias_ref, wuvt_ref, o_ref,
                sc_ref, scb_ref, acc_ref, *p_refs, topk):
    TQ = TQ_DSA
    i = pl.program_id(1)
    t0 = i * TQ
    nt = (t0 + TQ + TK - 1) // TK
    n_qb = TQ // LANES
    n_far = jnp.maximum(i * n_qb - N_SAT + 1, 0) // SUB
    kf = float(topk)

    qpos = lax.broadcasted_iota(jnp.int32, (1, TQ), 1) + t0
    limit = (qpos // CHUNK + 1) * CHUNK
    small = limit <= topk
    big = jnp.logical_not(small)
    w = wit_ref[...]
    qit = qit_ref[...]

    def score_tiles(js, carry):
        rmax, rmin = carry
        zs = [_dot(k_ref[j], qit) for j in js]
        for j, z in zip(js, zs):
            sc = jnp.maximum(z[:, 0:TQ], 0.0) * w[0:1, :]
            for h in range(1, H_IDX):
                sc = sc + jnp.maximum(z[:, h * TQ:(h + 1) * TQ], 0.0) * w[h:h + 1, :]
            kpos = lax.broadcasted_iota(jnp.int32, (TK, TQ), 0) + j * TK
            adm = kpos < limit
            sc_ref[j] = jnp.where(adm, sc, -jnp.inf)
            scb_ref[j] = jnp.where(adm, sc, -jnp.inf).astype(BF16)
            rmax = jnp.maximum(rmax, _fold_rows(jnp.where(adm, sc, -jnp.inf), jnp.maximum))
            rmin = jnp.minimum(rmin, _fold_rows(jnp.where(adm, sc, jnp.inf), jnp.minimum))
        return rmax, rmin

    rmax, rmin = _grouped_loop(
        0, nt, (jnp.full((FOLD_ROWS, TQ), -jnp.inf, F32), jnp.full((FOLD_ROWS, TQ), jnp.inf, F32)),
        score_tiles, GROUPS_DSA)
    hi0 = jnp.max(rmax, axis=0, keepdims=True)
    lo0 = jnp.min(rmin, axis=0, keepdims=True)

    def col_reduce(tile_fn, op, init, final):
        def fn(js, acc):
            for j in js:
                acc = op(acc, _fold_rows(tile_fn(j), op))
            return acc
        acc = _grouped_loop(0, nt, jnp.full((FOLD_ROWS, TQ), init, F32), fn, GROUPS_PASS)
        return final(acc, axis=0, keepdims=True)

    def count_ge(thr):
        return col_reduce(lambda j: jnp.where(sc_ref[j] >= thr, 1.0, 0.0), jnp.add, 0.0, jnp.sum)

    def max_le(thr):
        def tile(j):
            t = sc_ref[j]
            return jnp.where(t <= thr, t, -jnp.inf)
        return col_reduce(tile, jnp.maximum, -jnp.inf, jnp.max)

    def max_lt(thr):
        def tile(j):
            t = sc_ref[j]
            return jnp.where(t < thr, t, -jnp.inf)
        return col_reduce(tile, jnp.maximum, -jnp.inf, jnp.max)

    def count_ge_coarse(thr):
        def fn(js, acc):
            for j in js:
                hit = jnp.where(scb_ref[j] >= thr, jnp.ones((), BF16), jnp.zeros((), BF16))
                acc = acc + _fold_rows(hit, jnp.add)
            return acc
        acc = _grouped_loop(0, nt, jnp.zeros((FOLD_ROWS, TQ), BF16), fn, GROUPS_PASS)
        return jnp.sum(acc.astype(F32), axis=0, keepdims=True)

    def margin(v):
        return jnp.abs(v) * 2.0 ** -6 + 1e-30

    def coarse_body(_, carry):
        lo, hi = carry
        mid = (lo * 0.5 + hi * 0.5).astype(BF16)
        ge = count_ge_coarse(mid) >= kf
        mid = mid.astype(F32)
        return jnp.where(ge, mid, lo), jnp.where(ge, hi, mid)

    lo_c, hi_c = lax.fori_loop(0, N_COARSE, coarse_body,
                               ((lo0 - margin(lo0)).astype(BF16).astype(F32),
                                (hi0 + margin(hi0)).astype(BF16).astype(F32)))

    def bisect_body(_, carry):
        lo, hi = carry
        mid = lo * 0.5 + hi * 0.5
        ge = count_ge(mid) >= kf
        return jnp.where(ge, mid, lo), jnp.where(ge, hi, mid)

    lo, hi = lax.fori_loop(0, N_BISECT, bisect_body,
                           (jnp.maximum(lo_c - margin(lo_c), lo0), jnp.minimum(hi_c, hi0)))

    v0 = max_le(hi)
    c0 = count_ge(v0)

    def walk_cond(carry):
        _, cnt, it = carry
        pending = jnp.where((cnt < kf) & big, 1.0, 0.0)
        return (jnp.max(pending) > 0.0) & (it < topk + 2)

    def walk_body(carry):
        v, cnt, it = carry
        v = jnp.where((cnt < kf) & big, max_lt(v), v)
        return v, count_ge(v), it + 1

    tau, cnt, _ = lax.while_loop(walk_cond, walk_body, (v0, c0, jnp.int32(0)))
    tau = jnp.where(small, -jnp.finfo(F32).max, tau)

    tied = (cnt > kf) & big

    @pl.when(jnp.max(jnp.where(tied, 1.0, 0.0)) > 0.0)
    def _():
        excess = jnp.where(tied, cnt - kf, 0.0)
        kr = lax.broadcasted_iota(jnp.int32, (TK, TK), 0)
        kc = lax.broadcasted_iota(jnp.int32, (TK, TK), 1)
        upper = jnp.where(kc >= kr, 1.0, 0.0).astype(BF16)

        def drop_body(u, later):
            j = nt - 1 - u
            t = sc_ref[j]
            e = (t == tau) & tied
            behind = _dot(upper, jnp.where(e, 1.0, 0.0).astype(BF16)) + later
            sc_ref[j] = jnp.where(e & (behind <= excess), -jnp.inf, t)
            return behind[0:1, :]

        lax.fori_loop(0, nt, drop_body, jnp.zeros((1, TQ), F32))

    acc_ref[...] = jnp.zeros(acc_ref.shape, F32)
    qlt = qlt_ref[...]

    def attn_tiles(js, m, with_bias):
        s_alls = [_dot(c_ref[j], qlt).astype(BF16) for j in js]
        for g, (j, s_all) in enumerate(zip(js, s_alls)):
            p_ref = p_refs[g]
            pen = jnp.where(sc_ref[j] >= tau, 0.0, NEG_INF).astype(BF16)
            m_new, alpha = [], []
            for h in range(H_A):
                cols = slice(h * TQ, (h + 1) * TQ)
                s = s_all[:, cols] + pen
                if with_bias:
                    s = s + _bias_tile(bias_ref, h, i * n_qb, n_qb, j).astype(BF16)
                m_old = m[:, cols]
                m_h = jnp.maximum(m_old, jnp.max(_fold_rows(s, jnp.maximum), axis=0, keepdims=True).astype(F32))
                p_ref[:, cols] = jnp.exp2(s - m_h.astype(BF16))
                alpha.append(jnp.exp2(m_old - m_h))
                m_new.append(m_h)
            acc_ref[...] = acc_ref[...] * jnp.concatenate(alpha, axis=1) + _dot(ctx_ref[j], p_ref[...])
            m = jnp.concatenate(m_new, axis=1)
        return m

    m = jnp.full((1, H_A * TQ), NEG_INF, F32)
    m = _grouped_loop(0, n_far, m, lambda js, mm: attn_tiles(js, mm, False), GROUPS_DSA)
    _grouped_loop(n_far, nt, m, lambda js, mm: attn_tiles(js, mm, True), GROUPS_NEAR)

    outs = []
    for h in range(H_A):
        cols = slice(h * TQ, (h + 1) * TQ)
        o_t = (acc_ref[0:D_LAT, cols] / acc_ref[D_LAT:D_LAT + 1, cols]).astype(BF16)
        outs.append(_dot(wuvt_ref[h], o_t))
    for pr in range(H_A // 2):
        pair = jnp.concatenate([outs[2 * pr], outs[2 * pr + 1]], axis=0)
        o_ref[:, pr * LANES:(pr + 1) * LANES] = pair.T.astype(BF16)


def _dsa(qit, wit, qlt, ki, ckv, ctx, bias_a, w_uv, b, s):
    TQ = TQ_DSA
    nq, nt = s // TQ, s // TK
    topk = min(TOPK_MAX, s // 4)
    assert s // FOLD_ROWS <= 256
    k4 = ki.reshape(b, nt, TK, DH_IDX)
    c4 = ckv.reshape(b, nt, TK, D_LAT)
    ctx = ctx.reshape(b, nt, D_LAT + ONES_ROWS, TK)
    wuvt = jnp.swapaxes(w_uv, 1, 2).astype(BF16)
    qblk = lambda bb, i: (bb * nq + i, 0, 0)
    kv4 = lambda bb, i: (bb, 0, 0, 0)
    once = dict(pipeline_mode=pl.Buffered(1))
    return pl.pallas_call(
        functools.partial(_dsa_kernel, topk=topk),
        out_shape=jax.ShapeDtypeStruct((b * s, H_A * DH_A), BF16),
        grid=(b, nq),
        in_specs=[
            pl.BlockSpec((None, DH_IDX, H_IDX * TQ), qblk),
            pl.BlockSpec((None, H_IDX, TQ), qblk),
            pl.BlockSpec((None, D_LAT, H_A * TQ), qblk),
            pl.BlockSpec((None, nt, TK, DH_IDX), kv4, **once),
            pl.BlockSpec((None, nt, TK, D_LAT), kv4, **once),
            pl.BlockSpec((None, nt, D_LAT + ONES_ROWS, TK), kv4, **once),
            pl.BlockSpec((H_A * N_BT, LANES, LANES), lambda bb, i: (0, 0, 0), **once),
            pl.BlockSpec((H_A, DH_A, D_LAT), lambda bb, i: (0, 0, 0), **once),
        ],
        out_specs=pl.BlockSpec((TQ, H_A * DH_A), lambda bb, i: (bb * nq + i, 0)),
        scratch_shapes=[
            pltpu.VMEM((nt, TK, TQ), F32),
            pltpu.VMEM((nt, TK, TQ), BF16),
            pltpu.VMEM((D_LAT + ONES_ROWS, H_A * TQ), F32),
        ] + [pltpu.VMEM((TK, H_A * TQ), BF16)] * max(GROUPS_DSA + GROUPS_NEAR),
        compiler_params=pltpu.CompilerParams(vmem_limit_bytes=VMEM_LIMIT),
        name="dsa_attention",
    )(qit, wit, qlt, k4, c4, ctx, bias_a, wuvt)


def _diff_kernel(qt_ref, k_ref, vtx_ref, bias_ref, lam_ref, sub_ref, o_ref, acc_ref, *p_refs, lam_init):
    h = pl.program_id(1)
    i = pl.program_id(2)
    tq = TQ_DIFF
    n_qb = tq // LANES
    t0 = i * tq
    nt = (t0 + tq + TK - 1) // TK
    n_far = jnp.maximum(i * n_qb - N_SAT + 1, 0) // SUB
    qt = qt_ref[...]
    top = lax.broadcasted_iota(jnp.int32, qt.shape, 0) < DH_B
    q2 = jnp.concatenate([jnp.where(top, qt, jnp.zeros_like(qt)), jnp.where(top, jnp.zeros_like(qt), qt)], axis=1)
    qpos = lax.broadcasted_iota(jnp.int32, (1, tq), 1) + t0
    limit = (qpos // CHUNK + 1) * CHUNK
    acc_ref[...] = jnp.zeros(acc_ref.shape, F32)

    def tiles(js, m, near):
        s2s = [_dot(k_ref[j], q2).astype(BF16) for j in js]
        for g, (j, s2) in enumerate(zip(js, s2s)):
            p_ref = p_refs[g]
            if near:
                bias = _bias_tile(bias_ref, h, i * n_qb, n_qb, j).astype(BF16)
                adm = (lax.broadcasted_iota(jnp.int32, (TK, tq), 0) + j * TK) < limit
            m_new, alpha = [], []
            for mp in range(2):
                cols = slice(mp * tq, (mp + 1) * tq)
                s = s2[:, cols]
                if near:
                    s = jnp.where(adm, s + bias, jnp.asarray(NEG_INF, BF16))
                m_old = m[:, cols]
                m_h = jnp.maximum(m_old, jnp.max(_fold_rows(s, jnp.maximum), axis=0, keepdims=True).astype(F32))
                p_ref[:, cols] = jnp.exp2(s - m_h.astype(BF16))
                alpha.append(jnp.exp2(m_old - m_h))
                m_new.append(m_h)
            acc_ref[...] = acc_ref[...] * jnp.concatenate(alpha, axis=1) + _dot(vtx_ref[j], p_ref[...])
            m = jnp.concatenate(m_new, axis=1)
        return m

    m = jnp.full((1, 2 * tq), NEG_INF, F32)
    m = _grouped_loop(0, n_far, m, lambda js, mm: tiles(js, mm, False), GROUPS)
    _grouped_loop(n_far, nt, m, lambda js, mm: tiles(js, mm, True), GROUPS_NEAR)

    lf = lam_ref[...]
    lam = (jnp.exp(jnp.sum(lf[0:1, :] * lf[1:2, :], axis=1, keepdims=True))
           - jnp.exp(jnp.sum(lf[2:3, :] * lf[3:4, :], axis=1, keepdims=True)) + lam_init)
    dv = 2 * DH_B
    o = (acc_ref[0:dv, 0:tq] / acc_ref[dv:dv + 1, 0:tq]
         - lam * (acc_ref[0:dv, tq:2 * tq] / acc_ref[dv:dv + 1, tq:2 * tq]))
    o = o * lax.rsqrt(jnp.mean(o * o, axis=0, keepdims=True) + EPS) * sub_ref[...] * (1.0 - lam_init)
    o_ref[...] = o.T.astype(BF16)


def _diff(qbt, kb, vtx, bias_b, b_lambda, b_subln, lam_init, b, s):
    tq = TQ_DIFF
    nq, nt = s // tq, s // TK
    dv = 2 * DH_B
    k4 = kb.reshape(b, nt, TK, H_B * dv)
    vtx = vtx.reshape(b, nt, H_B, dv + ONES_ROWS, TK)
    return pl.pallas_call(
        functools.partial(_diff_kernel, lam_init=lam_init),
        out_shape=jax.ShapeDtypeStruct((b * s, H_B * dv), BF16),
        grid=(b, H_B, nq),
        in_specs=[
            pl.BlockSpec((None, None, dv, tq), lambda bb, h, i: (bb * nq + i, h, 0, 0)),
            pl.BlockSpec((None, nt, TK, dv), lambda bb, h, i: (bb, 0, 0, h)),
            pl.BlockSpec((None, nt, None, dv + ONES_ROWS, TK), lambda bb, h, i: (bb, 0, h, 0, 0)),
            pl.BlockSpec((H_B * N_BT, LANES, LANES), lambda bb, h, i: (0, 0, 0)),
            pl.BlockSpec((4, DH_B), lambda bb, h, i: (0, 0)),
            pl.BlockSpec((dv, 1), lambda bb, h, i: (0, 0)),
        ],
        out_specs=pl.BlockSpec((tq, dv), lambda bb, h, i: (bb * nq + i, h)),
        scratch_shapes=[
            pltpu.VMEM((dv + ONES_ROWS, 2 * tq), F32),
        ] + [pltpu.VMEM((TK, 2 * tq), BF16)] * max(GROUPS),
        compiler_params=pltpu.CompilerParams(vmem_limit_bytes=VMEM_LIMIT),
        name="diff_attention",
    )(qbt, k4, vtx, bias_b, b_lambda.astype(F32), b_subln.reshape(dv, 1).astype(F32))


def _swa_kernel(qt_ref, ka_ref, kb_ref, vta_ref, vtb_ref, bias_ref, sink_ref, o_ref):
    i = pl.program_id(1)
    pad_pen = jnp.where(i == 0, NEG_INF, 0.0)
    for kv in range(H_C_KV):
        qt = qt_ref[kv]
        s = jnp.concatenate([_dot(ka_ref[kv], qt) + pad_pen, _dot(kb_ref[kv], qt)], axis=0) + bias_ref[kv]
        sink = sink_ref[kv]
        m = jnp.maximum(jnp.max(s, axis=0, keepdims=True), sink)
        p = jnp.exp2(s - m).astype(BF16)
        acc = _dot(vta_ref[kv], p[0:TQ, :]) + _dot(vtb_ref[kv], p[TQ:2 * TQ, :])
        o = acc[0:DH_C, :] / (acc[DH_C:DH_C + 1, :] + jnp.exp2(sink - m))
        for pr in range(G_C // 2):
            pair = jnp.concatenate([o[:, (2 * pr) * TQ:(2 * pr + 1) * TQ], o[:, (2 * pr + 1) * TQ:(2 * pr + 2) * TQ]],
                                   axis=0)
            col = (kv * G_C + 2 * pr) * DH_C
            o_ref[:, col:col + LANES] = pair.T.astype(BF16)


def _swa(qt, k, vtx, table_c, sinks, b, s):
    assert WINDOW == TQ and TQ % CHUNK == 0
    nq = s // TQ
    length = 3 * TQ + 1
    m = np.arange(length)
    diff = np.where(m < 2 * TQ + 1, m, m - length)
    u = jnp.take(table_c.astype(F32), jnp.asarray(_t5_bucket_np(diff - TQ)), axis=0) * LOG2E
    bias = _toeplitz(jnp.moveaxis(u, -1, 0), TQ, 2 * TQ)
    rblk = np.arange(TQ)[:, None] // CHUNK + WINDOW // CHUNK
    cblk = np.arange(2 * TQ)[None, :] // CHUNK
    valid = (cblk <= rblk) & (cblk >= rblk - WINDOW // CHUNK)
    bias = jnp.where(jnp.asarray(valid)[None], bias, NEG_INF)
    bias_t = jnp.transpose(bias.reshape(H_C_KV, G_C, TQ, 2 * TQ), (0, 3, 1, 2)).reshape(H_C_KV, 2 * TQ, G_C * TQ)
    sink_row = jnp.repeat(sinks.astype(F32).reshape(H_C_KV, 1, G_C) * LOG2E, TQ, axis=2)

    dvx = DH_C + ONES_ROWS
    prev = lambda bb, i: bb * nq + jnp.maximum(i - 1, 0)
    return pl.pallas_call(
        _swa_kernel,
        out_shape=jax.ShapeDtypeStruct((b * s, H_C * DH_C), BF16),
        grid=(b, nq),
        in_specs=[
            pl.BlockSpec((None, H_C_KV, DH_C, G_C * TQ), lambda bb, i: (bb * nq + i, 0, 0, 0)),
            pl.BlockSpec((H_C_KV, TQ, DH_C), lambda bb, i: (0, prev(bb, i), 0)),
            pl.BlockSpec((H_C_KV, TQ, DH_C), lambda bb, i: (0, bb * nq + i, 0)),
            pl.BlockSpec((None, H_C_KV, dvx, TQ), lambda bb, i: (prev(bb, i), 0, 0, 0)),
            pl.BlockSpec((None, H_C_KV, dvx, TQ), lambda bb, i: (bb * nq + i, 0, 0, 0)),
            pl.BlockSpec((H_C_KV, 2 * TQ, G_C * TQ), lambda bb, i: (0, 0, 0)),
            pl.BlockSpec((H_C_KV, 1, G_C * TQ), lambda bb, i: (0, 0, 0)),
        ],
        out_specs=pl.BlockSpec((TQ, H_C * DH_C), lambda bb, i: (bb * nq + i, 0)),
        compiler_params=pltpu.CompilerParams(vmem_limit_bytes=VMEM_LIMIT),
        name="swa_attention",
    )(qt, k, k, vtx, vtx, bias_t, sink_row)


def _memkv_kernel(mem_ref, g_ref, w_ref, k_ref, v_ref):
    h = _rms(mem_ref[...], g_ref[...]).astype(BF16)
    kv = _dot(h, w_ref[...])
    k_ref[...] = kv[:, 0:H_X * DH_X].astype(BF16)
    v_ref[...] = kv[:, H_X * DH_X:].astype(BF16)


def _memkv(mem, g, wkv):
    b, m, _ = mem.shape
    n = H_X * DH_X
    return pl.pallas_call(
        _memkv_kernel,
        out_shape=[jax.ShapeDtypeStruct((b, m, n), BF16), jax.ShapeDtypeStruct((b, m, n), BF16)],
        grid=(b,),
        in_specs=[pl.BlockSpec((None, m, D_MODEL), lambda i: (i, 0, 0)),
                  pl.BlockSpec((1, D_MODEL), lambda i: (0, 0)),
                  pl.BlockSpec((D_MODEL, 2 * n), lambda i: (0, 0))],
        out_specs=[pl.BlockSpec((None, m, n), lambda i: (i, 0, 0)), pl.BlockSpec((None, m, n), lambda i: (i, 0, 0))],
        compiler_params=pltpu.CompilerParams(vmem_limit_bytes=VMEM_LIMIT),
        name="mem_kv",
    )(mem, g.reshape(1, D_MODEL), wkv.astype(BF16))


def _lane_half_mask(rows, half):
    lane = lax.broadcasted_iota(jnp.int32, (rows, LANES), 1)
    return (lane < 64) if half == 0 else (lane >= 64)


def _tail_kernel(*refs, n_a):
    x_ref, g_ref = refs[0], refs[1]
    a_refs = refs[2:2 + n_a]
    wout_refs = refs[2 + n_a:2 + 2 * n_a]
    wq_ref, kt_ref, v_ref, wo_ref, w1_ref, w2_ref, o_ref = refs[2 + 2 * n_a:]
    tc = x_ref.shape[0] // TAIL_CHAINS
    chains = [slice(c * tc, (c + 1) * tc) for c in range(TAIL_CHAINS)]
    g = g_ref[...]

    ys = []
    for r in chains:
        y = _dot(a_refs[0][r, :], wout_refs[0][...])
        for a_ref, w_ref in zip(a_refs[1:], wout_refs[1:]):
            y = y + _dot(a_ref[r, :], w_ref[...])
        ys.append(y)
    xs = [x_ref[r, :] + _rms(y, g[1:2, :]) for r, y in zip(chains, ys)]

    hqs = [_rms(x, g[2:3, :]).astype(BF16) for x in xs]
    qs = [_dot(hq, wq_ref[...]) for hq in hqs]
    qs = [(q * (DH_X ** -0.5)).astype(BF16) for q in qs]
    pairs = [[] for _ in chains]
    for pr in range(H_X // 2):
        vp = v_ref[:, pr * LANES:(pr + 1) * LANES]
        outs = [jnp.zeros((tc, LANES), F32) for _ in chains]
        for half in range(2):
            qzs = [jnp.where(_lane_half_mask(tc, half), q[:, pr * LANES:(pr + 1) * LANES], jnp.zeros((tc, LANES), BF16))
                   for q in qs]
            ss = [_dot(qz, kt_ref[pr * LANES:(pr + 1) * LANES, :]) for qz in qzs]
            vz = jnp.where(_lane_half_mask(vp.shape[0], half), vp, jnp.zeros_like(vp))
            ps = []
            for sc in ss:
                e = jnp.exp(sc - jnp.max(sc, axis=1, keepdims=True))
                ps.append((e / jnp.sum(e, axis=1, keepdims=True)).astype(BF16))
            outs = [out + _dot(p, vz) for out, p in zip(outs, ps)]
        for c, out in enumerate(outs):
            pairs[c].append(out.astype(BF16))
    ys = [_dot(jnp.concatenate(pc, axis=1), wo_ref[...]) for pc in pairs]
    xs = [x + _rms(y, g[3:4, :]) for x, y in zip(xs, ys)]

    hms = [_rms(x, g[4:5, :]).astype(BF16) for x in xs]
    n_chunks = D_FF // FF_CHUNK
    ys = [jnp.zeros((tc, D_MODEL), F32) for _ in chains]
    a_cur = [_dot(hm, w1_ref[:, 0:FF_CHUNK]) for hm in hms]
    for k in range(n_chunks):
        if k + 1 < n_chunks:
            a_next = [_dot(hm, w1_ref[:, (k + 1) * FF_CHUNK:(k + 2) * FF_CHUNK]) for hm in hms]
        acts = [jnp.square(jnp.maximum(a, 0.0)).astype(BF16) for a in a_cur]
        ys = [y + _dot(act, w2_ref[k * FF_CHUNK:(k + 1) * FF_CHUNK, :]) for y, act in zip(ys, acts)]
        if k + 1 < n_chunks:
            a_cur = a_next
    for r, x, y in zip(chains, xs, ys):
        o_ref[r, :] = x + _rms(y, g[5:6, :])


def _tail(x2, g, a_list, wout_list, wq, kt, v, wo, w1, w2, b, s):
    t = x2.shape[0]
    tm = min(TM_TAIL, s)
    per_b = s // tm
    n_a = len(a_list)
    row = lambda i: (i, 0)
    full2 = lambda i: (0, 0)
    once = dict(pipeline_mode=pl.Buffered(1))
    in_specs = [pl.BlockSpec((tm, D_MODEL), row), pl.BlockSpec((6, D_MODEL), full2)]
    in_specs += [pl.BlockSpec((tm, a.shape[1]), row) for a in a_list]
    in_specs += [pl.BlockSpec(w.shape, full2, **once) for w in wout_list]
    in_specs += [
        pl.BlockSpec(wq.shape, full2, **once),
        pl.BlockSpec((None,) + kt.shape[1:], lambda i: (i // per_b, 0, 0)),
        pl.BlockSpec((None,) + v.shape[1:], lambda i: (i // per_b, 0, 0)),
        pl.BlockSpec(wo.shape, full2, **once),
        pl.BlockSpec(w1.shape, full2, **once),
        pl.BlockSpec(w2.shape, full2, **once),
    ]
    return pl.pallas_call(
        functools.partial(_tail_kernel, n_a=n_a),
        out_shape=jax.ShapeDtypeStruct((t, D_MODEL), F32),
        grid=(t // tm,),
        in_specs=in_specs,
        out_specs=pl.BlockSpec((tm, D_MODEL), row),
        compiler_params=pltpu.CompilerParams(vmem_limit_bytes=VMEM_LIMIT),
        name="tail",
    )(x2, g, *a_list, *wout_list, wq, kt, v, wo, w1, w2)


def kernel(x, mem, rel_bias_table, norm_g, ev_w_in, ev_a_kv_norm, ev_a_w_uk, ev_a_w_uv, ev_b_lambda, ev_b_subln, ev_w_out, od_w_in, od_sinks, od_w_out, xa_wq, xa_wkv, xa_wo, xa_mem_norm, mlp_w1, mlp_w2):
    b, s, d = x.shape
    depth = norm_g.shape[0]
    assert d == D_MODEL and s % TK == 0 and TK % TQ_DIFF == 0 and TK % TQ_DSA == 0 and TQ_DSA % LANES == 0
    x2 = x.reshape(b * s, d)
    bias_a = _causal_bias_tiles(rel_bias_table[:, :H_A])
    bias_b = _causal_bias_tiles(rel_bias_table[:, H_A:H_A + H_B])
    table_c = rel_bias_table[:, H_A + H_B:]
    for l in range(depth):
        g = norm_g[l].astype(F32)
        if l % 2 == 0:
            e = l // 2
            lam_init = 0.8 - 0.6 * math.exp(-0.3 * l)
            qit, wit, qlt, ki, ckv, ctx, qbt, kb, vtx = _proj_even(x2, g, ev_w_in[e], ev_a_kv_norm[e], ev_a_w_uk[e])
            oa = _dsa(qit, wit, qlt, ki, ckv, ctx, bias_a, ev_a_w_uv[e], b, s)
            ob = _diff(qbt, kb, vtx, bias_b, ev_b_lambda[e], ev_b_subln[e], lam_init, b, s)
            n_a = H_A * DH_A
            a_list = [oa, ob]
            wout_list = [ev_w_out[e][:n_a].astype(BF16), ev_w_out[e][n_a:].astype(BF16)]
        else:
            o = l // 2
            qt, k, vtx = _proj_odd(x2, g, od_w_in[o])
            a_list = [_swa(qt, k, vtx, table_c, od_sinks[o], b, s)]
            wout_list = [od_w_out[o].astype(BF16)]
        mk, mv = _memkv(mem, xa_mem_norm[l], xa_wkv[l])
        x2 = _tail(x2, g, a_list, wout_list, xa_wq[l].astype(BF16), jnp.swapaxes(mk, 1, 2), mv,
                   xa_wo[l].astype(BF16), mlp_w1[l].astype(BF16), mlp_w2[l].astype(BF16), b, s)
    return x2.reshape(b, s, d)
```

```python
import functools
import math

import numpy as np
import jax
import jax.numpy as jnp
from jax import lax
from jax.experimental import pallas as pl
from jax.experimental.pallas import tpu as pltpu

F32 = jnp.float32
BF16 = jnp.bfloat16

D_MODEL = 1024
CHUNK = 64
EPS = 1e-6
NEG_INF = -1e30
LOG2E = math.log2(math.e)

H_A, DH_A, D_LAT = 8, 64, 128
H_IDX, DH_IDX = 8, 64
TOPK_MAX = 256
H_B, DH_B = 4, 64
H_C, H_C_KV, DH_C = 16, 2, 64
G_C = H_C // H_C_KV
WINDOW = 128
H_X, DH_X = 4, 64
D_FF = 4 * D_MODEL
N_BUCKETS = 32
MAX_DIST = 1024

LANES = 128
FOLD_ROWS = 64
TQ = 128
TQ_DSA = 256
TQ_DIFF = 512
TK = 512
SUB = TK // LANES
ONES_ROWS = 16
TM_PROJ = 512
TM_TAIL = 1024
TAIL_CHAINS = 2
FF_CHUNK = 1024
VMEM_LIMIT = 56 * 1024 * 1024
N_COARSE = 9
N_BISECT = 7
GROUPS = (4, 2, 1)
GROUPS_NEAR = (3, 2, 1)
GROUPS_DSA = (4, 2, 1)
GROUPS_PASS = (4, 1)


def _rms(x, g):
    return x * lax.rsqrt(jnp.mean(x * x, axis=-1, keepdims=True) + EPS) * g


def _dot(a, b):
    return jnp.dot(a, b, preferred_element_type=F32)


def _t5_bucket_np(rel):
    nb = N_BUCKETS // 2
    max_exact = nb // 2
    n = np.abs(rel)
    nf = np.maximum(n, 1).astype(np.float32)
    large = max_exact + (np.log(nf / max_exact) / math.log(MAX_DIST / max_exact) * (nb - max_exact)).astype(np.int32)
    large = np.minimum(large, nb - 1)
    return np.where(rel > 0, nb, 0) + np.where(n < max_exact, n, large)


def _sat_blocks():
    d = 1
    while _t5_bucket_np(np.array([-(d * LANES - (LANES - 1))]))[0] != N_BUCKETS // 2 - 1:
        d += 1
    return d


N_SAT = _sat_blocks()
N_BT = N_SAT + 1


def _toeplitz(u, rows, cols):
    length = u.shape[-1]
    flat = jnp.tile(u, (1,) * (u.ndim - 1) + (rows,))[..., :rows * (length - 1)]
    return flat.reshape(u.shape[:-1] + (rows, length - 1))[..., :cols]


def _causal_bias_tiles(table_cols):
    length = 2 * LANES + 1
    m = np.arange(length)
    diff = np.where(m < LANES + 1, m, m - length)
    rel = -diff[None, :] - (np.arange(N_BT) * LANES)[:, None]
    tab = table_cols.astype(F32)
    u = jnp.take(tab, jnp.asarray(_t5_bucket_np(rel)), axis=0)
    u = (u - tab[N_BUCKETS // 2 - 1][None, None, :]) * LOG2E
    tiles = _toeplitz(jnp.moveaxis(u, -1, 0), LANES, LANES)
    return tiles.reshape(table_cols.shape[1] * N_BT, LANES, LANES).astype(BF16)


def _fold_rows(x, op):
    acc = x[0:FOLD_ROWS, :]
    for r in range(1, x.shape[0] // FOLD_ROWS):
        acc = op(acc, x[r * FOLD_ROWS:(r + 1) * FOLD_ROWS, :])
    return acc


def _grouped_loop(lo, hi, carry, fn, groups):
    for g in groups:
        n = (hi - lo) // g
        carry = lax.fori_loop(0, n, lambda u, c, lo=lo, g=g: fn([lo + u * g + k for k in range(g)], c), carry)
        lo = lo + n * g
    return carry


def _bias_tile(bias_ref, h, qb0, n_qb, j):
    rows = []
    for c in range(SUB):
        cols = [bias_ref[h * N_BT + jnp.clip(qb0 + r - (j * SUB + c), 0, N_SAT)] for r in range(n_qb)]
        rows.append(cols[0] if n_qb == 1 else jnp.concatenate(cols, axis=1))
    return jnp.concatenate(rows, axis=0)


def _ones_rows(n):
    return jnp.ones((ONES_ROWS, n), BF16)


def _proj_even_kernel(x_ref, g_ref, w_ref, kvn_ref, wuk_ref,
                      qit_ref, wit_ref, qlt_ref, ki_ref, c_ref, ctx_ref, qbt_ref, kb_ref, vtx_ref):
    tm = x_ref.shape[0]
    h = _rms(x_ref[...], g_ref[0:1, :]).astype(BF16)
    z = _dot(h, w_ref[...])
    qa = z[:, 0:512].astype(BF16)
    ckv = _rms(z[:, 512:640], kvn_ref[...])
    c_ref[...] = ckv.astype(BF16)
    kb_ref[...] = z[:, 1664:2176].astype(BF16)
    ki_ref[...] = z[:, 2688:2688 + DH_IDX].astype(BF16)
    ctx_ref[0, D_LAT:, :] = _ones_rows(tm)
    for hb in range(H_B):
        vtx_ref[0, hb, 2 * DH_B:, :] = _ones_rows(tm)
    per_q, per_diff = TQ_DSA // LANES, TQ_DIFF // LANES
    for hh in range(H_A):
        ql = _dot(qa[:, hh * DH_A:(hh + 1) * DH_A], wuk_ref[hh]) * (DH_A ** -0.5 * LOG2E)
        for sb in range(tm // LANES):
            off = hh * TQ_DSA + (sb % per_q) * LANES
            qlt_ref[sb // per_q, :, off:off + LANES] = ql[sb * LANES:(sb + 1) * LANES, :].T.astype(BF16)
    for sb in range(tm // LANES):
        rows = slice(sb * LANES, (sb + 1) * LANES)
        blk, sub = sb // per_q, (sb % per_q) * LANES
        for pr in range(H_IDX // 2):
            t = z[rows, 640 + pr * LANES:640 + (pr + 1) * LANES].T
            for half in range(2):
                off = (2 * pr + half) * TQ_DSA + sub
                qit_ref[blk, :, off:off + LANES] = t[half * DH_IDX:(half + 1) * DH_IDX, :].astype(BF16)
        wit_ref[blk, :, sub:sub + LANES] = z[rows, 2816:2816 + LANES].T[0:H_IDX, :]
        ctx_ref[0, 0:D_LAT, rows] = ckv[rows, :].T.astype(BF16)
        for hb in range(H_B):
            qb = z[rows, 1152 + hb * LANES:1152 + (hb + 1) * LANES] * (DH_B ** -0.5 * LOG2E)
            lanes = slice((sb % per_diff) * LANES, (sb % per_diff + 1) * LANES)
            qbt_ref[sb // per_diff, hb, :, lanes] = qb.T.astype(BF16)
            vtx_ref[0, hb, 0:2 * DH_B, rows] = z[rows, 2176 + hb * LANES:2176 + (hb + 1) * LANES].T.astype(BF16)


def _proj_even(x2, g, w_in, kvn, w_uk):
    t = x2.shape[0]
    tm = TK
    assert t % tm == 0 and tm % TQ_DSA == 0 and tm % TQ_DIFF == 0
    pad = jnp.zeros((D_MODEL, LANES - DH_IDX), w_in.dtype)
    cols = [w_in[:, 0:512], w_in[:, 512:640], w_in[:, 640:1152], w_in[:, 1224:1736], w_in[:, 1736:2248],
            w_in[:, 2248:2760], w_in[:, 1152:1216], pad, w_in[:, 1216:1224],
            jnp.zeros((D_MODEL, LANES - H_IDX), w_in.dtype)]
    w = jnp.concatenate(cols, axis=1).astype(BF16)
    n = w.shape[1]
    dv = 2 * DH_B
    row = lambda i: (i, 0)
    full2 = lambda i: (0, 0)
    lead3 = lambda i: (i, 0, 0)
    lead4 = lambda i: (i, 0, 0, 0)
    outs = [
        ((t // TQ_DSA, DH_IDX, H_IDX * TQ_DSA), BF16, (tm // TQ_DSA, DH_IDX, H_IDX * TQ_DSA), lead3),
        ((t // TQ_DSA, H_IDX, TQ_DSA), F32, (tm // TQ_DSA, H_IDX, TQ_DSA), lead3),
        ((t // TQ_DSA, D_LAT, H_A * TQ_DSA), BF16, (tm // TQ_DSA, D_LAT, H_A * TQ_DSA), lead3),
        ((t, DH_IDX), BF16, (tm, DH_IDX), row),
        ((t, D_LAT), BF16, (tm, D_LAT), row),
        ((t // TK, D_LAT + ONES_ROWS, TK), BF16, (1, D_LAT + ONES_ROWS, TK), lead3),
        ((t // TQ_DIFF, H_B, dv, TQ_DIFF), BF16, (tm // TQ_DIFF, H_B, dv, TQ_DIFF), lead4),
        ((t, H_B * dv), BF16, (tm, H_B * dv), row),
        ((t // TK, H_B, dv + ONES_ROWS, TK), BF16, (1, H_B, dv + ONES_ROWS, TK), lead4),
    ]
    return pl.pallas_call(
        _proj_even_kernel,
        out_shape=[jax.ShapeDtypeStruct(shape, dt) for shape, dt, _, _ in outs],
        grid=(t // tm,),
        in_specs=[
            pl.BlockSpec((tm, D_MODEL), row),
            pl.BlockSpec((6, D_MODEL), full2),
            pl.BlockSpec((D_MODEL, n), full2),
            pl.BlockSpec((1, D_LAT), full2),
            pl.BlockSpec((H_A, DH_A, D_LAT), lambda i: (0, 0, 0)),
        ],
        out_specs=[pl.BlockSpec(block, imap) for _, _, block, imap in outs],
        compiler_params=pltpu.CompilerParams(vmem_limit_bytes=VMEM_LIMIT),
        name="proj_even",
    )(x2, g, w, kvn.reshape(1, D_LAT), w_uk.astype(BF16))


def _proj_odd_kernel(x_ref, g_ref, w_ref, qt_ref, k_ref, vtx_ref):
    tm = x_ref.shape[0]
    h = _rms(x_ref[...], g_ref[0:1, :]).astype(BF16)
    z = _dot(h, w_ref[...])
    for kv in range(H_C_KV):
        k_ref[kv] = z[:, 1024 + kv * DH_C:1024 + (kv + 1) * DH_C].astype(BF16)
    for sb in range(tm // TQ):
        rows = slice(sb * TQ, (sb + 1) * TQ)
        for pr in range(H_C // 2):
            t = (z[rows, pr * LANES:(pr + 1) * LANES] * (DH_C ** -0.5 * LOG2E)).T
            for half in range(2):
                hd = 2 * pr + half
                kv, gq = hd // G_C, hd % G_C
                qt_ref[sb, kv, :, gq * TQ:(gq + 1) * TQ] = t[half * DH_C:(half + 1) * DH_C, :].astype(BF16)
        vt = z[rows, 1152:1280].T
        for kv in range(H_C_KV):
            vtx_ref[sb, kv, 0:DH_C, :] = vt[kv * DH_C:(kv + 1) * DH_C, :].astype(BF16)
            vtx_ref[sb, kv, DH_C:, :] = _ones_rows(TQ)


def _proj_odd(x2, g, w_in):
    t = x2.shape[0]
    tm = min(TM_PROJ, t)
    assert TQ == LANES
    row = lambda i: (i, 0)
    full2 = lambda i: (0, 0)
    lead4 = lambda i: (i, 0, 0, 0)
    dvx = DH_C + ONES_ROWS
    return pl.pallas_call(
        _proj_odd_kernel,
        out_shape=[jax.ShapeDtypeStruct((t // TQ, H_C_KV, DH_C, G_C * TQ), BF16),
                   jax.ShapeDtypeStruct((H_C_KV, t, DH_C), BF16),
                   jax.ShapeDtypeStruct((t // TQ, H_C_KV, dvx, TQ), BF16)],
        grid=(t // tm,),
        in_specs=[pl.BlockSpec((tm, D_MODEL), row),
                  pl.BlockSpec((6, D_MODEL), full2),
                  pl.BlockSpec((D_MODEL, 1280), full2)],
        out_specs=[pl.BlockSpec((tm // TQ, H_C_KV, DH_C, G_C * TQ), lead4),
                   pl.BlockSpec((H_C_KV, tm, DH_C), lambda i: (0, i, 0)),
                   pl.BlockSpec((tm // TQ, H_C_KV, dvx, TQ), lead4)],
        compiler_params=pltpu.CompilerParams(vmem_limit_bytes=VMEM_LIMIT),
        name="proj_odd",
    )(x2, g, w_in.astype(BF16))


def _dsa_kernel(qit_ref, wit_ref, qlt_ref, k_ref, c_ref, ctx_ref, bias_ref, wuvt_ref, o_ref,
                sc_ref, scb_ref, acc_ref, *p_refs, topk):
    TQ = TQ_DSA
    i = pl.program_id(1)
    t0 = i * TQ
    nt = (t0 + TQ + TK - 1) // TK
    n_qb = TQ // LANES
    n_far = jnp.maximum(i * n_qb - N_SAT + 1, 0) // SUB
    kf = float(topk)

    qpos = lax.broadcasted_iota(jnp.int32, (1, TQ), 1) + t0
    limit = (qpos // CHUNK + 1) * CHUNK
    small = limit <= topk
    big = jnp.logical_not(small)
    w = wit_ref[...]
    qit = qit_ref[...]

    def score_tiles(js, carry):
        rmax, rmin = carry
        zs = [_dot(k_ref[j], qit) for j in js]
        for j, z in zip(js, zs):
            sc = jnp.maximum(z[:, 0:TQ], 0.0) * w[0:1, :]
            for h in range(1, H_IDX):
                sc = sc + jnp.maximum(z[:, h * TQ:(h + 1) * TQ], 0.0) * w[h:h + 1, :]
            kpos = lax.broadcasted_iota(jnp.int32, (TK, TQ), 0) + j * TK
            adm = kpos < limit
            sc_ref[j] = jnp.where(adm, sc, -jnp.inf)
            scb_ref[j] = jnp.where(adm, sc, -jnp.inf).astype(BF16)
            rmax = jnp.maximum(rmax, _fold_rows(jnp.where(adm, sc, -jnp.inf), jnp.maximum))
            rmin = jnp.minimum(rmin, _fold_rows(jnp.where(adm, sc, jnp.inf), jnp.minimum))
        return rmax, rmin

    rmax, rmin = _grouped_loop(
        0, nt, (jnp.full((FOLD_ROWS, TQ), -jnp.inf, F32), jnp.full((FOLD_ROWS, TQ), jnp.inf, F32)),
        score_tiles, GROUPS_DSA)
    hi0 = jnp.max(rmax, axis=0, keepdims=True)
    lo0 = jnp.min(rmin, axis=0, keepdims=True)

    def col_reduce(tile_fn, op, init, final):
        def fn(js, acc):
            for j in js:
                acc = op(acc, _fold_rows(tile_fn(j), op))
            return acc
        acc = _grouped_loop(0, nt, jnp.full((FOLD_ROWS, TQ), init, F32), fn, GROUPS_PASS)
        return final(acc, axis=0, keepdims=True)

    def count_ge(thr):
        return col_reduce(lambda j: jnp.where(sc_ref[j] >= thr, 1.0, 0.0), jnp.add, 0.0, jnp.sum)

    def max_le(thr):
        def tile(j):
            t = sc_ref[j]
            return jnp.where(t <= thr, t, -jnp.inf)
        return col_reduce(tile, jnp.maximum, -jnp.inf, jnp.max)

    def max_lt(thr):
        def tile(j):
            t = sc_ref[j]
            return jnp.where(t < thr, t, -jnp.inf)
        return col_reduce(tile, jnp.maximum, -jnp.inf, jnp.max)

    def count_ge_coarse(thr):
        def fn(js, acc):
            for j in js:
                hit = jnp.where(scb_ref[j] >= thr, jnp.ones((), BF16), jnp.zeros((), BF16))
                acc = acc + _fold_rows(hit, jnp.add)
            return acc
        acc = _grouped_loop(0, nt, jnp.zeros((FOLD_ROWS, TQ), BF16), fn, GROUPS_PASS)
        return jnp.sum(acc.astype(F32), axis=0, keepdims=True)

    def margin(v):
        return jnp.abs(v) * 2.0 ** -6 + 1e-30

    def coarse_body(_, carry):
        lo, hi = carry
        mid = (lo * 0.5 + hi * 0.5).astype(BF16)
        ge = count_ge_coarse(mid) >= kf
        mid = mid.astype(F32)
        return jnp.where(ge, mid, lo), jnp.where(ge, hi, mid)

    lo_c, hi_c = lax.fori_loop(0, N_COARSE, coarse_body,
                               ((lo0 - margin(lo0)).astype(BF16).astype(F32),
                                (hi0 + margin(hi0)).astype(BF16).astype(F32)))

    def bisect_body(_, carry):
        lo, hi = carry
        mid = lo * 0.5 + hi * 0.5
        ge = count_ge(mid) >= kf
        return jnp.where(ge, mid, lo), jnp.where(ge, hi, mid)

    lo, hi = lax.fori_loop(0, N_BISECT, bisect_body,
                           (jnp.maximum(lo_c - margin(lo_c), lo0), jnp.minimum(hi_c, hi0)))

    v0 = max_le(hi)
    c0 = count_ge(v0)

    def walk_cond(carry):
        _, cnt, it = carry
        pending = jnp.where((cnt < kf) & big, 1.0, 0.0)
        return (jnp.max(pending) > 0.0) & (it < topk + 2)

    def walk_body(carry):
        v, cnt, it = carry
        v = jnp.where((cnt < kf) & big, max_lt(v), v)
        return v, count_ge(v), it + 1

    tau, cnt, _ = lax.while_loop(walk_cond, walk_body, (v0, c0, jnp.int32(0)))
    tau = jnp.where(small, -jnp.finfo(F32).max, tau)

    tied = (cnt > kf) & big

    @pl.when(jnp.max(jnp.where(tied, 1.0, 0.0)) > 0.0)
    def _():
        excess = jnp.where(tied, cnt - kf, 0.0)
        kr = lax.broadcasted_iota(jnp.int32, (TK, TK), 0)
        kc = lax.broadcasted_iota(jnp.int32, (TK, TK), 1)
        upper = jnp.where(kc >= kr, 1.0, 0.0).astype(BF16)

        def drop_body(u, later):
            j = nt - 1 - u
            t = sc_ref[j]
            e = (t == tau) & tied
            behind = _dot(upper, jnp.where(e, 1.0, 0.0).astype(BF16)) + later
            sc_ref[j] = jnp.where(e & (behind <= excess), -jnp.inf, t)
            return behind[0:1, :]

        lax.fori_loop(0, nt, drop_body, jnp.zeros((1, TQ), F32))

    acc_ref[...] = jnp.zeros(acc_ref.shape, F32)
    qlt = qlt_ref[...]

    def attn_tiles(js, m, with_bias):
        s_alls = [_dot(c_ref[j], qlt).astype(BF16) for j in js]
        for g, (j, s_all) in enumerate(zip(js, s_alls)):
            p_ref = p_refs[g]
            pen = jnp.where(sc_ref[j] >= tau, 0.0, NEG_INF).astype(BF16)
            m_new, alpha = [], []
            for h in range(H_A):
                cols = slice(h * TQ, (h + 1) * TQ)
                s = s_all[:, cols] + pen
                if with_bias:
                    s = s + _bias_tile(bias_ref, h, i * n_qb, n_qb, j)
                m_old = m[:, cols]
                m_h = jnp.maximum(m_old, jnp.max(_fold_rows(s, jnp.maximum), axis=0, keepdims=True).astype(F32))
                p_ref[:, cols] = jnp.exp2(s - m_h.astype(BF16))
                alpha.append(jnp.exp2(m_old - m_h))
                m_new.append(m_h)
            acc_ref[...] = acc_ref[...] * jnp.concatenate(alpha, axis=1) + _dot(ctx_ref[j], p_ref[...])
            m = jnp.concatenate(m_new, axis=1)
        return m

    m = jnp.full((1, H_A * TQ), NEG_INF, F32)
    m = _grouped_loop(0, n_far, m, lambda js, mm: attn_tiles(js, mm, False), GROUPS_DSA)
    _grouped_loop(n_far, nt, m, lambda js, mm: attn_tiles(js, mm, True), GROUPS_NEAR)

    outs = []
    for h in range(H_A):
        cols = slice(h * TQ, (h + 1) * TQ)
        o_t = (acc_ref[0:D_LAT, cols] / acc_ref[D_LAT:D_LAT + 1, cols]).astype(BF16)
        outs.append(_dot(wuvt_ref[h], o_t))
    for pr in range(H_A // 2):
        pair = jnp.concatenate([outs[2 * pr], outs[2 * pr + 1]], axis=0)
        o_ref[:, pr * LANES:(pr + 1) * LANES] = pair.T.astype(BF16)


def _dsa(qit, wit, qlt, ki, ckv, ctx, bias_a, w_uv, b, s):
    TQ = TQ_DSA
    nq, nt = s // TQ, s // TK
    topk = min(TOPK_MAX, s // 4)
    assert s // FOLD_ROWS <= 256
    k4 = ki.reshape(b, nt, TK, DH_IDX)
    c4 = ckv.reshape(b, nt, TK, D_LAT)
    ctx = ctx.reshape(b, nt, D_LAT + ONES_ROWS, TK)
    wuvt = jnp.swapaxes(w_uv, 1, 2).astype(BF16)
    qblk = lambda bb, i: (bb * nq + i, 0, 0)
    kv4 = lambda bb, i: (bb, 0, 0, 0)
    once = dict(pipeline_mode=pl.Buffered(1))
    return pl.pallas_call(
        functools.partial(_dsa_kernel, topk=topk),
        out_shape=jax.ShapeDtypeStruct((b * s, H_A * DH_A), BF16),
        grid=(b, nq),
        in_specs=[
            pl.BlockSpec((None, DH_IDX, H_IDX * TQ), qblk),
            pl.BlockSpec((None, H_IDX, TQ), qblk),
            pl.BlockSpec((None, D_LAT, H_A * TQ), qblk),
            pl.BlockSpec((None, nt, TK, DH_IDX), kv4, **once),
            pl.BlockSpec((None, nt, TK, D_LAT), kv4, **once),
            pl.BlockSpec((None, nt, D_LAT + ONES_ROWS, TK), kv4, **once),
            pl.BlockSpec((H_A * N_BT, LANES, LANES), lambda bb, i: (0, 0, 0), **once),
            pl.BlockSpec((H_A, DH_A, D_LAT), lambda bb, i: (0, 0, 0), **once),
        ],
        out_specs=pl.BlockSpec((TQ, H_A * DH_A), lambda bb, i: (bb * nq + i, 0)),
        scratch_shapes=[
            pltpu.VMEM((nt, TK, TQ), F32),
            pltpu.VMEM((nt, TK, TQ), BF16),
            pltpu.VMEM((D_LAT + ONES_ROWS, H_A * TQ), F32),
        ] + [pltpu.VMEM((TK, H_A * TQ), BF16)] * max(GROUPS_DSA + GROUPS_NEAR),
        compiler_params=pltpu.CompilerParams(vmem_limit_bytes=VMEM_LIMIT),
        name="dsa_attention",
    )(qit, wit, qlt, k4, c4, ctx, bias_a, wuvt)


def _diff_kernel(qt_ref, k_ref, vtx_ref, bias_ref, lam_ref, sub_ref, o_ref, acc_ref, *p_refs, lam_init):
    h = pl.program_id(1)
    i = pl.program_id(2)
    tq = TQ_DIFF
    n_qb = tq // LANES
    t0 = i * tq
    nt = (t0 + tq + TK - 1) // TK
    n_far = jnp.maximum(i * n_qb - N_SAT + 1, 0) // SUB
    qt = qt_ref[...]
    top = lax.broadcasted_iota(jnp.int32, qt.shape, 0) < DH_B
    q2 = jnp.concatenate([jnp.where(top, qt, jnp.zeros_like(qt)), jnp.where(top, jnp.zeros_like(qt), qt)], axis=1)
    qpos = lax.broadcasted_iota(jnp.int32, (1, tq), 1) + t0
    limit = (qpos // CHUNK + 1) * CHUNK
    acc_ref[...] = jnp.zeros(acc_ref.shape, F32)

    def tiles(js, m, near):
        s2s = [_dot(k_ref[j], q2).astype(BF16) for j in js]
        for g, (j, s2) in enumerate(zip(js, s2s)):
            p_ref = p_refs[g]
            if near:
                bias = _bias_tile(bias_ref, h, i * n_qb, n_qb, j)
                adm = (lax.broadcasted_iota(jnp.int32, (TK, tq), 0) + j * TK) < limit
            m_new, alpha = [], []
            for mp in range(2):
                cols = slice(mp * tq, (mp + 1) * tq)
                s = s2[:, cols]
                if near:
                    s = jnp.where(adm, s + bias, jnp.asarray(NEG_INF, BF16))
                m_old = m[:, cols]
                m_h = jnp.maximum(m_old, jnp.max(_fold_rows(s, jnp.maximum), axis=0, keepdims=True).astype(F32))
                p_ref[:, cols] = jnp.exp2(s - m_h.astype(BF16))
                alpha.append(jnp.exp2(m_old - m_h))
                m_new.append(m_h)
            acc_ref[...] = acc_ref[...] * jnp.concatenate(alpha, axis=1) + _dot(vtx_ref[j], p_ref[...])
            m = jnp.concatenate(m_new, axis=1)
        return m

    m = jnp.full((1, 2 * tq), NEG_INF, F32)
    m = _grouped_loop(0, n_far, m, lambda js, mm: tiles(js, mm, False), GROUPS)
    _grouped_loop(n_far, nt, m, lambda js, mm: tiles(js, mm, True), GROUPS_NEAR)

    lf = lam_ref[...]
    lam = (jnp.exp(jnp.sum(lf[0:1, :] * lf[1:2, :], axis=1, keepdims=True))
           - jnp.exp(jnp.sum(lf[2:3, :] * lf[3:4, :], axis=1, keepdims=True)) + lam_init)
    dv = 2 * DH_B
    o = (acc_ref[0:dv, 0:tq] / acc_ref[dv:dv + 1, 0:tq]
         - lam * (acc_ref[0:dv, tq:2 * tq] / acc_ref[dv:dv + 1, tq:2 * tq]))
    o = o * lax.rsqrt(jnp.mean(o * o, axis=0, keepdims=True) + EPS) * sub_ref[...] * (1.0 - lam_init)
    o_ref[...] = o.T.astype(BF16)


def _diff(qbt, kb, vtx, bias_b, b_lambda, b_subln, lam_init, b, s):
    tq = TQ_DIFF
    nq, nt = s // tq, s // TK
    dv = 2 * DH_B
    k4 = kb.reshape(b, nt, TK, H_B * dv)
    vtx = vtx.reshape(b, nt, H_B, dv + ONES_ROWS, TK)
    return pl.pallas_call(
        functools.partial(_diff_kernel, lam_init=lam_init),
        out_shape=jax.ShapeDtypeStruct((b * s, H_B * dv), BF16),
        grid=(b, H_B, nq),
        in_specs=[
            pl.BlockSpec((None, None, dv, tq), lambda bb, h, i: (bb * nq + i, h, 0, 0)),
            pl.BlockSpec((None, nt, TK, dv), lambda bb, h, i: (bb, 0, 0, h)),
            pl.BlockSpec((None, nt, None, dv + ONES_ROWS, TK), lambda bb, h, i: (bb, 0, h, 0, 0)),
            pl.BlockSpec((H_B * N_BT, LANES, LANES), lambda bb, h, i: (0, 0, 0)),
            pl.BlockSpec((4, DH_B), lambda bb, h, i: (0, 0)),
            pl.BlockSpec((dv, 1), lambda bb, h, i: (0, 0)),
        ],
        out_specs=pl.BlockSpec((tq, dv), lambda bb, h, i: (bb * nq + i, h)),
        scratch_shapes=[
            pltpu.VMEM((dv + ONES_ROWS, 2 * tq), F32),
        ] + [pltpu.VMEM((TK, 2 * tq), BF16)] * max(GROUPS),
        compiler_params=pltpu.CompilerParams(vmem_limit_bytes=VMEM_LIMIT),
        name="diff_attention",
    )(qbt, k4, vtx, bias_b, b_lambda.astype(F32), b_subln.reshape(dv, 1).astype(F32))


def _swa_kernel(qt_ref, ka_ref, kb_ref, vta_ref, vtb_ref, bias_ref, sink_ref, o_ref):
    i = pl.program_id(1)
    pad_pen = jnp.where(i == 0, NEG_INF, 0.0)
    for kv in range(H_C_KV):
        qt = qt_ref[kv]
        s = jnp.concatenate([_dot(ka_ref[kv], qt) + pad_pen, _dot(kb_ref[kv], qt)], axis=0) + bias_ref[kv]
        sink = sink_ref[kv]
        m = jnp.maximum(jnp.max(s, axis=0, keepdims=True), sink)
        p = jnp.exp2(s - m).astype(BF16)
        acc = _dot(vta_ref[kv], p[0:TQ, :]) + _dot(vtb_ref[kv], p[TQ:2 * TQ, :])
        o = acc[0:DH_C, :] / (acc[DH_C:DH_C + 1, :] + jnp.exp2(sink - m))
        for pr in range(G_C // 2):
            pair = jnp.concatenate([o[:, (2 * pr) * TQ:(2 * pr + 1) * TQ], o[:, (2 * pr + 1) * TQ:(2 * pr + 2) * TQ]],
                                   axis=0)
            col = (kv * G_C + 2 * pr) * DH_C
            o_ref[:, col:col + LANES] = pair.T.astype(BF16)


def _swa(qt, k, vtx, table_c, sinks, b, s):
    assert WINDOW == TQ and TQ % CHUNK == 0
    nq = s // TQ
    length = 3 * TQ + 1
    m = np.arange(length)
    diff = np.where(m < 2 * TQ + 1, m, m - length)
    u = jnp.take(table_c.astype(F32), jnp.asarray(_t5_bucket_np(diff - TQ)), axis=0) * LOG2E
    bias = _toeplitz(jnp.moveaxis(u, -1, 0), TQ, 2 * TQ)
    rblk = np.arange(TQ)[:, None] // CHUNK + WINDOW // CHUNK
    cblk = np.arange(2 * TQ)[None, :] // CHUNK
    valid = (cblk <= rblk) & (cblk >= rblk - WINDOW // CHUNK)
    bias = jnp.where(jnp.asarray(valid)[None], bias, NEG_INF)
    bias_t = jnp.transpose(bias.reshape(H_C_KV, G_C, TQ, 2 * TQ), (0, 3, 1, 2)).reshape(H_C_KV, 2 * TQ, G_C * TQ)
    sink_row = jnp.repeat(sinks.astype(F32).reshape(H_C_KV, 1, G_C) * LOG2E, TQ, axis=2)

    dvx = DH_C + ONES_ROWS
    prev = lambda bb, i: bb * nq + jnp.maximum(i - 1, 0)
    return pl.pallas_call(
        _swa_kernel,
        out_shape=jax.ShapeDtypeStruct((b * s, H_C * DH_C), BF16),
        grid=(b, nq),
        in_specs=[
            pl.BlockSpec((None, H_C_KV, DH_C, G_C * TQ), lambda bb, i: (bb * nq + i, 0, 0, 0)),
            pl.BlockSpec((H_C_KV, TQ, DH_C), lambda bb, i: (0, prev(bb, i), 0)),
            pl.BlockSpec((H_C_KV, TQ, DH_C), lambda bb, i: (0, bb * nq + i, 0)),
            pl.BlockSpec((None, H_C_KV, dvx, TQ), lambda bb, i: (prev(bb, i), 0, 0, 0)),
            pl.BlockSpec((None, H_C_KV, dvx, TQ), lambda bb, i: (bb * nq + i, 0, 0, 0)),
            pl.BlockSpec((H_C_KV, 2 * TQ, G_C * TQ), lambda bb, i: (0, 0, 0)),
            pl.BlockSpec((H_C_KV, 1, G_C * TQ), lambda bb, i: (0, 0, 0)),
        ],
        out_specs=pl.BlockSpec((TQ, H_C * DH_C), lambda bb, i: (bb * nq + i, 0)),
        compiler_params=pltpu.CompilerParams(vmem_limit_bytes=VMEM_LIMIT),
        name="swa_attention",
    )(qt, k, k, vtx, vtx, bias_t, sink_row)


def _memkv_kernel(mem_ref, g_ref, w_ref, k_ref, v_ref):
    h = _rms(mem_ref[...], g_ref[...]).astype(BF16)
    kv = _dot(h, w_ref[...])
    k_ref[...] = kv[:, 0:H_X * DH_X].astype(BF16)
    v_ref[...] = kv[:, H_X * DH_X:].astype(BF16)


def _memkv(mem, g, wkv):
    b, m, _ = mem.shape
    n = H_X * DH_X
    return pl.pallas_call(
        _memkv_kernel,
        out_shape=[jax.ShapeDtypeStruct((b, m, n), BF16), jax.ShapeDtypeStruct((b, m, n), BF16)],
        grid=(b,),
        in_specs=[pl.BlockSpec((None, m, D_MODEL), lambda i: (i, 0, 0)),
                  pl.BlockSpec((1, D_MODEL), lambda i: (0, 0)),
                  pl.BlockSpec((D_MODEL, 2 * n), lambda i: (0, 0))],
        out_specs=[pl.BlockSpec((None, m, n), lambda i: (i, 0, 0)), pl.BlockSpec((None, m, n), lambda i: (i, 0, 0))],
        compiler_params=pltpu.CompilerParams(vmem_limit_bytes=VMEM_LIMIT),
        name="mem_kv",
    )(mem, g.reshape(1, D_MODEL), wkv.astype(BF16))


def _lane_half_mask(rows, half):
    lane = lax.broadcasted_iota(jnp.int32, (rows, LANES), 1)
    return (lane < 64) if half == 0 else (lane >= 64)


def _tail_kernel(*refs, n_a):
    x_ref, g_ref = refs[0], refs[1]
    a_refs = refs[2:2 + n_a]
    wout_refs = refs[2 + n_a:2 + 2 * n_a]
    wq_ref, kt_ref, v_ref, wo_ref, w1_ref, w2_ref, o_ref = refs[2 + 2 * n_a:]
    tc = x_ref.shape[0] // TAIL_CHAINS
    chains = [slice(c * tc, (c + 1) * tc) for c in range(TAIL_CHAINS)]
    g = g_ref[...]

    ys = []
    for r in chains:
        y = _dot(a_refs[0][r, :], wout_refs[0][...])
        for a_ref, w_ref in zip(a_refs[1:], wout_refs[1:]):
            y = y + _dot(a_ref[r, :], w_ref[...])
        ys.append(y)
    xs = [x_ref[r, :] + _rms(y, g[1:2, :]) for r, y in zip(chains, ys)]

    hqs = [_rms(x, g[2:3, :]).astype(BF16) for x in xs]
    qs = [_dot(hq, wq_ref[...]) for hq in hqs]
    qs = [(q * (DH_X ** -0.5)).astype(BF16) for q in qs]
    pairs = [[] for _ in chains]
    for pr in range(H_X // 2):
        vp = v_ref[:, pr * LANES:(pr + 1) * LANES]
        outs = [jnp.zeros((tc, LANES), F32) for _ in chains]
        for half in range(2):
            qzs = [jnp.where(_lane_half_mask(tc, half), q[:, pr * LANES:(pr + 1) * LANES], jnp.zeros((tc, LANES), BF16))
                   for q in qs]
            ss = [_dot(qz, kt_ref[pr * LANES:(pr + 1) * LANES, :]) for qz in qzs]
            vz = jnp.where(_lane_half_mask(vp.shape[0], half), vp, jnp.zeros_like(vp))
            ps = []
            for sc in ss:
                e = jnp.exp(sc - jnp.max(sc, axis=1, keepdims=True))
                ps.append((e / jnp.sum(e, axis=1, keepdims=True)).astype(BF16))
            outs = [out + _dot(p, vz) for out, p in zip(outs, ps)]
        for c, out in enumerate(outs):
            pairs[c].append(out.astype(BF16))
    ys = [_dot(jnp.concatenate(pc, axis=1), wo_ref[...]) for pc in pairs]
    xs = [x + _rms(y, g[3:4, :]) for x, y in zip(xs, ys)]

    hms = [_rms(x, g[4:5, :]).astype(BF16) for x in xs]
    n_chunks = D_FF // FF_CHUNK
    ys = [jnp.zeros((tc, D_MODEL), F32) for _ in chains]
    a_cur = [_dot(hm, w1_ref[:, 0:FF_CHUNK]) for hm in hms]
    for k in range(n_chunks):
        if k + 1 < n_chunks:
            a_next = [_dot(hm, w1_ref[:, (k + 1) * FF_CHUNK:(k + 2) * FF_CHUNK]) for hm in hms]
        acts = [jnp.square(jnp.maximum(a, 0.0)).astype(BF16) for a in a_cur]
        ys = [y + _dot(act, w2_ref[k * FF_CHUNK:(k + 1) * FF_CHUNK, :]) for y, act in zip(ys, acts)]
        if k + 1 < n_chunks:
            a_cur = a_next
    for r, x, y in zip(chains, xs, ys):
        o_ref[r, :] = x + _rms(y, g[5:6, :])


def _tail(x2, g, a_list, wout_list, wq, kt, v, wo, w1, w2, b, s):
    t = x2.shape[0]
    tm = min(TM_TAIL, s)
    per_b = s // tm
    n_a = len(a_list)
    row = lambda i: (i, 0)
    full2 = lambda i: (0, 0)
    once = dict(pipeline_mode=pl.Buffered(1))
    in_specs = [pl.BlockSpec((tm, D_MODEL), row), pl.BlockSpec((6, D_MODEL), full2)]
    in_specs += [pl.BlockSpec((tm, a.shape[1]), row) for a in a_list]
    in_specs += [pl.BlockSpec(w.shape, full2, **once) for w in wout_list]
    in_specs += [
        pl.BlockSpec(wq.shape, full2, **once),
        pl.BlockSpec((None,) + kt.shape[1:], lambda i: (i // per_b, 0, 0)),
        pl.BlockSpec((None,) + v.shape[1:], lambda i: (i // per_b, 0, 0)),
        pl.BlockSpec(wo.shape, full2, **once),
        pl.BlockSpec(w1.shape, full2, **once),
        pl.BlockSpec(w2.shape, full2, **once),
    ]
    return pl.pallas_call(
        functools.partial(_tail_kernel, n_a=n_a),
        out_shape=jax.ShapeDtypeStruct((t, D_MODEL), F32),
        grid=(t // tm,),
        in_specs=in_specs,
        out_specs=pl.BlockSpec((tm, D_MODEL), row),
        compiler_params=pltpu.CompilerParams(vmem_limit_bytes=VMEM_LIMIT),
        name="tail",
    )(x2, g, *a_list, *wout_list, wq, kt, v, wo, w1, w2)


def kernel(x, mem, rel_bias_table, norm_g, ev_w_in, ev_a_kv_norm, ev_a_w_uk, ev_a_w_uv, ev_b_lambda, ev_b_subln, ev_w_out, od_w_in, od_sinks, od_w_out, xa_wq, xa_wkv, xa_wo, xa_mem_norm, mlp_w1, mlp_w2):
    b, s, d = x.shape
    depth = norm_g.shape[0]
    assert d == D_MODEL and s % TK == 0 and TK % TQ_DIFF == 0 and TK % TQ_DSA == 0 and TQ_DSA % LANES == 0
    x2 = x.reshape(b * s, d)
    bias_a = _causal_bias_tiles(rel_bias_table[:, :H_A])
    bias_b = _causal_bias_tiles(rel_bias_table[:, H_A:H_A + H_B])
    table_c = rel_bias_table[:, H_A + H_B:]
    for l in range(depth):
        g = norm_g[l].astype(F32)
        if l % 2 == 0:
            e = l // 2
            lam_init = 0.8 - 0.6 * math.exp(-0.3 * l)
            qit, wit, qlt, ki, ckv, ctx, qbt, kb, vtx = _proj_even(x2, g, ev_w_in[e], ev_a_kv_norm[e], ev_a_w_uk[e])
            oa = _dsa(qit, wit, qlt, ki, ckv, ctx, bias_a, ev_a_w_uv[e], b, s)
            ob = _diff(qbt, kb, vtx, bias_b, ev_b_lambda[e], ev_b_subln[e], lam_init, b, s)
            n_a = H_A * DH_A
            a_list = [oa, ob]
            wout_list = [ev_w_out[e][:n_a].astype(BF16), ev_w_out[e][n_a:].astype(BF16)]
        else:
            o = l // 2
            qt, k, vtx = _proj_odd(x2, g, od_w_in[o])
            a_list = [_swa(qt, k, vtx, table_c, od_sinks[o], b, s)]
            wout_list = [od_w_out[o].astype(BF16)]
        mk, mv = _memkv(mem, xa_mem_norm[l], xa_wkv[l])
        x2 = _tail(x2, g, a_list, wout_list, xa_wq[l].astype(BF16), jnp.swapaxes(mk, 1, 2), mv,
                   xa_wo[l].astype(BF16), mlp_w1[l].astype(BF16), mlp_w2[l].astype(BF16), b, s)
    return x2.reshape(b, s, d)
```

```python
import functools
import math

import numpy as np
import jax
import jax.numpy as jnp
from jax import lax
from jax.experimental import pallas as pl
from jax.experimental.pallas import tpu as pltpu

F32 = jnp.float32
BF16 = jnp.bfloat16

D_MODEL = 1024
CHUNK = 64
EPS = 1e-6
NEG_INF = -1e30
LOG2E = math.log2(math.e)

H_A, DH_A, D_LAT = 8, 64, 128
H_IDX, DH_IDX = 8, 64
TOPK_MAX = 256
H_B, DH_B = 4, 64
H_C, H_C_KV, DH_C = 16, 2, 64
G_C = H_C // H_C_KV
WINDOW = 128
H_X, DH_X = 4, 64
D_FF = 4 * D_MODEL
N_BUCKETS = 32
MAX_DIST = 1024

LANES = 128
FOLD_ROWS = 64
TQ = 128
TQ_DSA = 256
TQ_DIFF = 512
TK = 512
SUB = TK // LANES
ONES_ROWS = 16
TM_PROJ = 1024
TM_TAIL = 1024
TAIL_CHAINS = 2
FF_CHUNK = 1024
VMEM_LIMIT = 56 * 1024 * 1024
N_COARSE = 9
N_BISECT = 7
GROUPS = (4, 2, 1)
GROUPS_NEAR = (3, 2, 1)
GROUPS_DSA = (4, 2, 1)
GROUPS_PASS = (4, 1)


def _rms(x, g):
    return x * lax.rsqrt(jnp.mean(x * x, axis=-1, keepdims=True) + EPS) * g


def _dot(a, b):
    return jnp.dot(a, b, preferred_element_type=F32)


def _t5_bucket_np(rel):
    nb = N_BUCKETS // 2
    max_exact = nb // 2
    n = np.abs(rel)
    nf = np.maximum(n, 1).astype(np.float32)
    large = max_exact + (np.log(nf / max_exact) / math.log(MAX_DIST / max_exact) * (nb - max_exact)).astype(np.int32)
    large = np.minimum(large, nb - 1)
    return np.where(rel > 0, nb, 0) + np.where(n < max_exact, n, large)


def _sat_blocks():
    d = 1
    while _t5_bucket_np(np.array([-(d * LANES - (LANES - 1))]))[0] != N_BUCKETS // 2 - 1:
        d += 1
    return d


N_SAT = _sat_blocks()
N_BT = N_SAT + 1


def _toeplitz(u, rows, cols):
    length = u.shape[-1]
    flat = jnp.tile(u, (1,) * (u.ndim - 1) + (rows,))[..., :rows * (length - 1)]
    return flat.reshape(u.shape[:-1] + (rows, length - 1))[..., :cols]


def _causal_bias_tiles(table_cols):
    length = 2 * LANES + 1
    m = np.arange(length)
    diff = np.where(m < LANES + 1, m, m - length)
    rel = -diff[None, :] - (np.arange(N_BT) * LANES)[:, None]
    tab = table_cols.astype(F32)
    u = jnp.take(tab, jnp.asarray(_t5_bucket_np(rel)), axis=0)
    u = (u - tab[N_BUCKETS // 2 - 1][None, None, :]) * LOG2E
    tiles = _toeplitz(jnp.moveaxis(u, -1, 0), LANES, LANES)
    return tiles.reshape(table_cols.shape[1] * N_BT, LANES, LANES).astype(BF16)


def _fold_rows(x, op):
    acc = x[0:FOLD_ROWS, :]
    for r in range(1, x.shape[0] // FOLD_ROWS):
        acc = op(acc, x[r * FOLD_ROWS:(r + 1) * FOLD_ROWS, :])
    return acc


def _grouped_loop(lo, hi, carry, fn, groups):
    for g in groups:
        n = (hi - lo) // g
        carry = lax.fori_loop(0, n, lambda u, c, lo=lo, g=g: fn([lo + u * g + k for k in range(g)], c), carry)
        lo = lo + n * g
    return carry


def _bias_tile(bias_ref, h, qb0, n_qb, j):
    rows = []
    for c in range(SUB):
        cols = [bias_ref[h * N_BT + jnp.clip(qb0 + r - (j * SUB + c), 0, N_SAT)] for r in range(n_qb)]
        rows.append(cols[0] if n_qb == 1 else jnp.concatenate(cols, axis=1))
    return jnp.concatenate(rows, axis=0)


def _ones_rows(n):
    return jnp.ones((ONES_ROWS, n), BF16)


def _proj_even_kernel(x_ref, g_ref, w_ref, kvn_ref, wuk_ref,
                      qit_ref, wit_ref, qlt_ref, ki_ref, c_ref, ctx_ref, qbt_ref, kb_ref, vtx_ref):
    tm = x_ref.shape[0]
    h = _rms(x_ref[...], g_ref[0:1, :]).astype(BF16)
    z = _dot(h, w_ref[...])
    qa = z[:, 0:512].astype(BF16)
    ckv = _rms(z[:, 512:640], kvn_ref[...])
    c_ref[...] = ckv.astype(BF16)
    kb_ref[...] = z[:, 1664:2176].astype(BF16)
    ki_ref[...] = z[:, 2688:2688 + DH_IDX].astype(BF16)
    ctx_ref[0, D_LAT:, :] = _ones_rows(tm)
    for hb in range(H_B):
        vtx_ref[0, hb, 2 * DH_B:, :] = _ones_rows(tm)
    per_q, per_diff = TQ_DSA // LANES, TQ_DIFF // LANES
    for hh in range(H_A):
        ql = _dot(qa[:, hh * DH_A:(hh + 1) * DH_A], wuk_ref[hh]) * (DH_A ** -0.5 * LOG2E)
        for sb in range(tm // LANES):
            off = hh * TQ_DSA + (sb % per_q) * LANES
            qlt_ref[sb // per_q, :, off:off + LANES] = ql[sb * LANES:(sb + 1) * LANES, :].T.astype(BF16)
    for sb in range(tm // LANES):
        rows = slice(sb * LANES, (sb + 1) * LANES)
        blk, sub = sb // per_q, (sb % per_q) * LANES
        for pr in range(H_IDX // 2):
            t = z[rows, 640 + pr * LANES:640 + (pr + 1) * LANES].T
            for half in range(2):
                off = (2 * pr + half) * TQ_DSA + sub
                qit_ref[blk, :, off:off + LANES] = t[half * DH_IDX:(half + 1) * DH_IDX, :].astype(BF16)
        wit_ref[blk, :, sub:sub + LANES] = z[rows, 2816:2816 + LANES].T[0:H_IDX, :]
        ctx_ref[0, 0:D_LAT, rows] = ckv[rows, :].T.astype(BF16)
        for hb in range(H_B):
            qb = z[rows, 1152 + hb * LANES:1152 + (hb + 1) * LANES] * (DH_B ** -0.5 * LOG2E)
            lanes = slice((sb % per_diff) * LANES, (sb % per_diff + 1) * LANES)
            qbt_ref[sb // per_diff, hb, :, lanes] = qb.T.astype(BF16)
            vtx_ref[0, hb, 0:2 * DH_B, rows] = z[rows, 2176 + hb * LANES:2176 + (hb + 1) * LANES].T.astype(BF16)


def _proj_even(x2, g, w_in, kvn, w_uk):
    t = x2.shape[0]
    tm = TK
    assert t % tm == 0 and tm % TQ_DSA == 0 and tm % TQ_DIFF == 0
    pad = jnp.zeros((D_MODEL, LANES - DH_IDX), w_in.dtype)
    cols = [w_in[:, 0:512], w_in[:, 512:640], w_in[:, 640:1152], w_in[:, 1224:1736], w_in[:, 1736:2248],
            w_in[:, 2248:2760], w_in[:, 1152:1216], pad, w_in[:, 1216:1224],
            jnp.zeros((D_MODEL, LANES - H_IDX), w_in.dtype)]
    w = jnp.concatenate(cols, axis=1).astype(BF16)
    n = w.shape[1]
    dv = 2 * DH_B
    row = lambda i: (i, 0)
    full2 = lambda i: (0, 0)
    lead3 = lambda i: (i, 0, 0)
    lead4 = lambda i: (i, 0, 0, 0)
    outs = [
        ((t // TQ_DSA, DH_IDX, H_IDX * TQ_DSA), BF16, (tm // TQ_DSA, DH_IDX, H_IDX * TQ_DSA), lead3),
        ((t // TQ_DSA, H_IDX, TQ_DSA), F32, (tm // TQ_DSA, H_IDX, TQ_DSA), lead3),
        ((t // TQ_DSA, D_LAT, H_A * TQ_DSA), BF16, (tm // TQ_DSA, D_LAT, H_A * TQ_DSA), lead3),
        ((t, DH_IDX), BF16, (tm, DH_IDX), row),
        ((t, D_LAT), BF16, (tm, D_LAT), row),
        ((t // TK, D_LAT + ONES_ROWS, TK), BF16, (1, D_LAT + ONES_ROWS, TK), lead3),
        ((t // TQ_DIFF, H_B, dv, TQ_DIFF), BF16, (tm // TQ_DIFF, H_B, dv, TQ_DIFF), lead4),
        ((t, H_B * dv), BF16, (tm, H_B * dv), row),
        ((t // TK, H_B, dv + ONES_ROWS, TK), BF16, (1, H_B, dv + ONES_ROWS, TK), lead4),
    ]
    return pl.pallas_call(
        _proj_even_kernel,
        out_shape=[jax.ShapeDtypeStruct(shape, dt) for shape, dt, _, _ in outs],
        grid=(t // tm,),
        in_specs=[
            pl.BlockSpec((tm, D_MODEL), row),
            pl.BlockSpec((6, D_MODEL), full2),
            pl.BlockSpec((D_MODEL, n), full2),
            pl.BlockSpec((1, D_LAT), full2),
            pl.BlockSpec((H_A, DH_A, D_LAT), lambda i: (0, 0, 0)),
        ],
        out_specs=[pl.BlockSpec(block, imap) for _, _, block, imap in outs],
        compiler_params=pltpu.CompilerParams(vmem_limit_bytes=VMEM_LIMIT),
        name="proj_even",
    )(x2, g, w, kvn.reshape(1, D_LAT), w_uk.astype(BF16))


def _proj_odd_kernel(x_ref, g_ref, w_ref, qt_ref, k_ref, vtx_ref):
    tm = x_ref.shape[0]
    h = _rms(x_ref[...], g_ref[0:1, :]).astype(BF16)
    z = _dot(h, w_ref[...])
    for kv in range(H_C_KV):
        k_ref[kv] = z[:, 1024 + kv * DH_C:1024 + (kv + 1) * DH_C].astype(BF16)
    for sb in range(tm // TQ):
        rows = slice(sb * TQ, (sb + 1) * TQ)
        for pr in range(H_C // 2):
            t = (z[rows, pr * LANES:(pr + 1) * LANES] * (DH_C ** -0.5 * LOG2E)).T
            for half in range(2):
                hd = 2 * pr + half
                kv, gq = hd // G_C, hd % G_C
                qt_ref[sb, kv, :, gq * TQ:(gq + 1) * TQ] = t[half * DH_C:(half + 1) * DH_C, :].astype(BF16)
        vt = z[rows, 1152:1280].T
        for kv in range(H_C_KV):
            vtx_ref[sb, kv, 0:DH_C, :] = vt[kv * DH_C:(kv + 1) * DH_C, :].astype(BF16)
            vtx_ref[sb, kv, DH_C:, :] = _ones_rows(TQ)


def _proj_odd(x2, g, w_in):
    t = x2.shape[0]
    tm = min(TM_PROJ, t)
    assert TQ == LANES
    row = lambda i: (i, 0)
    full2 = lambda i: (0, 0)
    lead4 = lambda i: (i, 0, 0, 0)
    dvx = DH_C + ONES_ROWS
    return pl.pallas_call(
        _proj_odd_kernel,
        out_shape=[jax.ShapeDtypeStruct((t // TQ, H_C_KV, DH_C, G_C * TQ), BF16),
                   jax.ShapeDtypeStruct((H_C_KV, t, DH_C), BF16),
                   jax.ShapeDtypeStruct((t // TQ, H_C_KV, dvx, TQ), BF16)],
        grid=(t // tm,),
        in_specs=[pl.BlockSpec((tm, D_MODEL), row),
                  pl.BlockSpec((6, D_MODEL), full2),
                  pl.BlockSpec((D_MODEL, 1280), full2)],
        out_specs=[pl.BlockSpec((tm // TQ, H_C_KV, DH_C, G_C * TQ), lead4),
                   pl.BlockSpec((H_C_KV, tm, DH_C), lambda i: (0, i, 0)),
                   pl.BlockSpec((tm // TQ, H_C_KV, dvx, TQ), lead4)],
        compiler_params=pltpu.CompilerParams(vmem_limit_bytes=VMEM_LIMIT),
        name="proj_odd",
    )(x2, g, w_in.astype(BF16))


def _dsa_kernel(qit_ref, wit_ref, qlt_ref, k_ref, c_ref, ctx_ref, bias_ref, wuvt_ref, o_ref,
                sc_ref, scb_ref, acc_ref, *p_refs, topk):
    TQ = TQ_DSA
    i = pl.program_id(1)
    t0 = i * TQ
    nt = (t0 + TQ + TK - 1) // TK
    n_qb = TQ // LANES
    n_far = jnp.maximum(i * n_qb - N_SAT + 1, 0) // SUB
    kf = float(topk)

    qpos = lax.broadcasted_iota(jnp.int32, (1, TQ), 1) + t0
    limit = (qpos // CHUNK + 1) * CHUNK
    small = limit <= topk
    big = jnp.logical_not(small)
    w = wit_ref[...]
    qit = qit_ref[...]

    def score_tiles(js, carry):
        rmax, rmin = carry
        zs = [_dot(k_ref[j], qit) for j in js]
        for j, z in zip(js, zs):
            sc = jnp.maximum(z[:, 0:TQ], 0.0) * w[0:1, :]
            for h in range(1, H_IDX):
                sc = sc + jnp.maximum(z[:, h * TQ:(h + 1) * TQ], 0.0) * w[h:h + 1, :]
            kpos = lax.broadcasted_iota(jnp.int32, (TK, TQ), 0) + j * TK
            adm = kpos < limit
            sc_ref[j] = jnp.where(adm, sc, -jnp.inf)
            scb_ref[j] = jnp.where(adm, sc, -jnp.inf).astype(BF16)
            rmax = jnp.maximum(rmax, _fold_rows(jnp.where(adm, sc, -jnp.inf), jnp.maximum))
            rmin = jnp.minimum(rmin, _fold_rows(jnp.where(adm, sc, jnp.inf), jnp.minimum))
        return rmax, rmin

    rmax, rmin = _grouped_loop(
        0, nt, (jnp.full((FOLD_ROWS, TQ), -jnp.inf, F32), jnp.full((FOLD_ROWS, TQ), jnp.inf, F32)),
        score_tiles, GROUPS_DSA)
    hi0 = jnp.max(rmax, axis=0, keepdims=True)
    lo0 = jnp.min(rmin, axis=0, keepdims=True)

    def col_reduce(tile_fn, op, init, final):
        def fn(js, acc):
            for j in js:
                acc = op(acc, _fold_rows(tile_fn(j), op))
            return acc
        acc = _grouped_loop(0, nt, jnp.full((FOLD_ROWS, TQ), init, F32), fn, GROUPS_PASS)
        return final(acc, axis=0, keepdims=True)

    def count_ge(thr):
        return col_reduce(lambda j: jnp.where(sc_ref[j] >= thr, 1.0, 0.0), jnp.add, 0.0, jnp.sum)

    def max_le(thr):
        def tile(j):
            t = sc_ref[j]
            return jnp.where(t <= thr, t, -jnp.inf)
        return col_reduce(tile, jnp.maximum, -jnp.inf, jnp.max)

    def max_lt(thr):
        def tile(j):
            t = sc_ref[j]
            return jnp.where(t < thr, t, -jnp.inf)
        return col_reduce(tile, jnp.maximum, -jnp.inf, jnp.max)

    def count_ge_coarse(thr):
        def fn(js, acc):
            for j in js:
                hit = jnp.where(scb_ref[j] >= thr, jnp.ones((), BF16), jnp.zeros((), BF16))
                acc = acc + _fold_rows(hit, jnp.add)
            return acc
        acc = _grouped_loop(0, nt, jnp.zeros((FOLD_ROWS, TQ), BF16), fn, GROUPS_PASS)
        return jnp.sum(acc.astype(F32), axis=0, keepdims=True)

    def margin(v):
        return jnp.abs(v) * 2.0 ** -6 + 1e-30

    def coarse_body(_, carry):
        lo, hi = carry
        mid = (lo * 0.5 + hi * 0.5).astype(BF16)
        ge = count_ge_coarse(mid) >= kf
        mid = mid.astype(F32)
        return jnp.where(ge, mid, lo), jnp.where(ge, hi, mid)

    lo_c, hi_c = lax.fori_loop(0, N_COARSE, coarse_body,
                               ((lo0 - margin(lo0)).astype(BF16).astype(F32),
                                (hi0 + margin(hi0)).astype(BF16).astype(F32)))

    def bisect_body(_, carry):
        lo, hi = carry
        mid = lo * 0.5 + hi * 0.5
        ge = count_ge(mid) >= kf
        return jnp.where(ge, mid, lo), jnp.where(ge, hi, mid)

    lo, hi = lax.fori_loop(0, N_BISECT, bisect_body,
                           (jnp.maximum(lo_c - margin(lo_c), lo0), jnp.minimum(hi_c, hi0)))

    v0 = max_le(hi)
    c0 = count_ge(v0)

    def walk_cond(carry):
        _, cnt, it = carry
        pending = jnp.where((cnt < kf) & big, 1.0, 0.0)
        return (jnp.max(pending) > 0.0) & (it < topk + 2)

    def walk_body(carry):
        v, cnt, it = carry
        v = jnp.where((cnt < kf) & big, max_lt(v), v)
        return v, count_ge(v), it + 1

    tau, cnt, _ = lax.while_loop(walk_cond, walk_body, (v0, c0, jnp.int32(0)))
    tau = jnp.where(small, -jnp.finfo(F32).max, tau)

    tied = (cnt > kf) & big

    @pl.when(jnp.max(jnp.where(tied, 1.0, 0.0)) > 0.0)
    def _():
        excess = jnp.where(tied, cnt - kf, 0.0)
        kr = lax.broadcasted_iota(jnp.int32, (TK, TK), 0)
        kc = lax.broadcasted_iota(jnp.int32, (TK, TK), 1)
        upper = jnp.where(kc >= kr, 1.0, 0.0).astype(BF16)

        def drop_body(u, later):
            j = nt - 1 - u
            t = sc_ref[j]
            e = (t == tau) & tied
            behind = _dot(upper, jnp.where(e, 1.0, 0.0).astype(BF16)) + later
            sc_ref[j] = jnp.where(e & (behind <= excess), -jnp.inf, t)
            return behind[0:1, :]

        lax.fori_loop(0, nt, drop_body, jnp.zeros((1, TQ), F32))

    acc_ref[...] = jnp.zeros(acc_ref.shape, F32)
    qlt = qlt_ref[...]

    def attn_tiles(js, m, with_bias):
        s_alls = [_dot(c_ref[j], qlt).astype(BF16) for j in js]
        for g, (j, s_all) in enumerate(zip(js, s_alls)):
            p_ref = p_refs[g]
            pen = jnp.where(sc_ref[j] >= tau, 0.0, NEG_INF).astype(BF16)
            m_new, alpha = [], []
            for h in range(H_A):
                cols = slice(h * TQ, (h + 1) * TQ)
                s = s_all[:, cols] + pen
                if with_bias:
                    s = s + _bias_tile(bias_ref, h, i * n_qb, n_qb, j)
                m_old = m[:, cols]
                m_h = jnp.maximum(m_old, jnp.max(_fold_rows(s, jnp.maximum), axis=0, keepdims=True).astype(F32))
                p_ref[:, cols] = jnp.exp2(s - m_h.astype(BF16))
                alpha.append(jnp.exp2(m_old - m_h))
                m_new.append(m_h)
            acc_ref[...] = acc_ref[...] * jnp.concatenate(alpha, axis=1) + _dot(ctx_ref[j], p_ref[...])
            m = jnp.concatenate(m_new, axis=1)
        return m

    m = jnp.full((1, H_A * TQ), NEG_INF, F32)
    m = _grouped_loop(0, n_far, m, lambda js, mm: attn_tiles(js, mm, False), GROUPS_DSA)
    _grouped_loop(n_far, nt, m, lambda js, mm: attn_tiles(js, mm, True), GROUPS_NEAR)

    outs = []
    for h in range(H_A):
        cols = slice(h * TQ, (h + 1) * TQ)
        o_t = (acc_ref[0:D_LAT, cols] / acc_ref[D_LAT:D_LAT + 1, cols]).astype(BF16)
        outs.append(_dot(wuvt_ref[h], o_t))
    for pr in range(H_A // 2):
        pair = jnp.concatenate([outs[2 * pr], outs[2 * pr + 1]], axis=0)
        o_ref[:, pr * LANES:(pr + 1) * LANES] = pair.T.astype(BF16)


def _dsa(qit, wit, qlt, ki, ckv, ctx, bias_a, w_uv, b, s):
    TQ = TQ_DSA
    nq, nt = s // TQ, s // TK
    topk = min(TOPK_MAX, s // 4)
    assert s // FOLD_ROWS <= 256
    k4 = ki.reshape(b, nt, TK, DH_IDX)
    c4 = ckv.reshape(b, nt, TK, D_LAT)
    ctx = ctx.reshape(b, nt, D_LAT + ONES_ROWS, TK)
    wuvt = jnp.swapaxes(w_uv, 1, 2).astype(BF16)
    qblk = lambda bb, i: (bb * nq + i, 0, 0)
    kv4 = lambda bb, i: (bb, 0, 0, 0)
    once = dict(pipeline_mode=pl.Buffered(1))
    return pl.pallas_call(
        functools.partial(_dsa_kernel, topk=topk),
        out_shape=jax.ShapeDtypeStruct((b * s, H_A * DH_A), BF16),
        grid=(b, nq),
        in_specs=[
            pl.BlockSpec((None, DH_IDX, H_IDX * TQ), qblk),
            pl.BlockSpec((None, H_IDX, TQ), qblk),
            pl.BlockSpec((None, D_LAT, H_A * TQ), qblk),
            pl.BlockSpec((None, nt, TK, DH_IDX), kv4, **once),
            pl.BlockSpec((None, nt, TK, D_LAT), kv4, **once),
            pl.BlockSpec((None, nt, D_LAT + ONES_ROWS, TK), kv4, **once),
            pl.BlockSpec((H_A * N_BT, LANES, LANES), lambda bb, i: (0, 0, 0), **once),
            pl.BlockSpec((H_A, DH_A, D_LAT), lambda bb, i: (0, 0, 0), **once),
        ],
        out_specs=pl.BlockSpec((TQ, H_A * DH_A), lambda bb, i: (bb * nq + i, 0)),
        scratch_shapes=[
            pltpu.VMEM((nt, TK, TQ), F32),
            pltpu.VMEM((nt, TK, TQ), BF16),
            pltpu.VMEM((D_LAT + ONES_ROWS, H_A * TQ), F32),
        ] + [pltpu.VMEM((TK, H_A * TQ), BF16)] * max(GROUPS_DSA + GROUPS_NEAR),
        compiler_params=pltpu.CompilerParams(vmem_limit_bytes=VMEM_LIMIT),
        name="dsa_attention",
    )(qit, wit, qlt, k4, c4, ctx, bias_a, wuvt)


def _diff_kernel(qt_ref, k_ref, vtx_ref, bias_ref, lam_ref, sub_ref, o_ref, acc_ref, *p_refs, lam_init):
    h = pl.program_id(1)
    i = pl.program_id(2)
    tq = TQ_DIFF
    n_qb = tq // LANES
    t0 = i * tq
    nt = (t0 + tq + TK - 1) // TK
    n_far = jnp.maximum(i * n_qb - N_SAT + 1, 0) // SUB
    qt = qt_ref[...]
    top = lax.broadcasted_iota(jnp.int32, qt.shape, 0) < DH_B
    q2 = jnp.concatenate([jnp.where(top, qt, jnp.zeros_like(qt)), jnp.where(top, jnp.zeros_like(qt), qt)], axis=1)
    qpos = lax.broadcasted_iota(jnp.int32, (1, tq), 1) + t0
    limit = (qpos // CHUNK + 1) * CHUNK
    acc_ref[...] = jnp.zeros(acc_ref.shape, F32)

    def tiles(js, m, near):
        s2s = [_dot(k_ref[j], q2).astype(BF16) for j in js]
        for g, (j, s2) in enumerate(zip(js, s2s)):
            p_ref = p_refs[g]
            if near:
                bias = _bias_tile(bias_ref, h, i * n_qb, n_qb, j)
                adm = (lax.broadcasted_iota(jnp.int32, (TK, tq), 0) + j * TK) < limit
            m_new, alpha = [], []
            for mp in range(2):
                cols = slice(mp * tq, (mp + 1) * tq)
                s = s2[:, cols]
                if near:
                    s = jnp.where(adm, s + bias, jnp.asarray(NEG_INF, BF16))
                m_old = m[:, cols]
                m_h = jnp.maximum(m_old, jnp.max(_fold_rows(s, jnp.maximum), axis=0, keepdims=True).astype(F32))
                p_ref[:, cols] = jnp.exp2(s - m_h.astype(BF16))
                alpha.append(jnp.exp2(m_old - m_h))
                m_new.append(m_h)
            acc_ref[...] = acc_ref[...] * jnp.concatenate(alpha, axis=1) + _dot(vtx_ref[j], p_ref[...])
            m = jnp.concatenate(m_new, axis=1)
        return m

    m = jnp.full((1, 2 * tq), NEG_INF, F32)
    m = _grouped_loop(0, n_far, m, lambda js, mm: tiles(js, mm, False), GROUPS)
    _grouped_loop(n_far, nt, m, lambda js, mm: tiles(js, mm, True), GROUPS_NEAR)

    lf = lam_ref[...]
    lam = (jnp.exp(jnp.sum(lf[0:1, :] * lf[1:2, :], axis=1, keepdims=True))
           - jnp.exp(jnp.sum(lf[2:3, :] * lf[3:4, :], axis=1, keepdims=True)) + lam_init)
    dv = 2 * DH_B
    o = (acc_ref[0:dv, 0:tq] / acc_ref[dv:dv + 1, 0:tq]
         - lam * (acc_ref[0:dv, tq:2 * tq] / acc_ref[dv:dv + 1, tq:2 * tq]))
    o = o * lax.rsqrt(jnp.mean(o * o, axis=0, keepdims=True) + EPS) * sub_ref[...] * (1.0 - lam_init)
    o_ref[...] = o.T.astype(BF16)


def _diff(qbt, kb, vtx, bias_b, b_lambda, b_subln, lam_init, b, s):
    tq = TQ_DIFF
    nq, nt = s // tq, s // TK
    dv = 2 * DH_B
    k4 = kb.reshape(b, nt, TK, H_B * dv)
    vtx = vtx.reshape(b, nt, H_B, dv + ONES_ROWS, TK)
    return pl.pallas_call(
        functools.partial(_diff_kernel, lam_init=lam_init),
        out_shape=jax.ShapeDtypeStruct((b * s, H_B * dv), BF16),
        grid=(b, H_B, nq),
        in_specs=[
            pl.BlockSpec((None, None, dv, tq), lambda bb, h, i: (bb * nq + i, h, 0, 0)),
            pl.BlockSpec((None, nt, TK, dv), lambda bb, h, i: (bb, 0, 0, h)),
            pl.BlockSpec((None, nt, None, dv + ONES_ROWS, TK), lambda bb, h, i: (bb, 0, h, 0, 0)),
            pl.BlockSpec((H_B * N_BT, LANES, LANES), lambda bb, h, i: (0, 0, 0)),
            pl.BlockSpec((4, DH_B), lambda bb, h, i: (0, 0)),
            pl.BlockSpec((dv, 1), lambda bb, h, i: (0, 0)),
        ],
        out_specs=pl.BlockSpec((tq, dv), lambda bb, h, i: (bb * nq + i, h)),
        scratch_shapes=[
            pltpu.VMEM((dv + ONES_ROWS, 2 * tq), F32),
        ] + [pltpu.VMEM((TK, 2 * tq), BF16)] * max(GROUPS),
        compiler_params=pltpu.CompilerParams(vmem_limit_bytes=VMEM_LIMIT),
        name="diff_attention",
    )(qbt, k4, vtx, bias_b, b_lambda.astype(F32), b_subln.reshape(dv, 1).astype(F32))


def _swa_kernel(qt_ref, ka_ref, kb_ref, vta_ref, vtb_ref, bias_ref, sink_ref, o_ref):
    i = pl.program_id(1)
    pad_pen = jnp.where(i == 0, NEG_INF, 0.0)
    for kv in range(H_C_KV):
        qt = qt_ref[kv]
        s = jnp.concatenate([(_dot(ka_ref[kv], qt) + pad_pen).astype(BF16), _dot(kb_ref[kv], qt).astype(BF16)],
                            axis=0) + bias_ref[kv]
        sink = sink_ref[kv]
        m = jnp.maximum(jnp.max(_fold_rows(s, jnp.maximum), axis=0, keepdims=True).astype(F32), sink)
        m = m.astype(BF16).astype(F32)
        p = jnp.exp2(s - m.astype(BF16))
        acc = _dot(vta_ref[kv], p[0:TQ, :]) + _dot(vtb_ref[kv], p[TQ:2 * TQ, :])
        o = acc[0:DH_C, :] / (acc[DH_C:DH_C + 1, :] + jnp.exp2(sink - m))
        for pr in range(G_C // 2):
            pair = jnp.concatenate([o[:, (2 * pr) * TQ:(2 * pr + 1) * TQ], o[:, (2 * pr + 1) * TQ:(2 * pr + 2) * TQ]],
                                   axis=0)
            col = (kv * G_C + 2 * pr) * DH_C
            o_ref[:, col:col + LANES] = pair.T.astype(BF16)


def _swa(qt, k, vtx, table_c, sinks, b, s):
    assert WINDOW == TQ and TQ % CHUNK == 0
    nq = s // TQ
    length = 3 * TQ + 1
    m = np.arange(length)
    diff = np.where(m < 2 * TQ + 1, m, m - length)
    u = jnp.take(table_c.astype(F32), jnp.asarray(_t5_bucket_np(diff - TQ)), axis=0) * LOG2E
    bias = _toeplitz(jnp.moveaxis(u, -1, 0), TQ, 2 * TQ)
    rblk = np.arange(TQ)[:, None] // CHUNK + WINDOW // CHUNK
    cblk = np.arange(2 * TQ)[None, :] // CHUNK
    valid = (cblk <= rblk) & (cblk >= rblk - WINDOW // CHUNK)
    bias = jnp.where(jnp.asarray(valid)[None], bias, NEG_INF)
    bias_t = jnp.transpose(bias.reshape(H_C_KV, G_C, TQ, 2 * TQ), (0, 3, 1, 2)).reshape(
        H_C_KV, 2 * TQ, G_C * TQ).astype(BF16)
    sink_row = jnp.repeat(sinks.astype(F32).reshape(H_C_KV, 1, G_C) * LOG2E, TQ, axis=2)

    dvx = DH_C + ONES_ROWS
    prev = lambda bb, i: bb * nq + jnp.maximum(i - 1, 0)
    return pl.pallas_call(
        _swa_kernel,
        out_shape=jax.ShapeDtypeStruct((b * s, H_C * DH_C), BF16),
        grid=(b, nq),
        in_specs=[
            pl.BlockSpec((None, H_C_KV, DH_C, G_C * TQ), lambda bb, i: (bb * nq + i, 0, 0, 0)),
            pl.BlockSpec((H_C_KV, TQ, DH_C), lambda bb, i: (0, prev(bb, i), 0)),
            pl.BlockSpec((H_C_KV, TQ, DH_C), lambda bb, i: (0, bb * nq + i, 0)),
            pl.BlockSpec((None, H_C_KV, dvx, TQ), lambda bb, i: (prev(bb, i), 0, 0, 0)),
            pl.BlockSpec((None, H_C_KV, dvx, TQ), lambda bb, i: (bb * nq + i, 0, 0, 0)),
            pl.BlockSpec((H_C_KV, 2 * TQ, G_C * TQ), lambda bb, i: (0, 0, 0)),
            pl.BlockSpec((H_C_KV, 1, G_C * TQ), lambda bb, i: (0, 0, 0)),
        ],
        out_specs=pl.BlockSpec((TQ, H_C * DH_C), lambda bb, i: (bb * nq + i, 0)),
        compiler_params=pltpu.CompilerParams(vmem_limit_bytes=VMEM_LIMIT),
        name="swa_attention",
    )(qt, k, k, vtx, vtx, bias_t, sink_row)


def _memkv_kernel(mem_ref, g_ref, w_ref, k_ref, v_ref):
    h = _rms(mem_ref[...], g_ref[...]).astype(BF16)
    kv = _dot(h, w_ref[...])
    k_ref[...] = kv[:, 0:H_X * DH_X].astype(BF16)
    v_ref[...] = kv[:, H_X * DH_X:].astype(BF16)


def _memkv(mem, g, wkv):
    b, m, _ = mem.shape
    n = H_X * DH_X
    return pl.pallas_call(
        _memkv_kernel,
        out_shape=[jax.ShapeDtypeStruct((b, m, n), BF16), jax.ShapeDtypeStruct((b, m, n), BF16)],
        grid=(b,),
        in_specs=[pl.BlockSpec((None, m, D_MODEL), lambda i: (i, 0, 0)),
                  pl.BlockSpec((1, D_MODEL), lambda i: (0, 0)),
                  pl.BlockSpec((D_MODEL, 2 * n), lambda i: (0, 0))],
        out_specs=[pl.BlockSpec((None, m, n), lambda i: (i, 0, 0)), pl.BlockSpec((None, m, n), lambda i: (i, 0, 0))],
        compiler_params=pltpu.CompilerParams(vmem_limit_bytes=VMEM_LIMIT),
        name="mem_kv",
    )(mem, g.reshape(1, D_MODEL), wkv.astype(BF16))


def _lane_half_mask(rows, half):
    lane = lax.broadcasted_iota(jnp.int32, (rows, LANES), 1)
    return (lane < 64) if half == 0 else (lane >= 64)


def _tail_kernel(*refs, n_a):
    x_ref, g_ref = refs[0], refs[1]
    a_refs = refs[2:2 + n_a]
    wout_refs = refs[2 + n_a:2 + 2 * n_a]
    wq_ref, kt_ref, v_ref, wo_ref, w1_ref, w2_ref, o_ref = refs[2 + 2 * n_a:]
    tc = x_ref.shape[0] // TAIL_CHAINS
    chains = [slice(c * tc, (c + 1) * tc) for c in range(TAIL_CHAINS)]
    g = g_ref[...]

    ys = []
    for r in chains:
        y = _dot(a_refs[0][r, :], wout_refs[0][...])
        for a_ref, w_ref in zip(a_refs[1:], wout_refs[1:]):
            y = y + _dot(a_ref[r, :], w_ref[...])
        ys.append(y)
    xs = [x_ref[r, :] + _rms(y, g[1:2, :]) for r, y in zip(chains, ys)]

    hqs = [_rms(x, g[2:3, :]).astype(BF16) for x in xs]
    qs = [_dot(hq, wq_ref[...]) for hq in hqs]
    qs = [(q * (DH_X ** -0.5)).astype(BF16) for q in qs]
    pairs = [[] for _ in chains]
    for pr in range(H_X // 2):
        vp = v_ref[:, pr * LANES:(pr + 1) * LANES]
        outs = [jnp.zeros((tc, LANES), F32) for _ in chains]
        for half in range(2):
            qzs = [jnp.where(_lane_half_mask(tc, half), q[:, pr * LANES:(pr + 1) * LANES], jnp.zeros((tc, LANES), BF16))
                   for q in qs]
            ss = [_dot(qz, kt_ref[pr * LANES:(pr + 1) * LANES, :]) for qz in qzs]
            vz = jnp.where(_lane_half_mask(vp.shape[0], half), vp, jnp.zeros_like(vp))
            ps = []
            for sc in ss:
                e = jnp.exp(sc - jnp.max(sc, axis=1, keepdims=True))
                ps.append((e / jnp.sum(e, axis=1, keepdims=True)).astype(BF16))
            outs = [out + _dot(p, vz) for out, p in zip(outs, ps)]
        for c, out in enumerate(outs):
            pairs[c].append(out.astype(BF16))
    ys = [_dot(jnp.concatenate(pc, axis=1), wo_ref[...]) for pc in pairs]
    xs = [x + _rms(y, g[3:4, :]) for x, y in zip(xs, ys)]

    hms = [_rms(x, g[4:5, :]).astype(BF16) for x in xs]
    n_chunks = D_FF // FF_CHUNK
    ys = [jnp.zeros((tc, D_MODEL), F32) for _ in chains]
    a_cur = [_dot(hm, w1_ref[:, 0:FF_CHUNK]) for hm in hms]
    for k in range(n_chunks):
        if k + 1 < n_chunks:
            a_next = [_dot(hm, w1_ref[:, (k + 1) * FF_CHUNK:(k + 2) * FF_CHUNK]) for hm in hms]
        acts = [jnp.square(jnp.maximum(a, 0.0)).astype(BF16) for a in a_cur]
        ys = [y + _dot(act, w2_ref[k * FF_CHUNK:(k + 1) * FF_CHUNK, :]) for y, act in zip(ys, acts)]
        if k + 1 < n_chunks:
            a_cur = a_next
    for r, x, y in zip(chains, xs, ys):
        o_ref[r, :] = x + _rms(y, g[5:6, :])


def _tail(x2, g, a_list, wout_list, wq, kt, v, wo, w1, w2, b, s):
    t = x2.shape[0]
    tm = min(TM_TAIL, s)
    per_b = s // tm
    n_a = len(a_list)
    row = lambda i: (i, 0)
    full2 = lambda i: (0, 0)
    once = dict(pipeline_mode=pl.Buffered(1))
    in_specs = [pl.BlockSpec((tm, D_MODEL), row), pl.BlockSpec((6, D_MODEL), full2)]
    in_specs += [pl.BlockSpec((tm, a.shape[1]), row) for a in a_list]
    in_specs += [pl.BlockSpec(w.shape, full2, **once) for w in wout_list]
    in_specs += [
        pl.BlockSpec(wq.shape, full2, **once),
        pl.BlockSpec((None,) + kt.shape[1:], lambda i: (i // per_b, 0, 0)),
        pl.BlockSpec((None,) + v.shape[1:], lambda i: (i // per_b, 0, 0)),
        pl.BlockSpec(wo.shape, full2, **once),
        pl.BlockSpec(w1.shape, full2, **once),
        pl.BlockSpec(w2.shape, full2, **once),
    ]
    return pl.pallas_call(
        functools.partial(_tail_kernel, n_a=n_a),
        out_shape=jax.ShapeDtypeStruct((t, D_MODEL), F32),
        grid=(t // tm,),
        in_specs=in_specs,
        out_specs=pl.BlockSpec((tm, D_MODEL), row),
        compiler_params=pltpu.CompilerParams(vmem_limit_bytes=VMEM_LIMIT),
        name="tail",
    )(x2, g, *a_list, *wout_list, wq, kt, v, wo, w1, w2)


def kernel(x, mem, rel_bias_table, norm_g, ev_w_in, ev_a_kv_norm, ev_a_w_uk, ev_a_w_uv, ev_b_lambda, ev_b_subln, ev_w_out, od_w_in, od_sinks, od_w_out, xa_wq, xa_wkv, xa_wo, xa_mem_norm, mlp_w1, mlp_w2):
    b, s, d = x.shape
    depth = norm_g.shape[0]
    assert d == D_MODEL and s % TK == 0 and TK % TQ_DIFF == 0 and TK % TQ_DSA == 0 and TQ_DSA % LANES == 0
    x2 = x.reshape(b * s, d)
    bias_a = _causal_bias_tiles(rel_bias_table[:, :H_A])
    bias_b = _causal_bias_tiles(rel_bias_table[:, H_A:H_A + H_B])
    table_c = rel_bias_table[:, H_A + H_B:]
    for l in range(depth):
        g = norm_g[l].astype(F32)
        if l % 2 == 0:
            e = l // 2
            lam_init = 0.8 - 0.6 * math.exp(-0.3 * l)
            qit, wit, qlt, ki, ckv, ctx, qbt, kb, vtx = _proj_even(x2, g, ev_w_in[e], ev_a_kv_norm[e], ev_a_w_uk[e])
            oa = _dsa(qit, wit, qlt, ki, ckv, ctx, bias_a, ev_a_w_uv[e], b, s)
            ob = _diff(qbt, kb, vtx, bias_b, ev_b_lambda[e], ev_b_subln[e], lam_init, b, s)
            n_a = H_A * DH_A
            a_list = [oa, ob]
            wout_list = [ev_w_out[e][:n_a].astype(BF16), ev_w_out[e][n_a:].astype(BF16)]
        else:
            o = l // 2
            qt, k, vtx = _proj_odd(x2, g, od_w_in[o])
            a_list = [_swa(qt, k, vtx, table_c, od_sinks[o], b, s)]
            wout_list = [od_w_out[o].astype(BF16)]
        mk, mv = _memkv(mem, xa_mem_norm[l], xa_wkv[l])
        x2 = _tail(x2, g, a_list, wout_list, xa_wq[l].astype(BF16), jnp.swapaxes(mk, 1, 2), mv,
                   xa_wo[l].astype(BF16), mlp_w1[l].astype(BF16), mlp_w2[l].astype(BF16), b, s)
    return x2.reshape(b, s, d)
```

```python
import functools
import math

import numpy as np
import jax
import jax.numpy as jnp
from jax import lax
from jax.experimental import pallas as pl
from jax.experimental.pallas import tpu as pltpu

F32 = jnp.float32
BF16 = jnp.bfloat16

D_MODEL = 1024
CHUNK = 64
EPS = 1e-6
NEG_INF = -1e30
LOG2E = math.log2(math.e)

H_A, DH_A, D_LAT = 8, 64, 128
H_IDX, DH_IDX = 8, 64
TOPK_MAX = 256
H_B, DH_B = 4, 64
H_C, H_C_KV, DH_C = 16, 2, 64
G_C = H_C // H_C_KV
WINDOW = 128
H_X, DH_X = 4, 64
D_FF = 4 * D_MODEL
N_BUCKETS = 32
MAX_DIST = 1024

LANES = 128
FOLD_ROWS = 64
TQ = 128
TQ_DSA = 256
TQ_DIFF = 512
TK = 512
SUB = TK // LANES
ONES_ROWS = 16
TM_PROJ = 1024
TM_TAIL = 1024
TAIL_CHAINS = 2
FF_CHUNK = 1024
VMEM_LIMIT = 56 * 1024 * 1024
N_COARSE = 9
N_BISECT = 7
GROUPS = (4, 2, 1)
GROUPS_NEAR = (3, 2, 1)
GROUPS_DSA = (4, 2, 1)
GROUPS_PASS = (4, 1)


def _rms(x, g):
    return x * lax.rsqrt(jnp.mean(x * x, axis=-1, keepdims=True) + EPS) * g


def _dot(a, b):
    return jnp.dot(a, b, preferred_element_type=F32)


def _t5_bucket_np(rel):
    nb = N_BUCKETS // 2
    max_exact = nb // 2
    n = np.abs(rel)
    nf = np.maximum(n, 1).astype(np.float32)
    large = max_exact + (np.log(nf / max_exact) / math.log(MAX_DIST / max_exact) * (nb - max_exact)).astype(np.int32)
    large = np.minimum(large, nb - 1)
    return np.where(rel > 0, nb, 0) + np.where(n < max_exact, n, large)


def _sat_blocks():
    d = 1
    while _t5_bucket_np(np.array([-(d * LANES - (LANES - 1))]))[0] != N_BUCKETS // 2 - 1:
        d += 1
    return d


N_SAT = _sat_blocks()
N_BT = N_SAT + 1


def _toeplitz(u, rows, cols):
    length = u.shape[-1]
    flat = jnp.tile(u, (1,) * (u.ndim - 1) + (rows,))[..., :rows * (length - 1)]
    return flat.reshape(u.shape[:-1] + (rows, length - 1))[..., :cols]


def _causal_bias_tiles(table_cols):
    length = 2 * LANES + 1
    m = np.arange(length)
    diff = np.where(m < LANES + 1, m, m - length)
    rel = -diff[None, :] - (np.arange(N_BT) * LANES)[:, None]
    tab = table_cols.astype(F32)
    u = jnp.take(tab, jnp.asarray(_t5_bucket_np(rel)), axis=0)
    u = (u - tab[N_BUCKETS // 2 - 1][None, None, :]) * LOG2E
    tiles = _toeplitz(jnp.moveaxis(u, -1, 0), LANES, LANES)
    return tiles.reshape(table_cols.shape[1] * N_BT, LANES, LANES).astype(BF16)


def _fold_rows(x, op):
    acc = x[0:FOLD_ROWS, :]
    for r in range(1, x.shape[0] // FOLD_ROWS):
        acc = op(acc, x[r * FOLD_ROWS:(r + 1) * FOLD_ROWS, :])
    return acc


def _grouped_loop(lo, hi, carry, fn, groups):
    for g in groups:
        n = (hi - lo) // g
        carry = lax.fori_loop(0, n, lambda u, c, lo=lo, g=g: fn([lo + u * g + k for k in range(g)], c), carry)
        lo = lo + n * g
    return carry


def _bias_tile(bias_ref, h, qb0, n_qb, j):
    rows = []
    for c in range(SUB):
        cols = [bias_ref[h * N_BT + jnp.clip(qb0 + r - (j * SUB + c), 0, N_SAT)] for r in range(n_qb)]
        rows.append(cols[0] if n_qb == 1 else jnp.concatenate(cols, axis=1))
    return jnp.concatenate(rows, axis=0)


def _ones_rows(n):
    return jnp.ones((ONES_ROWS, n), BF16)


def _proj_even_kernel(x_ref, g_ref, w_ref, kvn_ref, wuk_ref,
                      qit_ref, wit_ref, qlt_ref, ki_ref, c_ref, ctx_ref, qbt_ref, kb_ref, vtx_ref):
    tm = x_ref.shape[0]
    h = _rms(x_ref[...], g_ref[0:1, :]).astype(BF16)
    z = _dot(h, w_ref[...])
    qa = z[:, 0:512].astype(BF16)
    ckv = _rms(z[:, 512:640], kvn_ref[...])
    c_ref[...] = ckv.astype(BF16)
    kb_ref[...] = z[:, 1664:2176].astype(BF16)
    ki_ref[...] = z[:, 2688:2688 + DH_IDX].astype(BF16)
    ctx_ref[0, D_LAT:, :] = _ones_rows(tm)
    for hb in range(H_B):
        vtx_ref[0, hb, 2 * DH_B:, :] = _ones_rows(tm)
    per_q, per_diff = TQ_DSA // LANES, TQ_DIFF // LANES
    for hh in range(H_A):
        ql = _dot(qa[:, hh * DH_A:(hh + 1) * DH_A], wuk_ref[hh]) * (DH_A ** -0.5 * LOG2E)
        for sb in range(tm // LANES):
            off = hh * TQ_DSA + (sb % per_q) * LANES
            qlt_ref[sb // per_q, :, off:off + LANES] = ql[sb * LANES:(sb + 1) * LANES, :].T.astype(BF16)
    for sb in range(tm // LANES):
        rows = slice(sb * LANES, (sb + 1) * LANES)
        blk, sub = sb // per_q, (sb % per_q) * LANES
        for pr in range(H_IDX // 2):
            t = z[rows, 640 + pr * LANES:640 + (pr + 1) * LANES].T
            for half in range(2):
                off = (2 * pr + half) * TQ_DSA + sub
                qit_ref[blk, :, off:off + LANES] = t[half * DH_IDX:(half + 1) * DH_IDX, :].astype(BF16)
        wit_ref[blk, :, sub:sub + LANES] = z[rows, 2816:2816 + LANES].T[0:H_IDX, :]
        ctx_ref[0, 0:D_LAT, rows] = ckv[rows, :].T.astype(BF16)
        for hb in range(H_B):
            qb = z[rows, 1152 + hb * LANES:1152 + (hb + 1) * LANES] * (DH_B ** -0.5 * LOG2E)
            lanes = slice((sb % per_diff) * LANES, (sb % per_diff + 1) * LANES)
            qbt_ref[sb // per_diff, hb, :, lanes] = qb.T.astype(BF16)
            vtx_ref[0, hb, 0:2 * DH_B, rows] = z[rows, 2176 + hb * LANES:2176 + (hb + 1) * LANES].T.astype(BF16)


def _proj_even(x2, g, w_in, kvn, w_uk):
    t = x2.shape[0]
    tm = TK
    assert t % tm == 0 and tm % TQ_DSA == 0 and tm % TQ_DIFF == 0
    pad = jnp.zeros((D_MODEL, LANES - DH_IDX), w_in.dtype)
    cols = [w_in[:, 0:512], w_in[:, 512:640], w_in[:, 640:1152], w_in[:, 1224:1736], w_in[:, 1736:2248],
            w_in[:, 2248:2760], w_in[:, 1152:1216], pad, w_in[:, 1216:1224],
            jnp.zeros((D_MODEL, LANES - H_IDX), w_in.dtype)]
    w = jnp.concatenate(cols, axis=1).astype(BF16)
    n = w.shape[1]
    dv = 2 * DH_B
    row = lambda i: (i, 0)
    full2 = lambda i: (0, 0)
    lead3 = lambda i: (i, 0, 0)
    lead4 = lambda i: (i, 0, 0, 0)
    outs = [
        ((t // TQ_DSA, DH_IDX, H_IDX * TQ_DSA), BF16, (tm // TQ_DSA, DH_IDX, H_IDX * TQ_DSA), lead3),
        ((t // TQ_DSA, H_IDX, TQ_DSA), F32, (tm // TQ_DSA, H_IDX, TQ_DSA), lead3),
        ((t // TQ_DSA, D_LAT, H_A * TQ_DSA), BF16, (tm // TQ_DSA, D_LAT, H_A * TQ_DSA), lead3),
        ((t, DH_IDX), BF16, (tm, DH_IDX), row),
        ((t, D_LAT), BF16, (tm, D_LAT), row),
        ((t // TK, D_LAT + ONES_ROWS, TK), BF16, (1, D_LAT + ONES_ROWS, TK), lead3),
        ((t // TQ_DIFF, H_B, dv, TQ_DIFF), BF16, (tm // TQ_DIFF, H_B, dv, TQ_DIFF), lead4),
        ((t, H_B * dv), BF16, (tm, H_B * dv), row),
        ((t // TK, H_B, dv + ONES_ROWS, TK), BF16, (1, H_B, dv + ONES_ROWS, TK), lead4),
    ]
    return pl.pallas_call(
        _proj_even_kernel,
        out_shape=[jax.ShapeDtypeStruct(shape, dt) for shape, dt, _, _ in outs],
        grid=(t // tm,),
        in_specs=[
            pl.BlockSpec((tm, D_MODEL), row),
            pl.BlockSpec((6, D_MODEL), full2),
            pl.BlockSpec((D_MODEL, n), full2),
            pl.BlockSpec((1, D_LAT), full2),
            pl.BlockSpec((H_A, DH_A, D_LAT), lambda i: (0, 0, 0)),
        ],
        out_specs=[pl.BlockSpec(block, imap) for _, _, block, imap in outs],
        compiler_params=pltpu.CompilerParams(vmem_limit_bytes=VMEM_LIMIT),
        name="proj_even",
    )(x2, g, w, kvn.reshape(1, D_LAT), w_uk.astype(BF16))


def _proj_odd_kernel(x_ref, g_ref, w_ref, qt_ref, k_ref, vtx_ref):
    tm = x_ref.shape[0]
    h = _rms(x_ref[...], g_ref[0:1, :]).astype(BF16)
    z = _dot(h, w_ref[...])
    for kv in range(H_C_KV):
        k_ref[kv] = z[:, 1024 + kv * DH_C:1024 + (kv + 1) * DH_C].astype(BF16)
    for sb in range(tm // TQ):
        rows = slice(sb * TQ, (sb + 1) * TQ)
        for pr in range(H_C // 2):
            t = (z[rows, pr * LANES:(pr + 1) * LANES] * (DH_C ** -0.5 * LOG2E)).T
            for half in range(2):
                hd = 2 * pr + half
                kv, gq = hd // G_C, hd % G_C
                qt_ref[sb, kv, :, gq * TQ:(gq + 1) * TQ] = t[half * DH_C:(half + 1) * DH_C, :].astype(BF16)
        vt = z[rows, 1152:1280].T
        for kv in range(H_C_KV):
            vtx_ref[sb, kv, 0:DH_C, :] = vt[kv * DH_C:(kv + 1) * DH_C, :].astype(BF16)
            vtx_ref[sb, kv, DH_C:, :] = _ones_rows(TQ)


def _proj_odd(x2, g, w_in):
    t = x2.shape[0]
    tm = min(TM_PROJ, t)
    assert TQ == LANES
    row = lambda i: (i, 0)
    full2 = lambda i: (0, 0)
    lead4 = lambda i: (i, 0, 0, 0)
    dvx = DH_C + ONES_ROWS
    return pl.pallas_call(
        _proj_odd_kernel,
        out_shape=[jax.ShapeDtypeStruct((t // TQ, H_C_KV, DH_C, G_C * TQ), BF16),
                   jax.ShapeDtypeStruct((H_C_KV, t, DH_C), BF16),
                   jax.ShapeDtypeStruct((t // TQ, H_C_KV, dvx, TQ), BF16)],
        grid=(t // tm,),
        in_specs=[pl.BlockSpec((tm, D_MODEL), row),
                  pl.BlockSpec((6, D_MODEL), full2),
                  pl.BlockSpec((D_MODEL, 1280), full2)],
        out_specs=[pl.BlockSpec((tm // TQ, H_C_KV, DH_C, G_C * TQ), lead4),
                   pl.BlockSpec((H_C_KV, tm, DH_C), lambda i: (0, i, 0)),
                   pl.BlockSpec((tm // TQ, H_C_KV, dvx, TQ), lead4)],
        compiler_params=pltpu.CompilerParams(vmem_limit_bytes=VMEM_LIMIT),
        name="proj_odd",
    )(x2, g, w_in.astype(BF16))


def _dsa_kernel(qit_ref, wit_ref, qlt_ref, k_ref, c_ref, ctx_ref, bias_ref, wuvt_ref, o_ref,
                sc_ref, scb_ref, acc_ref, *p_refs, topk):
    tq = TQ_DSA
    i = pl.program_id(1)
    t0 = i * tq
    nt = (t0 + tq + TK - 1) // TK
    n_qb = tq // LANES
    n_far = jnp.maximum(i * n_qb - N_SAT + 1, 0) // SUB
    kf = float(topk)

    qpos = lax.broadcasted_iota(jnp.int32, (1, tq), 1) + t0
    limit = (qpos // CHUNK + 1) * CHUNK
    small = limit <= topk
    big = jnp.logical_not(small)
    w = wit_ref[...]
    qit = qit_ref[...]

    def score_tiles(js, carry):
        rmax, rmin = carry
        zs = [_dot(k_ref[j], qit) for j in js]
        for j, z in zip(js, zs):
            sc = jnp.maximum(z[:, 0:tq], 0.0) * w[0:1, :]
            for h in range(1, H_IDX):
                sc = sc + jnp.maximum(z[:, h * tq:(h + 1) * tq], 0.0) * w[h:h + 1, :]
            kpos = lax.broadcasted_iota(jnp.int32, (TK, tq), 0) + j * TK
            adm = kpos < limit
            sc_ref[j] = jnp.where(adm, sc, -jnp.inf)
            scb_ref[j] = jnp.where(adm, sc, -jnp.inf).astype(BF16)
            rmax = jnp.maximum(rmax, _fold_rows(jnp.where(adm, sc, -jnp.inf), jnp.maximum))
            rmin = jnp.minimum(rmin, _fold_rows(jnp.where(adm, sc, jnp.inf), jnp.minimum))
        return rmax, rmin

    rmax, rmin = _grouped_loop(
        0, nt, (jnp.full((FOLD_ROWS, tq), -jnp.inf, F32), jnp.full((FOLD_ROWS, tq), jnp.inf, F32)),
        score_tiles, GROUPS_DSA)
    hi0 = jnp.max(rmax, axis=0, keepdims=True)
    lo0 = jnp.min(rmin, axis=0, keepdims=True)

    def col_reduce(tile_fn, op, init, final):
        def fn(js, acc):
            for j in js:
                acc = op(acc, _fold_rows(tile_fn(j), op))
            return acc
        acc = _grouped_loop(0, nt, jnp.full((FOLD_ROWS, tq), init, F32), fn, GROUPS_PASS)
        return final(acc, axis=0, keepdims=True)

    def count_ge(thr):
        return col_reduce(lambda j: jnp.where(sc_ref[j] >= thr, 1.0, 0.0), jnp.add, 0.0, jnp.sum)

    def max_le(thr):
        def tile(j):
            t = sc_ref[j]
            return jnp.where(t <= thr, t, -jnp.inf)
        return col_reduce(tile, jnp.maximum, -jnp.inf, jnp.max)

    def max_lt(thr):
        def tile(j):
            t = sc_ref[j]
            return jnp.where(t < thr, t, -jnp.inf)
        return col_reduce(tile, jnp.maximum, -jnp.inf, jnp.max)

    def count_ge_coarse(thr):
        def fn(js, acc):
            for j in js:
                hit = jnp.where(scb_ref[j] >= thr, jnp.ones((), BF16), jnp.zeros((), BF16))
                acc = acc + _fold_rows(hit, jnp.add)
            return acc
        acc = _grouped_loop(0, nt, jnp.zeros((FOLD_ROWS, tq), BF16), fn, GROUPS_PASS)
        return jnp.sum(acc.astype(F32), axis=0, keepdims=True)

    def margin(v):
        return jnp.abs(v) * 2.0 ** -6 + 1e-30

    def coarse_body(_, carry):
        lo, hi = carry
        mid = (lo * 0.5 + hi * 0.5).astype(BF16)
        ge = count_ge_coarse(mid) >= kf
        mid = mid.astype(F32)
        return jnp.where(ge, mid, lo), jnp.where(ge, hi, mid)

    lo_c, hi_c = lax.fori_loop(0, N_COARSE, coarse_body,
                               ((lo0 - margin(lo0)).astype(BF16).astype(F32),
                                (hi0 + margin(hi0)).astype(BF16).astype(F32)))

    def bisect_body(_, carry):
        lo, hi = carry
        mid = lo * 0.5 + hi * 0.5
        ge = count_ge(mid) >= kf
        return jnp.where(ge, mid, lo), jnp.where(ge, hi, mid)

    lo, hi = lax.fori_loop(0, N_BISECT, bisect_body,
                           (jnp.maximum(lo_c - margin(lo_c), lo0), jnp.minimum(hi_c, hi0)))

    v0 = max_le(hi)
    c0 = count_ge(v0)

    def walk_cond(carry):
        _, cnt, it = carry
        pending = jnp.where((cnt < kf) & big, 1.0, 0.0)
        return (jnp.max(pending) > 0.0) & (it < topk + 2)

    def walk_body(carry):
        v, cnt, it = carry
        v = jnp.where((cnt < kf) & big, max_lt(v), v)
        return v, count_ge(v), it + 1

    tau, cnt, _ = lax.while_loop(walk_cond, walk_body, (v0, c0, jnp.int32(0)))
    tau = jnp.where(small, -jnp.finfo(F32).max, tau)

    tied = (cnt > kf) & big

    @pl.when(jnp.max(jnp.where(tied, 1.0, 0.0)) > 0.0)
    def _():
        excess = jnp.where(tied, cnt - kf, 0.0)
        kr = lax.broadcasted_iota(jnp.int32, (TK, TK), 0)
        kc = lax.broadcasted_iota(jnp.int32, (TK, TK), 1)
        upper = jnp.where(kc >= kr, 1.0, 0.0).astype(BF16)

        def drop_body(u, later):
            j = nt - 1 - u
            t = sc_ref[j]
            e = (t == tau) & tied
            behind = _dot(upper, jnp.where(e, 1.0, 0.0).astype(BF16)) + later
            sc_ref[j] = jnp.where(e & (behind <= excess), -jnp.inf, t)
            return behind[0:1, :]

        lax.fori_loop(0, nt, drop_body, jnp.zeros((1, tq), F32))

    acc_ref[...] = jnp.zeros(acc_ref.shape, F32)
    qlt = qlt_ref[...]

    def attn_tiles(js, m, with_bias):
        s_alls = [_dot(c_ref[j], qlt).astype(BF16) for j in js]
        for g, (j, s_all) in enumerate(zip(js, s_alls)):
            p_ref = p_refs[g]
            pen = jnp.where(sc_ref[j] >= tau, 0.0, NEG_INF).astype(BF16)
            m_new, alpha = [], []
            for h in range(H_A):
                cols = slice(h * tq, (h + 1) * tq)
                s = s_all[:, cols] + pen
                if with_bias:
                    s = s + _bias_tile(bias_ref, h, i * n_qb, n_qb, j)
                m_old = m[:, cols]
                m_h = jnp.maximum(m_old, jnp.max(_fold_rows(s, jnp.maximum), axis=0, keepdims=True).astype(F32))
                p_ref[:, cols] = jnp.exp2(s - m_h.astype(BF16))
                alpha.append(jnp.exp2(m_old - m_h))
                m_new.append(m_h)
            acc_ref[...] = acc_ref[...] * jnp.concatenate(alpha, axis=1) + _dot(ctx_ref[j], p_ref[...])
            m = jnp.concatenate(m_new, axis=1)
        return m

    m = jnp.full((1, H_A * tq), NEG_INF, F32)
    m = _grouped_loop(0, n_far, m, lambda js, mm: attn_tiles(js, mm, False), GROUPS_DSA)
    _grouped_loop(n_far, nt, m, lambda js, mm: attn_tiles(js, mm, True), GROUPS_NEAR)

    outs = []
    for h in range(H_A):
        cols = slice(h * tq, (h + 1) * tq)
        o_t = (acc_ref[0:D_LAT, cols] / acc_ref[D_LAT:D_LAT + 1, cols]).astype(BF16)
        outs.append(_dot(wuvt_ref[h], o_t))
    for pr in range(H_A // 2):
        pair = jnp.concatenate([outs[2 * pr], outs[2 * pr + 1]], axis=0)
        o_ref[:, pr * LANES:(pr + 1) * LANES] = pair.T.astype(BF16)


def _dsa(qit, wit, qlt, ki, ckv, ctx, bias_a, w_uv, b, s):
    tq = TQ_DSA
    nq, nt = s // tq, s // TK
    topk = min(TOPK_MAX, s // 4)
    assert s // FOLD_ROWS <= 256
    k4 = ki.reshape(b, nt, TK, DH_IDX)
    c4 = ckv.reshape(b, nt, TK, D_LAT)
    ctx = ctx.reshape(b, nt, D_LAT + ONES_ROWS, TK)
    wuvt = jnp.swapaxes(w_uv, 1, 2).astype(BF16)
    qblk = lambda bb, i: (bb * nq + i, 0, 0)
    kv4 = lambda bb, i: (bb, 0, 0, 0)
    once = dict(pipeline_mode=pl.Buffered(1))
    return pl.pallas_call(
        functools.partial(_dsa_kernel, topk=topk),
        out_shape=jax.ShapeDtypeStruct((b * s, H_A * DH_A), BF16),
        grid=(b, nq),
        in_specs=[
            pl.BlockSpec((None, DH_IDX, H_IDX * tq), qblk),
            pl.BlockSpec((None, H_IDX, tq), qblk),
            pl.BlockSpec((None, D_LAT, H_A * tq), qblk),
            pl.BlockSpec((None, nt, TK, DH_IDX), kv4, **once),
            pl.BlockSpec((None, nt, TK, D_LAT), kv4, **once),
            pl.BlockSpec((None, nt, D_LAT + ONES_ROWS, TK), kv4, **once),
            pl.BlockSpec((H_A * N_BT, LANES, LANES), lambda bb, i: (0, 0, 0), **once),
            pl.BlockSpec((H_A, DH_A, D_LAT), lambda bb, i: (0, 0, 0), **once),
        ],
        out_specs=pl.BlockSpec((tq, H_A * DH_A), lambda bb, i: (bb * nq + i, 0)),
        scratch_shapes=[
            pltpu.VMEM((nt, TK, tq), F32),
            pltpu.VMEM((nt, TK, tq), BF16),
            pltpu.VMEM((D_LAT + ONES_ROWS, H_A * tq), F32),
        ] + [pltpu.VMEM((TK, H_A * tq), BF16)] * max(GROUPS_DSA + GROUPS_NEAR),
        compiler_params=pltpu.CompilerParams(vmem_limit_bytes=VMEM_LIMIT),
        name="dsa_attention",
    )(qit, wit, qlt, k4, c4, ctx, bias_a, wuvt)


def _diff_kernel(qt_ref, k_ref, vtx_ref, bias_ref, lam_ref, sub_ref, o_ref, acc_ref, *p_refs, lam_init):
    h = pl.program_id(1)
    i = pl.program_id(2)
    tq = TQ_DIFF
    n_qb = tq // LANES
    t0 = i * tq
    nt = (t0 + tq + TK - 1) // TK
    n_far = jnp.maximum(i * n_qb - N_SAT + 1, 0) // SUB
    qt = qt_ref[...]
    top = lax.broadcasted_iota(jnp.int32, qt.shape, 0) < DH_B
    q2 = jnp.concatenate([jnp.where(top, qt, jnp.zeros_like(qt)), jnp.where(top, jnp.zeros_like(qt), qt)], axis=1)
    qpos = lax.broadcasted_iota(jnp.int32, (1, tq), 1) + t0
    limit = (qpos // CHUNK + 1) * CHUNK
    acc_ref[...] = jnp.zeros(acc_ref.shape, F32)

    def tiles(js, m, near):
        s2s = [_dot(k_ref[j], q2).astype(BF16) for j in js]
        for g, (j, s2) in enumerate(zip(js, s2s)):
            p_ref = p_refs[g]
            if near:
                bias = _bias_tile(bias_ref, h, i * n_qb, n_qb, j)
                adm = (lax.broadcasted_iota(jnp.int32, (TK, tq), 0) + j * TK) < limit
            m_new, alpha = [], []
            for mp in range(2):
                cols = slice(mp * tq, (mp + 1) * tq)
                s = s2[:, cols]
                if near:
                    s = jnp.where(adm, s + bias, jnp.asarray(NEG_INF, BF16))
                m_old = m[:, cols]
                m_h = jnp.maximum(m_old, jnp.max(_fold_rows(s, jnp.maximum), axis=0, keepdims=True).astype(F32))
                p_ref[:, cols] = jnp.exp2(s - m_h.astype(BF16))
                alpha.append(jnp.exp2(m_old - m_h))
                m_new.append(m_h)
            acc_ref[...] = acc_ref[...] * jnp.concatenate(alpha, axis=1) + _dot(vtx_ref[j], p_ref[...])
            m = jnp.concatenate(m_new, axis=1)
        return m

    m = jnp.full((1, 2 * tq), NEG_INF, F32)
    m = _grouped_loop(0, n_far, m, lambda js, mm: tiles(js, mm, False), GROUPS)
    _grouped_loop(n_far, nt, m, lambda js, mm: tiles(js, mm, True), GROUPS_NEAR)

    lf = lam_ref[...]
    lam = (jnp.exp(jnp.sum(lf[0:1, :] * lf[1:2, :], axis=1, keepdims=True))
           - jnp.exp(jnp.sum(lf[2:3, :] * lf[3:4, :], axis=1, keepdims=True)) + lam_init)
    dv = 2 * DH_B
    o = (acc_ref[0:dv, 0:tq] / acc_ref[dv:dv + 1, 0:tq]
         - lam * (acc_ref[0:dv, tq:2 * tq] / acc_ref[dv:dv + 1, tq:2 * tq]))
    o = o * lax.rsqrt(jnp.mean(o * o, axis=0, keepdims=True) + EPS) * sub_ref[...] * (1.0 - lam_init)
    o_ref[...] = o.T.astype(BF16)


def _diff(qbt, kb, vtx, bias_b, b_lambda, b_subln, lam_init, b, s):
    tq = TQ_DIFF
    nq, nt = s // tq, s // TK
    dv = 2 * DH_B
    k4 = kb.reshape(b, nt, TK, H_B * dv)
    vtx = vtx.reshape(b, nt, H_B, dv + ONES_ROWS, TK)
    return pl.pallas_call(
        functools.partial(_diff_kernel, lam_init=lam_init),
        out_shape=jax.ShapeDtypeStruct((b * s, H_B * dv), BF16),
        grid=(b, H_B, nq),
        in_specs=[
            pl.BlockSpec((None, None, dv, tq), lambda bb, h, i: (bb * nq + i, h, 0, 0)),
            pl.BlockSpec((None, nt, TK, dv), lambda bb, h, i: (bb, 0, 0, h)),
            pl.BlockSpec((None, nt, None, dv + ONES_ROWS, TK), lambda bb, h, i: (bb, 0, h, 0, 0)),
            pl.BlockSpec((H_B * N_BT, LANES, LANES), lambda bb, h, i: (0, 0, 0)),
            pl.BlockSpec((4, DH_B), lambda bb, h, i: (0, 0)),
            pl.BlockSpec((dv, 1), lambda bb, h, i: (0, 0)),
        ],
        out_specs=pl.BlockSpec((tq, dv), lambda bb, h, i: (bb * nq + i, h)),
        scratch_shapes=[
            pltpu.VMEM((dv + ONES_ROWS, 2 * tq), F32),
        ] + [pltpu.VMEM((TK, 2 * tq), BF16)] * max(GROUPS),
        compiler_params=pltpu.CompilerParams(vmem_limit_bytes=VMEM_LIMIT),
        name="diff_attention",
    )(qbt, k4, vtx, bias_b, b_lambda.astype(F32), b_subln.reshape(dv, 1).astype(F32))


def _swa_kernel(qt_ref, ka_ref, kb_ref, vta_ref, vtb_ref, bias_ref, sink_ref, o_ref):
    i = pl.program_id(1)
    pad_pen = jnp.where(i == 0, NEG_INF, 0.0)
    for kv in range(H_C_KV):
        qt = qt_ref[kv]
        s = jnp.concatenate([(_dot(ka_ref[kv], qt) + pad_pen).astype(BF16), _dot(kb_ref[kv], qt).astype(BF16)],
                            axis=0) + bias_ref[kv]
        sink = sink_ref[kv]
        m = jnp.maximum(jnp.max(_fold_rows(s, jnp.maximum), axis=0, keepdims=True).astype(F32), sink)
        m = m.astype(BF16).astype(F32)
        p = jnp.exp2(s - m.astype(BF16))
        acc = _dot(vta_ref[kv], p[0:TQ, :]) + _dot(vtb_ref[kv], p[TQ:2 * TQ, :])
        o = acc[0:DH_C, :] / (acc[DH_C:DH_C + 1, :] + jnp.exp2(sink - m))
        for pr in range(G_C // 2):
            pair = jnp.concatenate([o[:, (2 * pr) * TQ:(2 * pr + 1) * TQ], o[:, (2 * pr + 1) * TQ:(2 * pr + 2) * TQ]],
                                   axis=0)
            col = (kv * G_C + 2 * pr) * DH_C
            o_ref[:, col:col + LANES] = pair.T.astype(BF16)


def _swa(qt, k, vtx, table_c, sinks, b, s):
    assert WINDOW == TQ and TQ % CHUNK == 0
    nq = s // TQ
    length = 3 * TQ + 1
    m = np.arange(length)
    diff = np.where(m < 2 * TQ + 1, m, m - length)
    u = jnp.take(table_c.astype(F32), jnp.asarray(_t5_bucket_np(diff - TQ)), axis=0) * LOG2E
    bias = _toeplitz(jnp.moveaxis(u, -1, 0), TQ, 2 * TQ)
    rblk = np.arange(TQ)[:, None] // CHUNK + WINDOW // CHUNK
    cblk = np.arange(2 * TQ)[None, :] // CHUNK
    valid = (cblk <= rblk) & (cblk >= rblk - WINDOW // CHUNK)
    bias = jnp.where(jnp.asarray(valid)[None], bias, NEG_INF)
    bias_t = jnp.transpose(bias.reshape(H_C_KV, G_C, TQ, 2 * TQ), (0, 3, 1, 2)).reshape(
        H_C_KV, 2 * TQ, G_C * TQ).astype(BF16)
    sink_row = jnp.repeat(sinks.astype(F32).reshape(H_C_KV, 1, G_C) * LOG2E, TQ, axis=2)

    dvx = DH_C + ONES_ROWS
    prev = lambda bb, i: bb * nq + jnp.maximum(i - 1, 0)
    return pl.pallas_call(
        _swa_kernel,
        out_shape=jax.ShapeDtypeStruct((b * s, H_C * DH_C), BF16),
        grid=(b, nq),
        in_specs=[
            pl.BlockSpec((None, H_C_KV, DH_C, G_C * TQ), lambda bb, i: (bb * nq + i, 0, 0, 0)),
            pl.BlockSpec((H_C_KV, TQ, DH_C), lambda bb, i: (0, prev(bb, i), 0)),
            pl.BlockSpec((H_C_KV, TQ, DH_C), lambda bb, i: (0, bb * nq + i, 0)),
            pl.BlockSpec((None, H_C_KV, dvx, TQ), lambda bb, i: (prev(bb, i), 0, 0, 0)),
            pl.BlockSpec((None, H_C_KV, dvx, TQ), lambda bb, i: (bb * nq + i, 0, 0, 0)),
            pl.BlockSpec((H_C_KV, 2 * TQ, G_C * TQ), lambda bb, i: (0, 0, 0)),
            pl.BlockSpec((H_C_KV, 1, G_C * TQ), lambda bb, i: (0, 0, 0)),
        ],
        out_specs=pl.BlockSpec((TQ, H_C * DH_C), lambda bb, i: (bb * nq + i, 0)),
        compiler_params=pltpu.CompilerParams(vmem_limit_bytes=VMEM_LIMIT),
        name="swa_attention",
    )(qt, k, k, vtx, vtx, bias_t, sink_row)


def _memkv_kernel(mem_ref, g_ref, w_ref, k_ref, v_ref):
    h = _rms(mem_ref[...], g_ref[...]).astype(BF16)
    kv = _dot(h, w_ref[...])
    k_ref[...] = kv[:, 0:H_X * DH_X].astype(BF16)
    v_ref[...] = kv[:, H_X * DH_X:].astype(BF16)


def _memkv(mem, g, wkv):
    b, m, _ = mem.shape
    n = H_X * DH_X
    return pl.pallas_call(
        _memkv_kernel,
        out_shape=[jax.ShapeDtypeStruct((b, m, n), BF16), jax.ShapeDtypeStruct((b, m, n), BF16)],
        grid=(b,),
        in_specs=[pl.BlockSpec((None, m, D_MODEL), lambda i: (i, 0, 0)),
                  pl.BlockSpec((1, D_MODEL), lambda i: (0, 0)),
                  pl.BlockSpec((D_MODEL, 2 * n), lambda i: (0, 0))],
        out_specs=[pl.BlockSpec((None, m, n), lambda i: (i, 0, 0)), pl.BlockSpec((None, m, n), lambda i: (i, 0, 0))],
        compiler_params=pltpu.CompilerParams(vmem_limit_bytes=VMEM_LIMIT),
        name="mem_kv",
    )(mem, g.reshape(1, D_MODEL), wkv.astype(BF16))


def _lane_half_mask(rows, half):
    lane = lax.broadcasted_iota(jnp.int32, (rows, LANES), 1)
    return (lane < 64) if half == 0 else (lane >= 64)


def _tail_kernel(*refs, n_a):
    x_ref, g_ref = refs[0], refs[1]
    a_refs = refs[2:2 + n_a]
    wout_refs = refs[2 + n_a:2 + 2 * n_a]
    wq_ref, kt_ref, v_ref, wo_ref, w1_ref, w2_ref, o_ref = refs[2 + 2 * n_a:]
    tc = x_ref.shape[0] // TAIL_CHAINS
    chains = [slice(c * tc, (c + 1) * tc) for c in range(TAIL_CHAINS)]
    g = g_ref[...]

    ys = []
    for r in chains:
        y = _dot(a_refs[0][r, :], wout_refs[0][...])
        for a_ref, w_ref in zip(a_refs[1:], wout_refs[1:]):
            y = y + _dot(a_ref[r, :], w_ref[...])
        ys.append(y)
    xs = [x_ref[r, :] + _rms(y, g[1:2, :]) for r, y in zip(chains, ys)]

    hqs = [_rms(x, g[2:3, :]).astype(BF16) for x in xs]
    qs = [_dot(hq, wq_ref[...]) for hq in hqs]
    qs = [(q * (DH_X ** -0.5)).astype(BF16) for q in qs]
    pairs = [[] for _ in chains]
    for pr in range(H_X // 2):
        vp = v_ref[:, pr * LANES:(pr + 1) * LANES]
        outs = [jnp.zeros((tc, LANES), F32) for _ in chains]
        for half in range(2):
            qzs = [jnp.where(_lane_half_mask(tc, half), q[:, pr * LANES:(pr + 1) * LANES], jnp.zeros((tc, LANES), BF16))
                   for q in qs]
            ss = [_dot(qz, kt_ref[pr * LANES:(pr + 1) * LANES, :]) for qz in qzs]
            vz = jnp.where(_lane_half_mask(vp.shape[0], half), vp, jnp.zeros_like(vp))
            ps = []
            for sc in ss:
                e = jnp.exp(sc - jnp.max(sc, axis=1, keepdims=True))
                ps.append((e / jnp.sum(e, axis=1, keepdims=True)).astype(BF16))
            outs = [out + _dot(p, vz) for out, p in zip(outs, ps)]
        for c, out in enumerate(outs):
            pairs[c].append(out.astype(BF16))
    ys = [_dot(jnp.concatenate(pc, axis=1), wo_ref[...]) for pc in pairs]
    xs = [x + _rms(y, g[3:4, :]) for x, y in zip(xs, ys)]

    hms = [_rms(x, g[4:5, :]).astype(BF16) for x in xs]
    n_chunks = D_FF // FF_CHUNK
    ys = [jnp.zeros((tc, D_MODEL), F32) for _ in chains]
    a_cur = [_dot(hm, w1_ref[:, 0:FF_CHUNK]) for hm in hms]
    for k in range(n_chunks):
        if k + 1 < n_chunks:
            a_next = [_dot(hm, w1_ref[:, (k + 1) * FF_CHUNK:(k + 2) * FF_CHUNK]) for hm in hms]
        acts = [jnp.square(jnp.maximum(a, 0.0)).astype(BF16) for a in a_cur]
        ys = [y + _dot(act, w2_ref[k * FF_CHUNK:(k + 1) * FF_CHUNK, :]) for y, act in zip(ys, acts)]
        if k + 1 < n_chunks:
            a_cur = a_next
    for r, x, y in zip(chains, xs, ys):
        o_ref[r, :] = x + _rms(y, g[5:6, :])


def _tail(x2, g, a_list, wout_list, wq, kt, v, wo, w1, w2, b, s):
    t = x2.shape[0]
    tm = min(TM_TAIL, s)
    per_b = s // tm
    n_a = len(a_list)
    row = lambda i: (i, 0)
    full2 = lambda i: (0, 0)
    once = dict(pipeline_mode=pl.Buffered(1))
    in_specs = [pl.BlockSpec((tm, D_MODEL), row), pl.BlockSpec((6, D_MODEL), full2)]
    in_specs += [pl.BlockSpec((tm, a.shape[1]), row) for a in a_list]
    in_specs += [pl.BlockSpec(w.shape, full2, **once) for w in wout_list]
    in_specs += [
        pl.BlockSpec(wq.shape, full2, **once),
        pl.BlockSpec((None,) + kt.shape[1:], lambda i: (i // per_b, 0, 0)),
        pl.BlockSpec((None,) + v.shape[1:], lambda i: (i // per_b, 0, 0)),
        pl.BlockSpec(wo.shape, full2, **once),
        pl.BlockSpec(w1.shape, full2, **once),
        pl.BlockSpec(w2.shape, full2, **once),
    ]
    return pl.pallas_call(
        functools.partial(_tail_kernel, n_a=n_a),
        out_shape=jax.ShapeDtypeStruct((t, D_MODEL), F32),
        grid=(t // tm,),
        in_specs=in_specs,
        out_specs=pl.BlockSpec((tm, D_MODEL), row),
        compiler_params=pltpu.CompilerParams(vmem_limit_bytes=VMEM_LIMIT),
        name="tail",
    )(x2, g, *a_list, *wout_list, wq, kt, v, wo, w1, w2)


def kernel(x, mem, rel_bias_table, norm_g, ev_w_in, ev_a_kv_norm, ev_a_w_uk, ev_a_w_uv, ev_b_lambda, ev_b_subln, ev_w_out, od_w_in, od_sinks, od_w_out, xa_wq, xa_wkv, xa_wo, xa_mem_norm, mlp_w1, mlp_w2):
    b, s, d = x.shape
    depth = norm_g.shape[0]
    assert d == D_MODEL and s % TK == 0 and TK % TQ_DIFF == 0 and TK % TQ_DSA == 0 and TQ_DSA % LANES == 0
    x2 = x.reshape(b * s, d)
    bias_a = _causal_bias_tiles(rel_bias_table[:, :H_A])
    bias_b = _causal_bias_tiles(rel_bias_table[:, H_A:H_A + H_B])
    table_c = rel_bias_table[:, H_A + H_B:]
    for l in range(depth):
        g = norm_g[l].astype(F32)
        if l % 2 == 0:
            e = l // 2
            lam_init = 0.8 - 0.6 * math.exp(-0.3 * l)
            qit, wit, qlt, ki, ckv, ctx, qbt, kb, vtx = _proj_even(x2, g, ev_w_in[e], ev_a_kv_norm[e], ev_a_w_uk[e])
            oa = _dsa(qit, wit, qlt, ki, ckv, ctx, bias_a, ev_a_w_uv[e], b, s)
            ob = _diff(qbt, kb, vtx, bias_b, ev_b_lambda[e], ev_b_subln[e], lam_init, b, s)
            n_a = H_A * DH_A
            a_list = [oa, ob]
            wout_list = [ev_w_out[e][:n_a].astype(BF16), ev_w_out[e][n_a:].astype(BF16)]
        else:
            o = l // 2
            qt, k, vtx = _proj_odd(x2, g, od_w_in[o])
            a_list = [_swa(qt, k, vtx, table_c, od_sinks[o], b, s)]
            wout_list = [od_w_out[o].astype(BF16)]
        mk, mv = _memkv(mem, xa_mem_norm[l], xa_wkv[l])
        x2 = _tail(x2, g, a_list, wout_list, xa_wq[l].astype(BF16), jnp.swapaxes(mk, 1, 2), mv,
                   xa_wo[l].astype(BF16), mlp_w1[l].astype(BF16), mlp_w2[l].astype(BF16), b, s)
    return x2.reshape(b, s, d)
```

```python
import functools
import math

import numpy as np
import jax
import jax.numpy as jnp
from jax import lax
from jax.experimental import pallas as pl
from jax.experimental.pallas import tpu as pltpu

F32 = jnp.float32
BF16 = jnp.bfloat16

D_MODEL = 1024
CHUNK = 64
EPS = 1e-6
NEG_INF = -1e30
LOG2E = math.log2(math.e)

H_A, DH_A, D_LAT = 8, 64, 128
H_IDX, DH_IDX = 8, 64
TOPK_MAX = 256
H_B, DH_B = 4, 64
H_C, H_C_KV, DH_C = 16, 2, 64
G_C = H_C // H_C_KV
WINDOW = 128
H_X, DH_X = 4, 64
D_FF = 4 * D_MODEL
N_BUCKETS = 32
MAX_DIST = 1024

LANES = 128
FOLD_ROWS = 32
TQ = 128
TQ_DSA = 256
TQ_DIFF = 512
TK = 512
SUB = TK // LANES
ONES_ROWS = 16
TM_PROJ = 1024
TM_TAIL = 1024
TAIL_CHAINS = 2
FF_CHUNK = 1024
VMEM_LIMIT = 56 * 1024 * 1024
N_COARSE = 9
N_BISECT = 7
GROUPS = (4, 2, 1)
GROUPS_NEAR = (3, 2, 1)
GROUPS_DSA = (4, 2, 1)
GROUPS_PASS = (4, 1)


def _rms(x, g):
    return x * lax.rsqrt(jnp.mean(x * x, axis=-1, keepdims=True) + EPS) * g


def _dot(a, b):
    return jnp.dot(a, b, preferred_element_type=F32)


def _t5_bucket_np(rel):
    nb = N_BUCKETS // 2
    max_exact = nb // 2
    n = np.abs(rel)
    nf = np.maximum(n, 1).astype(np.float32)
    large = max_exact + (np.log(nf / max_exact) / math.log(MAX_DIST / max_exact) * (nb - max_exact)).astype(np.int32)
    large = np.minimum(large, nb - 1)
    return np.where(rel > 0, nb, 0) + np.where(n < max_exact, n, large)


def _sat_blocks():
    d = 1
    while _t5_bucket_np(np.array([-(d * LANES - (LANES - 1))]))[0] != N_BUCKETS // 2 - 1:
        d += 1
    return d


N_SAT = _sat_blocks()
N_BT = N_SAT + 1


def _toeplitz(u, rows, cols):
    length = u.shape[-1]
    flat = jnp.tile(u, (1,) * (u.ndim - 1) + (rows,))[..., :rows * (length - 1)]
    return flat.reshape(u.shape[:-1] + (rows, length - 1))[..., :cols]


def _causal_bias_tiles(table_cols):
    length = 2 * LANES + 1
    m = np.arange(length)
    diff = np.where(m < LANES + 1, m, m - length)
    rel = -diff[None, :] - (np.arange(N_BT) * LANES)[:, None]
    tab = table_cols.astype(F32)
    u = jnp.take(tab, jnp.asarray(_t5_bucket_np(rel)), axis=0)
    u = (u - tab[N_BUCKETS // 2 - 1][None, None, :]) * LOG2E
    tiles = _toeplitz(jnp.moveaxis(u, -1, 0), LANES, LANES)
    return tiles.reshape(table_cols.shape[1] * N_BT, LANES, LANES).astype(BF16)


def _fold_rows(x, op):
    acc = x[0:FOLD_ROWS, :]
    for r in range(1, x.shape[0] // FOLD_ROWS):
        acc = op(acc, x[r * FOLD_ROWS:(r + 1) * FOLD_ROWS, :])
    return acc


def _grouped_loop(lo, hi, carry, fn, groups):
    for g in groups:
        n = (hi - lo) // g
        carry = lax.fori_loop(0, n, lambda u, c, lo=lo, g=g: fn([lo + u * g + k for k in range(g)], c), carry)
        lo = lo + n * g
    return carry


def _bias_tile(bias_ref, h, qb0, n_qb, j):
    rows = []
    for c in range(SUB):
        cols = [bias_ref[h * N_BT + jnp.clip(qb0 + r - (j * SUB + c), 0, N_SAT)] for r in range(n_qb)]
        rows.append(cols[0] if n_qb == 1 else jnp.concatenate(cols, axis=1))
    return jnp.concatenate(rows, axis=0)


def _ones_rows(n):
    return jnp.ones((ONES_ROWS, n), BF16)


def _proj_even_kernel(x_ref, g_ref, w_ref, kvn_ref, wuk_ref,
                      qit_ref, wit_ref, qlt_ref, ki_ref, c_ref, ctx_ref, qbt_ref, kb_ref, vtx_ref):
    tm = x_ref.shape[0]
    h = _rms(x_ref[...], g_ref[0:1, :]).astype(BF16)
    z = _dot(h, w_ref[...])
    qa = z[:, 0:512].astype(BF16)
    ckv = _rms(z[:, 512:640], kvn_ref[...])
    c_ref[...] = ckv.astype(BF16)
    kb_ref[...] = z[:, 1664:2176].astype(BF16)
    ki_ref[...] = z[:, 2688:2688 + DH_IDX].astype(BF16)
    ctx_ref[0, D_LAT:, :] = _ones_rows(tm)
    for hb in range(H_B):
        vtx_ref[0, hb, 2 * DH_B:, :] = _ones_rows(tm)
    per_q, per_diff = TQ_DSA // LANES, TQ_DIFF // LANES
    for hh in range(H_A):
        ql = _dot(qa[:, hh * DH_A:(hh + 1) * DH_A], wuk_ref[hh]) * (DH_A ** -0.5 * LOG2E)
        for sb in range(tm // LANES):
            off = hh * TQ_DSA + (sb % per_q) * LANES
            qlt_ref[sb // per_q, :, off:off + LANES] = ql[sb * LANES:(sb + 1) * LANES, :].T.astype(BF16)
    for sb in range(tm // LANES):
        rows = slice(sb * LANES, (sb + 1) * LANES)
        blk, sub = sb // per_q, (sb % per_q) * LANES
        for pr in range(H_IDX // 2):
            t = z[rows, 640 + pr * LANES:640 + (pr + 1) * LANES].T
            for half in range(2):
                off = (2 * pr + half) * TQ_DSA + sub
                qit_ref[blk, :, off:off + LANES] = t[half * DH_IDX:(half + 1) * DH_IDX, :].astype(BF16)
        wit_ref[blk, :, sub:sub + LANES] = z[rows, 2816:2816 + LANES].T[0:H_IDX, :]
        ctx_ref[0, 0:D_LAT, rows] = ckv[rows, :].T.astype(BF16)
        for hb in range(H_B):
            qb = z[rows, 1152 + hb * LANES:1152 + (hb + 1) * LANES] * (DH_B ** -0.5 * LOG2E)
            lanes = slice((sb % per_diff) * LANES, (sb % per_diff + 1) * LANES)
            qbt_ref[sb // per_diff, hb, :, lanes] = qb.T.astype(BF16)
            vtx_ref[0, hb, 0:2 * DH_B, rows] = z[rows, 2176 + hb * LANES:2176 + (hb + 1) * LANES].T.astype(BF16)


def _proj_even(x2, g, w_in, kvn, w_uk):
    t = x2.shape[0]
    tm = TK
    assert t % tm == 0 and tm % TQ_DSA == 0 and tm % TQ_DIFF == 0
    pad = jnp.zeros((D_MODEL, LANES - DH_IDX), w_in.dtype)
    cols = [w_in[:, 0:512], w_in[:, 512:640], w_in[:, 640:1152], w_in[:, 1224:1736], w_in[:, 1736:2248],
            w_in[:, 2248:2760], w_in[:, 1152:1216], pad, w_in[:, 1216:1224],
            jnp.zeros((D_MODEL, LANES - H_IDX), w_in.dtype)]
    w = jnp.concatenate(cols, axis=1).astype(BF16)
    n = w.shape[1]
    dv = 2 * DH_B
    row = lambda i: (i, 0)
    full2 = lambda i: (0, 0)
    lead3 = lambda i: (i, 0, 0)
    lead4 = lambda i: (i, 0, 0, 0)
    outs = [
        ((t // TQ_DSA, DH_IDX, H_IDX * TQ_DSA), BF16, (tm // TQ_DSA, DH_IDX, H_IDX * TQ_DSA), lead3),
        ((t // TQ_DSA, H_IDX, TQ_DSA), F32, (tm // TQ_DSA, H_IDX, TQ_DSA), lead3),
        ((t // TQ_DSA, D_LAT, H_A * TQ_DSA), BF16, (tm // TQ_DSA, D_LAT, H_A * TQ_DSA), lead3),
        ((t, DH_IDX), BF16, (tm, DH_IDX), row),
        ((t, D_LAT), BF16, (tm, D_LAT), row),
        ((t // TK, D_LAT + ONES_ROWS, TK), BF16, (1, D_LAT + ONES_ROWS, TK), lead3),
        ((t // TQ_DIFF, H_B, dv, TQ_DIFF), BF16, (tm // TQ_DIFF, H_B, dv, TQ_DIFF), lead4),
        ((t, H_B * dv), BF16, (tm, H_B * dv), row),
        ((t // TK, H_B, dv + ONES_ROWS, TK), BF16, (1, H_B, dv + ONES_ROWS, TK), lead4),
    ]
    return pl.pallas_call(
        _proj_even_kernel,
        out_shape=[jax.ShapeDtypeStruct(shape, dt) for shape, dt, _, _ in outs],
        grid=(t // tm,),
        in_specs=[
            pl.BlockSpec((tm, D_MODEL), row),
            pl.BlockSpec((6, D_MODEL), full2),
            pl.BlockSpec((D_MODEL, n), full2),
            pl.BlockSpec((1, D_LAT), full2),
            pl.BlockSpec((H_A, DH_A, D_LAT), lambda i: (0, 0, 0)),
        ],
        out_specs=[pl.BlockSpec(block, imap) for _, _, block, imap in outs],
        compiler_params=pltpu.CompilerParams(vmem_limit_bytes=VMEM_LIMIT),
        name="proj_even",
    )(x2, g, w, kvn.reshape(1, D_LAT), w_uk.astype(BF16))


def _proj_odd_kernel(x_ref, g_ref, w_ref, qt_ref, k_ref, vtx_ref):
    tm = x_ref.shape[0]
    h = _rms(x_ref[...], g_ref[0:1, :]).astype(BF16)
    z = _dot(h, w_ref[...])
    for kv in range(H_C_KV):
        k_ref[kv] = z[:, 1024 + kv * DH_C:1024 + (kv + 1) * DH_C].astype(BF16)
    for sb in range(tm // TQ):
        rows = slice(sb * TQ, (sb + 1) * TQ)
        for pr in range(H_C // 2):
            t = (z[rows, pr * LANES:(pr + 1) * LANES] * (DH_C ** -0.5 * LOG2E)).T
            for half in range(2):
                hd = 2 * pr + half
                kv, gq = hd // G_C, hd % G_C
                qt_ref[sb, kv, :, gq * TQ:(gq + 1) * TQ] = t[half * DH_C:(half + 1) * DH_C, :].astype(BF16)
        vt = z[rows, 1152:1280].T
        for kv in range(H_C_KV):
            vtx_ref[sb, kv, 0:DH_C, :] = vt[kv * DH_C:(kv + 1) * DH_C, :].astype(BF16)
            vtx_ref[sb, kv, DH_C:, :] = _ones_rows(TQ)


def _proj_odd(x2, g, w_in):
    t = x2.shape[0]
    tm = min(TM_PROJ, t)
    assert TQ == LANES
    row = lambda i: (i, 0)
    full2 = lambda i: (0, 0)
    lead4 = lambda i: (i, 0, 0, 0)
    dvx = DH_C + ONES_ROWS
    return pl.pallas_call(
        _proj_odd_kernel,
        out_shape=[jax.ShapeDtypeStruct((t // TQ, H_C_KV, DH_C, G_C * TQ), BF16),
                   jax.ShapeDtypeStruct((H_C_KV, t, DH_C), BF16),
                   jax.ShapeDtypeStruct((t // TQ, H_C_KV, dvx, TQ), BF16)],
        grid=(t // tm,),
        in_specs=[pl.BlockSpec((tm, D_MODEL), row),
                  pl.BlockSpec((6, D_MODEL), full2),
                  pl.BlockSpec((D_MODEL, 1280), full2)],
        out_specs=[pl.BlockSpec((tm // TQ, H_C_KV, DH_C, G_C * TQ), lead4),
                   pl.BlockSpec((H_C_KV, tm, DH_C), lambda i: (0, i, 0)),
                   pl.BlockSpec((tm // TQ, H_C_KV, dvx, TQ), lead4)],
        compiler_params=pltpu.CompilerParams(vmem_limit_bytes=VMEM_LIMIT),
        name="proj_odd",
    )(x2, g, w_in.astype(BF16))


def _dsa_kernel(qit_ref, wit_ref, qlt_ref, k_ref, c_ref, ctx_ref, bias_ref, wuvt_ref, o_ref,
                sc_ref, scb_ref, acc_ref, *p_refs, topk):
    tq = TQ_DSA
    i = pl.program_id(1)
    t0 = i * tq
    nt = (t0 + tq + TK - 1) // TK
    n_qb = tq // LANES
    n_far = jnp.maximum(i * n_qb - N_SAT + 1, 0) // SUB
    kf = float(topk)

    qpos = lax.broadcasted_iota(jnp.int32, (1, tq), 1) + t0
    limit = (qpos // CHUNK + 1) * CHUNK
    small = limit <= topk
    big = jnp.logical_not(small)
    w = wit_ref[...]
    qit = qit_ref[...]

    def score_tiles(js, carry):
        rmax, rmin = carry
        zs = [_dot(k_ref[j], qit) for j in js]
        for j, z in zip(js, zs):
            sc = jnp.maximum(z[:, 0:tq], 0.0) * w[0:1, :]
            for h in range(1, H_IDX):
                sc = sc + jnp.maximum(z[:, h * tq:(h + 1) * tq], 0.0) * w[h:h + 1, :]
            kpos = lax.broadcasted_iota(jnp.int32, (TK, tq), 0) + j * TK
            adm = kpos < limit
            sc_ref[j] = jnp.where(adm, sc, -jnp.inf)
            scb_ref[j] = jnp.where(adm, sc, -jnp.inf).astype(BF16)
            rmax = jnp.maximum(rmax, _fold_rows(jnp.where(adm, sc, -jnp.inf), jnp.maximum))
            rmin = jnp.minimum(rmin, _fold_rows(jnp.where(adm, sc, jnp.inf), jnp.minimum))
        return rmax, rmin

    rmax, rmin = _grouped_loop(
        0, nt, (jnp.full((FOLD_ROWS, tq), -jnp.inf, F32), jnp.full((FOLD_ROWS, tq), jnp.inf, F32)),
        score_tiles, GROUPS_DSA)
    hi0 = jnp.max(rmax, axis=0, keepdims=True)
    lo0 = jnp.min(rmin, axis=0, keepdims=True)

    def col_reduce(tile_fn, op, init, final):
        def fn(js, acc):
            for j in js:
                acc = op(acc, _fold_rows(tile_fn(j), op))
            return acc
        acc = _grouped_loop(0, nt, jnp.full((FOLD_ROWS, tq), init, F32), fn, GROUPS_PASS)
        return final(acc, axis=0, keepdims=True)

    def count_ge(thr):
        return col_reduce(lambda j: jnp.where(sc_ref[j] >= thr, 1.0, 0.0), jnp.add, 0.0, jnp.sum)

    def max_le(thr):
        def tile(j):
            t = sc_ref[j]
            return jnp.where(t <= thr, t, -jnp.inf)
        return col_reduce(tile, jnp.maximum, -jnp.inf, jnp.max)

    def max_lt(thr):
        def tile(j):
            t = sc_ref[j]
            return jnp.where(t < thr, t, -jnp.inf)
        return col_reduce(tile, jnp.maximum, -jnp.inf, jnp.max)

    def count_ge_coarse(thr):
        def fn(js, acc):
            for j in js:
                hit = jnp.where(scb_ref[j] >= thr, jnp.ones((), BF16), jnp.zeros((), BF16))
                acc = acc + _fold_rows(hit, jnp.add)
            return acc
        acc = _grouped_loop(0, nt, jnp.zeros((FOLD_ROWS, tq), BF16), fn, GROUPS_PASS)
        return jnp.sum(acc.astype(F32), axis=0, keepdims=True)

    def margin(v):
        return jnp.abs(v) * 2.0 ** -6 + 1e-30

    def coarse_body(_, carry):
        lo, hi = carry
        mid = (lo * 0.5 + hi * 0.5).astype(BF16)
        ge = count_ge_coarse(mid) >= kf
        mid = mid.astype(F32)
        return jnp.where(ge, mid, lo), jnp.where(ge, hi, mid)

    lo_c, hi_c = lax.fori_loop(0, N_COARSE, coarse_body,
                               ((lo0 - margin(lo0)).astype(BF16).astype(F32),
                                (hi0 + margin(hi0)).astype(BF16).astype(F32)))

    def bisect_body(_, carry):
        lo, hi = carry
        mid = lo * 0.5 + hi * 0.5
        ge = count_ge(mid) >= kf
        return jnp.where(ge, mid, lo), jnp.where(ge, hi, mid)

    lo, hi = lax.fori_loop(0, N_BISECT, bisect_body,
                           (jnp.maximum(lo_c - margin(lo_c), lo0), jnp.minimum(hi_c, hi0)))

    v0 = max_le(hi)
    c0 = count_ge(v0)

    def walk_cond(carry):
        _, cnt, it = carry
        pending = jnp.where((cnt < kf) & big, 1.0, 0.0)
        return (jnp.max(pending) > 0.0) & (it < topk + 2)

    def walk_body(carry):
        v, cnt, it = carry
        v = jnp.where((cnt < kf) & big, max_lt(v), v)
        return v, count_ge(v), it + 1

    tau, cnt, _ = lax.while_loop(walk_cond, walk_body, (v0, c0, jnp.int32(0)))
    tau = jnp.where(small, -jnp.finfo(F32).max, tau)

    tied = (cnt > kf) & big

    @pl.when(jnp.max(jnp.where(tied, 1.0, 0.0)) > 0.0)
    def _():
        excess = jnp.where(tied, cnt - kf, 0.0)
        kr = lax.broadcasted_iota(jnp.int32, (TK, TK), 0)
        kc = lax.broadcasted_iota(jnp.int32, (TK, TK), 1)
        upper = jnp.where(kc >= kr, 1.0, 0.0).astype(BF16)

        def drop_body(u, later):
            j = nt - 1 - u
            t = sc_ref[j]
            e = (t == tau) & tied
            behind = _dot(upper, jnp.where(e, 1.0, 0.0).astype(BF16)) + later
            sc_ref[j] = jnp.where(e & (behind <= excess), -jnp.inf, t)
            return behind[0:1, :]

        lax.fori_loop(0, nt, drop_body, jnp.zeros((1, tq), F32))

    acc_ref[...] = jnp.zeros(acc_ref.shape, F32)
    qlt = qlt_ref[...]

    def attn_tiles(js, m, with_bias):
        s_alls = [_dot(c_ref[j], qlt).astype(BF16) for j in js]
        for g, (j, s_all) in enumerate(zip(js, s_alls)):
            p_ref = p_refs[g]
            pen = jnp.where(sc_ref[j] >= tau, 0.0, NEG_INF).astype(BF16)
            m_new, alpha = [], []
            for h in range(H_A):
                cols = slice(h * tq, (h + 1) * tq)
                s = s_all[:, cols] + pen
                if with_bias:
                    s = s + _bias_tile(bias_ref, h, i * n_qb, n_qb, j)
                m_old = m[:, cols]
                m_h = jnp.maximum(m_old, jnp.max(_fold_rows(s, jnp.maximum), axis=0, keepdims=True).astype(F32))
                p_ref[:, cols] = jnp.exp2(s - m_h.astype(BF16))
                alpha.append(jnp.exp2(m_old - m_h))
                m_new.append(m_h)
            acc_ref[...] = acc_ref[...] * jnp.concatenate(alpha, axis=1) + _dot(ctx_ref[j], p_ref[...])
            m = jnp.concatenate(m_new, axis=1)
        return m

    m = jnp.full((1, H_A * tq), NEG_INF, F32)
    m = _grouped_loop(0, n_far, m, lambda js, mm: attn_tiles(js, mm, False), GROUPS_DSA)
    _grouped_loop(n_far, nt, m, lambda js, mm: attn_tiles(js, mm, True), GROUPS_NEAR)

    outs = []
    for h in range(H_A):
        cols = slice(h * tq, (h + 1) * tq)
        o_t = (acc_ref[0:D_LAT, cols] / acc_ref[D_LAT:D_LAT + 1, cols]).astype(BF16)
        outs.append(_dot(wuvt_ref[h], o_t))
    for pr in range(H_A // 2):
        pair = jnp.concatenate([outs[2 * pr], outs[2 * pr + 1]], axis=0)
        o_ref[:, pr * LANES:(pr + 1) * LANES] = pair.T.astype(BF16)


def _dsa(qit, wit, qlt, ki, ckv, ctx, bias_a, w_uv, b, s):
    tq = TQ_DSA
    nq, nt = s // tq, s // TK
    topk = min(TOPK_MAX, s // 4)
    assert s // FOLD_ROWS <= 256
    k4 = ki.reshape(b, nt, TK, DH_IDX)
    c4 = ckv.reshape(b, nt, TK, D_LAT)
    ctx = ctx.reshape(b, nt, D_LAT + ONES_ROWS, TK)
    wuvt = jnp.swapaxes(w_uv, 1, 2).astype(BF16)
    qblk = lambda bb, i: (bb * nq + i, 0, 0)
    kv4 = lambda bb, i: (bb, 0, 0, 0)
    once = dict(pipeline_mode=pl.Buffered(1))
    return pl.pallas_call(
        functools.partial(_dsa_kernel, topk=topk),
        out_shape=jax.ShapeDtypeStruct((b * s, H_A * DH_A), BF16),
        grid=(b, nq),
        in_specs=[
            pl.BlockSpec((None, DH_IDX, H_IDX * tq), qblk),
            pl.BlockSpec((None, H_IDX, tq), qblk),
            pl.BlockSpec((None, D_LAT, H_A * tq), qblk),
            pl.BlockSpec((None, nt, TK, DH_IDX), kv4, **once),
            pl.BlockSpec((None, nt, TK, D_LAT), kv4, **once),
            pl.BlockSpec((None, nt, D_LAT + ONES_ROWS, TK), kv4, **once),
            pl.BlockSpec((H_A * N_BT, LANES, LANES), lambda bb, i: (0, 0, 0), **once),
            pl.BlockSpec((H_A, DH_A, D_LAT), lambda bb, i: (0, 0, 0), **once),
        ],
        out_specs=pl.BlockSpec((tq, H_A * DH_A), lambda bb, i: (bb * nq + i, 0)),
        scratch_shapes=[
            pltpu.VMEM((nt, TK, tq), F32),
            pltpu.VMEM((nt, TK, tq), BF16),
            pltpu.VMEM((D_LAT + ONES_ROWS, H_A * tq), F32),
        ] + [pltpu.VMEM((TK, H_A * tq), BF16)] * max(GROUPS_DSA + GROUPS_NEAR),
        compiler_params=pltpu.CompilerParams(vmem_limit_bytes=VMEM_LIMIT),
        name="dsa_attention",
    )(qit, wit, qlt, k4, c4, ctx, bias_a, wuvt)


def _diff_kernel(qt_ref, k_ref, vtx_ref, bias_ref, lam_ref, sub_ref, o_ref, acc_ref, *p_refs, lam_init):
    h = pl.program_id(1)
    i = pl.program_id(2)
    tq = TQ_DIFF
    n_qb = tq // LANES
    t0 = i * tq
    nt = (t0 + tq + TK - 1) // TK
    n_far = jnp.maximum(i * n_qb - N_SAT + 1, 0) // SUB
    qt = qt_ref[...]
    top = lax.broadcasted_iota(jnp.int32, qt.shape, 0) < DH_B
    q2 = jnp.concatenate([jnp.where(top, qt, jnp.zeros_like(qt)), jnp.where(top, jnp.zeros_like(qt), qt)], axis=1)
    qpos = lax.broadcasted_iota(jnp.int32, (1, tq), 1) + t0
    limit = (qpos // CHUNK + 1) * CHUNK
    acc_ref[...] = jnp.zeros(acc_ref.shape, F32)

    def tiles(js, m, near):
        s2s = [_dot(k_ref[j], q2).astype(BF16) for j in js]
        for g, (j, s2) in enumerate(zip(js, s2s)):
            p_ref = p_refs[g]
            if near:
                bias = _bias_tile(bias_ref, h, i * n_qb, n_qb, j)
                adm = (lax.broadcasted_iota(jnp.int32, (TK, tq), 0) + j * TK) < limit
            m_new, alpha = [], []
            for mp in range(2):
                cols = slice(mp * tq, (mp + 1) * tq)
                s = s2[:, cols]
                if near:
                    s = jnp.where(adm, s + bias, jnp.asarray(NEG_INF, BF16))
                m_old = m[:, cols]
                m_h = jnp.maximum(m_old, jnp.max(_fold_rows(s, jnp.maximum), axis=0, keepdims=True).astype(F32))
                p_ref[:, cols] = jnp.exp2(s - m_h.astype(BF16))
                alpha.append(jnp.exp2(m_old - m_h))
                m_new.append(m_h)
            acc_ref[...] = acc_ref[...] * jnp.concatenate(alpha, axis=1) + _dot(vtx_ref[j], p_ref[...])
            m = jnp.concatenate(m_new, axis=1)
        return m

    m = jnp.full((1, 2 * tq), NEG_INF, F32)
    m = _grouped_loop(0, n_far, m, lambda js, mm: tiles(js, mm, False), GROUPS)
    _grouped_loop(n_far, nt, m, lambda js, mm: tiles(js, mm, True), GROUPS_NEAR)

    lf = lam_ref[...]
    lam = (jnp.exp(jnp.sum(lf[0:1, :] * lf[1:2, :], axis=1, keepdims=True))
           - jnp.exp(jnp.sum(lf[2:3, :] * lf[3:4, :], axis=1, keepdims=True)) + lam_init)
    dv = 2 * DH_B
    o = (acc_ref[0:dv, 0:tq] / acc_ref[dv:dv + 1, 0:tq]
         - lam * (acc_ref[0:dv, tq:2 * tq] / acc_ref[dv:dv + 1, tq:2 * tq]))
    o = o * lax.rsqrt(jnp.mean(o * o, axis=0, keepdims=True) + EPS) * sub_ref[...] * (1.0 - lam_init)
    o_ref[...] = o.T.astype(BF16)


def _diff(qbt, kb, vtx, bias_b, b_lambda, b_subln, lam_init, b, s):
    tq = TQ_DIFF
    nq, nt = s // tq, s // TK
    dv = 2 * DH_B
    k4 = kb.reshape(b, nt, TK, H_B * dv)
    vtx = vtx.reshape(b, nt, H_B, dv + ONES_ROWS, TK)
    return pl.pallas_call(
        functools.partial(_diff_kernel, lam_init=lam_init),
        out_shape=jax.ShapeDtypeStruct((b * s, H_B * dv), BF16),
        grid=(b, H_B, nq),
        in_specs=[
            pl.BlockSpec((None, None, dv, tq), lambda bb, h, i: (bb * nq + i, h, 0, 0)),
            pl.BlockSpec((None, nt, TK, dv), lambda bb, h, i: (bb, 0, 0, h)),
            pl.BlockSpec((None, nt, None, dv + ONES_ROWS, TK), lambda bb, h, i: (bb, 0, h, 0, 0)),
            pl.BlockSpec((H_B * N_BT, LANES, LANES), lambda bb, h, i: (0, 0, 0)),
            pl.BlockSpec((4, DH_B), lambda bb, h, i: (0, 0)),
            pl.BlockSpec((dv, 1), lambda bb, h, i: (0, 0)),
        ],
        out_specs=pl.BlockSpec((tq, dv), lambda bb, h, i: (bb * nq + i, h)),
        scratch_shapes=[
            pltpu.VMEM((dv + ONES_ROWS, 2 * tq), F32),
        ] + [pltpu.VMEM((TK, 2 * tq), BF16)] * max(GROUPS),
        compiler_params=pltpu.CompilerParams(vmem_limit_bytes=VMEM_LIMIT),
        name="diff_attention",
    )(qbt, k4, vtx, bias_b, b_lambda.astype(F32), b_subln.reshape(dv, 1).astype(F32))


def _swa_kernel(qt_ref, ka_ref, kb_ref, vta_ref, vtb_ref, bias_ref, sink_ref, o_ref):
    i = pl.program_id(1)
    pad_pen = jnp.where(i == 0, NEG_INF, 0.0)
    for kv in range(H_C_KV):
        qt = qt_ref[kv]
        s = jnp.concatenate([(_dot(ka_ref[kv], qt) + pad_pen).astype(BF16), _dot(kb_ref[kv], qt).astype(BF16)],
                            axis=0) + bias_ref[kv]
        sink = sink_ref[kv]
        m = jnp.maximum(jnp.max(_fold_rows(s, jnp.maximum), axis=0, keepdims=True).astype(F32), sink)
        m = m.astype(BF16).astype(F32)
        p = jnp.exp2(s - m.astype(BF16))
        acc = _dot(vta_ref[kv], p[0:TQ, :]) + _dot(vtb_ref[kv], p[TQ:2 * TQ, :])
        o = acc[0:DH_C, :] / (acc[DH_C:DH_C + 1, :] + jnp.exp2(sink - m))
        for pr in range(G_C // 2):
            pair = jnp.concatenate([o[:, (2 * pr) * TQ:(2 * pr + 1) * TQ], o[:, (2 * pr + 1) * TQ:(2 * pr + 2) * TQ]],
                                   axis=0)
            col = (kv * G_C + 2 * pr) * DH_C
            o_ref[:, col:col + LANES] = pair.T.astype(BF16)


def _swa(qt, k, vtx, table_c, sinks, b, s):
    assert WINDOW == TQ and TQ % CHUNK == 0
    nq = s // TQ
    length = 3 * TQ + 1
    m = np.arange(length)
    diff = np.where(m < 2 * TQ + 1, m, m - length)
    u = jnp.take(table_c.astype(F32), jnp.asarray(_t5_bucket_np(diff - TQ)), axis=0) * LOG2E
    bias = _toeplitz(jnp.moveaxis(u, -1, 0), TQ, 2 * TQ)
    rblk = np.arange(TQ)[:, None] // CHUNK + WINDOW // CHUNK
    cblk = np.arange(2 * TQ)[None, :] // CHUNK
    valid = (cblk <= rblk) & (cblk >= rblk - WINDOW // CHUNK)
    bias = jnp.where(jnp.asarray(valid)[None], bias, NEG_INF)
    bias_t = jnp.transpose(bias.reshape(H_C_KV, G_C, TQ, 2 * TQ), (0, 3, 1, 2)).reshape(
        H_C_KV, 2 * TQ, G_C * TQ).astype(BF16)
    sink_row = jnp.repeat(sinks.astype(F32).reshape(H_C_KV, 1, G_C) * LOG2E, TQ, axis=2)

    dvx = DH_C + ONES_ROWS
    prev = lambda bb, i: bb * nq + jnp.maximum(i - 1, 0)
    return pl.pallas_call(
        _swa_kernel,
        out_shape=jax.ShapeDtypeStruct((b * s, H_C * DH_C), BF16),
        grid=(b, nq),
        in_specs=[
            pl.BlockSpec((None, H_C_KV, DH_C, G_C * TQ), lambda bb, i: (bb * nq + i, 0, 0, 0)),
            pl.BlockSpec((H_C_KV, TQ, DH_C), lambda bb, i: (0, prev(bb, i), 0)),
            pl.BlockSpec((H_C_KV, TQ, DH_C), lambda bb, i: (0, bb * nq + i, 0)),
            pl.BlockSpec((None, H_C_KV, dvx, TQ), lambda bb, i: (prev(bb, i), 0, 0, 0)),
            pl.BlockSpec((None, H_C_KV, dvx, TQ), lambda bb, i: (bb * nq + i, 0, 0, 0)),
            pl.BlockSpec((H_C_KV, 2 * TQ, G_C * TQ), lambda bb, i: (0, 0, 0)),
            pl.BlockSpec((H_C_KV, 1, G_C * TQ), lambda bb, i: (0, 0, 0)),
        ],
        out_specs=pl.BlockSpec((TQ, H_C * DH_C), lambda bb, i: (bb * nq + i, 0)),
        compiler_params=pltpu.CompilerParams(vmem_limit_bytes=VMEM_LIMIT),
        name="swa_attention",
    )(qt, k, k, vtx, vtx, bias_t, sink_row)


def _memkv_kernel(mem_ref, g_ref, w_ref, k_ref, v_ref):
    h = _rms(mem_ref[...], g_ref[...]).astype(BF16)
    kv = _dot(h, w_ref[...])
    k_ref[...] = kv[:, 0:H_X * DH_X].astype(BF16)
    v_ref[...] = kv[:, H_X * DH_X:].astype(BF16)


def _memkv(mem, g, wkv):
    b, m, _ = mem.shape
    n = H_X * DH_X
    return pl.pallas_call(
        _memkv_kernel,
        out_shape=[jax.ShapeDtypeStruct((b, m, n), BF16), jax.ShapeDtypeStruct((b, m, n), BF16)],
        grid=(b,),
        in_specs=[pl.BlockSpec((None, m, D_MODEL), lambda i: (i, 0, 0)),
                  pl.BlockSpec((1, D_MODEL), lambda i: (0, 0)),
                  pl.BlockSpec((D_MODEL, 2 * n), lambda i: (0, 0))],
        out_specs=[pl.BlockSpec((None, m, n), lambda i: (i, 0, 0)), pl.BlockSpec((None, m, n), lambda i: (i, 0, 0))],
        compiler_params=pltpu.CompilerParams(vmem_limit_bytes=VMEM_LIMIT),
        name="mem_kv",
    )(mem, g.reshape(1, D_MODEL), wkv.astype(BF16))


def _lane_half_mask(rows, half):
    lane = lax.broadcasted_iota(jnp.int32, (rows, LANES), 1)
    return (lane < 64) if half == 0 else (lane >= 64)


def _tail_kernel(*refs, n_a):
    x_ref, g_ref = refs[0], refs[1]
    a_refs = refs[2:2 + n_a]
    wout_refs = refs[2 + n_a:2 + 2 * n_a]
    wq_ref, kt_ref, v_ref, wo_ref, w1_ref, w2_ref, o_ref = refs[2 + 2 * n_a:]
    tc = x_ref.shape[0] // TAIL_CHAINS
    chains = [slice(c * tc, (c + 1) * tc) for c in range(TAIL_CHAINS)]
    g = g_ref[...]

    ys = []
    for r in chains:
        y = _dot(a_refs[0][r, :], wout_refs[0][...])
        for a_ref, w_ref in zip(a_refs[1:], wout_refs[1:]):
            y = y + _dot(a_ref[r, :], w_ref[...])
        ys.append(y)
    xs = [x_ref[r, :] + _rms(y, g[1:2, :]) for r, y in zip(chains, ys)]

    hqs = [_rms(x, g[2:3, :]).astype(BF16) for x in xs]
    qs = [_dot(hq, wq_ref[...]) for hq in hqs]
    qs = [(q * (DH_X ** -0.5)).astype(BF16) for q in qs]
    pairs = [[] for _ in chains]
    for pr in range(H_X // 2):
        vp = v_ref[:, pr * LANES:(pr + 1) * LANES]
        outs = [jnp.zeros((tc, LANES), F32) for _ in chains]
        for half in range(2):
            qzs = [jnp.where(_lane_half_mask(tc, half), q[:, pr * LANES:(pr + 1) * LANES], jnp.zeros((tc, LANES), BF16))
                   for q in qs]
            ss = [_dot(qz, kt_ref[pr * LANES:(pr + 1) * LANES, :]) for qz in qzs]
            vz = jnp.where(_lane_half_mask(vp.shape[0], half), vp, jnp.zeros_like(vp))
            ps = []
            for sc in ss:
                e = jnp.exp(sc - jnp.max(sc, axis=1, keepdims=True))
                ps.append((e / jnp.sum(e, axis=1, keepdims=True)).astype(BF16))
            outs = [out + _dot(p, vz) for out, p in zip(outs, ps)]
        for c, out in enumerate(outs):
            pairs[c].append(out.astype(BF16))
    ys = [_dot(jnp.concatenate(pc, axis=1), wo_ref[...]) for pc in pairs]
    xs = [x + _rms(y, g[3:4, :]) for x, y in zip(xs, ys)]

    hms = [_rms(x, g[4:5, :]).astype(BF16) for x in xs]
    n_chunks = D_FF // FF_CHUNK
    ys = [jnp.zeros((tc, D_MODEL), F32) for _ in chains]
    a_cur = [_dot(hm, w1_ref[:, 0:FF_CHUNK]) for hm in hms]
    for k in range(n_chunks):
        if k + 1 < n_chunks:
            a_next = [_dot(hm, w1_ref[:, (k + 1) * FF_CHUNK:(k + 2) * FF_CHUNK]) for hm in hms]
        acts = [jnp.square(jnp.maximum(a, 0.0)).astype(BF16) for a in a_cur]
        ys = [y + _dot(act, w2_ref[k * FF_CHUNK:(k + 1) * FF_CHUNK, :]) for y, act in zip(ys, acts)]
        if k + 1 < n_chunks:
            a_cur = a_next
    for r, x, y in zip(chains, xs, ys):
        o_ref[r, :] = x + _rms(y, g[5:6, :])


def _tail(x2, g, a_list, wout_list, wq, kt, v, wo, w1, w2, b, s):
    t = x2.shape[0]
    tm = min(TM_TAIL, s)
    per_b = s // tm
    n_a = len(a_list)
    row = lambda i: (i, 0)
    full2 = lambda i: (0, 0)
    once = dict(pipeline_mode=pl.Buffered(1))
    in_specs = [pl.BlockSpec((tm, D_MODEL), row), pl.BlockSpec((6, D_MODEL), full2)]
    in_specs += [pl.BlockSpec((tm, a.shape[1]), row) for a in a_list]
    in_specs += [pl.BlockSpec(w.shape, full2, **once) for w in wout_list]
    in_specs += [
        pl.BlockSpec(wq.shape, full2, **once),
        pl.BlockSpec((None,) + kt.shape[1:], lambda i: (i // per_b, 0, 0)),
        pl.BlockSpec((None,) + v.shape[1:], lambda i: (i // per_b, 0, 0)),
        pl.BlockSpec(wo.shape, full2, **once),
        pl.BlockSpec(w1.shape, full2, **once),
        pl.BlockSpec(w2.shape, full2, **once),
    ]
    return pl.pallas_call(
        functools.partial(_tail_kernel, n_a=n_a),
        out_shape=jax.ShapeDtypeStruct((t, D_MODEL), F32),
        grid=(t // tm,),
        in_specs=in_specs,
        out_specs=pl.BlockSpec((tm, D_MODEL), row),
        compiler_params=pltpu.CompilerParams(vmem_limit_bytes=VMEM_LIMIT),
        name="tail",
    )(x2, g, *a_list, *wout_list, wq, kt, v, wo, w1, w2)


def kernel(x, mem, rel_bias_table, norm_g, ev_w_in, ev_a_kv_norm, ev_a_w_uk, ev_a_w_uv, ev_b_lambda, ev_b_subln, ev_w_out, od_w_in, od_sinks, od_w_out, xa_wq, xa_wkv, xa_wo, xa_mem_norm, mlp_w1, mlp_w2):
    b, s, d = x.shape
    depth = norm_g.shape[0]
    assert d == D_MODEL and s % TK == 0 and TK % TQ_DIFF == 0 and TK % TQ_DSA == 0 and TQ_DSA % LANES == 0
    x2 = x.reshape(b * s, d)
    bias_a = _causal_bias_tiles(rel_bias_table[:, :H_A])
    bias_b = _causal_bias_tiles(rel_bias_table[:, H_A:H_A + H_B])
    table_c = rel_bias_table[:, H_A + H_B:]
    for l in range(depth):
        g = norm_g[l].astype(F32)
        if l % 2 == 0:
            e = l // 2
            lam_init = 0.8 - 0.6 * math.exp(-0.3 * l)
            qit, wit, qlt, ki, ckv, ctx, qbt, kb, vtx = _proj_even(x2, g, ev_w_in[e], ev_a_kv_norm[e], ev_a_w_uk[e])
            oa = _dsa(qit, wit, qlt, ki, ckv, ctx, bias_a, ev_a_w_uv[e], b, s)
            ob = _diff(qbt, kb, vtx, bias_b, ev_b_lambda[e], ev_b_subln[e], lam_init, b, s)
            n_a = H_A * DH_A
            a_list = [oa, ob]
            wout_list = [ev_w_out[e][:n_a].astype(BF16), ev_w_out[e][n_a:].astype(BF16)]
        else:
            o = l // 2
            qt, k, vtx = _proj_odd(x2, g, od_w_in[o])
            a_list = [_swa(qt, k, vtx, table_c, od_sinks[o], b, s)]
            wout_list = [od_w_out[o].astype(BF16)]
        mk, mv = _memkv(mem, xa_mem_norm[l], xa_wkv[l])
        x2 = _tail(x2, g, a_list, wout_list, xa_wq[l].astype(BF16), jnp.swapaxes(mk, 1, 2), mv,
                   xa_wo[l].astype(BF16), mlp_w1[l].astype(BF16), mlp_w2[l].astype(BF16), b, s)
    return x2.reshape(b, s, d)
```
